```python
import jax, jax.numpy as jnp
from jax import lax
import numpy as np

D_MODEL = 1024
BATCH = 4
SEQ = 4096
DEPTH = 2

GRID_W = 64
CTX_LEN = 256
HEAD_DIM = 64
NA_HEADS = 8
NA_WIDTH = NA_HEADS * HEAD_DIM
WIN_R = 8
WIN_C = 16
GQA_Q_HEADS = 8
GQA_KV_HEADS = 2
GQA_GROUP = GQA_Q_HEADS // GQA_KV_HEADS
GQA_WIDTH = GQA_Q_HEADS * HEAD_DIM
GQA_KV_WIDTH = GQA_KV_HEADS * HEAD_DIM
Q_BLOCK = 128
ROPE_THETA = 10000.0
ROT_AXIS = HEAD_DIM // 2
EPS = 1e-6
SPLIT_SIZES = (NA_WIDTH, NA_WIDTH, GQA_KV_WIDTH, GQA_KV_WIDTH,
               NA_WIDTH, GQA_WIDTH, NA_WIDTH, GQA_WIDTH, D_MODEL, D_MODEL)
KV_COLS = 2 * NA_WIDTH + 2 * GQA_KV_WIDTH
IN_COLS = sum(SPLIT_SIZES)

kernel_name = "hybrid_na_gqa_prefix_dit"


def _split(p, sizes):
    outs, off = [], 0
    for s in sizes:
        outs.append(p[..., off:off + s])
        off += s
    return outs


def _rmsnorm(x, g):
    xf = x.astype(jnp.float32)
    y = xf * lax.rsqrt(jnp.mean(xf * xf, axis=-1, keepdims=True) + EPS)
    return (y * g.astype(jnp.float32)).astype(x.dtype)


def _rope_axis(x, pos):
    half = ROT_AXIS // 2
    inv = 1.0 / (ROPE_THETA ** (jnp.arange(half, dtype=jnp.float32) / half))
    ang = pos.astype(jnp.float32)[:, None] * inv[None, :]
    cos = jnp.cos(ang)[None, :, None, :].astype(x.dtype)
    sin = jnp.sin(ang)[None, :, None, :].astype(x.dtype)
    x1, x2 = x[..., :half], x[..., half:]
    return jnp.concatenate([x1 * cos - x2 * sin, x2 * cos + x1 * sin], axis=-1)


def _rope_2d(x, pos_row, pos_col):
    return jnp.concatenate([_rope_axis(x[..., :ROT_AXIS], pos_row),
                            _rope_axis(x[..., ROT_AXIS:], pos_col)], axis=-1)


def _attend(q, k, v):
    s = jnp.einsum('bqkgd,btkd->bkgqt', q, k).astype(jnp.float32) * (HEAD_DIM ** -0.5)
    p = jax.nn.softmax(s, axis=-1).astype(v.dtype)
    return jnp.einsum('bkgqt,btkd->bqkgd', p, v)


def _na_latent(q, k, v, kc, vc, rpb, rows):
    B = q.shape[0]
    wr = min(WIN_R, rows)
    qg = q.reshape(B, rows, GRID_W, NA_HEADS, HEAD_DIM)
    kg = k.reshape(B, rows, GRID_W, NA_HEADS, HEAD_DIM)
    vg = v.reshape(B, rows, GRID_W, NA_HEADS, HEAD_DIM)
    cols = np.arange(GRID_W)
    cs = np.clip(cols - WIN_C // 2, 0, GRID_W - WIN_C)
    col_idx_np = (cs[:, None] + np.arange(WIN_C)[None, :]).astype(np.int32)
    col_off_np = (col_idx_np - cols[:, None] + (WIN_C - 1)).astype(np.int32)
    rws = np.arange(rows)
    rs = np.clip(rws - wr // 2, 0, rows - wr)
    row_idx_np = (rs[:, None] + np.arange(wr)[None, :]).astype(np.int32)
    row_off_np = (row_idx_np - rws[:, None] + (WIN_R - 1)).astype(np.int32)
    col_idx = jnp.asarray(col_idx_np)
    bias_all = rpb[:, row_off_np[:, :, None, None], col_off_np[None, None, :, :]]
    bias_all = jnp.transpose(bias_all, (1, 0, 3, 2, 4)).astype(jnp.float32)
    q_rows = jnp.transpose(qg, (1, 0, 2, 3, 4))
    scale = HEAD_DIM ** -0.5

    def row_block(args):
        qr, ridx, bias = args
        kb = jnp.take(kg, ridx, axis=1)
        vb = jnp.take(vg, ridx, axis=1)
        kw = jnp.take(kb, col_idx, axis=2)
        vw = jnp.take(vb, col_idx, axis=2)
        s_win = jnp.einsum('bchd,bicjhd->bhcij', qr, kw).astype(jnp.float32) * scale + bias[None]
        s_ctx = jnp.einsum('bchd,blhd->bhcl', qr, kc).astype(jnp.float32) * scale
        s = jnp.concatenate([s_win.reshape(B, NA_HEADS, GRID_W, wr * WIN_C), s_ctx], axis=-1)
        p = jax.nn.softmax(s, axis=-1).astype(v.dtype)
        p_win = p[..., :wr * WIN_C].reshape(B, NA_HEADS, GRID_W, wr, WIN_C)
        p_ctx = p[..., wr * WIN_C:]
        return (jnp.einsum('bhcij,bicjhd->bchd', p_win, vw)
                + jnp.einsum('bhcl,blhd->bchd', p_ctx, vc))

    out = lax.map(row_block, (q_rows, jnp.asarray(row_idx_np), bias_all))
    return jnp.transpose(out, (1, 0, 2, 3, 4)).reshape(B, rows * GRID_W, NA_WIDTH)


def _gqa_latent(q, k_all, v_all):
    B, S = q.shape[0], q.shape[1]
    nb = S // Q_BLOCK
    qb = q.reshape(B, nb, Q_BLOCK, GQA_KV_HEADS, GQA_GROUP, HEAD_DIM)
    qb = jnp.transpose(qb, (1, 0, 2, 3, 4, 5))
    out = lax.map(lambda qq: _attend(qq, k_all, v_all), qb)
    return jnp.transpose(out, (1, 0, 2, 3, 4, 5)).reshape(B, S, GQA_WIDTH)


def _merge(a_att, b_att, a_z, b_z, g_a, g_b, w_o_a, w_o_b, w_out):
    o_a = (a_att * jax.nn.silu(a_z)) @ w_o_a
    o_b = (b_att * jax.nn.silu(b_z)) @ w_o_b
    merged = jax.nn.sigmoid(g_a) * o_a + jax.nn.sigmoid(g_b) * o_b
    return merged @ w_out


def _layer(x, ctx, c, c_ctx, w_ada, b_ada, norm_g, w_in, q_norm_a, k_norm_a,
           q_norm_b, k_norm_b, rpb, w_o_a, w_o_b, w_out, pos_row, pos_col, update_ctx):
    B, S, _ = x.shape
    L = ctx.shape[1]
    rows = S // GRID_W
    mod_x = jax.nn.silu(c) @ w_ada + b_ada
    shift_x, scale_x, gate_x = jnp.split(mod_x[:, None, :], 3, axis=-1)
    mod_c = jax.nn.silu(c_ctx) @ w_ada + b_ada
    shift_c, scale_c, gate_c = jnp.split(mod_c, 3)
    hx = _rmsnorm(x, norm_g) * (1.0 + scale_x) + shift_x
    hc = _rmsnorm(ctx, norm_g) * (1.0 + scale_c) + shift_c

    px = hx @ w_in
    a_k, a_v, b_k, b_v, a_q, b_q, a_z, b_z, g_a, g_b = _split(px, SPLIT_SIZES)
    pc = hc @ (w_in if update_ctx else w_in[:, :KV_COLS])
    ca_k, ca_v, cb_k, cb_v = _split(pc[..., :KV_COLS], SPLIT_SIZES[:4])

    hs = lambda t, h: t.reshape(t.shape[0], t.shape[1], h, HEAD_DIM)
    qa = _rmsnorm(hs(a_q, NA_HEADS), q_norm_a)
    ka = _rmsnorm(hs(a_k, NA_HEADS), k_norm_a)
    va = hs(a_v, NA_HEADS)
    cka = _rmsnorm(hs(ca_k, NA_HEADS), k_norm_a)
    cva = hs(ca_v, NA_HEADS)
    a_att = _na_latent(qa, ka, va, cka, cva, rpb, rows)

    qb = _rope_2d(_rmsnorm(hs(b_q, GQA_Q_HEADS), q_norm_b), pos_row, pos_col)
    kb = _rope_2d(_rmsnorm(hs(b_k, GQA_KV_HEADS), k_norm_b), pos_row, pos_col)
    vb = hs(b_v, GQA_KV_HEADS)
    ckb = _rmsnorm(hs(cb_k, GQA_KV_HEADS), k_norm_b)
    cvb = hs(cb_v, GQA_KV_HEADS)
    k_all = jnp.concatenate([ckb, kb], axis=1)
    v_all = jnp.concatenate([cvb, vb], axis=1)
    b_att = _gqa_latent(qb, k_all, v_all)

    out_x = _merge(a_att, b_att, a_z, b_z, g_a, g_b, w_o_a, w_o_b, w_out)
    x_new = x + gate_x * out_x

    if update_ctx:
        _, _, _, _, c_aq, c_bq, c_az, c_bz, c_ga, c_gb = _split(pc, SPLIT_SIZES)
        cqa = _rmsnorm(hs(c_aq, NA_HEADS), q_norm_a)[:, :, :, None, :]
        c_a_att = _attend(cqa, cka, cva).reshape(B, L, NA_WIDTH)
        cqb = _rmsnorm(hs(c_bq, GQA_Q_HEADS), q_norm_b).reshape(
            B, L, GQA_KV_HEADS, GQA_GROUP, HEAD_DIM)
        c_b_att = _attend(cqb, ckb, cvb).reshape(B, L, GQA_WIDTH)
        out_c = _merge(c_a_att, c_b_att, c_az, c_bz, c_ga, c_gb, w_o_a, w_o_b, w_out)
        ctx = ctx + gate_c * out_c
    return x_new, ctx


def setup_inputs(seed: int = 0) -> dict:
    key = jax.random.key(seed)
    ks = jax.random.split(key, 16)
    f32 = jnp.float32
    nrm = lambda k, shape, s: jax.random.normal(k, shape, f32) * s
    return {
        "x": nrm(ks[0], (BATCH, SEQ, D_MODEL), 1.0),
        "c": nrm(ks[1], (BATCH, D_MODEL), 1.0),
        "ctx": nrm(ks[2], (BATCH, CTX_LEN, D_MODEL), 1.0),
        "c_ctx": nrm(ks[3], (D_MODEL,), 1.0),
        "w_ada": nrm(ks[4], (DEPTH, D_MODEL, 3 * D_MODEL), D_MODEL ** -0.5),
        "b_ada": nrm(ks[5], (DEPTH, 3 * D_MODEL), 0.02),
        "norm_g": 1.0 + nrm(ks[6], (DEPTH, D_MODEL), 0.05),
        "w_in": nrm(ks[7], (DEPTH, D_MODEL, IN_COLS), D_MODEL ** -0.5),
        "q_norm_a": 1.0 + nrm(ks[8], (DEPTH, HEAD_DIM), 0.05),
        "k_norm_a": 1.0 + nrm(ks[9], (DEPTH, HEAD_DIM), 0.05),
        "q_norm_b": 1.0 + nrm(ks[10], (DEPTH, HEAD_DIM), 0.05),
        "k_norm_b": 1.0 + nrm(ks[11], (DEPTH, HEAD_DIM), 0.05),
        "rpb": nrm(ks[12], (DEPTH, NA_HEADS, 2 * WIN_R - 1, 2 * WIN_C - 1), 0.1),
        "w_o_a": nrm(ks[13], (DEPTH, NA_WIDTH, D_MODEL), NA_WIDTH ** -0.5),
        "w_o_b": nrm(ks[14], (DEPTH, GQA_WIDTH, D_MODEL), GQA_WIDTH ** -0.5),
        "w_out": nrm(ks[15], (DEPTH, D_MODEL, D_MODEL), D_MODEL ** -0.5),
    }


def reference(x, c, ctx, c_ctx, w_ada, b_ada, norm_g, w_in, q_norm_a, k_norm_a,
              q_norm_b, k_norm_b, rpb, w_o_a, w_o_b, w_out):
    S = x.shape[1]
    t = jnp.arange(S, dtype=jnp.int32)
    pos_row = t // GRID_W
    pos_col = t % GRID_W
    for l in range(DEPTH):
        x, ctx = _layer(x, ctx, c, c_ctx, w_ada[l], b_ada[l], norm_g[l], w_in[l],
                        q_norm_a[l], k_norm_a[l], q_norm_b[l], k_norm_b[l], rpb[l],
                        w_o_a[l], w_o_b[l], w_out[l], pos_row, pos_col,
                        l < DEPTH - 1)
    return x
```

```python
import functools

import numpy as np
import jax
import jax.numpy as jnp
from jax import lax
from jax.experimental import pallas as pl
from jax.experimental.pallas import tpu as pltpu

D_MODEL = 1024
GRID_W = 64
HEAD_DIM = 64
NA_HEADS = 8
NA_WIDTH = NA_HEADS * HEAD_DIM
WIN_R = 8
WIN_C = 16
GQA_Q_HEADS = 8
GQA_KV_HEADS = 2
GQA_WIDTH = GQA_Q_HEADS * HEAD_DIM
GQA_KV_WIDTH = GQA_KV_HEADS * HEAD_DIM
ROPE_THETA = 10000.0
ROT_AXIS = HEAD_DIM // 2
ROT_HALF = ROT_AXIS // 2
EPS = 1e-6
QK_SCALE = HEAD_DIM ** -0.5

LANES = 128
MXU_COLS = 256
MASKED = -1e30
VMEM_LIMIT = 56 * 1024 * 1024

PROJ_ROWS = 512
GQA_Q_ROWS = 128
GQA_K_ROWS = 512

F32 = jnp.float32
BF16 = jnp.bfloat16

SEG_KA, SEG_VA, SEG_KB, SEG_VB, SEG_QA, SEG_QB, SEG_ZA, SEG_ZB, SEG_GA, SEG_GB = range(10)
SEG_WIDTHS = (NA_WIDTH, NA_WIDTH, 2 * GQA_KV_WIDTH, 2 * GQA_KV_WIDTH, NA_WIDTH, GQA_WIDTH,
              NA_WIDTH, GQA_WIDTH, D_MODEL, D_MODEL)
SEG_OFFSETS = tuple(int(v) for v in np.cumsum((0,) + SEG_WIDTHS))
KV_SEGS = 4
SEG_GAIN_ROW = {SEG_KA: 0, SEG_KB: 1, SEG_QA: 2, SEG_QB: 3}
SEG_ROPE = (SEG_KB, SEG_QB)


def _dot(a, b):
    return jnp.dot(a, b, preferred_element_type=F32)


def _dot_t(a, b):
    return lax.dot_general(a, b, (((1,), (1,)), ((), ())), preferred_element_type=F32)


def _params(semantics):
    return pltpu.CompilerParams(dimension_semantics=semantics, vmem_limit_bytes=VMEM_LIMIT)


def _mod_kernel(c_ref, w_ref, b_ref, o_ref):
    c = c_ref[...]
    a = c * jax.nn.sigmoid(c)
    o_ref[0] = jnp.dot(a, w_ref[0], preferred_element_type=F32,
                       precision=lax.Precision.HIGHEST) + b_ref[0]


def _modulation(c_rows, w_ada, b_ada):
    depth = w_ada.shape[0]
    rows = c_rows.shape[0]
    ncol = w_ada.shape[2] // D_MODEL
    return pl.pallas_call(
        _mod_kernel,
        grid=(depth, ncol),
        in_specs=[
            pl.BlockSpec((rows, D_MODEL), lambda l, j: (0, 0)),
            pl.BlockSpec((1, D_MODEL, D_MODEL), lambda l, j: (l, 0, j)),
            pl.BlockSpec((1, 1, D_MODEL), lambda l, j: (l, 0, j)),
        ],
        out_specs=pl.BlockSpec((1, rows, D_MODEL), lambda l, j: (l, 0, j)),
        out_shape=jax.ShapeDtypeStruct((depth, rows, ncol * D_MODEL), F32),
        compiler_params=_params(("arbitrary", "arbitrary")),
        name="mod",
    )(c_rows, w_ada, b_ada.reshape(depth, 1, -1))


def _head_norm(p, gsum, gain):
    sq = (p * p).astype(BF16)
    parts = []
    for j in range(p.shape[1] // MXU_COLS):
        sl = slice(j * MXU_COLS, (j + 1) * MXU_COLS)
        ss = _dot(sq[:, sl], gsum)
        parts.append(p[:, sl] * lax.rsqrt(ss * (1.0 / HEAD_DIM) + EPS))
    y = parts[0] if len(parts) == 1 else jnp.concatenate(parts, axis=1)
    return y * gain


def _rope(y, cos, sin):
    width = y.shape[1]
    lane = lax.broadcasted_iota(jnp.int32, y.shape, 1)
    ahead = pltpu.roll(y, width - ROT_HALF, axis=1)
    behind = pltpu.roll(y, ROT_HALF, axis=1)
    partner = jnp.where((lane % ROT_AXIS) < ROT_HALF, ahead, behind)
    reps = width // LANES
    cos = cos if reps == 1 else jnp.concatenate([cos] * reps, axis=1)
    sin = sin if reps == 1 else jnp.concatenate([sin] * reps, axis=1)
    return y * cos + partner * sin


def _proj_kernel(x_ref, shift_ref, scale_ref, g_ref, w_ref, gsum_ref, gain_ref, cos_ref, sin_ref,
                 *out_refs, nseg):
    x = x_ref[0]
    ms = jnp.mean(x * x, axis=-1, keepdims=True)
    gmod = g_ref[...] * (1.0 + scale_ref[0])
    h = (x * lax.rsqrt(ms + EPS) * gmod + shift_ref[0]).astype(BF16)
    gsum = gsum_ref[...]
    for seg in range(nseg):
        width = SEG_WIDTHS[seg]
        p = _dot(h, w_ref[:, SEG_OFFSETS[seg]:SEG_OFFSETS[seg + 1]])
        if seg in SEG_GAIN_ROW:
            row = SEG_GAIN_ROW[seg]
            p = _head_norm(p, gsum, gain_ref[row:row + 1, :width])
        if seg in SEG_ROPE:
            p = _rope(p, cos_ref[...], sin_ref[...])
        if seg in (SEG_ZA, SEG_ZB):
            p = p * jax.nn.sigmoid(p)
        if seg in (SEG_GA, SEG_GB):
            p = jax.nn.sigmoid(p)
        out_refs[seg][0] = p.astype(BF16)


def _project(x, shift, scale, norm_g, w_packed, gsum, gains, cos, sin, nseg, rows):
    groups, total, _ = x.shape
    ncols = SEG_OFFSETS[nseg]
    const = lambda g, i: (0, 0)
    return pl.pallas_call(
        functools.partial(_proj_kernel, nseg=nseg),
        grid=(groups, total // rows),
        in_specs=[
            pl.BlockSpec((1, rows, D_MODEL), lambda g, i: (g, i, 0)),
            pl.BlockSpec((1, 1, D_MODEL), lambda g, i: (g, 0, 0)),
            pl.BlockSpec((1, 1, D_MODEL), lambda g, i: (g, 0, 0)),
            pl.BlockSpec((1, D_MODEL), const),
            pl.BlockSpec((D_MODEL, ncols), const, pipeline_mode=pl.Buffered(1)),
            pl.BlockSpec((MXU_COLS, MXU_COLS), const),
            pl.BlockSpec(gains.shape, const),
            pl.BlockSpec((rows, LANES), lambda g, i: (i, 0)),
            pl.BlockSpec((rows, LANES), lambda g, i: (i, 0)),
        ],
        out_specs=[pl.BlockSpec((1, rows, SEG_WIDTHS[s]), lambda g, i: (g, i, 0)) for s in range(nseg)],
        out_shape=[jax.ShapeDtypeStruct((groups, total, SEG_WIDTHS[s]), BF16) for s in range(nseg)],
        compiler_params=_params(("arbitrary", "arbitrary")),
        name="proj",
    )(x, shift, scale, norm_g, w_packed, gsum, gains, cos, sin)


def _stack_heads(pair):
    lo = lax.broadcasted_iota(jnp.int32, pair.shape, 1) < HEAD_DIM
    zero = jnp.zeros_like(pair)
    return jnp.concatenate([jnp.where(lo, pair, zero), jnp.where(lo, zero, pair)], axis=0)


def _unstack_heads(o, rows):
    lo = lax.broadcasted_iota(jnp.int32, (rows, LANES), 1) < HEAD_DIM
    return jnp.where(lo, o[:rows], o[rows:])


def _na_kernel(q_ref, k_ref, v_ref, kc_ref, vc_ref, t_ref, o_ref, *, rows):
    kc = kc_ref[0]
    vc = vc_ref[0]
    win_keys = WIN_R * GRID_W

    def body(r, carry):
        start = jnp.clip(r - WIN_R // 2, 0, rows - WIN_R)
        q_off = pl.multiple_of(r * GRID_W, GRID_W)
        k_off = pl.multiple_of(start * GRID_W, GRID_W)
        qs = _stack_heads(q_ref[0, pl.ds(q_off, GRID_W), :])
        kw = k_ref[0, pl.ds(k_off, win_keys), :]
        vw = v_ref[0, pl.ds(k_off, win_keys), :]
        s = _dot_t(qs, kw) + t_ref[0, r - start]
        sc = _dot_t(qs, kc)
        m = jnp.maximum(jnp.max(s, axis=-1, keepdims=True), jnp.max(sc, axis=-1, keepdims=True))
        p = jnp.exp(s - m)
        pc = jnp.exp(sc - m)
        l = jnp.sum(p, axis=-1, keepdims=True) + jnp.sum(pc, axis=-1, keepdims=True)
        o = (_dot(p.astype(BF16), vw) + _dot(pc.astype(BF16), vc)) / l
        o_ref[0, pl.ds(q_off, GRID_W), :] = _unstack_heads(o, GRID_W).astype(BF16)
        return carry

    lax.fori_loop(0, rows, body, 0)


def _neighbourhood_attention(qa, ka, va, cka, cva, table):
    batch, seq, _ = qa.shape
    ctx_len = cka.shape[1]
    rows = seq // GRID_W
    pairs = NA_WIDTH // LANES
    blk = lambda n: pl.BlockSpec((1, n, LANES), lambda b, j: (b, 0, j))
    return pl.pallas_call(
        functools.partial(_na_kernel, rows=rows),
        grid=(batch, pairs),
        in_specs=[blk(seq), blk(seq), blk(seq), blk(ctx_len), blk(ctx_len),
                  pl.BlockSpec((1,) + table.shape[1:], lambda b, j: (j, 0, 0, 0))],
        out_specs=blk(seq),
        out_shape=jax.ShapeDtypeStruct((batch, seq, NA_WIDTH), BF16),
        compiler_params=_params(("arbitrary", "arbitrary")),
        name="na",
    )(qa, ka, va, cka, cva, table)


def _bias_table(rpb, rows):
    wr = min(WIN_R, rows)
    cols = np.arange(GRID_W)
    cstart = np.clip(cols - WIN_C // 2, 0, GRID_W - WIN_C)
    kcol = np.arange(GRID_W)
    valid = (kcol[None, :] >= cstart[:, None]) & (kcol[None, :] < cstart[:, None] + WIN_C)
    col_off = np.clip(kcol[None, :] - cols[:, None] + (WIN_C - 1), 0, 2 * WIN_C - 2)
    case = np.arange(wr)
    win = np.arange(wr)
    row_off = win[None, :] - case[:, None] + (WIN_R - 1)
    bias = rpb[:, row_off[:, None, :, None], col_off[None, :, None, :]]
    bias = jnp.where(valid[None, None, :, None, :], bias.astype(F32), MASKED)
    bias = bias.reshape(NA_HEADS // 2, 2, wr, GRID_W, wr * GRID_W)
    return jnp.transpose(bias, (0, 2, 1, 3, 4)).reshape(NA_HEADS // 2, wr, 2 * GRID_W, wr * GRID_W)


def _gqa_kernel(q_ref, k_ref, v_ref, kc_ref, vc_ref, o_ref, m_ref, l_ref, acc_ref, *, chunks, tq, tk):
    q = q_ref[0]
    qs = jnp.concatenate([_stack_heads(q[:, :LANES]), _stack_heads(q[:, LANES:])], axis=0)

    s = _dot_t(qs, kc_ref[0])
    m0 = jnp.max(s, axis=-1, keepdims=True)
    p = jnp.exp(s - m0)
    m_ref[...] = jnp.broadcast_to(m0, m_ref.shape)
    l_ref[...] = jnp.broadcast_to(jnp.sum(p, axis=-1, keepdims=True), l_ref.shape)
    acc_ref[...] = _dot(p.astype(BF16), vc_ref[0])

    def body(c, carry):
        off = pl.multiple_of(c * tk, tk)
        s = _dot_t(qs, k_ref[0, pl.ds(off, tk), :])
        m_old = m_ref[...]
        m_new = jnp.maximum(m_old, jnp.max(s, axis=-1, keepdims=True))
        alpha = jnp.exp(m_old - m_new)
        p = jnp.exp(s - jnp.concatenate([m_new] * (tk // LANES), axis=1))
        l_ref[...] = alpha * l_ref[...] + jnp.sum(p, axis=-1, keepdims=True)
        acc_ref[...] = alpha * acc_ref[...] + _dot(p.astype(BF16), v_ref[0, pl.ds(off, tk), :])
        m_ref[...] = m_new
        return carry

    lax.fori_loop(0, chunks, body, 0)
    o = acc_ref[...] / l_ref[...]
    o_ref[0] = jnp.concatenate([_unstack_heads(o[:2 * tq], tq), _unstack_heads(o[2 * tq:], tq)],
                               axis=1).astype(BF16)


def _gqa_attention(qb, kbd, vbd, ckbd, cvbd):
    batch, seq, _ = qb.shape
    ctx_len = ckbd.shape[1]
    tq, tk = GQA_Q_ROWS, GQA_K_ROWS
    stacked = (GQA_Q_HEADS // GQA_KV_HEADS) * tq
    kv = lambda n: pl.BlockSpec((1, n, LANES), lambda b, g, i: (b, 0, g))
    qspec = pl.BlockSpec((1, tq, 2 * LANES), lambda b, g, i: (b, i, g))
    return pl.pallas_call(
        functools.partial(_gqa_kernel, chunks=seq // tk, tq=tq, tk=tk),
        grid=(batch, GQA_KV_HEADS, seq // tq),
        in_specs=[qspec, kv(seq), kv(seq), kv(ctx_len), kv(ctx_len)],
        out_specs=qspec,
        out_shape=jax.ShapeDtypeStruct((batch, seq, GQA_WIDTH), BF16),
        scratch_shapes=[pltpu.VMEM((stacked, LANES), F32)] * 3,
        compiler_params=_params(("arbitrary", "arbitrary", "arbitrary")),
        name="gqa",
    )(qb, kbd, vbd, ckbd, cvbd)


def _softmax_attend(qs, k, v):
    s = _dot_t(qs, k)
    p = jnp.exp(s - jnp.max(s, axis=-1, keepdims=True))
    return _dot(p.astype(BF16), v) / jnp.sum(p, axis=-1, keepdims=True)


def _ctx_kernel(qa_ref, ka_ref, va_ref, qb_ref, kb_ref, vb_ref, oa_ref, ob_ref):
    n = qa_ref.shape[1]
    for j in range(NA_WIDTH // LANES):
        sl = slice(j * LANES, (j + 1) * LANES)
        o = _softmax_attend(_stack_heads(qa_ref[0, :, sl]), ka_ref[0, :, sl], va_ref[0, :, sl])
        oa_ref[0, :, sl] = _unstack_heads(o, n).astype(BF16)
    per_kv = GQA_WIDTH // GQA_KV_HEADS // LANES
    for j in range(GQA_WIDTH // LANES):
        sl = slice(j * LANES, (j + 1) * LANES)
        g = j // per_kv
        kv = slice(g * LANES, (g + 1) * LANES)
        o = _softmax_attend(_stack_heads(qb_ref[0, :, sl]), kb_ref[0, :, kv], vb_ref[0, :, kv])
        ob_ref[0, :, sl] = _unstack_heads(o, n).astype(BF16)


def _context_attention(cqa, cka, cva, cqb, ckbd, cvbd):
    batch, n, _ = cqa.shape
    spec = lambda a: pl.BlockSpec((1,) + a.shape[1:], lambda b: (b, 0, 0))
    args = (cqa, cka, cva, cqb, ckbd, cvbd)
    return pl.pallas_call(
        _ctx_kernel,
        grid=(batch,),
        in_specs=[spec(a) for a in args],
        out_specs=[spec(cqa), spec(cqb)],
        out_shape=[jax.ShapeDtypeStruct(cqa.shape, BF16), jax.ShapeDtypeStruct(cqb.shape, BF16)],
        compiler_params=_params(("arbitrary",)),
        name="ctx",
    )(*args)


def _merge_kernel(a_ref, b_ref, za_ref, zb_ref, ga_ref, gb_ref, x_ref, gate_ref,
                  woa_ref, wob_ref, wout_ref, o_ref):
    o_a = _dot(a_ref[0] * za_ref[0], woa_ref[...])
    o_b = _dot(b_ref[0] * zb_ref[0], wob_ref[...])
    merged = ga_ref[0].astype(F32) * o_a + gb_ref[0].astype(F32) * o_b
    o_ref[0] = x_ref[0] + gate_ref[0] * _dot(merged.astype(BF16), wout_ref[...])


def _merge(a_att, b_att, za, zb, ga, gb, x, gate, w_o_a, w_o_b, w_out, rows):
    groups, total, _ = x.shape
    act = lambda w: pl.BlockSpec((1, rows, w), lambda g, i: (g, i, 0))
    const = lambda a: pl.BlockSpec(a.shape, lambda g, i: (0, 0))
    return pl.pallas_call(
        _merge_kernel,
        grid=(groups, total // rows),
        in_specs=[act(NA_WIDTH), act(GQA_WIDTH), act(NA_WIDTH), act(GQA_WIDTH), act(D_MODEL), act(D_MODEL),
                  act(D_MODEL), pl.BlockSpec((1, 1, D_MODEL), lambda g, i: (g, 0, 0)),
                  const(w_o_a), const(w_o_b), const(w_out)],
        out_specs=act(D_MODEL),
        out_shape=jax.ShapeDtypeStruct(x.shape, F32),
        compiler_params=_params(("arbitrary", "arbitrary")),
        name="merge",
    )(a_att, b_att, za, zb, ga, gb, x, gate, w_o_a, w_o_b, w_out)


def _pack_w_in(w):
    o = NA_WIDTH * 2
    dup = lambda c: jnp.concatenate(
        [c[:, h * HEAD_DIM:(h + 1) * HEAD_DIM] for h in range(GQA_KV_HEADS) for _ in range(2)], axis=1)
    b_k = w[:, o:o + GQA_KV_WIDTH]
    b_v = w[:, o + GQA_KV_WIDTH:o + 2 * GQA_KV_WIDTH]
    return jnp.concatenate([w[:, :o], dup(b_k), dup(b_v), w[:, o + 2 * GQA_KV_WIDTH:]], axis=1).astype(BF16)


def _rope_tables(seq):
    t = jnp.arange(seq, dtype=jnp.int32)
    inv = 1.0 / (ROPE_THETA ** (jnp.arange(ROT_HALF, dtype=F32) / ROT_HALF))
    ang_r = (t // GRID_W).astype(F32)[:, None] * inv[None, :]
    ang_c = (t % GRID_W).astype(F32)[:, None] * inv[None, :]
    cos = jnp.concatenate([jnp.cos(ang_r)] * 2 + [jnp.cos(ang_c)] * 2, axis=1)
    sin = jnp.concatenate([-jnp.sin(ang_r), jnp.sin(ang_r), -jnp.sin(ang_c), jnp.sin(ang_c)], axis=1)
    reps = LANES // HEAD_DIM
    return jnp.tile(cos, (1, reps)), jnp.tile(sin, (1, reps))


def _group_sum_matrix():
    idx = np.arange(MXU_COLS) // HEAD_DIM
    return jnp.asarray(idx[:, None] == idx[None, :], dtype=BF16)


def kernel(x, c, ctx, c_ctx, w_ada, b_ada, norm_g, w_in, q_norm_a, k_norm_a, q_norm_b, k_norm_b,
           rpb, w_o_a, w_o_b, w_out):
    batch, seq, _ = x.shape
    ctx_len = ctx.shape[1]
    depth = w_ada.shape[0]
    rows = seq // GRID_W
    assert seq % GRID_W == 0 and rows >= WIN_R and seq % PROJ_ROWS == 0 and seq % GQA_K_ROWS == 0

    pad = -(batch + 1) % 8
    c_rows = jnp.concatenate([c, c_ctx[None, :], jnp.zeros((pad, D_MODEL), F32)], axis=0)
    mod = _modulation(c_rows, w_ada, b_ada)

    cos_x, sin_x = _rope_tables(seq)
    cos_c = jnp.ones((ctx_len, LANES), F32)
    sin_c = jnp.zeros((ctx_len, LANES), F32)
    gsum = _group_sum_matrix()

    for l in range(depth):
        update_ctx = l < depth - 1
        shift, scale, gate = (mod[l, :, i * D_MODEL:(i + 1) * D_MODEL] for i in range(3))
        per_batch = lambda m: m[:batch, None, :]
        for_ctx = lambda m: jnp.broadcast_to(m[batch][None, None, :], (batch, 1, D_MODEL))
        w_packed = _pack_w_in(w_in[l])
        tile = lambda g, n: jnp.tile(g, n)
        gains = jnp.stack([tile(k_norm_a[l], NA_HEADS), tile(k_norm_b[l], NA_HEADS),
                           tile(q_norm_a[l], NA_HEADS) * QK_SCALE,
                           tile(q_norm_b[l], GQA_Q_HEADS) * QK_SCALE])
        g_row = norm_g[l][None, :]

        ka, va, kbd, vbd, qa, qb, za, zb, ga, gb = _project(
            x, per_batch(shift), per_batch(scale), g_row, w_packed, gsum, gains, cos_x, sin_x,
            len(SEG_WIDTHS), PROJ_ROWS)
        nseg_c = len(SEG_WIDTHS) if update_ctx else KV_SEGS
        w_ctx = w_packed if update_ctx else w_packed[:, :SEG_OFFSETS[KV_SEGS]]
        pc = _project(ctx, for_ctx(shift), for_ctx(scale), g_row, w_ctx, gsum, gains, cos_c, sin_c,
                      nseg_c, ctx_len)
        cka, cva, ckbd, cvbd = pc[:KV_SEGS]

        a_att = _neighbourhood_attention(qa, ka, va, cka, cva, _bias_table(rpb[l], rows))
        b_att = _gqa_attention(qb, kbd, vbd, ckbd, cvbd)
        wa, wb, wo = w_o_a[l].astype(BF16), w_o_b[l].astype(BF16), w_out[l].astype(BF16)
        x_new = _merge(a_att, b_att, za, zb, ga, gb, x, per_batch(gate), wa, wb, wo, PROJ_ROWS)

        if update_ctx:
            _, _, _, _, cqa, cqb, cza, czb, cga, cgb = pc
            c_a, c_b = _context_attention(cqa, cka, cva, cqb, ckbd, cvbd)
            ctx = _merge(c_a, c_b, cza, czb, cga, cgb, ctx, for_ctx(gate), wa, wb, wo, ctx_len)
        x = x_new
    return x
```

```python
import functools

import numpy as np
import jax
import jax.numpy as jnp
from jax import lax
from jax.experimental import pallas as pl
from jax.experimental.pallas import tpu as pltpu

D_MODEL = 1024
GRID_W = 64
HEAD_DIM = 64
NA_HEADS = 8
NA_WIDTH = NA_HEADS * HEAD_DIM
WIN_R = 8
WIN_C = 16
GQA_Q_HEADS = 8
GQA_KV_HEADS = 2
GQA_WIDTH = GQA_Q_HEADS * HEAD_DIM
GQA_KV_WIDTH = GQA_KV_HEADS * HEAD_DIM
ROPE_THETA = 10000.0
ROT_AXIS = HEAD_DIM // 2
ROT_HALF = ROT_AXIS // 2
EPS = 1e-6
QK_SCALE = HEAD_DIM ** -0.5

LANES = 128
MXU_COLS = 256
MASKED = -1e30
VMEM_LIMIT = 56 * 1024 * 1024

PROJ_ROWS = 512
GQA_Q_ROWS = 128
GQA_K_ROWS = 512

F32 = jnp.float32
BF16 = jnp.bfloat16

SEG_KA, SEG_VA, SEG_KB, SEG_VB, SEG_QA, SEG_QB, SEG_ZA, SEG_ZB, SEG_GA, SEG_GB = range(10)
SEG_WIDTHS = (NA_WIDTH, NA_WIDTH, 2 * GQA_KV_WIDTH, 2 * GQA_KV_WIDTH, NA_WIDTH, GQA_WIDTH,
              NA_WIDTH, GQA_WIDTH, D_MODEL, D_MODEL)
SEG_OFFSETS = tuple(int(v) for v in np.cumsum((0,) + SEG_WIDTHS))
KV_SEGS = 4
SEG_GAIN_ROW = {SEG_KA: 0, SEG_KB: 1, SEG_QA: 2, SEG_QB: 3}
SEG_ROPE = (SEG_KB, SEG_QB)


def _dot(a, b):
    return jnp.dot(a, b, preferred_element_type=F32)


def _dot_t(a, b):
    return lax.dot_general(a, b, (((1,), (1,)), ((), ())), preferred_element_type=F32)


def _params(semantics):
    return pltpu.CompilerParams(dimension_semantics=semantics, vmem_limit_bytes=VMEM_LIMIT)


def _mod_kernel(c_ref, w_ref, b_ref, o_ref):
    c = c_ref[...]
    a = c * jax.nn.sigmoid(c)
    o_ref[0] = jnp.dot(a, w_ref[0], preferred_element_type=F32,
                       precision=lax.Precision.HIGHEST) + b_ref[0]


def _modulation(c_rows, w_ada, b_ada):
    depth = w_ada.shape[0]
    rows = c_rows.shape[0]
    ncol = w_ada.shape[2] // D_MODEL
    return pl.pallas_call(
        _mod_kernel,
        grid=(depth, ncol),
        in_specs=[
            pl.BlockSpec((rows, D_MODEL), lambda l, j: (0, 0)),
            pl.BlockSpec((1, D_MODEL, D_MODEL), lambda l, j: (l, 0, j)),
            pl.BlockSpec((1, 1, D_MODEL), lambda l, j: (l, 0, j)),
        ],
        out_specs=pl.BlockSpec((1, rows, D_MODEL), lambda l, j: (l, 0, j)),
        out_shape=jax.ShapeDtypeStruct((depth, rows, ncol * D_MODEL), F32),
        compiler_params=_params(("arbitrary", "arbitrary")),
        name="mod",
    )(c_rows, w_ada, b_ada.reshape(depth, 1, -1))


def _head_norm(p, gsum, gain):
    sq = (p * p).astype(BF16)
    parts = []
    for j in range(p.shape[1] // MXU_COLS):
        sl = slice(j * MXU_COLS, (j + 1) * MXU_COLS)
        ss = _dot(sq[:, sl], gsum)
        parts.append(p[:, sl] * lax.rsqrt(ss * (1.0 / HEAD_DIM) + EPS))
    y = parts[0] if len(parts) == 1 else jnp.concatenate(parts, axis=1)
    return y * gain


def _rope(y, cos, sin):
    width = y.shape[1]
    lane = lax.broadcasted_iota(jnp.int32, y.shape, 1)
    ahead = pltpu.roll(y, width - ROT_HALF, axis=1)
    behind = pltpu.roll(y, ROT_HALF, axis=1)
    partner = jnp.where((lane % ROT_AXIS) < ROT_HALF, ahead, behind)
    reps = width // LANES
    cos = cos if reps == 1 else jnp.concatenate([cos] * reps, axis=1)
    sin = sin if reps == 1 else jnp.concatenate([sin] * reps, axis=1)
    return y * cos + partner * sin


def _proj_kernel(x_ref, shift_ref, scale_ref, g_ref, w_ref, gsum_ref, gain_ref, cos_ref, sin_ref,
                 *out_refs, nseg):
    x = x_ref[0]
    ms = jnp.mean(x * x, axis=-1, keepdims=True)
    gmod = g_ref[...] * (1.0 + scale_ref[0])
    h = (x * lax.rsqrt(ms + EPS) * gmod + shift_ref[0]).astype(BF16)
    gsum = gsum_ref[...]
    for seg in range(nseg):
        width = SEG_WIDTHS[seg]
        p = _dot(h, w_ref[:, SEG_OFFSETS[seg]:SEG_OFFSETS[seg + 1]])
        if seg in SEG_GAIN_ROW:
            row = SEG_GAIN_ROW[seg]
            p = _head_norm(p, gsum, gain_ref[row:row + 1, :width])
        if seg in SEG_ROPE:
            p = _rope(p, cos_ref[...], sin_ref[...])
        if seg in (SEG_ZA, SEG_ZB):
            p = p * jax.nn.sigmoid(p)
        if seg in (SEG_GA, SEG_GB):
            p = jax.nn.sigmoid(p)
        out_refs[seg][0] = p.astype(BF16)


def _project(x, shift, scale, norm_g, w_packed, gsum, gains, cos, sin, nseg, rows):
    groups, total, _ = x.shape
    ncols = SEG_OFFSETS[nseg]
    const = lambda g, i: (0, 0)
    return pl.pallas_call(
        functools.partial(_proj_kernel, nseg=nseg),
        grid=(groups, total // rows),
        in_specs=[
            pl.BlockSpec((1, rows, D_MODEL), lambda g, i: (g, i, 0)),
            pl.BlockSpec((1, 1, D_MODEL), lambda g, i: (g, 0, 0)),
            pl.BlockSpec((1, 1, D_MODEL), lambda g, i: (g, 0, 0)),
            pl.BlockSpec((1, D_MODEL), const),
            pl.BlockSpec((D_MODEL, ncols), const, pipeline_mode=pl.Buffered(1)),
            pl.BlockSpec((MXU_COLS, MXU_COLS), const),
            pl.BlockSpec(gains.shape, const),
            pl.BlockSpec((rows, LANES), lambda g, i: (i, 0)),
            pl.BlockSpec((rows, LANES), lambda g, i: (i, 0)),
        ],
        out_specs=[pl.BlockSpec((1, rows, SEG_WIDTHS[s]), lambda g, i: (g, i, 0)) for s in range(nseg)],
        out_shape=[jax.ShapeDtypeStruct((groups, total, SEG_WIDTHS[s]), BF16) for s in range(nseg)],
        compiler_params=_params(("arbitrary", "arbitrary")),
        name="proj",
    )(x, shift, scale, norm_g, w_packed, gsum, gains, cos, sin)


def _stack_heads(pair):
    lo = lax.broadcasted_iota(jnp.int32, pair.shape, 1) < HEAD_DIM
    zero = jnp.zeros_like(pair)
    return jnp.concatenate([jnp.where(lo, pair, zero), jnp.where(lo, zero, pair)], axis=0)


def _unstack_heads(o, rows):
    lo = lax.broadcasted_iota(jnp.int32, (rows, LANES), 1) < HEAD_DIM
    return jnp.where(lo, o[:rows], o[rows:])


def _na_kernel(q_ref, k_ref, v_ref, kc_ref, vc_ref, t_ref, o_ref, *, rows):
    kc = kc_ref[0]
    vc = vc_ref[0]
    win_keys = WIN_R * GRID_W

    def body(r, carry):
        start = jnp.clip(r - WIN_R // 2, 0, rows - WIN_R)
        q_off = pl.multiple_of(r * GRID_W, GRID_W)
        k_off = pl.multiple_of(start * GRID_W, GRID_W)
        qs = _stack_heads(q_ref[0, pl.ds(q_off, GRID_W), :])
        kw = k_ref[0, pl.ds(k_off, win_keys), :]
        vw = v_ref[0, pl.ds(k_off, win_keys), :]
        s = _dot_t(qs, kw) + t_ref[0, r - start]
        sc = _dot_t(qs, kc)
        m = jnp.maximum(jnp.max(s, axis=-1, keepdims=True), jnp.max(sc, axis=-1, keepdims=True))
        p = jnp.exp(s - m)
        pc = jnp.exp(sc - m)
        l = jnp.sum(p, axis=-1, keepdims=True) + jnp.sum(pc, axis=-1, keepdims=True)
        o = (_dot(p.astype(BF16), vw) + _dot(pc.astype(BF16), vc)) / l
        o_ref[0, pl.ds(q_off, GRID_W), :] = _unstack_heads(o, GRID_W).astype(BF16)
        return carry

    lax.fori_loop(0, rows, body, 0)


def _neighbourhood_attention(qa, ka, va, cka, cva, table):
    batch, seq, _ = qa.shape
    ctx_len = cka.shape[1]
    rows = seq // GRID_W
    pairs = NA_WIDTH // LANES
    blk = lambda n: pl.BlockSpec((1, n, LANES), lambda b, j: (b, 0, j))
    return pl.pallas_call(
        functools.partial(_na_kernel, rows=rows),
        grid=(batch, pairs),
        in_specs=[blk(seq), blk(seq), blk(seq), blk(ctx_len), blk(ctx_len),
                  pl.BlockSpec((1,) + table.shape[1:], lambda b, j: (j, 0, 0, 0))],
        out_specs=blk(seq),
        out_shape=jax.ShapeDtypeStruct((batch, seq, NA_WIDTH), BF16),
        compiler_params=_params(("arbitrary", "arbitrary")),
        name="na",
    )(qa, ka, va, cka, cva, table)


def _bias_kernel(rpb_ref, onehot_ref, mask_ref, o_ref):
    o_ref[0] = jnp.dot(rpb_ref[0], onehot_ref[...], preferred_element_type=F32,
                       precision=lax.Precision.HIGHEST) + mask_ref[...]


def _bias_tables(rpb, rows):
    depth = rpb.shape[0]
    wr = min(WIN_R, rows)
    n_row_off, n_col_off = 2 * WIN_R - 1, 2 * WIN_C - 1
    cols = np.arange(GRID_W)
    cstart = np.clip(cols - WIN_C // 2, 0, GRID_W - WIN_C)
    kcol = np.arange(GRID_W)
    valid = (kcol[None, :] >= cstart[:, None]) & (kcol[None, :] < cstart[:, None] + WIN_C)
    col_off = kcol[None, :] - cols[:, None] + (WIN_C - 1)
    pad_r, pad_c = -n_row_off % 8, -n_col_off % 8
    onehot = (np.arange(n_col_off + pad_c)[:, None, None] == col_off[None]) & valid[None]
    onehot = jnp.asarray(onehot.reshape(n_col_off + pad_c, GRID_W * GRID_W), dtype=F32)
    mask = jnp.asarray(np.where(valid, 0.0, MASKED).reshape(1, GRID_W * GRID_W), dtype=F32)
    heads = depth * NA_HEADS
    rpb_p = jnp.pad(rpb.reshape(heads, n_row_off, n_col_off), ((0, 0), (0, pad_r), (0, pad_c)))
    toep = pl.pallas_call(
        _bias_kernel,
        grid=(heads,),
        in_specs=[pl.BlockSpec((1,) + rpb_p.shape[1:], lambda h: (h, 0, 0)),
                  pl.BlockSpec(onehot.shape, lambda h: (0, 0)),
                  pl.BlockSpec(mask.shape, lambda h: (0, 0))],
        out_specs=pl.BlockSpec((1, n_row_off + pad_r, GRID_W * GRID_W), lambda h: (h, 0, 0)),
        out_shape=jax.ShapeDtypeStruct((heads, n_row_off + pad_r, GRID_W * GRID_W), F32),
        compiler_params=_params(("arbitrary",)),
        name="bias",
    )(rpb_p, onehot, mask)
    toep = toep.reshape(depth, NA_HEADS // 2, 2, n_row_off + pad_r, GRID_W, GRID_W)
    cases = [toep[:, :, :, WIN_R - 1 - case:WIN_R - 1 - case + wr] for case in range(wr)]
    table = jnp.stack(cases, axis=2)
    table = jnp.transpose(table, (0, 1, 2, 3, 5, 4, 6))
    return table.reshape(depth, NA_HEADS // 2, wr, 2 * GRID_W, wr * GRID_W)


def _gqa_kernel(q_ref, k_ref, v_ref, kc_ref, vc_ref, o_ref, m_ref, l_ref, acc_ref, *, chunks, tq, tk):
    q = q_ref[0]
    qs = jnp.concatenate([_stack_heads(q[:, :LANES]), _stack_heads(q[:, LANES:])], axis=0)

    s = _dot_t(qs, kc_ref[0])
    m0 = jnp.max(s, axis=-1, keepdims=True)
    p = jnp.exp(s - m0)
    m_ref[...] = jnp.broadcast_to(m0, m_ref.shape)
    l_ref[...] = jnp.broadcast_to(jnp.sum(p, axis=-1, keepdims=True), l_ref.shape)
    acc_ref[...] = _dot(p.astype(BF16), vc_ref[0])

    def body(c, carry):
        off = pl.multiple_of(c * tk, tk)
        s = _dot_t(qs, k_ref[0, pl.ds(off, tk), :])
        m_old = m_ref[...]
        m_new = jnp.maximum(m_old, jnp.max(s, axis=-1, keepdims=True))
        alpha = jnp.exp(m_old - m_new)
        p = jnp.exp(s - jnp.concatenate([m_new] * (tk // LANES), axis=1))
        l_ref[...] = alpha * l_ref[...] + jnp.sum(p, axis=-1, keepdims=True)
        acc_ref[...] = alpha * acc_ref[...] + _dot(p.astype(BF16), v_ref[0, pl.ds(off, tk), :])
        m_ref[...] = m_new
        return carry

    lax.fori_loop(0, chunks, body, 0)
    o = acc_ref[...] / l_ref[...]
    o_ref[0] = jnp.concatenate([_unstack_heads(o[:2 * tq], tq), _unstack_heads(o[2 * tq:], tq)],
                               axis=1).astype(BF16)


def _gqa_attention(qb, kbd, vbd, ckbd, cvbd):
    batch, seq, _ = qb.shape
    ctx_len = ckbd.shape[1]
    tq, tk = GQA_Q_ROWS, GQA_K_ROWS
    stacked = (GQA_Q_HEADS // GQA_KV_HEADS) * tq
    kv = lambda n: pl.BlockSpec((1, n, LANES), lambda b, g, i: (b, 0, g))
    qspec = pl.BlockSpec((1, tq, 2 * LANES), lambda b, g, i: (b, i, g))
    return pl.pallas_call(
        functools.partial(_gqa_kernel, chunks=seq // tk, tq=tq, tk=tk),
        grid=(batch, GQA_KV_HEADS, seq // tq),
        in_specs=[qspec, kv(seq), kv(seq), kv(ctx_len), kv(ctx_len)],
        out_specs=qspec,
        out_shape=jax.ShapeDtypeStruct((batch, seq, GQA_WIDTH), BF16),
        scratch_shapes=[pltpu.VMEM((stacked, LANES), F32)] * 3,
        compiler_params=_params(("arbitrary", "arbitrary", "arbitrary")),
        name="gqa",
    )(qb, kbd, vbd, ckbd, cvbd)


def _softmax_attend(qs, k, v):
    s = _dot_t(qs, k)
    p = jnp.exp(s - jnp.max(s, axis=-1, keepdims=True))
    return _dot(p.astype(BF16), v) / jnp.sum(p, axis=-1, keepdims=True)


def _ctx_kernel(qa_ref, ka_ref, va_ref, qb_ref, kb_ref, vb_ref, oa_ref, ob_ref):
    n = qa_ref.shape[1]
    for j in range(NA_WIDTH // LANES):
        sl = slice(j * LANES, (j + 1) * LANES)
        o = _softmax_attend(_stack_heads(qa_ref[0, :, sl]), ka_ref[0, :, sl], va_ref[0, :, sl])
        oa_ref[0, :, sl] = _unstack_heads(o, n).astype(BF16)
    per_kv = GQA_WIDTH // GQA_KV_HEADS // LANES
    for j in range(GQA_WIDTH // LANES):
        sl = slice(j * LANES, (j + 1) * LANES)
        g = j // per_kv
        kv = slice(g * LANES, (g + 1) * LANES)
        o = _softmax_attend(_stack_heads(qb_ref[0, :, sl]), kb_ref[0, :, kv], vb_ref[0, :, kv])
        ob_ref[0, :, sl] = _unstack_heads(o, n).astype(BF16)


def _context_attention(cqa, cka, cva, cqb, ckbd, cvbd):
    batch, n, _ = cqa.shape
    spec = lambda a: pl.BlockSpec((1,) + a.shape[1:], lambda b: (b, 0, 0))
    args = (cqa, cka, cva, cqb, ckbd, cvbd)
    return pl.pallas_call(
        _ctx_kernel,
        grid=(batch,),
        in_specs=[spec(a) for a in args],
        out_specs=[spec(cqa), spec(cqb)],
        out_shape=[jax.ShapeDtypeStruct(cqa.shape, BF16), jax.ShapeDtypeStruct(cqb.shape, BF16)],
        compiler_params=_params(("arbitrary",)),
        name="ctx",
    )(*args)


def _merge_kernel(a_ref, b_ref, za_ref, zb_ref, ga_ref, gb_ref, x_ref, gate_ref,
                  woa_ref, wob_ref, wout_ref, o_ref):
    o_a = _dot(a_ref[0] * za_ref[0], woa_ref[...])
    o_b = _dot(b_ref[0] * zb_ref[0], wob_ref[...])
    merged = ga_ref[0].astype(F32) * o_a + gb_ref[0].astype(F32) * o_b
    o_ref[0] = x_ref[0] + gate_ref[0] * _dot(merged.astype(BF16), wout_ref[...])


def _merge(a_att, b_att, za, zb, ga, gb, x, gate, w_o_a, w_o_b, w_out, rows):
    groups, total, _ = x.shape
    act = lambda w: pl.BlockSpec((1, rows, w), lambda g, i: (g, i, 0))
    const = lambda a: pl.BlockSpec(a.shape, lambda g, i: (0, 0))
    return pl.pallas_call(
        _merge_kernel,
        grid=(groups, total // rows),
        in_specs=[act(NA_WIDTH), act(GQA_WIDTH), act(NA_WIDTH), act(GQA_WIDTH), act(D_MODEL), act(D_MODEL),
                  act(D_MODEL), pl.BlockSpec((1, 1, D_MODEL), lambda g, i: (g, 0, 0)),
                  const(w_o_a), const(w_o_b), const(w_out)],
        out_specs=act(D_MODEL),
        out_shape=jax.ShapeDtypeStruct(x.shape, F32),
        compiler_params=_params(("arbitrary", "arbitrary")),
        name="merge",
    )(a_att, b_att, za, zb, ga, gb, x, gate, w_o_a, w_o_b, w_out)


def _pack_w_in(w):
    o = NA_WIDTH * 2
    dup = lambda c: jnp.concatenate(
        [c[:, h * HEAD_DIM:(h + 1) * HEAD_DIM] for h in range(GQA_KV_HEADS) for _ in range(2)], axis=1)
    b_k = w[:, o:o + GQA_KV_WIDTH]
    b_v = w[:, o + GQA_KV_WIDTH:o + 2 * GQA_KV_WIDTH]
    return jnp.concatenate([w[:, :o], dup(b_k), dup(b_v), w[:, o + 2 * GQA_KV_WIDTH:]], axis=1).astype(BF16)


def _rope_tables(seq):
    t = jnp.arange(seq, dtype=jnp.int32)
    inv = 1.0 / (ROPE_THETA ** (jnp.arange(ROT_HALF, dtype=F32) / ROT_HALF))
    ang_r = (t // GRID_W).astype(F32)[:, None] * inv[None, :]
    ang_c = (t % GRID_W).astype(F32)[:, None] * inv[None, :]
    cos = jnp.concatenate([jnp.cos(ang_r)] * 2 + [jnp.cos(ang_c)] * 2, axis=1)
    sin = jnp.concatenate([-jnp.sin(ang_r), jnp.sin(ang_r), -jnp.sin(ang_c), jnp.sin(ang_c)], axis=1)
    reps = LANES // HEAD_DIM
    return jnp.tile(cos, (1, reps)), jnp.tile(sin, (1, reps))


def _group_sum_matrix():
    idx = np.arange(MXU_COLS) // HEAD_DIM
    return jnp.asarray(idx[:, None] == idx[None, :], dtype=BF16)


def kernel(x, c, ctx, c_ctx, w_ada, b_ada, norm_g, w_in, q_norm_a, k_norm_a, q_norm_b, k_norm_b,
           rpb, w_o_a, w_o_b, w_out):
    batch, seq, _ = x.shape
    ctx_len = ctx.shape[1]
    depth = w_ada.shape[0]
    rows = seq // GRID_W
    assert seq % GRID_W == 0 and rows >= WIN_R and seq % PROJ_ROWS == 0 and seq % GQA_K_ROWS == 0

    pad = -(batch + 1) % 8
    c_rows = jnp.concatenate([c, c_ctx[None, :], jnp.zeros((pad, D_MODEL), F32)], axis=0)
    mod = _modulation(c_rows, w_ada, b_ada)

    cos_x, sin_x = _rope_tables(seq)
    cos_c = jnp.ones((ctx_len, LANES), F32)
    sin_c = jnp.zeros((ctx_len, LANES), F32)
    gsum = _group_sum_matrix()
    tables = _bias_tables(rpb, rows)

    for l in range(depth):
        update_ctx = l < depth - 1
        shift, scale, gate = (mod[l, :, i * D_MODEL:(i + 1) * D_MODEL] for i in range(3))
        per_batch = lambda m: m[:batch, None, :]
        for_ctx = lambda m: jnp.broadcast_to(m[batch][None, None, :], (batch, 1, D_MODEL))
        w_packed = _pack_w_in(w_in[l])
        tile = lambda g, n: jnp.tile(g, n)
        gains = jnp.stack([tile(k_norm_a[l], NA_HEADS), tile(k_norm_b[l], NA_HEADS),
                           tile(q_norm_a[l], NA_HEADS) * QK_SCALE,
                           tile(q_norm_b[l], GQA_Q_HEADS) * QK_SCALE])
        g_row = norm_g[l][None, :]

        ka, va, kbd, vbd, qa, qb, za, zb, ga, gb = _project(
            x, per_batch(shift), per_batch(scale), g_row, w_packed, gsum, gains, cos_x, sin_x,
            len(SEG_WIDTHS), PROJ_ROWS)
        nseg_c = len(SEG_WIDTHS) if update_ctx else KV_SEGS
        w_ctx = w_packed if update_ctx else w_packed[:, :SEG_OFFSETS[KV_SEGS]]
        pc = _project(ctx, for_ctx(shift), for_ctx(scale), g_row, w_ctx, gsum, gains, cos_c, sin_c,
                      nseg_c, ctx_len)
        cka, cva, ckbd, cvbd = pc[:KV_SEGS]

        a_att = _neighbourhood_attention(qa, ka, va, cka, cva, tables[l])
        b_att = _gqa_attention(qb, kbd, vbd, ckbd, cvbd)
        wa, wb, wo = w_o_a[l].astype(BF16), w_o_b[l].astype(BF16), w_out[l].astype(BF16)
        x_new = _merge(a_att, b_att, za, zb, ga, gb, x, per_batch(gate), wa, wb, wo, PROJ_ROWS)

        if update_ctx:
            _, _, _, _, cqa, cqb, cza, czb, cga, cgb = pc
            c_a, c_b = _context_attention(cqa, cka, cva, cqb, ckbd, cvbd)
            ctx = _merge(c_a, c_b, cza, czb, cga, cgb, ctx, for_ctx(gate), wa, wb, wo, ctx_len)
        x = x_new
    return x
```

```python
import functools

import numpy as np
import jax
import jax.numpy as jnp
from jax import lax
from jax.experimental import pallas as pl
from jax.experimental.pallas import tpu as pltpu

D_MODEL = 1024
GRID_W = 64
HEAD_DIM = 64
NA_HEADS = 8
NA_WIDTH = NA_HEADS * HEAD_DIM
WIN_R = 8
WIN_C = 16
GQA_Q_HEADS = 8
GQA_KV_HEADS = 2
GQA_WIDTH = GQA_Q_HEADS * HEAD_DIM
GQA_KV_WIDTH = GQA_KV_HEADS * HEAD_DIM
ROPE_THETA = 10000.0
ROT_AXIS = HEAD_DIM // 2
ROT_HALF = ROT_AXIS // 2
EPS = 1e-6
QK_SCALE = HEAD_DIM ** -0.5
LOG2E = 1.4426950408889634

LANES = 128
MXU_COLS = 256
MASKED = -1e30
VMEM_LIMIT = 56 * 1024 * 1024

PROJ_ROWS = 512
GQA_Q_ROWS = 128

F32 = jnp.float32
BF16 = jnp.bfloat16

SEG_KA, SEG_VA, SEG_KB, SEG_VB, SEG_QA, SEG_QB, SEG_ZA, SEG_ZB, SEG_GA, SEG_GB = range(10)
SEG_WIDTHS = (NA_WIDTH, NA_WIDTH, 2 * GQA_KV_WIDTH, GQA_KV_WIDTH, NA_WIDTH, GQA_WIDTH,
              NA_WIDTH, GQA_WIDTH, D_MODEL, D_MODEL)
SEG_OFFSETS = tuple(int(v) for v in np.cumsum((0,) + SEG_WIDTHS))
KV_SEGS = 4
SEG_GAIN_ROW = {SEG_KA: 0, SEG_KB: 1, SEG_QA: 2, SEG_QB: 3}
SEG_ROPE = (SEG_KB, SEG_QB)
SEG_TRANSPOSED = (SEG_VB,)


def _dot(a, b):
    return jnp.dot(a, b, preferred_element_type=F32)


def _dot_t(a, b):
    return lax.dot_general(a, b, (((1,), (1,)), ((), ())), preferred_element_type=F32)


def _params(semantics):
    return pltpu.CompilerParams(dimension_semantics=semantics, vmem_limit_bytes=VMEM_LIMIT)


def _mod_kernel(c_ref, w_ref, b_ref, o_ref):
    c = c_ref[...]
    a = c * jax.nn.sigmoid(c)
    o_ref[0] = jnp.dot(a, w_ref[0], preferred_element_type=F32,
                       precision=lax.Precision.HIGHEST) + b_ref[0]


def _modulation(c_rows, w_ada, b_ada):
    depth = w_ada.shape[0]
    rows = c_rows.shape[0]
    ncol = w_ada.shape[2] // D_MODEL
    return pl.pallas_call(
        _mod_kernel,
        grid=(depth, ncol),
        in_specs=[
            pl.BlockSpec((rows, D_MODEL), lambda l, j: (0, 0)),
            pl.BlockSpec((1, D_MODEL, D_MODEL), lambda l, j: (l, 0, j)),
            pl.BlockSpec((1, 1, D_MODEL), lambda l, j: (l, 0, j)),
        ],
        out_specs=pl.BlockSpec((1, rows, D_MODEL), lambda l, j: (l, 0, j)),
        out_shape=jax.ShapeDtypeStruct((depth, rows, ncol * D_MODEL), F32),
        compiler_params=_params(("arbitrary", "arbitrary")),
        name="mod",
    )(c_rows, w_ada, b_ada.reshape(depth, 1, -1))


def _head_norm(p, gsum, gain):
    sq = (p * p).astype(BF16)
    parts = []
    for j in range(p.shape[1] // MXU_COLS):
        sl = slice(j * MXU_COLS, (j + 1) * MXU_COLS)
        ss = _dot(sq[:, sl], gsum)
        parts.append(p[:, sl] * lax.rsqrt(ss * (1.0 / HEAD_DIM) + EPS))
    y = parts[0] if len(parts) == 1 else jnp.concatenate(parts, axis=1)
    return y * gain


def _rope(y, cos, sin):
    width = y.shape[1]
    lane = lax.broadcasted_iota(jnp.int32, y.shape, 1)
    ahead = pltpu.roll(y, width - ROT_HALF, axis=1)
    behind = pltpu.roll(y, ROT_HALF, axis=1)
    partner = jnp.where((lane % ROT_AXIS) < ROT_HALF, ahead, behind)
    reps = width // LANES
    cos = cos if reps == 1 else jnp.concatenate([cos] * reps, axis=1)
    sin = sin if reps == 1 else jnp.concatenate([sin] * reps, axis=1)
    return y * cos + partner * sin


def _proj_kernel(x_ref, shift_ref, scale_ref, g_ref, w_ref, gsum_ref, gain_ref, cos_ref, sin_ref,
                 *out_refs, nseg):
    x = x_ref[0]
    ms = jnp.mean(x * x, axis=-1, keepdims=True)
    gmod = g_ref[...] * (1.0 + scale_ref[0])
    h = (x * lax.rsqrt(ms + EPS) * gmod + shift_ref[0]).astype(BF16)
    gsum = gsum_ref[...]
    for seg in range(nseg):
        width = SEG_WIDTHS[seg]
        p = _dot(h, w_ref[:, SEG_OFFSETS[seg]:SEG_OFFSETS[seg + 1]])
        if seg in SEG_GAIN_ROW:
            row = SEG_GAIN_ROW[seg]
            p = _head_norm(p, gsum, gain_ref[row:row + 1, :width])
        if seg in SEG_ROPE:
            p = _rope(p, cos_ref[...], sin_ref[...])
        if seg in (SEG_ZA, SEG_ZB):
            p = p * jax.nn.sigmoid(p)
        if seg in (SEG_GA, SEG_GB):
            p = jax.nn.sigmoid(p)
        if seg in SEG_TRANSPOSED:
            out_refs[seg][0, 0] = p.T.astype(BF16)
        else:
            out_refs[seg][0] = p.astype(BF16)


def _project(x, shift, scale, norm_g, w_packed, gsum, gains, cos, sin, nseg, rows):
    groups, total, _ = x.shape
    ncols = SEG_OFFSETS[nseg]
    const = lambda g, i: (0, 0)
    return pl.pallas_call(
        functools.partial(_proj_kernel, nseg=nseg),
        grid=(groups, total // rows),
        in_specs=[
            pl.BlockSpec((1, rows, D_MODEL), lambda g, i: (g, i, 0)),
            pl.BlockSpec((1, 1, D_MODEL), lambda g, i: (g, 0, 0)),
            pl.BlockSpec((1, 1, D_MODEL), lambda g, i: (g, 0, 0)),
            pl.BlockSpec((1, D_MODEL), const),
            pl.BlockSpec((D_MODEL, ncols), const, pipeline_mode=pl.Buffered(1)),
            pl.BlockSpec((MXU_COLS, MXU_COLS), const),
            pl.BlockSpec(gains.shape, const),
            pl.BlockSpec((rows, LANES), lambda g, i: (i, 0)),
            pl.BlockSpec((rows, LANES), lambda g, i: (i, 0)),
        ],
        out_specs=[pl.BlockSpec((1, 1, SEG_WIDTHS[s], rows), lambda g, i: (g, i, 0, 0)) if s in SEG_TRANSPOSED
                   else pl.BlockSpec((1, rows, SEG_WIDTHS[s]), lambda g, i: (g, i, 0)) for s in range(nseg)],
        out_shape=[jax.ShapeDtypeStruct((groups, total // rows, SEG_WIDTHS[s], rows) if s in SEG_TRANSPOSED
                                        else (groups, total, SEG_WIDTHS[s]), BF16) for s in range(nseg)],
        compiler_params=_params(("arbitrary", "arbitrary")),
        name="proj",
    )(x, shift, scale, norm_g, w_packed, gsum, gains, cos, sin)


def _stack_heads(pair):
    lo = lax.broadcasted_iota(jnp.int32, pair.shape, 1) < HEAD_DIM
    zero = jnp.zeros_like(pair)
    return jnp.concatenate([jnp.where(lo, pair, zero), jnp.where(lo, zero, pair)], axis=0)


def _unstack_heads(o, rows):
    lo = lax.broadcasted_iota(jnp.int32, (rows, LANES), 1) < HEAD_DIM
    return jnp.where(lo, o[:rows], o[rows:])


def _na_kernel(q_ref, k_ref, v_ref, kc_ref, vc_ref, t_ref, o_ref, *, rows):
    kc = kc_ref[0]
    vc = vc_ref[0]
    win_keys = WIN_R * GRID_W

    def body(r, carry):
        start = jnp.clip(r - WIN_R // 2, 0, rows - WIN_R)
        q_off = pl.multiple_of(r * GRID_W, GRID_W)
        k_off = pl.multiple_of(start * GRID_W, GRID_W)
        qs = _stack_heads(q_ref[0, pl.ds(q_off, GRID_W), :])
        kw = k_ref[0, pl.ds(k_off, win_keys), :]
        vw = v_ref[0, pl.ds(k_off, win_keys), :]
        s = _dot_t(qs, kw) + t_ref[0, r - start]
        sc = _dot_t(qs, kc)
        m = jnp.maximum(jnp.max(s, axis=-1, keepdims=True), jnp.max(sc, axis=-1, keepdims=True))
        p = jnp.exp2(s - m)
        pc = jnp.exp2(sc - m)
        l = jnp.sum(p, axis=-1, keepdims=True) + jnp.sum(pc, axis=-1, keepdims=True)
        o = (_dot(p.astype(BF16), vw) + _dot(pc.astype(BF16), vc)) / l
        o_ref[0, pl.ds(q_off, GRID_W), :] = _unstack_heads(o, GRID_W).astype(BF16)
        return carry

    lax.fori_loop(0, rows, body, 0)


def _neighbourhood_attention(qa, ka, va, cka, cva, table):
    batch, seq, _ = qa.shape
    ctx_len = cka.shape[1]
    rows = seq // GRID_W
    pairs = NA_WIDTH // LANES
    blk = lambda n: pl.BlockSpec((1, n, LANES), lambda b, j: (b, 0, j))
    return pl.pallas_call(
        functools.partial(_na_kernel, rows=rows),
        grid=(batch, pairs),
        in_specs=[blk(seq), blk(seq), blk(seq), blk(ctx_len), blk(ctx_len),
                  pl.BlockSpec((1,) + table.shape[1:], lambda b, j: (j, 0, 0, 0))],
        out_specs=blk(seq),
        out_shape=jax.ShapeDtypeStruct((batch, seq, NA_WIDTH), BF16),
        compiler_params=_params(("arbitrary", "arbitrary")),
        name="na",
    )(qa, ka, va, cka, cva, table)


def _bias_kernel(rpb_ref, onehot_ref, mask_ref, o_ref):
    o_ref[0] = (jnp.dot(rpb_ref[0], onehot_ref[...], preferred_element_type=F32,
                        precision=lax.Precision.HIGHEST) + mask_ref[...]) * LOG2E


def _bias_tables(rpb, rows):
    depth = rpb.shape[0]
    wr = min(WIN_R, rows)
    n_row_off, n_col_off = 2 * WIN_R - 1, 2 * WIN_C - 1
    cols = np.arange(GRID_W)
    cstart = np.clip(cols - WIN_C // 2, 0, GRID_W - WIN_C)
    kcol = np.arange(GRID_W)
    valid = (kcol[None, :] >= cstart[:, None]) & (kcol[None, :] < cstart[:, None] + WIN_C)
    col_off = kcol[None, :] - cols[:, None] + (WIN_C - 1)
    pad_r, pad_c = -n_row_off % 8, -n_col_off % 8
    onehot = (np.arange(n_col_off + pad_c)[:, None, None] == col_off[None]) & valid[None]
    onehot = jnp.asarray(onehot.reshape(n_col_off + pad_c, GRID_W * GRID_W), dtype=F32)
    mask = jnp.asarray(np.where(valid, 0.0, MASKED).reshape(1, GRID_W * GRID_W), dtype=F32)
    heads = depth * NA_HEADS
    rpb_p = jnp.pad(rpb.reshape(heads, n_row_off, n_col_off), ((0, 0), (0, pad_r), (0, pad_c)))
    toep = pl.pallas_call(
        _bias_kernel,
        grid=(heads,),
        in_specs=[pl.BlockSpec((1,) + rpb_p.shape[1:], lambda h: (h, 0, 0)),
                  pl.BlockSpec(onehot.shape, lambda h: (0, 0)),
                  pl.BlockSpec(mask.shape, lambda h: (0, 0))],
        out_specs=pl.BlockSpec((1, n_row_off + pad_r, GRID_W * GRID_W), lambda h: (h, 0, 0)),
        out_shape=jax.ShapeDtypeStruct((heads, n_row_off + pad_r, GRID_W * GRID_W), F32),
        compiler_params=_params(("arbitrary",)),
        name="bias",
    )(rpb_p, onehot, mask)
    toep = toep.reshape(depth, NA_HEADS // 2, 2, n_row_off + pad_r, GRID_W, GRID_W)
    cases = [toep[:, :, :, WIN_R - 1 - case:WIN_R - 1 - case + wr] for case in range(wr)]
    table = jnp.stack(cases, axis=2)
    table = jnp.transpose(table, (0, 1, 2, 3, 5, 4, 6))
    return table.reshape(depth, NA_HEADS // 2, wr, 2 * GRID_W, wr * GRID_W)


def _stack_group(q):
    return jnp.concatenate([_stack_heads(q[:, :LANES]), _stack_heads(q[:, LANES:])], axis=0)


def _unstack_group_t(o, rows):
    pairs = [jnp.concatenate([o[:, (2 * j) * rows:(2 * j + 1) * rows],
                              o[:, (2 * j + 1) * rows:(2 * j + 2) * rows]], axis=0).T for j in range(2)]
    return jnp.concatenate(pairs, axis=1)


def _scores_t(qs, k):
    return _dot_t(k, qs)


def _attend_t(s, vt, state):
    m_chunk = jnp.max(s, axis=0, keepdims=True)
    if state is None:
        p = jnp.exp2(s - m_chunk)
        return m_chunk, jnp.sum(p, axis=0, keepdims=True), _dot(vt, p.astype(BF16))
    m_old, l, acc = state
    m = jnp.maximum(m_old, m_chunk)
    alpha = jnp.exp2(m_old - m)
    p = jnp.exp2(s - m)
    return m, alpha * l + jnp.sum(p, axis=0, keepdims=True), alpha * acc + _dot(vt, p.astype(BF16))


def _gqa_kernel(q_ref, k_ref, vt_ref, kc_ref, vct_ref, o_ref, *, chunks, tq, tk):
    qs = _stack_group(q_ref[0])
    s, vt, state = _scores_t(qs, kc_ref[0]), vct_ref[0, 0], None
    for c in range(chunks):
        s_next = _scores_t(qs, k_ref[0, c * tk:(c + 1) * tk, :])
        state = _attend_t(s, vt, state)
        s, vt = s_next, vt_ref[0, c]
    _, l, acc = _attend_t(s, vt, state)
    o_ref[0] = _unstack_group_t(acc / l, tq).astype(BF16)


def _gqa_attention(qb, kbd, vbt, ckbd, cvbt):
    batch, seq, _ = qb.shape
    ctx_len = ckbd.shape[1]
    tq, tk = GQA_Q_ROWS, vbt.shape[3]
    kspec = lambda n: pl.BlockSpec((1, n, LANES), lambda b, g, i: (b, 0, g))
    vspec = lambda a: pl.BlockSpec((1, a.shape[1], HEAD_DIM, a.shape[3]), lambda b, g, i: (b, 0, g, 0))
    qspec = pl.BlockSpec((1, tq, 2 * LANES), lambda b, g, i: (b, i, g))
    return pl.pallas_call(
        functools.partial(_gqa_kernel, chunks=seq // tk, tq=tq, tk=tk),
        grid=(batch, GQA_KV_HEADS, seq // tq),
        in_specs=[qspec, kspec(seq), vspec(vbt), kspec(ctx_len), vspec(cvbt)],
        out_specs=qspec,
        out_shape=jax.ShapeDtypeStruct((batch, seq, GQA_WIDTH), BF16),
        compiler_params=_params(("arbitrary", "arbitrary", "arbitrary")),
        name="gqa",
    )(qb, kbd, vbt, ckbd, cvbt)


def _softmax_attend(qs, k, v):
    s = _dot_t(qs, k)
    p = jnp.exp2(s - jnp.max(s, axis=-1, keepdims=True))
    return _dot(p.astype(BF16), v) / jnp.sum(p, axis=-1, keepdims=True)


def _ctx_kernel(qa_ref, ka_ref, va_ref, qb_ref, kb_ref, vbt_ref, oa_ref, ob_ref):
    n = qa_ref.shape[1]
    for j in range(NA_WIDTH // LANES):
        sl = slice(j * LANES, (j + 1) * LANES)
        o = _softmax_attend(_stack_heads(qa_ref[0, :, sl]), ka_ref[0, :, sl], va_ref[0, :, sl])
        oa_ref[0, :, sl] = _unstack_heads(o, n).astype(BF16)
    for g in range(GQA_KV_HEADS):
        sl = slice(g * 2 * LANES, (g + 1) * 2 * LANES)
        s = _scores_t(_stack_group(qb_ref[0, :, sl]), kb_ref[0, :, g * LANES:(g + 1) * LANES])
        _, l, acc = _attend_t(s, vbt_ref[0, 0, g * HEAD_DIM:(g + 1) * HEAD_DIM, :], None)
        ob_ref[0, :, sl] = _unstack_group_t(acc / l, n).astype(BF16)


def _context_attention(cqa, cka, cva, cqb, ckbd, cvbt):
    batch, n, _ = cqa.shape
    spec = lambda a: pl.BlockSpec((1,) + a.shape[1:], lambda b: (b,) + (0,) * (a.ndim - 1))
    args = (cqa, cka, cva, cqb, ckbd, cvbt)
    return pl.pallas_call(
        _ctx_kernel,
        grid=(batch,),
        in_specs=[spec(a) for a in args],
        out_specs=[spec(cqa), spec(cqb)],
        out_shape=[jax.ShapeDtypeStruct(cqa.shape, BF16), jax.ShapeDtypeStruct(cqb.shape, BF16)],
        compiler_params=_params(("arbitrary",)),
        name="ctx",
    )(*args)


def _merge_kernel(a_ref, b_ref, za_ref, zb_ref, ga_ref, gb_ref, x_ref, gate_ref,
                  woa_ref, wob_ref, wout_ref, o_ref):
    o_a = _dot(a_ref[0] * za_ref[0], woa_ref[...])
    o_b = _dot(b_ref[0] * zb_ref[0], wob_ref[...])
    merged = ga_ref[0].astype(F32) * o_a + gb_ref[0].astype(F32) * o_b
    o_ref[0] = x_ref[0] + gate_ref[0] * _dot(merged.astype(BF16), wout_ref[...])


def _merge(a_att, b_att, za, zb, ga, gb, x, gate, w_o_a, w_o_b, w_out, rows):
    groups, total, _ = x.shape
    act = lambda w: pl.BlockSpec((1, rows, w), lambda g, i: (g, i, 0))
    const = lambda a: pl.BlockSpec(a.shape, lambda g, i: (0, 0))
    return pl.pallas_call(
        _merge_kernel,
        grid=(groups, total // rows),
        in_specs=[act(NA_WIDTH), act(GQA_WIDTH), act(NA_WIDTH), act(GQA_WIDTH), act(D_MODEL), act(D_MODEL),
                  act(D_MODEL), pl.BlockSpec((1, 1, D_MODEL), lambda g, i: (g, 0, 0)),
                  const(w_o_a), const(w_o_b), const(w_out)],
        out_specs=act(D_MODEL),
        out_shape=jax.ShapeDtypeStruct(x.shape, F32),
        compiler_params=_params(("arbitrary", "arbitrary")),
        name="merge",
    )(a_att, b_att, za, zb, ga, gb, x, gate, w_o_a, w_o_b, w_out)


def _pack_w_in(w):
    o = NA_WIDTH * 2
    b_k = w[:, o:o + GQA_KV_WIDTH]
    dup = jnp.concatenate(
        [b_k[:, h * HEAD_DIM:(h + 1) * HEAD_DIM] for h in range(GQA_KV_HEADS) for _ in range(2)], axis=1)
    return jnp.concatenate([w[:, :o], dup, w[:, o + GQA_KV_WIDTH:]], axis=1).astype(BF16)


def _rope_tables(seq):
    t = jnp.arange(seq, dtype=jnp.int32)
    inv = 1.0 / (ROPE_THETA ** (jnp.arange(ROT_HALF, dtype=F32) / ROT_HALF))
    ang_r = (t // GRID_W).astype(F32)[:, None] * inv[None, :]
    ang_c = (t % GRID_W).astype(F32)[:, None] * inv[None, :]
    cos = jnp.concatenate([jnp.cos(ang_r)] * 2 + [jnp.cos(ang_c)] * 2, axis=1)
    sin = jnp.concatenate([-jnp.sin(ang_r), jnp.sin(ang_r), -jnp.sin(ang_c), jnp.sin(ang_c)], axis=1)
    reps = LANES // HEAD_DIM
    return jnp.tile(cos, (1, reps)), jnp.tile(sin, (1, reps))


def _group_sum_matrix():
    idx = np.arange(MXU_COLS) // HEAD_DIM
    return jnp.asarray(idx[:, None] == idx[None, :], dtype=BF16)


def kernel(x, c, ctx, c_ctx, w_ada, b_ada, norm_g, w_in, q_norm_a, k_norm_a, q_norm_b, k_norm_b,
           rpb, w_o_a, w_o_b, w_out):
    batch, seq, _ = x.shape
    ctx_len = ctx.shape[1]
    depth = w_ada.shape[0]
    rows = seq // GRID_W
    assert seq % GRID_W == 0 and rows >= WIN_R and seq % PROJ_ROWS == 0 and seq % GQA_Q_ROWS == 0

    pad = -(batch + 1) % 8
    c_rows = jnp.concatenate([c, c_ctx[None, :], jnp.zeros((pad, D_MODEL), F32)], axis=0)
    mod = _modulation(c_rows, w_ada, b_ada)

    cos_x, sin_x = _rope_tables(seq)
    cos_c = jnp.ones((ctx_len, LANES), F32)
    sin_c = jnp.zeros((ctx_len, LANES), F32)
    gsum = _group_sum_matrix()
    tables = _bias_tables(rpb, rows)

    for l in range(depth):
        update_ctx = l < depth - 1
        shift, scale, gate = (mod[l, :, i * D_MODEL:(i + 1) * D_MODEL] for i in range(3))
        per_batch = lambda m: m[:batch, None, :]
        for_ctx = lambda m: jnp.broadcast_to(m[batch][None, None, :], (batch, 1, D_MODEL))
        w_packed = _pack_w_in(w_in[l])
        tile = lambda g, n: jnp.tile(g, n)
        gains = jnp.stack([tile(k_norm_a[l], NA_HEADS), tile(k_norm_b[l], NA_HEADS),
                           tile(q_norm_a[l], NA_HEADS) * (QK_SCALE * LOG2E),
                           tile(q_norm_b[l], GQA_Q_HEADS) * (QK_SCALE * LOG2E)])
        g_row = norm_g[l][None, :]

        ka, va, kbd, vbt, qa, qb, za, zb, ga, gb = _project(
            x, per_batch(shift), per_batch(scale), g_row, w_packed, gsum, gains, cos_x, sin_x,
            len(SEG_WIDTHS), PROJ_ROWS)
        nseg_c = len(SEG_WIDTHS) if update_ctx else KV_SEGS
        w_ctx = w_packed if update_ctx else w_packed[:, :SEG_OFFSETS[KV_SEGS]]
        pc = _project(ctx, for_ctx(shift), for_ctx(scale), g_row, w_ctx, gsum, gains, cos_c, sin_c,
                      nseg_c, ctx_len)
        cka, cva, ckbd, cvbt = pc[:KV_SEGS]

        a_att = _neighbourhood_attention(qa, ka, va, cka, cva, tables[l])
        b_att = _gqa_attention(qb, kbd, vbt, ckbd, cvbt)
        wa, wb, wo = w_o_a[l].astype(BF16), w_o_b[l].astype(BF16), w_out[l].astype(BF16)
        x_new = _merge(a_att, b_att, za, zb, ga, gb, x, per_batch(gate), wa, wb, wo, PROJ_ROWS)

        if update_ctx:
            _, _, _, _, cqa, cqb, cza, czb, cga, cgb = pc
            c_a, c_b = _context_attention(cqa, cka, cva, cqb, ckbd, cvbt)
            ctx = _merge(c_a, c_b, cza, czb, cga, cgb, ctx, for_ctx(gate), wa, wb, wo, ctx_len)
        x = x_new
    return x
```

```python
import functools

import numpy as np
import jax
import jax.numpy as jnp
from jax import lax
from jax.experimental import pallas as pl
from jax.experimental.pallas import tpu as pltpu

D_MODEL = 1024
GRID_W = 64
HEAD_DIM = 64
NA_HEADS = 8
NA_WIDTH = NA_HEADS * HEAD_DIM
WIN_R = 8
WIN_C = 16
GQA_Q_HEADS = 8
GQA_KV_HEADS = 2
GQA_WIDTH = GQA_Q_HEADS * HEAD_DIM
GQA_KV_WIDTH = GQA_KV_HEADS * HEAD_DIM
ROPE_THETA = 10000.0
ROT_AXIS = HEAD_DIM // 2
ROT_HALF = ROT_AXIS // 2
EPS = 1e-6
QK_SCALE = HEAD_DIM ** -0.5
LOG2E = 1.4426950408889634

LANES = 128
MXU_COLS = 256
MASKED = -1e30
VMEM_LIMIT = 56 * 1024 * 1024

PROJ_ROWS = 512
GQA_Q_ROWS = 256

F32 = jnp.float32
BF16 = jnp.bfloat16

SEG_KA, SEG_VA, SEG_KB, SEG_VB, SEG_QA, SEG_QB, SEG_ZA, SEG_ZB, SEG_GA, SEG_GB = range(10)
SEG_WIDTHS = (NA_WIDTH, NA_WIDTH, 2 * GQA_KV_WIDTH, GQA_KV_WIDTH, NA_WIDTH, GQA_WIDTH,
              NA_WIDTH, GQA_WIDTH, D_MODEL, D_MODEL)
SEG_OFFSETS = tuple(int(v) for v in np.cumsum((0,) + SEG_WIDTHS))
KV_SEGS = 4
SEG_GAIN_ROW = {SEG_KA: 0, SEG_KB: 1, SEG_QA: 2, SEG_QB: 3}
SEG_ROPE = (SEG_KB, SEG_QB)
SEG_TRANSPOSED = (SEG_VB,)


def _dot(a, b):
    return jnp.dot(a, b, preferred_element_type=F32)


def _dot_t(a, b):
    return lax.dot_general(a, b, (((1,), (1,)), ((), ())), preferred_element_type=F32)


def _params(semantics):
    return pltpu.CompilerParams(dimension_semantics=semantics, vmem_limit_bytes=VMEM_LIMIT)


def _mod_kernel(c_ref, w_ref, b_ref, o_ref):
    c = c_ref[...]
    a = c * jax.nn.sigmoid(c)
    o_ref[0] = jnp.dot(a, w_ref[0], preferred_element_type=F32,
                       precision=lax.Precision.HIGHEST) + b_ref[0]


def _modulation(c_rows, w_ada, b_ada):
    depth = w_ada.shape[0]
    rows = c_rows.shape[0]
    ncol = w_ada.shape[2] // D_MODEL
    return pl.pallas_call(
        _mod_kernel,
        grid=(depth, ncol),
        in_specs=[
            pl.BlockSpec((rows, D_MODEL), lambda l, j: (0, 0)),
            pl.BlockSpec((1, D_MODEL, D_MODEL), lambda l, j: (l, 0, j)),
            pl.BlockSpec((1, 1, D_MODEL), lambda l, j: (l, 0, j)),
        ],
        out_specs=pl.BlockSpec((1, rows, D_MODEL), lambda l, j: (l, 0, j)),
        out_shape=jax.ShapeDtypeStruct((depth, rows, ncol * D_MODEL), F32),
        compiler_params=_params(("arbitrary", "arbitrary")),
        name="mod",
    )(c_rows, w_ada, b_ada.reshape(depth, 1, -1))


def _head_norm(p, gsum, gain):
    sq = (p * p).astype(BF16)
    parts = []
    for j in range(p.shape[1] // MXU_COLS):
        sl = slice(j * MXU_COLS, (j + 1) * MXU_COLS)
        ss = _dot(sq[:, sl], gsum)
        parts.append(p[:, sl] * lax.rsqrt(ss * (1.0 / HEAD_DIM) + EPS))
    y = parts[0] if len(parts) == 1 else jnp.concatenate(parts, axis=1)
    return y * gain


def _rope(y, cos, sin):
    width = y.shape[1]
    lane = lax.broadcasted_iota(jnp.int32, y.shape, 1)
    ahead = pltpu.roll(y, width - ROT_HALF, axis=1)
    behind = pltpu.roll(y, ROT_HALF, axis=1)
    partner = jnp.where((lane % ROT_AXIS) < ROT_HALF, ahead, behind)
    reps = width // LANES
    cos = cos if reps == 1 else jnp.concatenate([cos] * reps, axis=1)
    sin = sin if reps == 1 else jnp.concatenate([sin] * reps, axis=1)
    return y * cos + partner * sin


def _proj_kernel(x_ref, shift_ref, scale_ref, g_ref, w_ref, gsum_ref, gain_ref, cos_ref, sin_ref,
                 *out_refs, nseg):
    x = x_ref[0]
    ms = jnp.mean(x * x, axis=-1, keepdims=True)
    gmod = g_ref[...] * (1.0 + scale_ref[0])
    h = (x * lax.rsqrt(ms + EPS) * gmod + shift_ref[0]).astype(BF16)
    gsum = gsum_ref[...]
    for seg in range(nseg):
        width = SEG_WIDTHS[seg]
        p = _dot(h, w_ref[:, SEG_OFFSETS[seg]:SEG_OFFSETS[seg + 1]])
        if seg in SEG_GAIN_ROW:
            row = SEG_GAIN_ROW[seg]
            p = _head_norm(p, gsum, gain_ref[row:row + 1, :width])
        if seg in SEG_ROPE:
            p = _rope(p, cos_ref[...], sin_ref[...])
        if seg in (SEG_ZA, SEG_ZB):
            p = p * jax.nn.sigmoid(p)
        if seg in (SEG_GA, SEG_GB):
            p = jax.nn.sigmoid(p)
        if seg in SEG_TRANSPOSED:
            out_refs[seg][0, 0] = p.T.astype(BF16)
        else:
            out_refs[seg][0] = p.astype(BF16)


def _project(x, shift, scale, norm_g, w_packed, gsum, gains, cos, sin, nseg, rows):
    groups, total, _ = x.shape
    ncols = SEG_OFFSETS[nseg]
    const = lambda g, i: (0, 0)
    return pl.pallas_call(
        functools.partial(_proj_kernel, nseg=nseg),
        grid=(groups, total // rows),
        in_specs=[
            pl.BlockSpec((1, rows, D_MODEL), lambda g, i: (g, i, 0)),
            pl.BlockSpec((1, 1, D_MODEL), lambda g, i: (g, 0, 0)),
            pl.BlockSpec((1, 1, D_MODEL), lambda g, i: (g, 0, 0)),
            pl.BlockSpec((1, D_MODEL), const),
            pl.BlockSpec((D_MODEL, ncols), const, pipeline_mode=pl.Buffered(1)),
            pl.BlockSpec((MXU_COLS, MXU_COLS), const),
            pl.BlockSpec(gains.shape, const),
            pl.BlockSpec((rows, LANES), lambda g, i: (i, 0)),
            pl.BlockSpec((rows, LANES), lambda g, i: (i, 0)),
        ],
        out_specs=[pl.BlockSpec((1, 1, SEG_WIDTHS[s], rows), lambda g, i: (g, i, 0, 0)) if s in SEG_TRANSPOSED
                   else pl.BlockSpec((1, rows, SEG_WIDTHS[s]), lambda g, i: (g, i, 0)) for s in range(nseg)],
        out_shape=[jax.ShapeDtypeStruct((groups, total // rows, SEG_WIDTHS[s], rows) if s in SEG_TRANSPOSED
                                        else (groups, total, SEG_WIDTHS[s]), BF16) for s in range(nseg)],
        compiler_params=_params(("arbitrary", "arbitrary")),
        name="proj",
    )(x, shift, scale, norm_g, w_packed, gsum, gains, cos, sin)


def _stack_heads(pair):
    lo = lax.broadcasted_iota(jnp.int32, pair.shape, 1) < HEAD_DIM
    zero = jnp.zeros_like(pair)
    return jnp.concatenate([jnp.where(lo, pair, zero), jnp.where(lo, zero, pair)], axis=0)


def _unstack_heads(o, rows):
    lo = lax.broadcasted_iota(jnp.int32, (rows, LANES), 1) < HEAD_DIM
    return jnp.where(lo, o[:rows], o[rows:])


def _na_kernel(q_ref, k_ref, v_ref, kc_ref, vc_ref, t_ref, o_ref, *, rows):
    kc = kc_ref[0]
    vc = vc_ref[0]
    win_keys = WIN_R * GRID_W

    def body(r, carry):
        start = jnp.clip(r - WIN_R // 2, 0, rows - WIN_R)
        q_off = pl.multiple_of(r * GRID_W, GRID_W)
        k_off = pl.multiple_of(start * GRID_W, GRID_W)
        qs = _stack_heads(q_ref[0, pl.ds(q_off, GRID_W), :])
        kw = k_ref[0, pl.ds(k_off, win_keys), :]
        vw = v_ref[0, pl.ds(k_off, win_keys), :]
        s = _dot_t(qs, kw) + t_ref[0, r - start]
        sc = _dot_t(qs, kc)
        m = jnp.maximum(jnp.max(s, axis=-1, keepdims=True), jnp.max(sc, axis=-1, keepdims=True))
        p = jnp.exp2(s - m)
        pc = jnp.exp2(sc - m)
        l = jnp.sum(p, axis=-1, keepdims=True) + jnp.sum(pc, axis=-1, keepdims=True)
        o = (_dot(p.astype(BF16), vw) + _dot(pc.astype(BF16), vc)) / l
        o_ref[0, pl.ds(q_off, GRID_W), :] = _unstack_heads(o, GRID_W).astype(BF16)
        return carry

    lax.fori_loop(0, rows, body, 0)


def _neighbourhood_attention(qa, ka, va, cka, cva, table):
    batch, seq, _ = qa.shape
    ctx_len = cka.shape[1]
    rows = seq // GRID_W
    pairs = NA_WIDTH // LANES
    blk = lambda n: pl.BlockSpec((1, n, LANES), lambda b, j: (b, 0, j))
    return pl.pallas_call(
        functools.partial(_na_kernel, rows=rows),
        grid=(batch, pairs),
        in_specs=[blk(seq), blk(seq), blk(seq), blk(ctx_len), blk(ctx_len),
                  pl.BlockSpec((1,) + table.shape[1:], lambda b, j: (j, 0, 0, 0))],
        out_specs=blk(seq),
        out_shape=jax.ShapeDtypeStruct((batch, seq, NA_WIDTH), BF16),
        compiler_params=_params(("arbitrary", "arbitrary")),
        name="na",
    )(qa, ka, va, cka, cva, table)


def _bias_kernel(rpb_ref, onehot_ref, mask_ref, o_ref):
    o_ref[0] = (jnp.dot(rpb_ref[0], onehot_ref[...], preferred_element_type=F32,
                        precision=lax.Precision.HIGHEST) + mask_ref[...]) * LOG2E


def _bias_tables(rpb, rows):
    depth = rpb.shape[0]
    wr = min(WIN_R, rows)
    n_row_off, n_col_off = 2 * WIN_R - 1, 2 * WIN_C - 1
    cols = np.arange(GRID_W)
    cstart = np.clip(cols - WIN_C // 2, 0, GRID_W - WIN_C)
    kcol = np.arange(GRID_W)
    valid = (kcol[None, :] >= cstart[:, None]) & (kcol[None, :] < cstart[:, None] + WIN_C)
    col_off = kcol[None, :] - cols[:, None] + (WIN_C - 1)
    pad_r, pad_c = -n_row_off % 8, -n_col_off % 8
    onehot = (np.arange(n_col_off + pad_c)[:, None, None] == col_off[None]) & valid[None]
    onehot = jnp.asarray(onehot.reshape(n_col_off + pad_c, GRID_W * GRID_W), dtype=F32)
    mask = jnp.asarray(np.where(valid, 0.0, MASKED).reshape(1, GRID_W * GRID_W), dtype=F32)
    heads = depth * NA_HEADS
    rpb_p = jnp.pad(rpb.reshape(heads, n_row_off, n_col_off), ((0, 0), (0, pad_r), (0, pad_c)))
    toep = pl.pallas_call(
        _bias_kernel,
        grid=(heads,),
        in_specs=[pl.BlockSpec((1,) + rpb_p.shape[1:], lambda h: (h, 0, 0)),
                  pl.BlockSpec(onehot.shape, lambda h: (0, 0)),
                  pl.BlockSpec(mask.shape, lambda h: (0, 0))],
        out_specs=pl.BlockSpec((1, n_row_off + pad_r, GRID_W * GRID_W), lambda h: (h, 0, 0)),
        out_shape=jax.ShapeDtypeStruct((heads, n_row_off + pad_r, GRID_W * GRID_W), F32),
        compiler_params=_params(("arbitrary",)),
        name="bias",
    )(rpb_p, onehot, mask)
    toep = toep.reshape(depth, NA_HEADS // 2, 2, n_row_off + pad_r, GRID_W, GRID_W)
    cases = [toep[:, :, :, WIN_R - 1 - case:WIN_R - 1 - case + wr] for case in range(wr)]
    table = jnp.stack(cases, axis=2)
    table = jnp.transpose(table, (0, 1, 2, 3, 5, 4, 6))
    return table.reshape(depth, NA_HEADS // 2, wr, 2 * GRID_W, wr * GRID_W)


def _stack_group(q):
    return jnp.concatenate([_stack_heads(q[:, :LANES]), _stack_heads(q[:, LANES:])], axis=0)


def _unstack_group_t(o, rows):
    pairs = [jnp.concatenate([o[:, (2 * j) * rows:(2 * j + 1) * rows],
                              o[:, (2 * j + 1) * rows:(2 * j + 2) * rows]], axis=0).T for j in range(2)]
    return jnp.concatenate(pairs, axis=1)


def _scores_t(qs, k):
    return _dot_t(k, qs)


def _attend_t(s, vt, state):
    m_chunk = jnp.max(s, axis=0, keepdims=True)
    if state is None:
        p = jnp.exp2(s - m_chunk)
        return m_chunk, jnp.sum(p, axis=0, keepdims=True), _dot(vt, p.astype(BF16))
    m_old, l, acc = state
    m = jnp.maximum(m_old, m_chunk)
    alpha = jnp.exp2(m_old - m)
    p = jnp.exp2(s - m)
    return m, alpha * l + jnp.sum(p, axis=0, keepdims=True), alpha * acc + _dot(vt, p.astype(BF16))


def _gqa_kernel(q_ref, k_ref, vt_ref, kc_ref, vct_ref, o_ref, *, chunks, tq, tk):
    qs = _stack_group(q_ref[0])
    s, vt, state = _scores_t(qs, kc_ref[0]), vct_ref[0, 0], None
    for c in range(chunks):
        s_next = _scores_t(qs, k_ref[0, c * tk:(c + 1) * tk, :])
        state = _attend_t(s, vt, state)
        s, vt = s_next, vt_ref[0, c]
    _, l, acc = _attend_t(s, vt, state)
    o_ref[0] = _unstack_group_t(acc / l, tq).astype(BF16)


def _gqa_attention(qb, kbd, vbt, ckbd, cvbt):
    batch, seq, _ = qb.shape
    ctx_len = ckbd.shape[1]
    tq, tk = GQA_Q_ROWS, vbt.shape[3]
    kspec = lambda n: pl.BlockSpec((1, n, LANES), lambda b, g, i: (b, 0, g))
    vspec = lambda a: pl.BlockSpec((1, a.shape[1], HEAD_DIM, a.shape[3]), lambda b, g, i: (b, 0, g, 0))
    qspec = pl.BlockSpec((1, tq, 2 * LANES), lambda b, g, i: (b, i, g))
    return pl.pallas_call(
        functools.partial(_gqa_kernel, chunks=seq // tk, tq=tq, tk=tk),
        grid=(batch, GQA_KV_HEADS, seq // tq),
        in_specs=[qspec, kspec(seq), vspec(vbt), kspec(ctx_len), vspec(cvbt)],
        out_specs=qspec,
        out_shape=jax.ShapeDtypeStruct((batch, seq, GQA_WIDTH), BF16),
        compiler_params=_params(("arbitrary", "arbitrary", "arbitrary")),
        name="gqa",
    )(qb, kbd, vbt, ckbd, cvbt)


def _softmax_attend(qs, k, v):
    s = _dot_t(qs, k)
    p = jnp.exp2(s - jnp.max(s, axis=-1, keepdims=True))
    return _dot(p.astype(BF16), v) / jnp.sum(p, axis=-1, keepdims=True)


def _ctx_kernel(qa_ref, ka_ref, va_ref, qb_ref, kb_ref, vbt_ref, oa_ref, ob_ref):
    n = qa_ref.shape[1]
    for j in range(NA_WIDTH // LANES):
        sl = slice(j * LANES, (j + 1) * LANES)
        o = _softmax_attend(_stack_heads(qa_ref[0, :, sl]), ka_ref[0, :, sl], va_ref[0, :, sl])
        oa_ref[0, :, sl] = _unstack_heads(o, n).astype(BF16)
    for g in range(GQA_KV_HEADS):
        sl = slice(g * 2 * LANES, (g + 1) * 2 * LANES)
        s = _scores_t(_stack_group(qb_ref[0, :, sl]), kb_ref[0, :, g * LANES:(g + 1) * LANES])
        _, l, acc = _attend_t(s, vbt_ref[0, 0, g * HEAD_DIM:(g + 1) * HEAD_DIM, :], None)
        ob_ref[0, :, sl] = _unstack_group_t(acc / l, n).astype(BF16)


def _context_attention(cqa, cka, cva, cqb, ckbd, cvbt):
    batch, n, _ = cqa.shape
    spec = lambda a: pl.BlockSpec((1,) + a.shape[1:], lambda b: (b,) + (0,) * (a.ndim - 1))
    args = (cqa, cka, cva, cqb, ckbd, cvbt)
    return pl.pallas_call(
        _ctx_kernel,
        grid=(batch,),
        in_specs=[spec(a) for a in args],
        out_specs=[spec(cqa), spec(cqb)],
        out_shape=[jax.ShapeDtypeStruct(cqa.shape, BF16), jax.ShapeDtypeStruct(cqb.shape, BF16)],
        compiler_params=_params(("arbitrary",)),
        name="ctx",
    )(*args)


def _merge_kernel(a_ref, b_ref, za_ref, zb_ref, ga_ref, gb_ref, x_ref, gate_ref,
                  woa_ref, wob_ref, wout_ref, o_ref):
    o_a = _dot(a_ref[0] * za_ref[0], woa_ref[...])
    o_b = _dot(b_ref[0] * zb_ref[0], wob_ref[...])
    merged = ga_ref[0].astype(F32) * o_a + gb_ref[0].astype(F32) * o_b
    o_ref[0] = x_ref[0] + gate_ref[0] * _dot(merged.astype(BF16), wout_ref[...])


def _merge(a_att, b_att, za, zb, ga, gb, x, gate, w_o_a, w_o_b, w_out, rows):
    groups, total, _ = x.shape
    act = lambda w: pl.BlockSpec((1, rows, w), lambda g, i: (g, i, 0))
    const = lambda a: pl.BlockSpec(a.shape, lambda g, i: (0, 0))
    return pl.pallas_call(
        _merge_kernel,
        grid=(groups, total // rows),
        in_specs=[act(NA_WIDTH), act(GQA_WIDTH), act(NA_WIDTH), act(GQA_WIDTH), act(D_MODEL), act(D_MODEL),
                  act(D_MODEL), pl.BlockSpec((1, 1, D_MODEL), lambda g, i: (g, 0, 0)),
                  const(w_o_a), const(w_o_b), const(w_out)],
        out_specs=act(D_MODEL),
        out_shape=jax.ShapeDtypeStruct(x.shape, F32),
        compiler_params=_params(("arbitrary", "arbitrary")),
        name="merge",
    )(a_att, b_att, za, zb, ga, gb, x, gate, w_o_a, w_o_b, w_out)


def _pack_w_in(w):
    o = NA_WIDTH * 2
    b_k = w[:, o:o + GQA_KV_WIDTH]
    dup = jnp.concatenate(
        [b_k[:, h * HEAD_DIM:(h + 1) * HEAD_DIM] for h in range(GQA_KV_HEADS) for _ in range(2)], axis=1)
    return jnp.concatenate([w[:, :o], dup, w[:, o + GQA_KV_WIDTH:]], axis=1).astype(BF16)


def _rope_tables(seq):
    t = jnp.arange(seq, dtype=jnp.int32)
    inv = 1.0 / (ROPE_THETA ** (jnp.arange(ROT_HALF, dtype=F32) / ROT_HALF))
    ang_r = (t // GRID_W).astype(F32)[:, None] * inv[None, :]
    ang_c = (t % GRID_W).astype(F32)[:, None] * inv[None, :]
    cos = jnp.concatenate([jnp.cos(ang_r)] * 2 + [jnp.cos(ang_c)] * 2, axis=1)
    sin = jnp.concatenate([-jnp.sin(ang_r), jnp.sin(ang_r), -jnp.sin(ang_c), jnp.sin(ang_c)], axis=1)
    reps = LANES // HEAD_DIM
    return jnp.tile(cos, (1, reps)), jnp.tile(sin, (1, reps))


def _group_sum_matrix():
    idx = np.arange(MXU_COLS) // HEAD_DIM
    return jnp.asarray(idx[:, None] == idx[None, :], dtype=BF16)


def kernel(x, c, ctx, c_ctx, w_ada, b_ada, norm_g, w_in, q_norm_a, k_norm_a, q_norm_b, k_norm_b,
           rpb, w_o_a, w_o_b, w_out):
    batch, seq, _ = x.shape
    ctx_len = ctx.shape[1]
    depth = w_ada.shape[0]
    rows = seq // GRID_W
    assert seq % GRID_W == 0 and rows >= WIN_R and seq % PROJ_ROWS == 0 and seq % GQA_Q_ROWS == 0

    pad = -(batch + 1) % 8
    c_rows = jnp.concatenate([c, c_ctx[None, :], jnp.zeros((pad, D_MODEL), F32)], axis=0)
    mod = _modulation(c_rows, w_ada, b_ada)

    cos_x, sin_x = _rope_tables(seq)
    cos_c = jnp.ones((ctx_len, LANES), F32)
    sin_c = jnp.zeros((ctx_len, LANES), F32)
    gsum = _group_sum_matrix()
    tables = _bias_tables(rpb, rows)

    for l in range(depth):
        update_ctx = l < depth - 1
        shift, scale, gate = (mod[l, :, i * D_MODEL:(i + 1) * D_MODEL] for i in range(3))
        per_batch = lambda m: m[:batch, None, :]
        for_ctx = lambda m: jnp.broadcast_to(m[batch][None, None, :], (batch, 1, D_MODEL))
        w_packed = _pack_w_in(w_in[l])
        tile = lambda g, n: jnp.tile(g, n)
        gains = jnp.stack([tile(k_norm_a[l], NA_HEADS), tile(k_norm_b[l], NA_HEADS),
                           tile(q_norm_a[l], NA_HEADS) * (QK_SCALE * LOG2E),
                           tile(q_norm_b[l], GQA_Q_HEADS) * (QK_SCALE * LOG2E)])
        g_row = norm_g[l][None, :]

        ka, va, kbd, vbt, qa, qb, za, zb, ga, gb = _project(
            x, per_batch(shift), per_batch(scale), g_row, w_packed, gsum, gains, cos_x, sin_x,
            len(SEG_WIDTHS), PROJ_ROWS)
        nseg_c = len(SEG_WIDTHS) if update_ctx else KV_SEGS
        w_ctx = w_packed if update_ctx else w_packed[:, :SEG_OFFSETS[KV_SEGS]]
        pc = _project(ctx, for_ctx(shift), for_ctx(scale), g_row, w_ctx, gsum, gains, cos_c, sin_c,
                      nseg_c, ctx_len)
        cka, cva, ckbd, cvbt = pc[:KV_SEGS]

        a_att = _neighbourhood_attention(qa, ka, va, cka, cva, tables[l])
        b_att = _gqa_attention(qb, kbd, vbt, ckbd, cvbt)
        wa, wb, wo = w_o_a[l].astype(BF16), w_o_b[l].astype(BF16), w_out[l].astype(BF16)
        x_new = _merge(a_att, b_att, za, zb, ga, gb, x, per_batch(gate), wa, wb, wo, PROJ_ROWS)

        if update_ctx:
            _, _, _, _, cqa, cqb, cza, czb, cga, cgb = pc
            c_a, c_b = _context_attention(cqa, cka, cva, cqb, ckbd, cvbt)
            ctx = _merge(c_a, c_b, cza, czb, cga, cgb, ctx, for_ctx(gate), wa, wb, wo, ctx_len)
        x = x_new
    return x
```

```python
import functools

import numpy as np
import jax
import jax.numpy as jnp
from jax import lax
from jax.experimental import pallas as pl
from jax.experimental.pallas import tpu as pltpu

D_MODEL = 1024
GRID_W = 64
HEAD_DIM = 64
NA_HEADS = 8
NA_WIDTH = NA_HEADS * HEAD_DIM
WIN_R = 8
WIN_C = 16
GQA_Q_HEADS = 8
GQA_KV_HEADS = 2
GQA_WIDTH = GQA_Q_HEADS * HEAD_DIM
GQA_KV_WIDTH = GQA_KV_HEADS * HEAD_DIM
ROPE_THETA = 10000.0
ROT_AXIS = HEAD_DIM // 2
ROT_HALF = ROT_AXIS // 2
EPS = 1e-6
QK_SCALE = HEAD_DIM ** -0.5
LOG2E = 1.4426950408889634

LANES = 128
MXU_COLS = 256
MASKED = -1e30
VMEM_LIMIT = 56 * 1024 * 1024

PROJ_ROWS = 512
GQA_Q_ROWS = 256
QUAD = 4
NA_SLABS = WIN_R // 2 + 1
NA_UNROLL = 8

F32 = jnp.float32
BF16 = jnp.bfloat16

SEG_KA, SEG_VA, SEG_KB, SEG_VB, SEG_QA, SEG_QB, SEG_ZA, SEG_ZB, SEG_GA, SEG_GB = range(10)
SEG_WIDTHS = (NA_WIDTH, NA_WIDTH, 2 * GQA_KV_WIDTH, GQA_KV_WIDTH, NA_WIDTH, GQA_WIDTH,
              NA_WIDTH, GQA_WIDTH, D_MODEL, D_MODEL)
SEG_OFFSETS = tuple(int(v) for v in np.cumsum((0,) + SEG_WIDTHS))
KV_SEGS = 4
SEG_GAIN_ROW = {SEG_KA: 0, SEG_KB: 1, SEG_QA: 2, SEG_QB: 3}
SEG_ROPE = (SEG_KB, SEG_QB)
SEG_T_SLAB = {SEG_VA: LANES, SEG_VB: None}


def _dot(a, b):
    return jnp.dot(a, b, preferred_element_type=F32)


def _dot_t(a, b):
    return lax.dot_general(a, b, (((1,), (1,)), ((), ())), preferred_element_type=F32)


def _params(semantics):
    return pltpu.CompilerParams(dimension_semantics=semantics, vmem_limit_bytes=VMEM_LIMIT)


def _mod_kernel(c_ref, w_ref, b_ref, o_ref):
    c = c_ref[...]
    a = c * jax.nn.sigmoid(c)
    o_ref[0] = jnp.dot(a, w_ref[0], preferred_element_type=F32,
                       precision=lax.Precision.HIGHEST) + b_ref[0]


def _modulation(c_rows, w_ada, b_ada):
    depth = w_ada.shape[0]
    rows = c_rows.shape[0]
    ncol = w_ada.shape[2] // D_MODEL
    return pl.pallas_call(
        _mod_kernel,
        grid=(depth, ncol),
        in_specs=[
            pl.BlockSpec((rows, D_MODEL), lambda l, j: (0, 0)),
            pl.BlockSpec((1, D_MODEL, D_MODEL), lambda l, j: (l, 0, j)),
            pl.BlockSpec((1, 1, D_MODEL), lambda l, j: (l, 0, j)),
        ],
        out_specs=pl.BlockSpec((1, rows, D_MODEL), lambda l, j: (l, 0, j)),
        out_shape=jax.ShapeDtypeStruct((depth, rows, ncol * D_MODEL), F32),
        compiler_params=_params(("arbitrary", "arbitrary")),
        name="mod",
    )(c_rows, w_ada, b_ada.reshape(depth, 1, -1))


def _head_norm(p, gsum, gain):
    sq = (p * p).astype(BF16)
    parts = []
    for j in range(p.shape[1] // MXU_COLS):
        sl = slice(j * MXU_COLS, (j + 1) * MXU_COLS)
        ss = _dot(sq[:, sl], gsum)
        parts.append(p[:, sl] * lax.rsqrt(ss * (1.0 / HEAD_DIM) + EPS))
    y = parts[0] if len(parts) == 1 else jnp.concatenate(parts, axis=1)
    return y * gain


def _rope(y, cos, sin):
    width = y.shape[1]
    lane = lax.broadcasted_iota(jnp.int32, y.shape, 1)
    ahead = pltpu.roll(y, width - ROT_HALF, axis=1)
    behind = pltpu.roll(y, ROT_HALF, axis=1)
    partner = jnp.where((lane % ROT_AXIS) < ROT_HALF, ahead, behind)
    reps = width // LANES
    cos = cos if reps == 1 else jnp.concatenate([cos] * reps, axis=1)
    sin = sin if reps == 1 else jnp.concatenate([sin] * reps, axis=1)
    return y * cos + partner * sin


def _proj_kernel(x_ref, shift_ref, scale_ref, g_ref, w_ref, gsum_ref, gain_ref, cos_ref, sin_ref,
                 *out_refs, nseg):
    x = x_ref[0]
    ms = jnp.mean(x * x, axis=-1, keepdims=True)
    gmod = g_ref[...] * (1.0 + scale_ref[0])
    h = (x * lax.rsqrt(ms + EPS) * gmod + shift_ref[0]).astype(BF16)
    gsum = gsum_ref[...]
    for seg in range(nseg):
        width = SEG_WIDTHS[seg]
        p = _dot(h, w_ref[:, SEG_OFFSETS[seg]:SEG_OFFSETS[seg + 1]])
        if seg in SEG_GAIN_ROW:
            row = SEG_GAIN_ROW[seg]
            p = _head_norm(p, gsum, gain_ref[row:row + 1, :width])
        if seg in SEG_ROPE:
            p = _rope(p, cos_ref[...], sin_ref[...])
        if seg in (SEG_ZA, SEG_ZB):
            p = p * jax.nn.sigmoid(p)
        if seg in (SEG_GA, SEG_GB):
            p = jax.nn.sigmoid(p)
        if seg in SEG_T_SLAB:
            pt = p.T.astype(BF16)
            slab = SEG_T_SLAB[seg] or pt.shape[1]
            for j in range(pt.shape[1] // slab):
                out_refs[seg][0, j] = pt[:, j * slab:(j + 1) * slab]
        else:
            out_refs[seg][0] = p.astype(BF16)


def _project(x, shift, scale, norm_g, w_packed, gsum, gains, cos, sin, nseg, rows):
    groups, total, _ = x.shape
    ncols = SEG_OFFSETS[nseg]
    const = lambda g, i: (0, 0)

    def out_layout(s):
        if s not in SEG_T_SLAB:
            return ((groups, total, SEG_WIDTHS[s]),
                    pl.BlockSpec((1, rows, SEG_WIDTHS[s]), lambda g, i: (g, i, 0)))
        slab = SEG_T_SLAB[s] or rows
        return ((groups, total // slab, SEG_WIDTHS[s], slab),
                pl.BlockSpec((1, rows // slab, SEG_WIDTHS[s], slab), lambda g, i: (g, i, 0, 0)))

    layouts = [out_layout(s) for s in range(nseg)]
    return pl.pallas_call(
        functools.partial(_proj_kernel, nseg=nseg),
        grid=(groups, total // rows),
        in_specs=[
            pl.BlockSpec((1, rows, D_MODEL), lambda g, i: (g, i, 0)),
            pl.BlockSpec((1, 1, D_MODEL), lambda g, i: (g, 0, 0)),
            pl.BlockSpec((1, 1, D_MODEL), lambda g, i: (g, 0, 0)),
            pl.BlockSpec((1, D_MODEL), const),
            pl.BlockSpec((D_MODEL, ncols), const, pipeline_mode=pl.Buffered(1)),
            pl.BlockSpec((MXU_COLS, MXU_COLS), const),
            pl.BlockSpec(gains.shape, const),
            pl.BlockSpec((rows, LANES), lambda g, i: (i, 0)),
            pl.BlockSpec((rows, LANES), lambda g, i: (i, 0)),
        ],
        out_specs=[spec for _, spec in layouts],
        out_shape=[jax.ShapeDtypeStruct(shape, BF16) for shape, _ in layouts],
        compiler_params=_params(("arbitrary", "arbitrary")),
        name="proj",
    )(x, shift, scale, norm_g, w_packed, gsum, gains, cos, sin)


def _stack_heads(pair):
    lo = lax.broadcasted_iota(jnp.int32, pair.shape, 1) < HEAD_DIM
    zero = jnp.zeros_like(pair)
    return jnp.concatenate([jnp.where(lo, pair, zero), jnp.where(lo, zero, pair)], axis=0)


def _stack_quad(q):
    head = lax.broadcasted_iota(jnp.int32, q.shape, 1) // HEAD_DIM
    zero = jnp.zeros_like(q)
    return jnp.concatenate([jnp.where(head == h, q, zero) for h in range(QUAD)], axis=0)


def _unstack_quad_t(o, rows):
    ot = o.T
    head = lax.broadcasted_iota(jnp.int32, (rows, QUAD * HEAD_DIM), 1) // HEAD_DIM
    out = ot[:rows]
    for h in range(1, QUAD):
        out = jnp.where(head == h, ot[h * rows:(h + 1) * rows], out)
    return out


def _softmax_pv_t(logits, values_t):
    m = functools.reduce(jnp.maximum, [jnp.max(s, axis=0, keepdims=True) for s in logits])
    ps = [jnp.exp2(s - m) for s in logits]
    l = functools.reduce(jnp.add, [jnp.sum(p, axis=0, keepdims=True) for p in ps])
    acc = functools.reduce(jnp.add, [_dot(vt, p.astype(BF16)) for vt, p in zip(values_t, ps)])
    return acc * (1.0 / l)


def _na_window(r, rows):
    start = jnp.clip(r - WIN_R // 2, 0, rows - WIN_R)
    slab0 = jnp.minimum(start // 2, rows // 2 - NA_SLABS)
    edge = WIN_R // 2
    case = jnp.where(r < edge, r, jnp.where(r < rows - edge, edge + (r - edge) % 2, r - (rows - 2 * NA_SLABS)))
    return slab0, case


def _na_kernel(q_ref, k_ref, vt_ref, kc_ref, vct_ref, t_ref, o_ref, *, rows, unroll):
    kc = kc_ref[0]
    vct = jnp.concatenate([vct_ref[0, j] for j in range(vct_ref.shape[1])], axis=1)
    win_keys = NA_SLABS * LANES

    def logits(r):
        slab0, case = _na_window(r, rows)
        qs = _stack_quad(q_ref[0, pl.ds(pl.multiple_of(r * GRID_W, GRID_W), GRID_W), :])
        kw = k_ref[0, pl.ds(pl.multiple_of(slab0 * LANES, LANES), win_keys), :]
        return _dot_t(kw, qs) + t_ref[0, case], _dot_t(kc, qs), slab0

    def finish(r, s_win, s_ctx, slab0):
        vt = vt_ref[0, pl.ds(slab0, NA_SLABS)]
        vtw = jnp.concatenate([vt[j] for j in range(NA_SLABS)], axis=1)
        o = _softmax_pv_t([s_win, s_ctx], [vtw, vct])
        o_ref[0, pl.ds(pl.multiple_of(r * GRID_W, GRID_W), GRID_W), :] = _unstack_quad_t(o, GRID_W).astype(BF16)

    def body(i, carry):
        pending = [logits(i * unroll + e) for e in range(unroll)]
        for e, (s_win, s_ctx, slab0) in enumerate(pending):
            finish(i * unroll + e, s_win, s_ctx, slab0)
        return carry

    lax.fori_loop(0, rows // unroll, body, 0)


def _neighbourhood_attention(qa, ka, vat, cka, cvat, table):
    batch, seq, _ = qa.shape
    ctx_len = cka.shape[1]
    rows = seq // GRID_W
    width = QUAD * HEAD_DIM
    blk = lambda n: pl.BlockSpec((1, n, width), lambda b, j: (b, 0, j))
    vblk = lambda a: pl.BlockSpec((1, a.shape[1], width, LANES), lambda b, j: (b, 0, j, 0))
    return pl.pallas_call(
        functools.partial(_na_kernel, rows=rows, unroll=NA_UNROLL),
        grid=(batch, NA_HEADS // QUAD),
        in_specs=[blk(seq), blk(seq), vblk(vat), blk(ctx_len), vblk(cvat),
                  pl.BlockSpec((1,) + table.shape[1:], lambda b, j: (j, 0, 0, 0))],
        out_specs=blk(seq),
        out_shape=jax.ShapeDtypeStruct((batch, seq, NA_WIDTH), BF16),
        compiler_params=_params(("arbitrary", "arbitrary")),
        name="na",
    )(qa, ka, vat, cka, cvat, table)


def _bias_kernel(rpb_ref, onehot_ref, mask_ref, o_ref):
    o_ref[0] = (jnp.dot(rpb_ref[0], onehot_ref[...], preferred_element_type=F32,
                        precision=lax.Precision.HIGHEST) + mask_ref[...]) * LOG2E


def _bias_tables(rpb, rows):
    depth = rpb.shape[0]
    wr = min(WIN_R, rows)
    n_row_off, n_col_off = 2 * WIN_R - 1, 2 * WIN_C - 1
    cols = np.arange(GRID_W)
    cstart = np.clip(cols - WIN_C // 2, 0, GRID_W - WIN_C)
    kcol = np.arange(GRID_W)
    valid = (kcol[None, :] >= cstart[:, None]) & (kcol[None, :] < cstart[:, None] + WIN_C)
    col_off = kcol[None, :] - cols[:, None] + (WIN_C - 1)
    pad_r, pad_c = -n_row_off % 8, -n_col_off % 8
    onehot = (np.arange(n_col_off + pad_c)[:, None, None] == col_off[None]) & valid[None]
    onehot = jnp.asarray(onehot.reshape(n_col_off + pad_c, GRID_W * GRID_W), dtype=F32)
    mask = jnp.asarray(np.where(valid, 0.0, MASKED).reshape(1, GRID_W * GRID_W), dtype=F32)
    heads = depth * NA_HEADS
    rpb_p = jnp.pad(rpb.reshape(heads, n_row_off, n_col_off), ((0, 0), (0, pad_r), (0, pad_c)))
    toep = pl.pallas_call(
        _bias_kernel,
        grid=(heads,),
        in_specs=[pl.BlockSpec((1,) + rpb_p.shape[1:], lambda h: (h, 0, 0)),
                  pl.BlockSpec(onehot.shape, lambda h: (0, 0)),
                  pl.BlockSpec(mask.shape, lambda h: (0, 0))],
        out_specs=pl.BlockSpec((1, n_row_off + pad_r, GRID_W * GRID_W), lambda h: (h, 0, 0)),
        out_shape=jax.ShapeDtypeStruct((heads, n_row_off + pad_r, GRID_W * GRID_W), F32),
        compiler_params=_params(("arbitrary",)),
        name="bias",
    )(rpb_p, onehot, mask)
    groups = NA_HEADS // QUAD
    toep = toep.reshape(depth, groups, QUAD, n_row_off + pad_r, GRID_W, GRID_W)
    masked = jnp.full((depth, groups, QUAD, GRID_W, GRID_W), MASKED * LOG2E, F32)
    win_rows = 2 * NA_SLABS
    edge = WIN_R // 2
    layout = ([(u, 0) for u in range(edge)] + [(edge, 0), (edge + 1, 1)]
              + [(u, win_rows - wr) for u in range(win_rows - edge, win_rows)])
    cases = []
    for u, first in layout:
        blocks = [toep[:, :, :, w - u + WIN_R - 1] if first <= w < first + wr else masked for w in range(win_rows)]
        cases.append(jnp.stack(blocks, axis=3))
    table = jnp.stack(cases, axis=2)
    table = jnp.transpose(table, (0, 1, 2, 4, 6, 3, 5))
    return table.reshape(depth, groups, len(layout), win_rows * GRID_W, QUAD * GRID_W)


def _stack_group(q):
    return jnp.concatenate([_stack_heads(q[:, :LANES]), _stack_heads(q[:, LANES:])], axis=0)


def _unstack_group_t(o, rows):
    pairs = [jnp.concatenate([o[:, (2 * j) * rows:(2 * j + 1) * rows],
                              o[:, (2 * j + 1) * rows:(2 * j + 2) * rows]], axis=0).T for j in range(2)]
    return jnp.concatenate(pairs, axis=1)


def _scores_t(qs, k):
    return _dot_t(k, qs)


def _attend_t(s, vt, state):
    m_chunk = jnp.max(s, axis=0, keepdims=True)
    if state is None:
        p = jnp.exp2(s - m_chunk)
        return m_chunk, jnp.sum(p, axis=0, keepdims=True), _dot(vt, p.astype(BF16))
    m_old, l, acc = state
    m = jnp.maximum(m_old, m_chunk)
    alpha = jnp.exp2(m_old - m)
    p = jnp.exp2(s - m)
    return m, alpha * l + jnp.sum(p, axis=0, keepdims=True), alpha * acc + _dot(vt, p.astype(BF16))


def _gqa_kernel(q_ref, k_ref, vt_ref, kc_ref, vct_ref, o_ref, *, chunks, tq, tk):
    qs = _stack_group(q_ref[0])
    s, vt, state = _scores_t(qs, kc_ref[0]), vct_ref[0, 0], None
    for c in range(chunks):
        s_next = _scores_t(qs, k_ref[0, c * tk:(c + 1) * tk, :])
        state = _attend_t(s, vt, state)
        s, vt = s_next, vt_ref[0, c]
    _, l, acc = _attend_t(s, vt, state)
    o_ref[0] = _unstack_group_t(acc / l, tq).astype(BF16)


def _gqa_attention(qb, kbd, vbt, ckbd, cvbt):
    batch, seq, _ = qb.shape
    ctx_len = ckbd.shape[1]
    tq, tk = GQA_Q_ROWS, vbt.shape[3]
    kspec = lambda n: pl.BlockSpec((1, n, LANES), lambda b, g, i: (b, 0, g))
    vspec = lambda a: pl.BlockSpec((1, a.shape[1], HEAD_DIM, a.shape[3]), lambda b, g, i: (b, 0, g, 0))
    qspec = pl.BlockSpec((1, tq, 2 * LANES), lambda b, g, i: (b, i, g))
    return pl.pallas_call(
        functools.partial(_gqa_kernel, chunks=seq // tk, tq=tq, tk=tk),
        grid=(batch, GQA_KV_HEADS, seq // tq),
        in_specs=[qspec, kspec(seq), vspec(vbt), kspec(ctx_len), vspec(cvbt)],
        out_specs=qspec,
        out_shape=jax.ShapeDtypeStruct((batch, seq, GQA_WIDTH), BF16),
        compiler_params=_params(("arbitrary", "arbitrary", "arbitrary")),
        name="gqa",
    )(qb, kbd, vbt, ckbd, cvbt)


def _ctx_kernel(qa_ref, ka_ref, vat_ref, qb_ref, kb_ref, vbt_ref, oa_ref, ob_ref):
    n = qa_ref.shape[1]
    width = QUAD * HEAD_DIM
    for g in range(NA_HEADS // QUAD):
        sl = slice(g * width, (g + 1) * width)
        vt = jnp.concatenate([vat_ref[0, j, sl, :] for j in range(vat_ref.shape[1])], axis=1)
        o = _softmax_pv_t([_dot_t(ka_ref[0, :, sl], _stack_quad(qa_ref[0, :, sl]))], [vt])
        oa_ref[0, :, sl] = _unstack_quad_t(o, n).astype(BF16)
    for g in range(GQA_KV_HEADS):
        sl = slice(g * 2 * LANES, (g + 1) * 2 * LANES)
        s = _scores_t(_stack_group(qb_ref[0, :, sl]), kb_ref[0, :, g * LANES:(g + 1) * LANES])
        _, l, acc = _attend_t(s, vbt_ref[0, 0, g * HEAD_DIM:(g + 1) * HEAD_DIM, :], None)
        ob_ref[0, :, sl] = _unstack_group_t(acc / l, n).astype(BF16)


def _context_attention(cqa, cka, cvat, cqb, ckbd, cvbt):
    batch, n, _ = cqa.shape
    spec = lambda a: pl.BlockSpec((1,) + a.shape[1:], lambda b: (b,) + (0,) * (a.ndim - 1))
    args = (cqa, cka, cvat, cqb, ckbd, cvbt)
    return pl.pallas_call(
        _ctx_kernel,
        grid=(batch,),
        in_specs=[spec(a) for a in args],
        out_specs=[spec(cqa), spec(cqb)],
        out_shape=[jax.ShapeDtypeStruct(cqa.shape, BF16), jax.ShapeDtypeStruct(cqb.shape, BF16)],
        compiler_params=_params(("arbitrary",)),
        name="ctx",
    )(*args)


def _merge_kernel(a_ref, b_ref, za_ref, zb_ref, ga_ref, gb_ref, x_ref, gate_ref,
                  woa_ref, wob_ref, wout_ref, o_ref):
    o_a = _dot(a_ref[0] * za_ref[0], woa_ref[...])
    o_b = _dot(b_ref[0] * zb_ref[0], wob_ref[...])
    merged = ga_ref[0].astype(F32) * o_a + gb_ref[0].astype(F32) * o_b
    o_ref[0] = x_ref[0] + gate_ref[0] * _dot(merged.astype(BF16), wout_ref[...])


def _merge(a_att, b_att, za, zb, ga, gb, x, gate, w_o_a, w_o_b, w_out, rows):
    groups, total, _ = x.shape
    act = lambda w: pl.BlockSpec((1, rows, w), lambda g, i: (g, i, 0))
    const = lambda a: pl.BlockSpec(a.shape, lambda g, i: (0, 0))
    return pl.pallas_call(
        _merge_kernel,
        grid=(groups, total // rows),
        in_specs=[act(NA_WIDTH), act(GQA_WIDTH), act(NA_WIDTH), act(GQA_WIDTH), act(D_MODEL), act(D_MODEL),
                  act(D_MODEL), pl.BlockSpec((1, 1, D_MODEL), lambda g, i: (g, 0, 0)),
                  const(w_o_a), const(w_o_b), const(w_out)],
        out_specs=act(D_MODEL),
        out_shape=jax.ShapeDtypeStruct(x.shape, F32),
        compiler_params=_params(("arbitrary", "arbitrary")),
        name="merge",
    )(a_att, b_att, za, zb, ga, gb, x, gate, w_o_a, w_o_b, w_out)


def _pack_w_in(w):
    o = NA_WIDTH * 2
    b_k = w[:, o:o + GQA_KV_WIDTH]
    dup = jnp.concatenate(
        [b_k[:, h * HEAD_DIM:(h + 1) * HEAD_DIM] for h in range(GQA_KV_HEADS) for _ in range(2)], axis=1)
    return jnp.concatenate([w[:, :o], dup, w[:, o + GQA_KV_WIDTH:]], axis=1).astype(BF16)


def _rope_tables(seq):
    t = jnp.arange(seq, dtype=jnp.int32)
    inv = 1.0 / (ROPE_THETA ** (jnp.arange(ROT_HALF, dtype=F32) / ROT_HALF))
    ang_r = (t // GRID_W).astype(F32)[:, None] * inv[None, :]
    ang_c = (t % GRID_W).astype(F32)[:, None] * inv[None, :]
    cos = jnp.concatenate([jnp.cos(ang_r)] * 2 + [jnp.cos(ang_c)] * 2, axis=1)
    sin = jnp.concatenate([-jnp.sin(ang_r), jnp.sin(ang_r), -jnp.sin(ang_c), jnp.sin(ang_c)], axis=1)
    reps = LANES // HEAD_DIM
    return jnp.tile(cos, (1, reps)), jnp.tile(sin, (1, reps))


def _group_sum_matrix():
    idx = np.arange(MXU_COLS) // HEAD_DIM
    return jnp.asarray(idx[:, None] == idx[None, :], dtype=BF16)


def kernel(x, c, ctx, c_ctx, w_ada, b_ada, norm_g, w_in, q_norm_a, k_norm_a, q_norm_b, k_norm_b,
           rpb, w_o_a, w_o_b, w_out):
    batch, seq, _ = x.shape
    ctx_len = ctx.shape[1]
    depth = w_ada.shape[0]
    rows = seq // GRID_W
    assert seq % GRID_W == 0 and rows >= WIN_R and seq % PROJ_ROWS == 0 and seq % GQA_Q_ROWS == 0

    pad = -(batch + 1) % 8
    c_rows = jnp.concatenate([c, c_ctx[None, :], jnp.zeros((pad, D_MODEL), F32)], axis=0)
    mod = _modulation(c_rows, w_ada, b_ada)

    cos_x, sin_x = _rope_tables(seq)
    cos_c = jnp.ones((ctx_len, LANES), F32)
    sin_c = jnp.zeros((ctx_len, LANES), F32)
    gsum = _group_sum_matrix()
    tables = _bias_tables(rpb, rows)

    for l in range(depth):
        update_ctx = l < depth - 1
        shift, scale, gate = (mod[l, :, i * D_MODEL:(i + 1) * D_MODEL] for i in range(3))
        per_batch = lambda m: m[:batch, None, :]
        for_ctx = lambda m: jnp.broadcast_to(m[batch][None, None, :], (batch, 1, D_MODEL))
        w_packed = _pack_w_in(w_in[l])
        tile = lambda g, n: jnp.tile(g, n)
        gains = jnp.stack([tile(k_norm_a[l], NA_HEADS), tile(k_norm_b[l], NA_HEADS),
                           tile(q_norm_a[l], NA_HEADS) * (QK_SCALE * LOG2E),
                           tile(q_norm_b[l], GQA_Q_HEADS) * (QK_SCALE * LOG2E)])
        g_row = norm_g[l][None, :]

        ka, vat, kbd, vbt, qa, qb, za, zb, ga, gb = _project(
            x, per_batch(shift), per_batch(scale), g_row, w_packed, gsum, gains, cos_x, sin_x,
            len(SEG_WIDTHS), PROJ_ROWS)
        nseg_c = len(SEG_WIDTHS) if update_ctx else KV_SEGS
        w_ctx = w_packed if update_ctx else w_packed[:, :SEG_OFFSETS[KV_SEGS]]
        pc = _project(ctx, for_ctx(shift), for_ctx(scale), g_row, w_ctx, gsum, gains, cos_c, sin_c,
                      nseg_c, ctx_len)
        cka, cvat, ckbd, cvbt = pc[:KV_SEGS]

        a_att = _neighbourhood_attention(qa, ka, vat, cka, cvat, tables[l])
        b_att = _gqa_attention(qb, kbd, vbt, ckbd, cvbt)
        wa, wb, wo = w_o_a[l].astype(BF16), w_o_b[l].astype(BF16), w_out[l].astype(BF16)
        x_new = _merge(a_att, b_att, za, zb, ga, gb, x, per_batch(gate), wa, wb, wo, PROJ_ROWS)

        if update_ctx:
            _, _, _, _, cqa, cqb, cza, czb, cga, cgb = pc
            c_a, c_b = _context_attention(cqa, cka, cvat, cqb, ckbd, cvbt)
            ctx = _merge(c_a, c_b, cza, czb, cga, cgb, ctx, for_ctx(gate), wa, wb, wo, ctx_len)
        x = x_new
    return x
```

```python
import functools

import numpy as np
import jax
import jax.numpy as jnp
from jax import lax
from jax.experimental import pallas as pl
from jax.experimental.pallas import tpu as pltpu

D_MODEL = 1024
GRID_W = 64
HEAD_DIM = 64
NA_HEADS = 8
NA_WIDTH = NA_HEADS * HEAD_DIM
WIN_R = 8
WIN_C = 16
GQA_Q_HEADS = 8
GQA_KV_HEADS = 2
GQA_WIDTH = GQA_Q_HEADS * HEAD_DIM
GQA_KV_WIDTH = GQA_KV_HEADS * HEAD_DIM
ROPE_THETA = 10000.0
ROT_AXIS = HEAD_DIM // 2
ROT_HALF = ROT_AXIS // 2
EPS = 1e-6
QK_SCALE = HEAD_DIM ** -0.5
LOG2E = 1.4426950408889634

LANES = 128
MXU_COLS = 256
MASKED = -1e30
VMEM_LIMIT = 56 * 1024 * 1024

PROJ_ROWS = 512
GQA_Q_ROWS = 256
QUAD = 4
NA_SLABS = WIN_R // 2 + 1
NA_UNROLL = 8

F32 = jnp.float32
BF16 = jnp.bfloat16

SEG_KA, SEG_VA, SEG_KB, SEG_VB, SEG_QA, SEG_QB, SEG_ZA, SEG_ZB, SEG_GA, SEG_GB = range(10)
SEG_WIDTHS = (NA_WIDTH, NA_WIDTH, GQA_KV_WIDTH, GQA_KV_WIDTH, NA_WIDTH, GQA_WIDTH,
              NA_WIDTH, GQA_WIDTH, D_MODEL, D_MODEL)
SEG_OFFSETS = tuple(int(v) for v in np.cumsum((0,) + SEG_WIDTHS))
SEG_OUT_WIDTHS = tuple(2 * w if s == SEG_KB else w for s, w in enumerate(SEG_WIDTHS))
KV_SEGS = 4
SEG_GAIN_ROW = {SEG_KA: 0, SEG_KB: 1, SEG_QA: 2, SEG_QB: 3}
SEG_ROPE = (SEG_KB, SEG_QB)
SEG_T_SLAB = {SEG_VA: LANES, SEG_VB: None}


def _dot(a, b):
    return jnp.dot(a, b, preferred_element_type=F32)


def _dot_t(a, b):
    return lax.dot_general(a, b, (((1,), (1,)), ((), ())), preferred_element_type=F32)


def _params(semantics):
    return pltpu.CompilerParams(dimension_semantics=semantics, vmem_limit_bytes=VMEM_LIMIT)


def _mod_kernel(c_ref, w_ref, b_ref, o_ref):
    c = c_ref[...]
    a = c * jax.nn.sigmoid(c)
    o_ref[0] = jnp.dot(a, w_ref[0], preferred_element_type=F32,
                       precision=lax.Precision.HIGHEST) + b_ref[0]


def _modulation(c_rows, w_ada, b_ada):
    depth = w_ada.shape[0]
    rows = c_rows.shape[0]
    ncol = w_ada.shape[2] // D_MODEL
    return pl.pallas_call(
        _mod_kernel,
        grid=(depth, ncol),
        in_specs=[
            pl.BlockSpec((rows, D_MODEL), lambda l, j: (0, 0)),
            pl.BlockSpec((1, D_MODEL, D_MODEL), lambda l, j: (l, 0, j)),
            pl.BlockSpec((1, 1, D_MODEL), lambda l, j: (l, 0, j)),
        ],
        out_specs=pl.BlockSpec((1, rows, D_MODEL), lambda l, j: (l, 0, j)),
        out_shape=jax.ShapeDtypeStruct((depth, rows, ncol * D_MODEL), F32),
        compiler_params=_params(("arbitrary", "arbitrary")),
        name="mod",
    )(c_rows, w_ada, b_ada.reshape(depth, 1, -1))


def _head_norm(p, gsum, gain):
    sq = (p * p).astype(BF16)
    cols = min(MXU_COLS, p.shape[1])
    parts = []
    for j in range(p.shape[1] // cols):
        sl = slice(j * cols, (j + 1) * cols)
        ss = _dot(sq[:, sl], gsum[:cols, :cols])
        parts.append(p[:, sl] * lax.rsqrt(ss * (1.0 / HEAD_DIM) + EPS))
    y = parts[0] if len(parts) == 1 else jnp.concatenate(parts, axis=1)
    return y * gain


def _rope(y, cos, sin):
    width = y.shape[1]
    lane = lax.broadcasted_iota(jnp.int32, y.shape, 1)
    ahead = pltpu.roll(y, width - ROT_HALF, axis=1)
    behind = pltpu.roll(y, ROT_HALF, axis=1)
    partner = jnp.where((lane % ROT_AXIS) < ROT_HALF, ahead, behind)
    reps = width // LANES
    cos = cos if reps == 1 else jnp.concatenate([cos] * reps, axis=1)
    sin = sin if reps == 1 else jnp.concatenate([sin] * reps, axis=1)
    return y * cos + partner * sin


def _duplicate_heads(y):
    lo = lax.broadcasted_iota(jnp.int32, y.shape, 1) < HEAD_DIM
    swapped = pltpu.roll(y, HEAD_DIM, axis=1)
    return jnp.concatenate([jnp.where(lo, y, swapped), jnp.where(lo, swapped, y)], axis=1)


def _proj_kernel(x_ref, shift_ref, scale_ref, g_ref, w_ref, gsum_ref, gain_ref, cos_ref, sin_ref,
                 *out_refs, nseg):
    x = x_ref[0]
    ms = jnp.mean(x * x, axis=-1, keepdims=True)
    gmod = g_ref[...] * (1.0 + scale_ref[0])
    h = (x * lax.rsqrt(ms + EPS) * gmod + shift_ref[0]).astype(BF16)
    gsum = gsum_ref[...]
    for seg in range(nseg):
        width = SEG_WIDTHS[seg]
        p = _dot(h, w_ref[:, SEG_OFFSETS[seg]:SEG_OFFSETS[seg + 1]])
        if seg in SEG_GAIN_ROW:
            row = SEG_GAIN_ROW[seg]
            p = _head_norm(p, gsum, gain_ref[row:row + 1, :width])
        if seg in SEG_ROPE:
            p = _rope(p, cos_ref[...], sin_ref[...])
        if seg == SEG_KB:
            p = _duplicate_heads(p)
        if seg in (SEG_ZA, SEG_ZB):
            p = p * jax.nn.sigmoid(p)
        if seg in (SEG_GA, SEG_GB):
            p = jax.nn.sigmoid(p)
        if seg in SEG_T_SLAB:
            pt = p.T.astype(BF16)
            slab = SEG_T_SLAB[seg] or pt.shape[1]
            for j in range(pt.shape[1] // slab):
                out_refs[seg][0, j] = pt[:, j * slab:(j + 1) * slab]
        else:
            out_refs[seg][0] = p.astype(BF16)


def _project(x, shift, scale, norm_g, w_all, layer, gsum, gains, cos, sin, nseg, rows):
    groups, total, _ = x.shape
    ncols = SEG_OFFSETS[nseg]
    const = lambda g, i: (0, 0)

    def out_layout(s):
        width = SEG_OUT_WIDTHS[s]
        if s not in SEG_T_SLAB:
            return (groups, total, width), pl.BlockSpec((1, rows, width), lambda g, i: (g, i, 0))
        slab = SEG_T_SLAB[s] or rows
        return ((groups, total // slab, width, slab),
                pl.BlockSpec((1, rows // slab, width, slab), lambda g, i: (g, i, 0, 0)))

    layouts = [out_layout(s) for s in range(nseg)]
    return pl.pallas_call(
        functools.partial(_proj_kernel, nseg=nseg),
        grid=(groups, total // rows),
        in_specs=[
            pl.BlockSpec((1, rows, D_MODEL), lambda g, i: (g, i, 0)),
            pl.BlockSpec((1, 1, D_MODEL), lambda g, i: (g, 0, 0)),
            pl.BlockSpec((1, 1, D_MODEL), lambda g, i: (g, 0, 0)),
            pl.BlockSpec((1, D_MODEL), const),
            pl.BlockSpec((None, D_MODEL, ncols), lambda g, i: (layer, 0, 0), pipeline_mode=pl.Buffered(1)),
            pl.BlockSpec((MXU_COLS, MXU_COLS), const),
            pl.BlockSpec(gains.shape, const),
            pl.BlockSpec((rows, LANES), lambda g, i: (i, 0)),
            pl.BlockSpec((rows, LANES), lambda g, i: (i, 0)),
        ],
        out_specs=[spec for _, spec in layouts],
        out_shape=[jax.ShapeDtypeStruct(shape, BF16) for shape, _ in layouts],
        compiler_params=_params(("arbitrary", "arbitrary")),
        name="proj",
    )(x, shift, scale, norm_g, w_all, gsum, gains, cos, sin)


def _stack_heads(pair):
    lo = lax.broadcasted_iota(jnp.int32, pair.shape, 1) < HEAD_DIM
    zero = jnp.zeros_like(pair)
    return jnp.concatenate([jnp.where(lo, pair, zero), jnp.where(lo, zero, pair)], axis=0)


def _stack_quad(q):
    head = lax.broadcasted_iota(jnp.int32, q.shape, 1) // HEAD_DIM
    zero = jnp.zeros_like(q)
    return jnp.concatenate([jnp.where(head == h, q, zero) for h in range(QUAD)], axis=0)


def _unstack_quad_t(o, rows):
    ot = o.T
    head = lax.broadcasted_iota(jnp.int32, (rows, QUAD * HEAD_DIM), 1) // HEAD_DIM
    out = ot[:rows]
    for h in range(1, QUAD):
        out = jnp.where(head == h, ot[h * rows:(h + 1) * rows], out)
    return out


def _softmax_pv_t(logits, values_t):
    m = functools.reduce(jnp.maximum, [jnp.max(s, axis=0, keepdims=True) for s in logits])
    ps = [jnp.exp2(s - m) for s in logits]
    l = functools.reduce(jnp.add, [jnp.sum(p, axis=0, keepdims=True) for p in ps])
    acc = functools.reduce(jnp.add, [_dot(vt, p.astype(BF16)) for vt, p in zip(values_t, ps)])
    return acc * (1.0 / l)


def _na_window(r, rows):
    start = jnp.clip(r - WIN_R // 2, 0, rows - WIN_R)
    slab0 = jnp.minimum(start // 2, rows // 2 - NA_SLABS)
    first = 2 * slab0
    blocks = []
    for w in range(2 * NA_SLABS):
        key_row = first + w
        valid = (key_row >= start) & (key_row < start + WIN_R)
        blocks.append(jnp.where(valid, key_row - r + (WIN_R - 1), 2 * WIN_R - 1))
    return slab0, blocks


def _na_kernel(q_ref, k_ref, vt_ref, kc_ref, vct_ref, t_ref, o_ref, *, rows, unroll):
    kc = kc_ref[0]
    vct = jnp.concatenate([vct_ref[0, j] for j in range(vct_ref.shape[1])], axis=1)
    win_keys = NA_SLABS * LANES

    def logits(r):
        slab0, blocks = _na_window(r, rows)
        qs = _stack_quad(q_ref[0, pl.ds(pl.multiple_of(r * GRID_W, GRID_W), GRID_W), :])
        kw = k_ref[0, pl.ds(pl.multiple_of(slab0 * LANES, LANES), win_keys), :]
        bias = jnp.concatenate([t_ref[b] for b in blocks], axis=0)
        return _dot_t(kw, qs) + bias, _dot_t(kc, qs), slab0

    def finish(r, s_win, s_ctx, slab0):
        vt = vt_ref[0, pl.ds(slab0, NA_SLABS)]
        vtw = jnp.concatenate([vt[j] for j in range(NA_SLABS)], axis=1)
        o = _softmax_pv_t([s_win, s_ctx], [vtw, vct])
        o_ref[0, pl.ds(pl.multiple_of(r * GRID_W, GRID_W), GRID_W), :] = _unstack_quad_t(o, GRID_W).astype(BF16)

    def body(i, carry):
        pending = [logits(i * unroll + e) for e in range(unroll)]
        for e, (s_win, s_ctx, slab0) in enumerate(pending):
            finish(i * unroll + e, s_win, s_ctx, slab0)
        return carry

    lax.fori_loop(0, rows // unroll, body, 0)


def _neighbourhood_attention(qa, ka, vat, cka, cvat, tables, layer):
    batch, seq, _ = qa.shape
    ctx_len = cka.shape[1]
    rows = seq // GRID_W
    width = QUAD * HEAD_DIM
    blk = lambda n: pl.BlockSpec((1, n, width), lambda b, j: (b, 0, j))
    vblk = lambda a: pl.BlockSpec((1, a.shape[1], width, LANES), lambda b, j: (b, 0, j, 0))
    return pl.pallas_call(
        functools.partial(_na_kernel, rows=rows, unroll=NA_UNROLL),
        grid=(batch, NA_HEADS // QUAD),
        in_specs=[blk(seq), blk(seq), vblk(vat), blk(ctx_len), vblk(cvat),
                  pl.BlockSpec((None, None) + tables.shape[2:], lambda b, j: (layer, j, 0, 0, 0))],
        out_specs=blk(seq),
        out_shape=jax.ShapeDtypeStruct((batch, seq, NA_WIDTH), BF16),
        compiler_params=_params(("arbitrary", "arbitrary")),
        name="na",
    )(qa, ka, vat, cka, cvat, tables)


def _bias_kernel(rpb_ref, onehot_ref, mask_ref, o_ref):
    o_ref[0] = (jnp.dot(rpb_ref[0], onehot_ref[...], preferred_element_type=F32,
                        precision=lax.Precision.HIGHEST) + mask_ref[...]) * LOG2E


def _bias_tables(rpb):
    depth = rpb.shape[0]
    n_row_off, n_col_off = 2 * WIN_R - 1, 2 * WIN_C - 1
    cols = np.arange(GRID_W)
    cstart = np.clip(cols - WIN_C // 2, 0, GRID_W - WIN_C)
    kcol = np.arange(GRID_W)
    valid = (kcol[None, :] >= cstart[:, None]) & (kcol[None, :] < cstart[:, None] + WIN_C)
    col_off = kcol[None, :] - cols[:, None] + (WIN_C - 1)
    pad_r, pad_c = -n_row_off % 8, -n_col_off % 8
    assert pad_r >= 1
    onehot = (np.arange(n_col_off + pad_c)[:, None, None] == col_off[None]) & valid[None]
    onehot = jnp.asarray(onehot.reshape(n_col_off + pad_c, GRID_W * GRID_W), dtype=F32)
    mask = jnp.asarray(np.where(valid, 0.0, MASKED).reshape(1, GRID_W * GRID_W), dtype=F32)
    heads = depth * NA_HEADS
    rpb_p = jnp.pad(rpb.reshape(heads, n_row_off, n_col_off), ((0, 0), (0, pad_r), (0, pad_c)))
    toep = pl.pallas_call(
        _bias_kernel,
        grid=(heads,),
        in_specs=[pl.BlockSpec((1,) + rpb_p.shape[1:], lambda h: (h, 0, 0)),
                  pl.BlockSpec(onehot.shape, lambda h: (0, 0)),
                  pl.BlockSpec(mask.shape, lambda h: (0, 0))],
        out_specs=pl.BlockSpec((1, n_row_off + pad_r, GRID_W * GRID_W), lambda h: (h, 0, 0)),
        out_shape=jax.ShapeDtypeStruct((heads, n_row_off + pad_r, GRID_W * GRID_W), F32),
        compiler_params=_params(("arbitrary",)),
        name="bias",
    )(rpb_p, onehot, mask)
    groups = NA_HEADS // QUAD
    toep = toep.reshape(depth, groups, QUAD, n_row_off + pad_r, GRID_W, GRID_W)
    table = jnp.transpose(toep, (0, 1, 3, 5, 2, 4)).reshape(depth, groups, -1, GRID_W, QUAD * GRID_W)
    return jnp.where((jnp.arange(table.shape[2]) < n_row_off)[None, None, :, None, None], table, MASKED * LOG2E)


def _stack_group(q):
    return jnp.concatenate([_stack_heads(q[:, :LANES]), _stack_heads(q[:, LANES:])], axis=0)


def _unstack_group_t(o, rows):
    pairs = [jnp.concatenate([o[:, (2 * j) * rows:(2 * j + 1) * rows],
                              o[:, (2 * j + 1) * rows:(2 * j + 2) * rows]], axis=0).T for j in range(2)]
    return jnp.concatenate(pairs, axis=1)


def _scores_t(qs, k):
    return _dot_t(k, qs)


def _attend_t(s, vt, state):
    m_chunk = jnp.max(s, axis=0, keepdims=True)
    if state is None:
        p = jnp.exp2(s - m_chunk)
        return m_chunk, jnp.sum(p, axis=0, keepdims=True), _dot(vt, p.astype(BF16))
    m_old, l, acc = state
    m = jnp.maximum(m_old, m_chunk)
    alpha = jnp.exp2(m_old - m)
    p = jnp.exp2(s - m)
    return m, alpha * l + jnp.sum(p, axis=0, keepdims=True), alpha * acc + _dot(vt, p.astype(BF16))


def _gqa_kernel(q_ref, k_ref, vt_ref, kc_ref, vct_ref, o_ref, *, chunks, tq, tk):
    qs = _stack_group(q_ref[0])
    s, vt, state = _scores_t(qs, kc_ref[0]), vct_ref[0, 0], None
    for c in range(chunks):
        s_next = _scores_t(qs, k_ref[0, c * tk:(c + 1) * tk, :])
        state = _attend_t(s, vt, state)
        s, vt = s_next, vt_ref[0, c]
    _, l, acc = _attend_t(s, vt, state)
    o_ref[0] = _unstack_group_t(acc / l, tq).astype(BF16)


def _gqa_attention(qb, kbd, vbt, ckbd, cvbt):
    batch, seq, _ = qb.shape
    ctx_len = ckbd.shape[1]
    tq, tk = GQA_Q_ROWS, vbt.shape[3]
    kspec = lambda n: pl.BlockSpec((1, n, LANES), lambda b, g, i: (b, 0, g))
    vspec = lambda a: pl.BlockSpec((1, a.shape[1], HEAD_DIM, a.shape[3]), lambda b, g, i: (b, 0, g, 0))
    qspec = pl.BlockSpec((1, tq, 2 * LANES), lambda b, g, i: (b, i, g))
    return pl.pallas_call(
        functools.partial(_gqa_kernel, chunks=seq // tk, tq=tq, tk=tk),
        grid=(batch, GQA_KV_HEADS, seq // tq),
        in_specs=[qspec, kspec(seq), vspec(vbt), kspec(ctx_len), vspec(cvbt)],
        out_specs=qspec,
        out_shape=jax.ShapeDtypeStruct((batch, seq, GQA_WIDTH), BF16),
        compiler_params=_params(("arbitrary", "arbitrary", "arbitrary")),
        name="gqa",
    )(qb, kbd, vbt, ckbd, cvbt)


def _ctx_kernel(qa_ref, ka_ref, vat_ref, qb_ref, kb_ref, vbt_ref, oa_ref, ob_ref):
    n = qa_ref.shape[1]
    width = QUAD * HEAD_DIM
    for g in range(NA_HEADS // QUAD):
        sl = slice(g * width, (g + 1) * width)
        vt = jnp.concatenate([vat_ref[0, j, sl, :] for j in range(vat_ref.shape[1])], axis=1)
        o = _softmax_pv_t([_dot_t(ka_ref[0, :, sl], _stack_quad(qa_ref[0, :, sl]))], [vt])
        oa_ref[0, :, sl] = _unstack_quad_t(o, n).astype(BF16)
    for g in range(GQA_KV_HEADS):
        sl = slice(g * 2 * LANES, (g + 1) * 2 * LANES)
        s = _scores_t(_stack_group(qb_ref[0, :, sl]), kb_ref[0, :, g * LANES:(g + 1) * LANES])
        _, l, acc = _attend_t(s, vbt_ref[0, 0, g * HEAD_DIM:(g + 1) * HEAD_DIM, :], None)
        ob_ref[0, :, sl] = _unstack_group_t(acc / l, n).astype(BF16)


def _context_attention(cqa, cka, cvat, cqb, ckbd, cvbt):
    batch, n, _ = cqa.shape
    spec = lambda a: pl.BlockSpec((1,) + a.shape[1:], lambda b: (b,) + (0,) * (a.ndim - 1))
    args = (cqa, cka, cvat, cqb, ckbd, cvbt)
    return pl.pallas_call(
        _ctx_kernel,
        grid=(batch,),
        in_specs=[spec(a) for a in args],
        out_specs=[spec(cqa), spec(cqb)],
        out_shape=[jax.ShapeDtypeStruct(cqa.shape, BF16), jax.ShapeDtypeStruct(cqb.shape, BF16)],
        compiler_params=_params(("arbitrary",)),
        name="ctx",
    )(*args)


def _merge_kernel(a_ref, b_ref, za_ref, zb_ref, ga_ref, gb_ref, x_ref, gate_ref,
                  woa_ref, wob_ref, wout_ref, o_ref):
    o_a = _dot(a_ref[0] * za_ref[0], woa_ref[...])
    o_b = _dot(b_ref[0] * zb_ref[0], wob_ref[...])
    merged = ga_ref[0].astype(F32) * o_a + gb_ref[0].astype(F32) * o_b
    o_ref[0] = x_ref[0] + gate_ref[0] * _dot(merged.astype(BF16), wout_ref[...])


def _merge(a_att, b_att, za, zb, ga, gb, x, gate, w_o_a, w_o_b, w_out, layer, rows):
    groups, total, _ = x.shape
    act = lambda w: pl.BlockSpec((1, rows, w), lambda g, i: (g, i, 0))
    const = lambda a: pl.BlockSpec((None,) + a.shape[1:], lambda g, i: (layer, 0, 0))
    return pl.pallas_call(
        _merge_kernel,
        grid=(groups, total // rows),
        in_specs=[act(NA_WIDTH), act(GQA_WIDTH), act(NA_WIDTH), act(GQA_WIDTH), act(D_MODEL), act(D_MODEL),
                  act(D_MODEL), pl.BlockSpec((1, 1, D_MODEL), lambda g, i: (g, 0, 0)),
                  const(w_o_a), const(w_o_b), const(w_out)],
        out_specs=act(D_MODEL),
        out_shape=jax.ShapeDtypeStruct(x.shape, F32),
        compiler_params=_params(("arbitrary", "arbitrary")),
        name="merge",
    )(a_att, b_att, za, zb, ga, gb, x, gate, w_o_a, w_o_b, w_out)


def _rope_tables(seq):
    t = jnp.arange(seq, dtype=jnp.int32)
    inv = 1.0 / (ROPE_THETA ** (jnp.arange(ROT_HALF, dtype=F32) / ROT_HALF))
    ang_r = (t // GRID_W).astype(F32)[:, None] * inv[None, :]
    ang_c = (t % GRID_W).astype(F32)[:, None] * inv[None, :]
    cos = jnp.concatenate([jnp.cos(ang_r)] * 2 + [jnp.cos(ang_c)] * 2, axis=1)
    sin = jnp.concatenate([-jnp.sin(ang_r), jnp.sin(ang_r), -jnp.sin(ang_c), jnp.sin(ang_c)], axis=1)
    reps = LANES // HEAD_DIM
    return jnp.tile(cos, (1, reps)), jnp.tile(sin, (1, reps))


def _group_sum_matrix():
    idx = np.arange(MXU_COLS) // HEAD_DIM
    return jnp.asarray(idx[:, None] == idx[None, :], dtype=BF16)


def kernel(x, c, ctx, c_ctx, w_ada, b_ada, norm_g, w_in, q_norm_a, k_norm_a, q_norm_b, k_norm_b,
           rpb, w_o_a, w_o_b, w_out):
    batch, seq, _ = x.shape
    ctx_len = ctx.shape[1]
    depth = w_ada.shape[0]
    rows = seq // GRID_W
    assert seq % GRID_W == 0 and rows >= WIN_R and seq % PROJ_ROWS == 0 and seq % GQA_Q_ROWS == 0

    pad = -(batch + 1) % 8
    c_rows = jnp.concatenate([c, c_ctx[None, :], jnp.zeros((pad, D_MODEL), F32)], axis=0)
    mod = _modulation(c_rows, w_ada, b_ada)

    cos_x, sin_x = _rope_tables(seq)
    cos_c = jnp.ones((ctx_len, LANES), F32)
    sin_c = jnp.zeros((ctx_len, LANES), F32)
    gsum = _group_sum_matrix()
    tables = _bias_tables(rpb)
    w_in, w_o_a, w_o_b, w_out = (w.astype(BF16) for w in (w_in, w_o_a, w_o_b, w_out))

    for l in range(depth):
        update_ctx = l < depth - 1
        shift, scale, gate = (mod[l, :, i * D_MODEL:(i + 1) * D_MODEL] for i in range(3))
        per_batch = lambda m: m[:batch, None, :]
        for_ctx = lambda m: jnp.broadcast_to(m[batch][None, None, :], (batch, 1, D_MODEL))
        tile = lambda g, n: jnp.tile(g, n)
        gains = jnp.stack([tile(k_norm_a[l], NA_HEADS), tile(k_norm_b[l], NA_HEADS),
                           tile(q_norm_a[l], NA_HEADS) * (QK_SCALE * LOG2E),
                           tile(q_norm_b[l], GQA_Q_HEADS) * (QK_SCALE * LOG2E)])
        g_row = norm_g[l][None, :]

        ka, vat, kbd, vbt, qa, qb, za, zb, ga, gb = _project(
            x, per_batch(shift), per_batch(scale), g_row, w_in, l, gsum, gains, cos_x, sin_x,
            len(SEG_WIDTHS), PROJ_ROWS)
        nseg_c = len(SEG_WIDTHS) if update_ctx else KV_SEGS
        pc = _project(ctx, for_ctx(shift), for_ctx(scale), g_row, w_in, l, gsum, gains, cos_c, sin_c,
                      nseg_c, ctx_len)
        cka, cvat, ckbd, cvbt = pc[:KV_SEGS]

        a_att = _neighbourhood_attention(qa, ka, vat, cka, cvat, tables, l)
        b_att = _gqa_attention(qb, kbd, vbt, ckbd, cvbt)
        x_new = _merge(a_att, b_att, za, zb, ga, gb, x, per_batch(gate), w_o_a, w_o_b, w_out, l, PROJ_ROWS)

        if update_ctx:
            _, _, _, _, cqa, cqb, cza, czb, cga, cgb = pc
            c_a, c_b = _context_attention(cqa, cka, cvat, cqb, ckbd, cvbt)
            ctx = _merge(c_a, c_b, cza, czb, cga, cgb, ctx, for_ctx(gate), w_o_a, w_o_b, w_out, l, ctx_len)
        x = x_new
    return x
```

```python
import functools

import numpy as np
import jax
import jax.numpy as jnp
from jax import lax
from jax.experimental import pallas as pl
from jax.experimental.pallas import tpu as pltpu

D_MODEL = 1024
GRID_W = 64
HEAD_DIM = 64
NA_HEADS = 8
NA_WIDTH = NA_HEADS * HEAD_DIM
WIN_R = 8
WIN_C = 16
GQA_Q_HEADS = 8
GQA_KV_HEADS = 2
GQA_WIDTH = GQA_Q_HEADS * HEAD_DIM
GQA_KV_WIDTH = GQA_KV_HEADS * HEAD_DIM
ROPE_THETA = 10000.0
ROT_AXIS = HEAD_DIM // 2
ROT_HALF = ROT_AXIS // 2
EPS = 1e-6
QK_SCALE = HEAD_DIM ** -0.5
LOG2E = 1.4426950408889634

LANES = 128
MXU_COLS = 256
MASKED = -1e30
VMEM_LIMIT = 56 * 1024 * 1024

PROJ_ROWS = 512
GQA_Q_ROWS = 256
GQA_K_ROWS = 256
QUAD = 4
ONES_ROWS = 16
NA_SLABS = WIN_R // 2 + 1
NA_UNROLL = 8

F32 = jnp.float32
BF16 = jnp.bfloat16

SEG_KA, SEG_VA, SEG_KB, SEG_VB, SEG_QA, SEG_QB, SEG_ZA, SEG_ZB, SEG_GA, SEG_GB = range(10)
SEG_WIDTHS = (NA_WIDTH, NA_WIDTH, GQA_KV_WIDTH, GQA_KV_WIDTH, NA_WIDTH, GQA_WIDTH,
              NA_WIDTH, GQA_WIDTH, D_MODEL, D_MODEL)
SEG_OFFSETS = tuple(int(v) for v in np.cumsum((0,) + SEG_WIDTHS))
SEG_OUT_WIDTHS = tuple(2 * w if s == SEG_KB else w for s, w in enumerate(SEG_WIDTHS))
KV_SEGS = 4
SEG_GAIN_ROW = {SEG_KA: 0, SEG_KB: 1, SEG_QA: 2, SEG_QB: 3}
SEG_ROPE = (SEG_KB, SEG_QB)
SEG_T_SLAB = {SEG_VA: LANES, SEG_VB: GQA_K_ROWS}


def _dot(a, b):
    return jnp.dot(a, b, preferred_element_type=F32)


def _dot_t(a, b):
    return lax.dot_general(a, b, (((1,), (1,)), ((), ())), preferred_element_type=F32)


def _params(semantics):
    return pltpu.CompilerParams(dimension_semantics=semantics, vmem_limit_bytes=VMEM_LIMIT)


def _mod_kernel(c_ref, w_ref, b_ref, o_ref):
    c = c_ref[...]
    a = c * jax.nn.sigmoid(c)
    o_ref[0] = jnp.dot(a, w_ref[0], preferred_element_type=F32,
                       precision=lax.Precision.HIGHEST) + b_ref[0]


def _modulation(c_rows, w_ada, b_ada):
    depth = w_ada.shape[0]
    rows = c_rows.shape[0]
    ncol = w_ada.shape[2] // D_MODEL
    return pl.pallas_call(
        _mod_kernel,
        grid=(depth, ncol),
        in_specs=[
            pl.BlockSpec((rows, D_MODEL), lambda l, j: (0, 0)),
            pl.BlockSpec((1, D_MODEL, D_MODEL), lambda l, j: (l, 0, j)),
            pl.BlockSpec((1, 1, D_MODEL), lambda l, j: (l, 0, j)),
        ],
        out_specs=pl.BlockSpec((1, rows, D_MODEL), lambda l, j: (l, 0, j)),
        out_shape=jax.ShapeDtypeStruct((depth, rows, ncol * D_MODEL), F32),
        compiler_params=_params(("arbitrary", "arbitrary")),
        name="mod",
    )(c_rows, w_ada, b_ada.reshape(depth, 1, -1))


def _head_norm(p, gsum, gain):
    sq = (p * p).astype(BF16)
    cols = min(MXU_COLS, p.shape[1])
    parts = []
    for j in range(p.shape[1] // cols):
        sl = slice(j * cols, (j + 1) * cols)
        ss = _dot(sq[:, sl], gsum[:cols, :cols])
        parts.append(p[:, sl] * lax.rsqrt(ss * (1.0 / HEAD_DIM) + EPS))
    y = parts[0] if len(parts) == 1 else jnp.concatenate(parts, axis=1)
    return y * gain


def _rope(y, cos, sin):
    width = y.shape[1]
    lane = lax.broadcasted_iota(jnp.int32, y.shape, 1)
    ahead = pltpu.roll(y, width - ROT_HALF, axis=1)
    behind = pltpu.roll(y, ROT_HALF, axis=1)
    partner = jnp.where((lane % ROT_AXIS) < ROT_HALF, ahead, behind)
    reps = width // LANES
    cos = cos if reps == 1 else jnp.concatenate([cos] * reps, axis=1)
    sin = sin if reps == 1 else jnp.concatenate([sin] * reps, axis=1)
    return y * cos + partner * sin


def _duplicate_heads(y):
    lo = lax.broadcasted_iota(jnp.int32, y.shape, 1) < HEAD_DIM
    swapped = pltpu.roll(y, HEAD_DIM, axis=1)
    return jnp.concatenate([jnp.where(lo, y, swapped), jnp.where(lo, swapped, y)], axis=1)


def _proj_kernel(x_ref, shift_ref, scale_ref, g_ref, w_ref, gsum_ref, gain_ref, cos_ref, sin_ref,
                 *out_refs, nseg):
    x = x_ref[0]
    ms = jnp.mean(x * x, axis=-1, keepdims=True)
    gmod = g_ref[...] * (1.0 + scale_ref[0])
    h = (x * lax.rsqrt(ms + EPS) * gmod + shift_ref[0]).astype(BF16)
    gsum = gsum_ref[...]
    for seg in range(nseg):
        width = SEG_WIDTHS[seg]
        p = _dot(h, w_ref[:, SEG_OFFSETS[seg]:SEG_OFFSETS[seg + 1]])
        if seg in SEG_GAIN_ROW:
            row = SEG_GAIN_ROW[seg]
            p = _head_norm(p, gsum, gain_ref[row:row + 1, :width])
        if seg in SEG_ROPE:
            p = _rope(p, cos_ref[...], sin_ref[...])
        if seg == SEG_KB:
            p = _duplicate_heads(p)
        if seg in (SEG_ZA, SEG_ZB):
            p = p * jax.nn.sigmoid(p)
        if seg in (SEG_GA, SEG_GB):
            p = jax.nn.sigmoid(p)
        if seg in SEG_T_SLAB:
            pt = p.T.astype(BF16)
            slab = min(SEG_T_SLAB[seg], pt.shape[1])
            for j in range(pt.shape[1] // slab):
                out_refs[seg][0, j] = pt[:, j * slab:(j + 1) * slab]
        else:
            out_refs[seg][0] = p.astype(BF16)


def _project(x, shift, scale, norm_g, w_all, layer, gsum, gains, cos, sin, nseg, rows):
    groups, total, _ = x.shape
    ncols = SEG_OFFSETS[nseg]
    const = lambda g, i: (0, 0)

    def out_layout(s):
        width = SEG_OUT_WIDTHS[s]
        if s not in SEG_T_SLAB:
            return (groups, total, width), pl.BlockSpec((1, rows, width), lambda g, i: (g, i, 0))
        slab = min(SEG_T_SLAB[s], rows)
        return ((groups, total // slab, width, slab),
                pl.BlockSpec((1, rows // slab, width, slab), lambda g, i: (g, i, 0, 0)))

    layouts = [out_layout(s) for s in range(nseg)]
    return pl.pallas_call(
        functools.partial(_proj_kernel, nseg=nseg),
        grid=(groups, total // rows),
        in_specs=[
            pl.BlockSpec((1, rows, D_MODEL), lambda g, i: (g, i, 0)),
            pl.BlockSpec((1, 1, D_MODEL), lambda g, i: (g, 0, 0)),
            pl.BlockSpec((1, 1, D_MODEL), lambda g, i: (g, 0, 0)),
            pl.BlockSpec((1, D_MODEL), const),
            pl.BlockSpec((None, D_MODEL, ncols), lambda g, i: (layer, 0, 0), pipeline_mode=pl.Buffered(1)),
            pl.BlockSpec((MXU_COLS, MXU_COLS), const),
            pl.BlockSpec(gains.shape, const),
            pl.BlockSpec((rows, LANES), lambda g, i: (i, 0)),
            pl.BlockSpec((rows, LANES), lambda g, i: (i, 0)),
        ],
        out_specs=[spec for _, spec in layouts],
        out_shape=[jax.ShapeDtypeStruct(shape, BF16) for shape, _ in layouts],
        compiler_params=_params(("arbitrary", "arbitrary")),
        name="proj",
    )(x, shift, scale, norm_g, w_all, gsum, gains, cos, sin)


def _stack_heads(pair):
    lo = lax.broadcasted_iota(jnp.int32, pair.shape, 1) < HEAD_DIM
    zero = jnp.zeros_like(pair)
    return jnp.concatenate([jnp.where(lo, pair, zero), jnp.where(lo, zero, pair)], axis=0)


def _stack_quad(q):
    head = lax.broadcasted_iota(jnp.int32, q.shape, 1) // HEAD_DIM
    zero = jnp.zeros_like(q)
    return jnp.concatenate([jnp.where(head == h, q, zero) for h in range(QUAD)], axis=0)


def _unstack_quad_t(o, rows):
    ot = o.T
    head = lax.broadcasted_iota(jnp.int32, (rows, QUAD * HEAD_DIM), 1) // HEAD_DIM
    out = ot[:rows]
    for h in range(1, QUAD):
        out = jnp.where(head == h, ot[h * rows:(h + 1) * rows], out)
    return out


def _pv_t(vt, s, m):
    lhs = jnp.concatenate([vt, jnp.ones((ONES_ROWS, vt.shape[1]), vt.dtype)], axis=0)
    return _dot(lhs, jnp.exp2((s - m).astype(BF16)))


def _normalise_t(acc, dims):
    return acc[:dims] * (1.0 / acc[dims:dims + 1])


def _softmax_pv_t(logits, values_t):
    m = functools.reduce(jnp.maximum, [jnp.max(s, axis=0, keepdims=True) for s in logits])
    acc = functools.reduce(jnp.add, [_pv_t(vt, s, m) for vt, s in zip(values_t, logits)])
    return _normalise_t(acc, values_t[0].shape[0])


def _na_window(r, rows):
    start = jnp.clip(r - WIN_R // 2, 0, rows - WIN_R)
    slab0 = jnp.minimum(start // 2, rows // 2 - NA_SLABS)
    first = 2 * slab0
    blocks = []
    for w in range(2 * NA_SLABS):
        key_row = first + w
        valid = (key_row >= start) & (key_row < start + WIN_R)
        blocks.append(jnp.where(valid, key_row - r + (WIN_R - 1), 2 * WIN_R - 1))
    return slab0, blocks


def _na_kernel(q_ref, k_ref, vt_ref, kc_ref, vct_ref, t_ref, o_ref, *, rows, unroll):
    kc = kc_ref[0]
    vct = jnp.concatenate([vct_ref[0, j] for j in range(vct_ref.shape[1])], axis=1)
    win_keys = NA_SLABS * LANES

    def logits(r):
        slab0, blocks = _na_window(r, rows)
        qs = _stack_quad(q_ref[0, pl.ds(pl.multiple_of(r * GRID_W, GRID_W), GRID_W), :])
        kw = k_ref[0, pl.ds(pl.multiple_of(slab0 * LANES, LANES), win_keys), :]
        bias = jnp.concatenate([t_ref[b] for b in blocks], axis=0)
        return _dot_t(kw, qs) + bias, _dot_t(kc, qs), slab0

    def finish(r, s_win, s_ctx, slab0):
        vt = vt_ref[0, pl.ds(slab0, NA_SLABS)]
        vtw = jnp.concatenate([vt[j] for j in range(NA_SLABS)], axis=1)
        o = _softmax_pv_t([s_win, s_ctx], [vtw, vct])
        o_ref[0, pl.ds(pl.multiple_of(r * GRID_W, GRID_W), GRID_W), :] = _unstack_quad_t(o, GRID_W).astype(BF16)

    def body(i, carry):
        pending = [logits(i * unroll + e) for e in range(unroll)]
        for e, (s_win, s_ctx, slab0) in enumerate(pending):
            finish(i * unroll + e, s_win, s_ctx, slab0)
        return carry

    lax.fori_loop(0, rows // unroll, body, 0)


def _neighbourhood_attention(qa, ka, vat, cka, cvat, tables, layer):
    batch, seq, _ = qa.shape
    ctx_len = cka.shape[1]
    rows = seq // GRID_W
    width = QUAD * HEAD_DIM
    blk = lambda n: pl.BlockSpec((1, n, width), lambda b, j: (b, 0, j))
    vblk = lambda a: pl.BlockSpec((1, a.shape[1], width, LANES), lambda b, j: (b, 0, j, 0))
    return pl.pallas_call(
        functools.partial(_na_kernel, rows=rows, unroll=NA_UNROLL),
        grid=(batch, NA_HEADS // QUAD),
        in_specs=[blk(seq), blk(seq), vblk(vat), blk(ctx_len), vblk(cvat),
                  pl.BlockSpec((None, None) + tables.shape[2:], lambda b, j: (layer, j, 0, 0, 0))],
        out_specs=blk(seq),
        out_shape=jax.ShapeDtypeStruct((batch, seq, NA_WIDTH), BF16),
        compiler_params=_params(("arbitrary", "arbitrary")),
        name="na",
    )(qa, ka, vat, cka, cvat, tables)


def _bias_kernel(rpb_ref, onehot_ref, mask_ref, o_ref):
    o_ref[0] = (jnp.dot(rpb_ref[0], onehot_ref[...], preferred_element_type=F32,
                        precision=lax.Precision.HIGHEST) + mask_ref[...]) * LOG2E


def _bias_tables(rpb):
    depth = rpb.shape[0]
    n_row_off, n_col_off = 2 * WIN_R - 1, 2 * WIN_C - 1
    cols = np.arange(GRID_W)
    cstart = np.clip(cols - WIN_C // 2, 0, GRID_W - WIN_C)
    kcol = np.arange(GRID_W)
    valid = (kcol[None, :] >= cstart[:, None]) & (kcol[None, :] < cstart[:, None] + WIN_C)
    col_off = kcol[None, :] - cols[:, None] + (WIN_C - 1)
    pad_r, pad_c = -n_row_off % 8, -n_col_off % 8
    assert pad_r >= 1
    onehot = (np.arange(n_col_off + pad_c)[:, None, None] == col_off[None]) & valid[None]
    onehot = jnp.asarray(onehot.reshape(n_col_off + pad_c, GRID_W * GRID_W), dtype=F32)
    mask = jnp.asarray(np.where(valid, 0.0, MASKED).reshape(1, GRID_W * GRID_W), dtype=F32)
    heads = depth * NA_HEADS
    rpb_p = jnp.pad(rpb.reshape(heads, n_row_off, n_col_off), ((0, 0), (0, pad_r), (0, pad_c)))
    toep = pl.pallas_call(
        _bias_kernel,
        grid=(heads,),
        in_specs=[pl.BlockSpec((1,) + rpb_p.shape[1:], lambda h: (h, 0, 0)),
                  pl.BlockSpec(onehot.shape, lambda h: (0, 0)),
                  pl.BlockSpec(mask.shape, lambda h: (0, 0))],
        out_specs=pl.BlockSpec((1, n_row_off + pad_r, GRID_W * GRID_W), lambda h: (h, 0, 0)),
        out_shape=jax.ShapeDtypeStruct((heads, n_row_off + pad_r, GRID_W * GRID_W), F32),
        compiler_params=_params(("arbitrary",)),
        name="bias",
    )(rpb_p, onehot, mask)
    groups = NA_HEADS // QUAD
    toep = toep.reshape(depth, groups, QUAD, n_row_off + pad_r, GRID_W, GRID_W)
    table = jnp.transpose(toep, (0, 1, 3, 5, 2, 4)).reshape(depth, groups, -1, GRID_W, QUAD * GRID_W)
    return jnp.where((jnp.arange(table.shape[2]) < n_row_off)[None, None, :, None, None], table, MASKED * LOG2E)


def _stack_group(q):
    return jnp.concatenate([_stack_heads(q[:, :LANES]), _stack_heads(q[:, LANES:])], axis=0)


def _unstack_group_t(o, rows):
    pairs = [jnp.concatenate([o[:, (2 * j) * rows:(2 * j + 1) * rows],
                              o[:, (2 * j + 1) * rows:(2 * j + 2) * rows]], axis=0).T for j in range(2)]
    return jnp.concatenate(pairs, axis=1)


def _scores_t(qs, k):
    return _dot_t(k, qs)


def _attend_t(s, vt, state):
    m_chunk = jnp.max(s, axis=0, keepdims=True)
    if state is None:
        return m_chunk, _pv_t(vt, s, m_chunk)
    m_old, acc = state
    m = jnp.maximum(m_old, m_chunk)
    return m, jnp.exp2(m_old - m) * acc + _pv_t(vt, s, m)


def _gqa_kernel(q_ref, k_ref, vt_ref, kc_ref, vct_ref, o_ref, *, chunks, tq, tk):
    qs = _stack_group(q_ref[0])
    s, vt, state = _scores_t(qs, kc_ref[0]), vct_ref[0, 0], None
    for c in range(chunks):
        s_next = _scores_t(qs, k_ref[0, c * tk:(c + 1) * tk, :])
        state = _attend_t(s, vt, state)
        s, vt = s_next, vt_ref[0, c]
    _, acc = _attend_t(s, vt, state)
    o_ref[0] = _unstack_group_t(_normalise_t(acc, HEAD_DIM), tq).astype(BF16)


def _gqa_attention(qb, kbd, vbt, ckbd, cvbt):
    batch, seq, _ = qb.shape
    ctx_len = ckbd.shape[1]
    tq, tk = GQA_Q_ROWS, vbt.shape[3]
    kspec = lambda n: pl.BlockSpec((1, n, LANES), lambda b, g, i: (b, 0, g))
    vspec = lambda a: pl.BlockSpec((1, a.shape[1], HEAD_DIM, a.shape[3]), lambda b, g, i: (b, 0, g, 0))
    qspec = pl.BlockSpec((1, tq, 2 * LANES), lambda b, g, i: (b, i, g))
    return pl.pallas_call(
        functools.partial(_gqa_kernel, chunks=seq // tk, tq=tq, tk=tk),
        grid=(batch, GQA_KV_HEADS, seq // tq),
        in_specs=[qspec, kspec(seq), vspec(vbt), kspec(ctx_len), vspec(cvbt)],
        out_specs=qspec,
        out_shape=jax.ShapeDtypeStruct((batch, seq, GQA_WIDTH), BF16),
        compiler_params=_params(("arbitrary", "arbitrary", "arbitrary")),
        name="gqa",
    )(qb, kbd, vbt, ckbd, cvbt)


def _ctx_kernel(qa_ref, ka_ref, vat_ref, qb_ref, kb_ref, vbt_ref, oa_ref, ob_ref):
    n = qa_ref.shape[1]
    width = QUAD * HEAD_DIM
    for g in range(NA_HEADS // QUAD):
        sl = slice(g * width, (g + 1) * width)
        vt = jnp.concatenate([vat_ref[0, j, sl, :] for j in range(vat_ref.shape[1])], axis=1)
        o = _softmax_pv_t([_dot_t(ka_ref[0, :, sl], _stack_quad(qa_ref[0, :, sl]))], [vt])
        oa_ref[0, :, sl] = _unstack_quad_t(o, n).astype(BF16)
    for g in range(GQA_KV_HEADS):
        sl = slice(g * 2 * LANES, (g + 1) * 2 * LANES)
        s = _scores_t(_stack_group(qb_ref[0, :, sl]), kb_ref[0, :, g * LANES:(g + 1) * LANES])
        o = _softmax_pv_t([s], [vbt_ref[0, 0, g * HEAD_DIM:(g + 1) * HEAD_DIM, :]])
        ob_ref[0, :, sl] = _unstack_group_t(o, n).astype(BF16)


def _context_attention(cqa, cka, cvat, cqb, ckbd, cvbt):
    batch, n, _ = cqa.shape
    spec = lambda a: pl.BlockSpec((1,) + a.shape[1:], lambda b: (b,) + (0,) * (a.ndim - 1))
    args = (cqa, cka, cvat, cqb, ckbd, cvbt)
    return pl.pallas_call(
        _ctx_kernel,
        grid=(batch,),
        in_specs=[spec(a) for a in args],
        out_specs=[spec(cqa), spec(cqb)],
        out_shape=[jax.ShapeDtypeStruct(cqa.shape, BF16), jax.ShapeDtypeStruct(cqb.shape, BF16)],
        compiler_params=_params(("arbitrary",)),
        name="ctx",
    )(*args)


def _merge_kernel(a_ref, b_ref, za_ref, zb_ref, ga_ref, gb_ref, x_ref, gate_ref,
                  woa_ref, wob_ref, wout_ref, o_ref):
    o_a = _dot(a_ref[0] * za_ref[0], woa_ref[...])
    o_b = _dot(b_ref[0] * zb_ref[0], wob_ref[...])
    merged = ga_ref[0].astype(F32) * o_a + gb_ref[0].astype(F32) * o_b
    o_ref[0] = x_ref[0] + gate_ref[0] * _dot(merged.astype(BF16), wout_ref[...])


def _merge(a_att, b_att, za, zb, ga, gb, x, gate, w_o_a, w_o_b, w_out, layer, rows):
    groups, total, _ = x.shape
    act = lambda w: pl.BlockSpec((1, rows, w), lambda g, i: (g, i, 0))
    const = lambda a: pl.BlockSpec((None,) + a.shape[1:], lambda g, i: (layer, 0, 0))
    return pl.pallas_call(
        _merge_kernel,
        grid=(groups, total // rows),
        in_specs=[act(NA_WIDTH), act(GQA_WIDTH), act(NA_WIDTH), act(GQA_WIDTH), act(D_MODEL), act(D_MODEL),
                  act(D_MODEL), pl.BlockSpec((1, 1, D_MODEL), lambda g, i: (g, 0, 0)),
                  const(w_o_a), const(w_o_b), const(w_out)],
        out_specs=act(D_MODEL),
        out_shape=jax.ShapeDtypeStruct(x.shape, F32),
        compiler_params=_params(("arbitrary", "arbitrary")),
        name="merge",
    )(a_att, b_att, za, zb, ga, gb, x, gate, w_o_a, w_o_b, w_out)


def _rope_tables(seq):
    t = jnp.arange(seq, dtype=jnp.int32)
    inv = 1.0 / (ROPE_THETA ** (jnp.arange(ROT_HALF, dtype=F32) / ROT_HALF))
    ang_r = (t // GRID_W).astype(F32)[:, None] * inv[None, :]
    ang_c = (t % GRID_W).astype(F32)[:, None] * inv[None, :]
    cos = jnp.concatenate([jnp.cos(ang_r)] * 2 + [jnp.cos(ang_c)] * 2, axis=1)
    sin = jnp.concatenate([-jnp.sin(ang_r), jnp.sin(ang_r), -jnp.sin(ang_c), jnp.sin(ang_c)], axis=1)
    reps = LANES // HEAD_DIM
    return jnp.tile(cos, (1, reps)), jnp.tile(sin, (1, reps))


def _group_sum_matrix():
    idx = np.arange(MXU_COLS) // HEAD_DIM
    return jnp.asarray(idx[:, None] == idx[None, :], dtype=BF16)


def kernel(x, c, ctx, c_ctx, w_ada, b_ada, norm_g, w_in, q_norm_a, k_norm_a, q_norm_b, k_norm_b,
           rpb, w_o_a, w_o_b, w_out):
    batch, seq, _ = x.shape
    ctx_len = ctx.shape[1]
    depth = w_ada.shape[0]
    rows = seq // GRID_W
    assert seq % GRID_W == 0 and rows >= WIN_R and seq % PROJ_ROWS == 0 and seq % GQA_Q_ROWS == 0

    pad = -(batch + 1) % 8
    c_rows = jnp.concatenate([c, c_ctx[None, :], jnp.zeros((pad, D_MODEL), F32)], axis=0)
    mod = _modulation(c_rows, w_ada, b_ada)

    cos_x, sin_x = _rope_tables(seq)
    cos_c = jnp.ones((ctx_len, LANES), F32)
    sin_c = jnp.zeros((ctx_len, LANES), F32)
    gsum = _group_sum_matrix()
    tables = _bias_tables(rpb)
    w_in, w_o_a, w_o_b, w_out = (w.astype(BF16) for w in (w_in, w_o_a, w_o_b, w_out))

    for l in range(depth):
        update_ctx = l < depth - 1
        shift, scale, gate = (mod[l, :, i * D_MODEL:(i + 1) * D_MODEL] for i in range(3))
        per_batch = lambda m: m[:batch, None, :]
        for_ctx = lambda m: jnp.broadcast_to(m[batch][None, None, :], (batch, 1, D_MODEL))
        tile = lambda g, n: jnp.tile(g, n)
        gains = jnp.stack([tile(k_norm_a[l], NA_HEADS), tile(k_norm_b[l], NA_HEADS),
                           tile(q_norm_a[l], NA_HEADS) * (QK_SCALE * LOG2E),
                           tile(q_norm_b[l], GQA_Q_HEADS) * (QK_SCALE * LOG2E)])
        g_row = norm_g[l][None, :]

        ka, vat, kbd, vbt, qa, qb, za, zb, ga, gb = _project(
            x, per_batch(shift), per_batch(scale), g_row, w_in, l, gsum, gains, cos_x, sin_x,
            len(SEG_WIDTHS), PROJ_ROWS)
        nseg_c = len(SEG_WIDTHS) if update_ctx else KV_SEGS
        pc = _project(ctx, for_ctx(shift), for_ctx(scale), g_row, w_in, l, gsum, gains, cos_c, sin_c,
                      nseg_c, ctx_len)
        cka, cvat, ckbd, cvbt = pc[:KV_SEGS]

        a_att = _neighbourhood_attention(qa, ka, vat, cka, cvat, tables, l)
        b_att = _gqa_attention(qb, kbd, vbt, ckbd, cvbt)
        x_new = _merge(a_att, b_att, za, zb, ga, gb, x, per_batch(gate), w_o_a, w_o_b, w_out, l, PROJ_ROWS)

        if update_ctx:
            _, _, _, _, cqa, cqb, cza, czb, cga, cgb = pc
            c_a, c_b = _context_attention(cqa, cka, cvat, cqb, ckbd, cvbt)
            ctx = _merge(c_a, c_b, cza, czb, cga, cgb, ctx, for_ctx(gate), w_o_a, w_o_b, w_out, l, ctx_len)
        x = x_new
    return x
```

```python
import functools

import numpy as np
import jax
import jax.numpy as jnp
from jax import lax
from jax.experimental import pallas as pl
from jax.experimental.pallas import tpu as pltpu

D_MODEL = 1024
GRID_W = 64
HEAD_DIM = 64
NA_HEADS = 8
NA_WIDTH = NA_HEADS * HEAD_DIM
WIN_R = 8
WIN_C = 16
GQA_Q_HEADS = 8
GQA_KV_HEADS = 2
GQA_WIDTH = GQA_Q_HEADS * HEAD_DIM
GQA_KV_WIDTH = GQA_KV_HEADS * HEAD_DIM
ROPE_THETA = 10000.0
ROT_AXIS = HEAD_DIM // 2
ROT_HALF = ROT_AXIS // 2
EPS = 1e-6
QK_SCALE = HEAD_DIM ** -0.5
LOG2E = 1.4426950408889634

LANES = 128
MXU_COLS = 256
MASKED = -1e30
VMEM_LIMIT = 56 * 1024 * 1024

PROJ_ROWS = 512
GQA_Q_ROWS = 256
GQA_K_ROWS = 256
GQA_AHEAD = 1
QUAD = 4
ONES_ROWS = 16
NA_SLABS = WIN_R // 2 + 1
NA_UNROLL = 8

F32 = jnp.float32
BF16 = jnp.bfloat16

SEG_KA, SEG_VA, SEG_KB, SEG_VB, SEG_QA, SEG_QB, SEG_ZA, SEG_ZB, SEG_GA, SEG_GB = range(10)
SEG_WIDTHS = (NA_WIDTH, NA_WIDTH, GQA_KV_WIDTH, GQA_KV_WIDTH, NA_WIDTH, GQA_WIDTH,
              NA_WIDTH, GQA_WIDTH, D_MODEL, D_MODEL)
SEG_OFFSETS = tuple(int(v) for v in np.cumsum((0,) + SEG_WIDTHS))
SEG_OUT_WIDTHS = tuple(2 * w if s == SEG_KB else w for s, w in enumerate(SEG_WIDTHS))
KV_SEGS = 4
SEG_GAIN_ROW = {SEG_KA: 0, SEG_KB: 1, SEG_QA: 2, SEG_QB: 3}
SEG_ROPE = (SEG_KB, SEG_QB)
SEG_T_SLAB = {SEG_VA: LANES, SEG_VB: GQA_K_ROWS}


def _dot(a, b):
    return jnp.dot(a, b, preferred_element_type=F32)


def _dot_t(a, b):
    return lax.dot_general(a, b, (((1,), (1,)), ((), ())), preferred_element_type=F32)


def _params(semantics):
    return pltpu.CompilerParams(dimension_semantics=semantics, vmem_limit_bytes=VMEM_LIMIT)


def _mod_kernel(c_ref, w_ref, b_ref, o_ref):
    c = c_ref[...]
    a = c * jax.nn.sigmoid(c)
    o_ref[0] = jnp.dot(a, w_ref[0], preferred_element_type=F32,
                       precision=lax.Precision.HIGHEST) + b_ref[0]


def _modulation(c_rows, w_ada, b_ada):
    depth = w_ada.shape[0]
    rows = c_rows.shape[0]
    ncol = w_ada.shape[2] // D_MODEL
    return pl.pallas_call(
        _mod_kernel,
        grid=(depth, ncol),
        in_specs=[
            pl.BlockSpec((rows, D_MODEL), lambda l, j: (0, 0)),
            pl.BlockSpec((1, D_MODEL, D_MODEL), lambda l, j: (l, 0, j)),
            pl.BlockSpec((1, 1, D_MODEL), lambda l, j: (l, 0, j)),
        ],
        out_specs=pl.BlockSpec((1, rows, D_MODEL), lambda l, j: (l, 0, j)),
        out_shape=jax.ShapeDtypeStruct((depth, rows, ncol * D_MODEL), F32),
        compiler_params=_params(("arbitrary", "arbitrary")),
        name="mod",
    )(c_rows, w_ada, b_ada.reshape(depth, 1, -1))


def _head_norm(p, gsum, gain):
    sq = (p * p).astype(BF16)
    cols = min(MXU_COLS, p.shape[1])
    parts = []
    for j in range(p.shape[1] // cols):
        sl = slice(j * cols, (j + 1) * cols)
        ss = _dot(sq[:, sl], gsum[:cols, :cols])
        parts.append(p[:, sl] * lax.rsqrt(ss * (1.0 / HEAD_DIM) + EPS))
    y = parts[0] if len(parts) == 1 else jnp.concatenate(parts, axis=1)
    return y * gain


def _rope(y, cos, sin):
    width = y.shape[1]
    lane = lax.broadcasted_iota(jnp.int32, y.shape, 1)
    ahead = pltpu.roll(y, width - ROT_HALF, axis=1)
    behind = pltpu.roll(y, ROT_HALF, axis=1)
    partner = jnp.where((lane % ROT_AXIS) < ROT_HALF, ahead, behind)
    reps = width // LANES
    cos = cos if reps == 1 else jnp.concatenate([cos] * reps, axis=1)
    sin = sin if reps == 1 else jnp.concatenate([sin] * reps, axis=1)
    return y * cos + partner * sin


def _duplicate_heads(y):
    lo = lax.broadcasted_iota(jnp.int32, y.shape, 1) < HEAD_DIM
    swapped = pltpu.roll(y, HEAD_DIM, axis=1)
    return jnp.concatenate([jnp.where(lo, y, swapped), jnp.where(lo, swapped, y)], axis=1)


def _proj_kernel(x_ref, shift_ref, scale_ref, g_ref, w_ref, gsum_ref, gain_ref, cos_ref, sin_ref,
                 *out_refs, nseg):
    x = x_ref[0]
    ms = jnp.mean(x * x, axis=-1, keepdims=True)
    gmod = g_ref[...] * (1.0 + scale_ref[0])
    h = (x * lax.rsqrt(ms + EPS) * gmod + shift_ref[0]).astype(BF16)
    gsum = gsum_ref[...]
    for seg in range(nseg):
        width = SEG_WIDTHS[seg]
        p = _dot(h, w_ref[:, SEG_OFFSETS[seg]:SEG_OFFSETS[seg + 1]])
        if seg in SEG_GAIN_ROW:
            row = SEG_GAIN_ROW[seg]
            p = _head_norm(p, gsum, gain_ref[row:row + 1, :width])
        if seg in SEG_ROPE:
            p = _rope(p, cos_ref[...], sin_ref[...])
        if seg == SEG_KB:
            p = _duplicate_heads(p)
        if seg in (SEG_ZA, SEG_ZB):
            p = p * jax.nn.sigmoid(p)
        if seg in (SEG_GA, SEG_GB):
            p = jax.nn.sigmoid(p)
        if seg in SEG_T_SLAB:
            pt = p.T.astype(BF16)
            slab = min(SEG_T_SLAB[seg], pt.shape[1])
            for j in range(pt.shape[1] // slab):
                out_refs[seg][0, j] = pt[:, j * slab:(j + 1) * slab]
        else:
            out_refs[seg][0] = p.astype(BF16)


def _project(x, shift, scale, norm_g, w_all, layer, gsum, gains, cos, sin, nseg, rows):
    groups, total, _ = x.shape
    ncols = SEG_OFFSETS[nseg]
    const = lambda g, i: (0, 0)

    def out_layout(s):
        width = SEG_OUT_WIDTHS[s]
        if s not in SEG_T_SLAB:
            return (groups, total, width), pl.BlockSpec((1, rows, width), lambda g, i: (g, i, 0))
        slab = min(SEG_T_SLAB[s], rows)
        return ((groups, total // slab, width, slab),
                pl.BlockSpec((1, rows // slab, width, slab), lambda g, i: (g, i, 0, 0)))

    layouts = [out_layout(s) for s in range(nseg)]
    return pl.pallas_call(
        functools.partial(_proj_kernel, nseg=nseg),
        grid=(groups, total // rows),
        in_specs=[
            pl.BlockSpec((1, rows, D_MODEL), lambda g, i: (g, i, 0)),
            pl.BlockSpec((1, 1, D_MODEL), lambda g, i: (g, 0, 0)),
            pl.BlockSpec((1, 1, D_MODEL), lambda g, i: (g, 0, 0)),
            pl.BlockSpec((1, D_MODEL), const),
            pl.BlockSpec((None, D_MODEL, ncols), lambda g, i: (layer, 0, 0), pipeline_mode=pl.Buffered(1)),
            pl.BlockSpec((MXU_COLS, MXU_COLS), const),
            pl.BlockSpec(gains.shape, const),
            pl.BlockSpec((rows, LANES), lambda g, i: (i, 0)),
            pl.BlockSpec((rows, LANES), lambda g, i: (i, 0)),
        ],
        out_specs=[spec for _, spec in layouts],
        out_shape=[jax.ShapeDtypeStruct(shape, BF16) for shape, _ in layouts],
        compiler_params=_params(("arbitrary", "arbitrary")),
        name="proj",
    )(x, shift, scale, norm_g, w_all, gsum, gains, cos, sin)


def _stack_heads(pair):
    lo = lax.broadcasted_iota(jnp.int32, pair.shape, 1) < HEAD_DIM
    zero = jnp.zeros_like(pair)
    return jnp.concatenate([jnp.where(lo, pair, zero), jnp.where(lo, zero, pair)], axis=0)


def _stack_quad(q):
    head = lax.broadcasted_iota(jnp.int32, q.shape, 1) // HEAD_DIM
    zero = jnp.zeros_like(q)
    return jnp.concatenate([jnp.where(head == h, q, zero) for h in range(QUAD)], axis=0)


def _unstack_quad_t(o, rows):
    ot = o.T
    head = lax.broadcasted_iota(jnp.int32, (rows, QUAD * HEAD_DIM), 1) // HEAD_DIM
    out = ot[:rows]
    for h in range(1, QUAD):
        out = jnp.where(head == h, ot[h * rows:(h + 1) * rows], out)
    return out


def _pv_t(vt, s, m):
    lhs = jnp.concatenate([vt, jnp.ones((ONES_ROWS, vt.shape[1]), vt.dtype)], axis=0)
    return _dot(lhs, jnp.exp2((s - m).astype(BF16)))


def _normalise_t(acc, dims):
    return acc[:dims] * (1.0 / acc[dims:dims + 1])


def _softmax_pv_t(logits, values_t):
    m = functools.reduce(jnp.maximum, [jnp.max(s, axis=0, keepdims=True) for s in logits])
    acc = functools.reduce(jnp.add, [_pv_t(vt, s, m) for vt, s in zip(values_t, logits)])
    return _normalise_t(acc, values_t[0].shape[0])


def _na_window(r, rows):
    start = jnp.clip(r - WIN_R // 2, 0, rows - WIN_R)
    slab0 = jnp.minimum(start // 2, rows // 2 - NA_SLABS)
    first = 2 * slab0
    blocks = []
    for w in range(2 * NA_SLABS):
        key_row = first + w
        valid = (key_row >= start) & (key_row < start + WIN_R)
        blocks.append(jnp.where(valid, key_row - r + (WIN_R - 1), 2 * WIN_R - 1))
    return slab0, blocks


def _na_kernel(q_ref, k_ref, vt_ref, kc_ref, vct_ref, t_ref, o_ref, qt_ref, *, rows, unroll):
    kc = kc_ref[0]
    vct = jnp.concatenate([vct_ref[0, j] for j in range(vct_ref.shape[1])], axis=1)
    win_keys = NA_SLABS * LANES

    def logits(r, slot):
        slab0, blocks = _na_window(r, rows)
        qt_ref[slot] = _stack_quad(q_ref[0, pl.ds(pl.multiple_of(r * GRID_W, GRID_W), GRID_W), :]).T
        qs_t = qt_ref[slot]
        kw = k_ref[0, pl.ds(pl.multiple_of(slab0 * LANES, LANES), win_keys), :]
        bias = jnp.concatenate([t_ref[b] for b in blocks], axis=0)
        return _dot(kw, qs_t) + bias, _dot(kc, qs_t), slab0

    def finish(r, s_win, s_ctx, slab0):
        vt = vt_ref[0, pl.ds(slab0, NA_SLABS)]
        vtw = jnp.concatenate([vt[j] for j in range(NA_SLABS)], axis=1)
        o = _softmax_pv_t([s_win, s_ctx], [vtw, vct])
        o_ref[0, pl.ds(pl.multiple_of(r * GRID_W, GRID_W), GRID_W), :] = _unstack_quad_t(o, GRID_W).astype(BF16)

    def body(i, carry):
        pending = [logits(i * unroll + e, e) for e in range(unroll)]
        for e, (s_win, s_ctx, slab0) in enumerate(pending):
            finish(i * unroll + e, s_win, s_ctx, slab0)
        return carry

    lax.fori_loop(0, rows // unroll, body, 0)


def _neighbourhood_attention(qa, ka, vat, cka, cvat, tables, layer):
    batch, seq, _ = qa.shape
    ctx_len = cka.shape[1]
    rows = seq // GRID_W
    width = QUAD * HEAD_DIM
    blk = lambda n: pl.BlockSpec((1, n, width), lambda b, j: (b, 0, j))
    vblk = lambda a: pl.BlockSpec((1, a.shape[1], width, LANES), lambda b, j: (b, 0, j, 0))
    return pl.pallas_call(
        functools.partial(_na_kernel, rows=rows, unroll=NA_UNROLL),
        grid=(batch, NA_HEADS // QUAD),
        in_specs=[blk(seq), blk(seq), vblk(vat), blk(ctx_len), vblk(cvat),
                  pl.BlockSpec((None, None) + tables.shape[2:], lambda b, j: (layer, j, 0, 0, 0))],
        out_specs=blk(seq),
        out_shape=jax.ShapeDtypeStruct((batch, seq, NA_WIDTH), BF16),
        scratch_shapes=[pltpu.VMEM((NA_UNROLL, width, QUAD * GRID_W), BF16)],
        compiler_params=_params(("arbitrary", "arbitrary")),
        name="na",
    )(qa, ka, vat, cka, cvat, tables)


def _bias_kernel(rpb_ref, onehot_ref, mask_ref, o_ref):
    o_ref[0] = (jnp.dot(rpb_ref[0], onehot_ref[...], preferred_element_type=F32,
                        precision=lax.Precision.HIGHEST) + mask_ref[...]) * LOG2E


def _bias_tables(rpb):
    depth = rpb.shape[0]
    n_row_off, n_col_off = 2 * WIN_R - 1, 2 * WIN_C - 1
    cols = np.arange(GRID_W)
    cstart = np.clip(cols - WIN_C // 2, 0, GRID_W - WIN_C)
    kcol = np.arange(GRID_W)
    valid = (kcol[None, :] >= cstart[:, None]) & (kcol[None, :] < cstart[:, None] + WIN_C)
    col_off = kcol[None, :] - cols[:, None] + (WIN_C - 1)
    pad_r, pad_c = -n_row_off % 8, -n_col_off % 8
    assert pad_r >= 1
    onehot = (np.arange(n_col_off + pad_c)[:, None, None] == col_off[None]) & valid[None]
    onehot = jnp.asarray(onehot.reshape(n_col_off + pad_c, GRID_W * GRID_W), dtype=F32)
    mask = jnp.asarray(np.where(valid, 0.0, MASKED).reshape(1, GRID_W * GRID_W), dtype=F32)
    heads = depth * NA_HEADS
    rpb_p = jnp.pad(rpb.reshape(heads, n_row_off, n_col_off), ((0, 0), (0, pad_r), (0, pad_c)))
    toep = pl.pallas_call(
        _bias_kernel,
        grid=(heads,),
        in_specs=[pl.BlockSpec((1,) + rpb_p.shape[1:], lambda h: (h, 0, 0)),
                  pl.BlockSpec(onehot.shape, lambda h: (0, 0)),
                  pl.BlockSpec(mask.shape, lambda h: (0, 0))],
        out_specs=pl.BlockSpec((1, n_row_off + pad_r, GRID_W * GRID_W), lambda h: (h, 0, 0)),
        out_shape=jax.ShapeDtypeStruct((heads, n_row_off + pad_r, GRID_W * GRID_W), F32),
        compiler_params=_params(("arbitrary",)),
        name="bias",
    )(rpb_p, onehot, mask)
    groups = NA_HEADS // QUAD
    toep = toep.reshape(depth, groups, QUAD, n_row_off + pad_r, GRID_W, GRID_W)
    table = jnp.transpose(toep, (0, 1, 3, 5, 2, 4)).reshape(depth, groups, -1, GRID_W, QUAD * GRID_W)
    return jnp.where((jnp.arange(table.shape[2]) < n_row_off)[None, None, :, None, None], table, MASKED * LOG2E)


def _stack_group(q):
    return jnp.concatenate([_stack_heads(q[:, :LANES]), _stack_heads(q[:, LANES:])], axis=0)


def _unstack_group_t(o, rows):
    pairs = [jnp.concatenate([o[:, (2 * j) * rows:(2 * j + 1) * rows],
                              o[:, (2 * j + 1) * rows:(2 * j + 2) * rows]], axis=0).T for j in range(2)]
    return jnp.concatenate(pairs, axis=1)


def _scores_t(qs_t, k):
    return _dot(k, qs_t)


def _attend_t(s, vt, state):
    m_chunk = jnp.max(s, axis=0, keepdims=True)
    if state is None:
        return m_chunk, _pv_t(vt, s, m_chunk)
    m_old, acc = state
    m = jnp.maximum(m_old, m_chunk)
    return m, jnp.exp2(m_old - m) * acc + _pv_t(vt, s, m)


def _gqa_kernel(q_ref, k_ref, vt_ref, kc_ref, vct_ref, o_ref, qt_ref, *, chunks, tq, tk):
    qt_ref[...] = _stack_group(q_ref[0]).T
    qs = qt_ref[...]
    pending, state = [(_scores_t(qs, kc_ref[0]), vct_ref[0, 0])], None
    for c in range(chunks):
        pending.append((_scores_t(qs, k_ref[0, c * tk:(c + 1) * tk, :]), vt_ref[0, c]))
        if len(pending) > GQA_AHEAD:
            state = _attend_t(*pending.pop(0), state)
    for s, vt in pending:
        state = _attend_t(s, vt, state)
    _, acc = state
    o_ref[0] = _unstack_group_t(_normalise_t(acc, HEAD_DIM), tq).astype(BF16)


def _gqa_attention(qb, kbd, vbt, ckbd, cvbt):
    batch, seq, _ = qb.shape
    ctx_len = ckbd.shape[1]
    tq, tk = GQA_Q_ROWS, vbt.shape[3]
    kspec = lambda n: pl.BlockSpec((1, n, LANES), lambda b, g, i: (b, 0, g))
    vspec = lambda a: pl.BlockSpec((1, a.shape[1], HEAD_DIM, a.shape[3]), lambda b, g, i: (b, 0, g, 0))
    qspec = pl.BlockSpec((1, tq, 2 * LANES), lambda b, g, i: (b, i, g))
    return pl.pallas_call(
        functools.partial(_gqa_kernel, chunks=seq // tk, tq=tq, tk=tk),
        grid=(batch, GQA_KV_HEADS, seq // tq),
        in_specs=[qspec, kspec(seq), vspec(vbt), kspec(ctx_len), vspec(cvbt)],
        out_specs=qspec,
        out_shape=jax.ShapeDtypeStruct((batch, seq, GQA_WIDTH), BF16),
        scratch_shapes=[pltpu.VMEM((LANES, (GQA_Q_HEADS // GQA_KV_HEADS) * tq), BF16)],
        compiler_params=_params(("arbitrary", "arbitrary", "arbitrary")),
        name="gqa",
    )(qb, kbd, vbt, ckbd, cvbt)


def _ctx_kernel(qa_ref, ka_ref, vat_ref, qb_ref, kb_ref, vbt_ref, oa_ref, ob_ref):
    n = qa_ref.shape[1]
    width = QUAD * HEAD_DIM
    for g in range(NA_HEADS // QUAD):
        sl = slice(g * width, (g + 1) * width)
        vt = jnp.concatenate([vat_ref[0, j, sl, :] for j in range(vat_ref.shape[1])], axis=1)
        o = _softmax_pv_t([_dot_t(ka_ref[0, :, sl], _stack_quad(qa_ref[0, :, sl]))], [vt])
        oa_ref[0, :, sl] = _unstack_quad_t(o, n).astype(BF16)
    for g in range(GQA_KV_HEADS):
        sl = slice(g * 2 * LANES, (g + 1) * 2 * LANES)
        s = _scores_t(_stack_group(qb_ref[0, :, sl]).T, kb_ref[0, :, g * LANES:(g + 1) * LANES])
        o = _softmax_pv_t([s], [vbt_ref[0, 0, g * HEAD_DIM:(g + 1) * HEAD_DIM, :]])
        ob_ref[0, :, sl] = _unstack_group_t(o, n).astype(BF16)


def _context_attention(cqa, cka, cvat, cqb, ckbd, cvbt):
    batch, n, _ = cqa.shape
    spec = lambda a: pl.BlockSpec((1,) + a.shape[1:], lambda b: (b,) + (0,) * (a.ndim - 1))
    args = (cqa, cka, cvat, cqb, ckbd, cvbt)
    return pl.pallas_call(
        _ctx_kernel,
        grid=(batch,),
        in_specs=[spec(a) for a in args],
        out_specs=[spec(cqa), spec(cqb)],
        out_shape=[jax.ShapeDtypeStruct(cqa.shape, BF16), jax.ShapeDtypeStruct(cqb.shape, BF16)],
        compiler_params=_params(("arbitrary",)),
        name="ctx",
    )(*args)


def _merge_kernel(a_ref, b_ref, za_ref, zb_ref, ga_ref, gb_ref, x_ref, gate_ref,
                  woa_ref, wob_ref, wout_ref, o_ref):
    o_a = _dot(a_ref[0] * za_ref[0], woa_ref[...])
    o_b = _dot(b_ref[0] * zb_ref[0], wob_ref[...])
    merged = ga_ref[0].astype(F32) * o_a + gb_ref[0].astype(F32) * o_b
    o_ref[0] = x_ref[0] + gate_ref[0] * _dot(merged.astype(BF16), wout_ref[...])


def _merge(a_att, b_att, za, zb, ga, gb, x, gate, w_o_a, w_o_b, w_out, layer, rows):
    groups, total, _ = x.shape
    act = lambda w: pl.BlockSpec((1, rows, w), lambda g, i: (g, i, 0))
    const = lambda a: pl.BlockSpec((None,) + a.shape[1:], lambda g, i: (layer, 0, 0))
    return pl.pallas_call(
        _merge_kernel,
        grid=(groups, total // rows),
        in_specs=[act(NA_WIDTH), act(GQA_WIDTH), act(NA_WIDTH), act(GQA_WIDTH), act(D_MODEL), act(D_MODEL),
                  act(D_MODEL), pl.BlockSpec((1, 1, D_MODEL), lambda g, i: (g, 0, 0)),
                  const(w_o_a), const(w_o_b), const(w_out)],
        out_specs=act(D_MODEL),
        out_shape=jax.ShapeDtypeStruct(x.shape, F32),
        compiler_params=_params(("arbitrary", "arbitrary")),
        name="merge",
    )(a_att, b_att, za, zb, ga, gb, x, gate, w_o_a, w_o_b, w_out)


def _rope_tables(seq):
    t = jnp.arange(seq, dtype=jnp.int32)
    inv = 1.0 / (ROPE_THETA ** (jnp.arange(ROT_HALF, dtype=F32) / ROT_HALF))
    ang_r = (t // GRID_W).astype(F32)[:, None] * inv[None, :]
    ang_c = (t % GRID_W).astype(F32)[:, None] * inv[None, :]
    cos = jnp.concatenate([jnp.cos(ang_r)] * 2 + [jnp.cos(ang_c)] * 2, axis=1)
    sin = jnp.concatenate([-jnp.sin(ang_r), jnp.sin(ang_r), -jnp.sin(ang_c), jnp.sin(ang_c)], axis=1)
    reps = LANES // HEAD_DIM
    return jnp.tile(cos, (1, reps)), jnp.tile(sin, (1, reps))


def _group_sum_matrix():
    idx = np.arange(MXU_COLS) // HEAD_DIM
    return jnp.asarray(idx[:, None] == idx[None, :], dtype=BF16)


def kernel(x, c, ctx, c_ctx, w_ada, b_ada, norm_g, w_in, q_norm_a, k_norm_a, q_norm_b, k_norm_b,
           rpb, w_o_a, w_o_b, w_out):
    batch, seq, _ = x.shape
    ctx_len = ctx.shape[1]
    depth = w_ada.shape[0]
    rows = seq // GRID_W
    assert seq % GRID_W == 0 and rows >= WIN_R and seq % PROJ_ROWS == 0 and seq % GQA_Q_ROWS == 0

    pad = -(batch + 1) % 8
    c_rows = jnp.concatenate([c, c_ctx[None, :], jnp.zeros((pad, D_MODEL), F32)], axis=0)
    mod = _modulation(c_rows, w_ada, b_ada)

    cos_x, sin_x = _rope_tables(seq)
    cos_c = jnp.ones((ctx_len, LANES), F32)
    sin_c = jnp.zeros((ctx_len, LANES), F32)
    gsum = _group_sum_matrix()
    tables = _bias_tables(rpb)
    w_in, w_o_a, w_o_b, w_out = (w.astype(BF16) for w in (w_in, w_o_a, w_o_b, w_out))

    for l in range(depth):
        update_ctx = l < depth - 1
        shift, scale, gate = (mod[l, :, i * D_MODEL:(i + 1) * D_MODEL] for i in range(3))
        per_batch = lambda m: m[:batch, None, :]
        for_ctx = lambda m: jnp.broadcast_to(m[batch][None, None, :], (batch, 1, D_MODEL))
        tile = lambda g, n: jnp.tile(g, n)
        gains = jnp.stack([tile(k_norm_a[l], NA_HEADS), tile(k_norm_b[l], NA_HEADS),
                           tile(q_norm_a[l], NA_HEADS) * (QK_SCALE * LOG2E),
                           tile(q_norm_b[l], GQA_Q_HEADS) * (QK_SCALE * LOG2E)])
        g_row = norm_g[l][None, :]

        ka, vat, kbd, vbt, qa, qb, za, zb, ga, gb = _project(
            x, per_batch(shift), per_batch(scale), g_row, w_in, l, gsum, gains, cos_x, sin_x,
            len(SEG_WIDTHS), PROJ_ROWS)
        nseg_c = len(SEG_WIDTHS) if update_ctx else KV_SEGS
        pc = _project(ctx, for_ctx(shift), for_ctx(scale), g_row, w_in, l, gsum, gains, cos_c, sin_c,
                      nseg_c, ctx_len)
        cka, cvat, ckbd, cvbt = pc[:KV_SEGS]

        a_att = _neighbourhood_attention(qa, ka, vat, cka, cvat, tables, l)
        b_att = _gqa_attention(qb, kbd, vbt, ckbd, cvbt)
        x_new = _merge(a_att, b_att, za, zb, ga, gb, x, per_batch(gate), w_o_a, w_o_b, w_out, l, PROJ_ROWS)

        if update_ctx:
            _, _, _, _, cqa, cqb, cza, czb, cga, cgb = pc
            c_a, c_b = _context_attention(cqa, cka, cvat, cqb, ckbd, cvbt)
            ctx = _merge(c_a, c_b, cza, czb, cga, cgb, ctx, for_ctx(gate), w_o_a, w_o_b, w_out, l, ctx_len)
        x = x_new
    return x
```

```python
import functools

import numpy as np
import jax
import jax.numpy as jnp
from jax import lax
from jax.experimental import pallas as pl
from jax.experimental.pallas import tpu as pltpu

D_MODEL = 1024
GRID_W = 64
HEAD_DIM = 64
NA_HEADS = 8
NA_WIDTH = NA_HEADS * HEAD_DIM
WIN_R = 8
WIN_C = 16
GQA_Q_HEADS = 8
GQA_KV_HEADS = 2
GQA_WIDTH = GQA_Q_HEADS * HEAD_DIM
GQA_KV_WIDTH = GQA_KV_HEADS * HEAD_DIM
ROPE_THETA = 10000.0
ROT_AXIS = HEAD_DIM // 2
ROT_HALF = ROT_AXIS // 2
EPS = 1e-6
QK_SCALE = HEAD_DIM ** -0.5
LOG2E = 1.4426950408889634

LANES = 128
MXU_COLS = 256
MASKED = -1e30
VMEM_LIMIT = 56 * 1024 * 1024

PROJ_ROWS = 512
MERGE_ROWS = 1024
GQA_Q_ROWS = 256
GQA_K_ROWS = 256
GQA_AHEAD = 1
QUAD = 4
ONES_ROWS = 16
NA_SLABS = WIN_R // 2 + 1
NA_AHEAD = 4
NA_UNROLL = 8

F32 = jnp.float32
BF16 = jnp.bfloat16

SEG_KA, SEG_VA, SEG_KB, SEG_VB, SEG_QA, SEG_QB, SEG_ZA, SEG_ZB, SEG_GA, SEG_GB = range(10)
SEG_WIDTHS = (NA_WIDTH, NA_WIDTH, GQA_KV_WIDTH, GQA_KV_WIDTH, NA_WIDTH, GQA_WIDTH,
              NA_WIDTH, GQA_WIDTH, D_MODEL, D_MODEL)
SEG_OFFSETS = tuple(int(v) for v in np.cumsum((0,) + SEG_WIDTHS))
SEG_OUT_WIDTHS = tuple(2 * w if s == SEG_KB else w for s, w in enumerate(SEG_WIDTHS))
KV_SEGS = 4
SEG_ISSUE_RANK = {SEG_GA: 0, SEG_GB: 1, SEG_ZA: 2, SEG_ZB: 3, SEG_QB: 4, SEG_KB: 5, SEG_QA: 6, SEG_KA: 7,
                  SEG_VB: 8, SEG_VA: 9}
SEG_GAIN_ROW = {SEG_KA: 0, SEG_KB: 1, SEG_QA: 2, SEG_QB: 3}
SEG_ROPE = (SEG_KB, SEG_QB)
SEG_T_SLAB = {SEG_VA: LANES, SEG_VB: GQA_K_ROWS}


def _dot(a, b):
    return jnp.dot(a, b, preferred_element_type=F32)


def _dot_t(a, b):
    return lax.dot_general(a, b, (((1,), (1,)), ((), ())), preferred_element_type=F32)


def _params(semantics):
    return pltpu.CompilerParams(dimension_semantics=semantics, vmem_limit_bytes=VMEM_LIMIT)


def _mod_kernel(c_ref, w_ref, b_ref, o_ref):
    c = c_ref[...]
    a = c * jax.nn.sigmoid(c)
    o_ref[0] = jnp.dot(a, w_ref[0], preferred_element_type=F32,
                       precision=lax.Precision.HIGHEST) + b_ref[0]


def _modulation(c_rows, w_ada, b_ada):
    depth = w_ada.shape[0]
    rows = c_rows.shape[0]
    ncol = w_ada.shape[2] // D_MODEL
    return pl.pallas_call(
        _mod_kernel,
        grid=(depth, ncol),
        in_specs=[
            pl.BlockSpec((rows, D_MODEL), lambda l, j: (0, 0)),
            pl.BlockSpec((1, D_MODEL, D_MODEL), lambda l, j: (l, 0, j)),
            pl.BlockSpec((1, 1, D_MODEL), lambda l, j: (l, 0, j)),
        ],
        out_specs=pl.BlockSpec((1, rows, D_MODEL), lambda l, j: (l, 0, j)),
        out_shape=jax.ShapeDtypeStruct((depth, rows, ncol * D_MODEL), F32),
        compiler_params=_params(("arbitrary", "arbitrary")),
        name="mod",
    )(c_rows, w_ada, b_ada.reshape(depth, 1, -1))


def _head_norm(p, gsum, gain):
    sq = (p * p).astype(BF16)
    cols = min(MXU_COLS, p.shape[1])
    parts = []
    for j in range(p.shape[1] // cols):
        sl = slice(j * cols, (j + 1) * cols)
        ss = _dot(sq[:, sl], gsum[:cols, :cols])
        parts.append(p[:, sl] * lax.rsqrt(ss * (1.0 / HEAD_DIM) + EPS))
    y = parts[0] if len(parts) == 1 else jnp.concatenate(parts, axis=1)
    return y * gain


def _rope(y, cos, sin):
    width = y.shape[1]
    lane = lax.broadcasted_iota(jnp.int32, y.shape, 1)
    ahead = pltpu.roll(y, width - ROT_HALF, axis=1)
    behind = pltpu.roll(y, ROT_HALF, axis=1)
    partner = jnp.where((lane % ROT_AXIS) < ROT_HALF, ahead, behind)
    reps = width // LANES
    cos = cos if reps == 1 else jnp.concatenate([cos] * reps, axis=1)
    sin = sin if reps == 1 else jnp.concatenate([sin] * reps, axis=1)
    return y * cos + partner * sin


def _duplicate_heads(y):
    lo = lax.broadcasted_iota(jnp.int32, y.shape, 1) < HEAD_DIM
    swapped = pltpu.roll(y, HEAD_DIM, axis=1)
    return jnp.concatenate([jnp.where(lo, y, swapped), jnp.where(lo, swapped, y)], axis=1)


def _proj_kernel(x_ref, shift_ref, scale_ref, g_ref, w_ref, gsum_ref, gain_ref, cos_ref, sin_ref,
                 *out_refs, nseg):
    x = x_ref[0]
    ms = jnp.mean(x * x, axis=-1, keepdims=True)
    gmod = g_ref[...] * (1.0 + scale_ref[0])
    h = (x * lax.rsqrt(ms + EPS) * gmod + shift_ref[0]).astype(BF16)
    gsum = gsum_ref[...]
    for seg in sorted(range(nseg), key=SEG_ISSUE_RANK.__getitem__):
        width = SEG_WIDTHS[seg]
        p = _dot(h, w_ref[:, SEG_OFFSETS[seg]:SEG_OFFSETS[seg + 1]])
        if seg in SEG_GAIN_ROW:
            row = SEG_GAIN_ROW[seg]
            p = _head_norm(p, gsum, gain_ref[row:row + 1, :width])
        if seg in SEG_ROPE:
            p = _rope(p, cos_ref[...], sin_ref[...])
        if seg == SEG_KB:
            p = _duplicate_heads(p)
        if seg in (SEG_ZA, SEG_ZB):
            p = p * jax.nn.sigmoid(p)
        if seg in (SEG_GA, SEG_GB):
            p = jax.nn.sigmoid(p)
        if seg in SEG_T_SLAB:
            pt = p.T.astype(BF16)
            slab = min(SEG_T_SLAB[seg], pt.shape[1])
            for j in range(pt.shape[1] // slab):
                out_refs[seg][0, j] = pt[:, j * slab:(j + 1) * slab]
        else:
            out_refs[seg][0] = p.astype(BF16)


def _project(x, shift, scale, norm_g, w_all, layer, gsum, gains, cos, sin, nseg, rows):
    groups, total, _ = x.shape
    ncols = SEG_OFFSETS[nseg]
    const = lambda g, i: (0, 0)

    def out_layout(s):
        width = SEG_OUT_WIDTHS[s]
        if s not in SEG_T_SLAB:
            return (groups, total, width), pl.BlockSpec((1, rows, width), lambda g, i: (g, i, 0))
        slab = min(SEG_T_SLAB[s], rows)
        return ((groups, total // slab, width, slab),
                pl.BlockSpec((1, rows // slab, width, slab), lambda g, i: (g, i, 0, 0)))

    layouts = [out_layout(s) for s in range(nseg)]
    return pl.pallas_call(
        functools.partial(_proj_kernel, nseg=nseg),
        grid=(groups, total // rows),
        in_specs=[
            pl.BlockSpec((1, rows, D_MODEL), lambda g, i: (g, i, 0)),
            pl.BlockSpec((1, 1, D_MODEL), lambda g, i: (g, 0, 0)),
            pl.BlockSpec((1, 1, D_MODEL), lambda g, i: (g, 0, 0)),
            pl.BlockSpec((1, D_MODEL), const),
            pl.BlockSpec((None, D_MODEL, ncols), lambda g, i: (layer, 0, 0), pipeline_mode=pl.Buffered(1)),
            pl.BlockSpec((MXU_COLS, MXU_COLS), const),
            pl.BlockSpec(gains.shape, const),
            pl.BlockSpec((rows, LANES), lambda g, i: (i, 0)),
            pl.BlockSpec((rows, LANES), lambda g, i: (i, 0)),
        ],
        out_specs=[spec for _, spec in layouts],
        out_shape=[jax.ShapeDtypeStruct(shape, BF16) for shape, _ in layouts],
        compiler_params=_params(("arbitrary", "arbitrary")),
        name="proj",
    )(x, shift, scale, norm_g, w_all, gsum, gains, cos, sin)


def _stack_heads(pair):
    lo = lax.broadcasted_iota(jnp.int32, pair.shape, 1) < HEAD_DIM
    zero = jnp.zeros_like(pair)
    return jnp.concatenate([jnp.where(lo, pair, zero), jnp.where(lo, zero, pair)], axis=0)


def _stack_quad(q):
    head = lax.broadcasted_iota(jnp.int32, q.shape, 1) // HEAD_DIM
    zero = jnp.zeros_like(q)
    return jnp.concatenate([jnp.where(head == h, q, zero) for h in range(QUAD)], axis=0)


def _unstack_quad_t(o, rows):
    ot = o.T
    head = lax.broadcasted_iota(jnp.int32, (rows, QUAD * HEAD_DIM), 1) // HEAD_DIM
    out = ot[:rows]
    for h in range(1, QUAD):
        out = jnp.where(head == h, ot[h * rows:(h + 1) * rows], out)
    return out


def _pv_t(vt, s, m):
    lhs = jnp.concatenate([vt, jnp.ones((ONES_ROWS, vt.shape[1]), vt.dtype)], axis=0)
    return _dot(lhs, jnp.exp2((s - m).astype(BF16)))


def _normalise_t(acc, dims):
    return acc[:dims] * (1.0 / acc[dims:dims + 1])


def _softmax_pv_t(logits, values_t):
    m = functools.reduce(jnp.maximum, [jnp.max(s, axis=0, keepdims=True) for s in logits])
    acc = functools.reduce(jnp.add, [_pv_t(vt, s, m) for vt, s in zip(values_t, logits)])
    return _normalise_t(acc, values_t[0].shape[0])


def _na_window(r, rows):
    start = jnp.clip(r - WIN_R // 2, 0, rows - WIN_R)
    slab0 = jnp.minimum(start // 2, rows // 2 - NA_SLABS)
    first = 2 * slab0
    blocks = []
    for w in range(2 * NA_SLABS):
        key_row = first + w
        valid = (key_row >= start) & (key_row < start + WIN_R)
        blocks.append(jnp.where(valid, key_row - r + (WIN_R - 1), 2 * WIN_R - 1))
    return slab0, blocks


def _na_kernel(q_ref, k_ref, vt_ref, kc_ref, vct_ref, t_ref, o_ref, qt_ref, *, rows, unroll):
    kc = kc_ref[0]
    vct = jnp.concatenate([vct_ref[0, j] for j in range(vct_ref.shape[1])], axis=1)
    win_keys = NA_SLABS * LANES

    def logits(r, slot):
        slab0, blocks = _na_window(r, rows)
        qt_ref[slot] = _stack_quad(q_ref[0, pl.ds(pl.multiple_of(r * GRID_W, GRID_W), GRID_W), :]).T
        qs_t = qt_ref[slot]
        kw = k_ref[0, pl.ds(pl.multiple_of(slab0 * LANES, LANES), win_keys), :]
        bias = jnp.concatenate([t_ref[b] for b in blocks], axis=0)
        return _dot(kw, qs_t) + bias, _dot(kc, qs_t), slab0

    def finish(r, s_win, s_ctx, slab0):
        vt = vt_ref[0, pl.ds(slab0, NA_SLABS)]
        vtw = jnp.concatenate([vt[j] for j in range(NA_SLABS)], axis=1)
        o = _softmax_pv_t([s_win, s_ctx], [vtw, vct])
        o_ref[0, pl.ds(pl.multiple_of(r * GRID_W, GRID_W), GRID_W), :] = _unstack_quad_t(o, GRID_W).astype(BF16)

    def body(i, carry):
        pending = []
        for e in range(unroll):
            pending.append((i * unroll + e,) + logits(i * unroll + e, e))
            if len(pending) > NA_AHEAD:
                finish(*pending.pop(0))
        for args in pending:
            finish(*args)
        return carry

    lax.fori_loop(0, rows // unroll, body, 0)


def _neighbourhood_attention(qa, ka, vat, cka, cvat, tables, layer):
    batch, seq, _ = qa.shape
    ctx_len = cka.shape[1]
    rows = seq // GRID_W
    width = QUAD * HEAD_DIM
    blk = lambda n: pl.BlockSpec((1, n, width), lambda b, j: (b, 0, j))
    vblk = lambda a: pl.BlockSpec((1, a.shape[1], width, LANES), lambda b, j: (b, 0, j, 0))
    return pl.pallas_call(
        functools.partial(_na_kernel, rows=rows, unroll=NA_UNROLL),
        grid=(batch, NA_HEADS // QUAD),
        in_specs=[blk(seq), blk(seq), vblk(vat), blk(ctx_len), vblk(cvat),
                  pl.BlockSpec((None, None) + tables.shape[2:], lambda b, j: (layer, j, 0, 0, 0))],
        out_specs=blk(seq),
        out_shape=jax.ShapeDtypeStruct((batch, seq, NA_WIDTH), BF16),
        scratch_shapes=[pltpu.VMEM((NA_UNROLL, width, QUAD * GRID_W), BF16)],
        compiler_params=_params(("arbitrary", "arbitrary")),
        name="na",
    )(qa, ka, vat, cka, cvat, tables)


def _bias_kernel(rpb_ref, onehot_ref, mask_ref, o_ref):
    o_ref[0] = (jnp.dot(rpb_ref[0], onehot_ref[...], preferred_element_type=F32,
                        precision=lax.Precision.HIGHEST) + mask_ref[...]) * LOG2E


def _bias_tables(rpb):
    depth = rpb.shape[0]
    n_row_off, n_col_off = 2 * WIN_R - 1, 2 * WIN_C - 1
    cols = np.arange(GRID_W)
    cstart = np.clip(cols - WIN_C // 2, 0, GRID_W - WIN_C)
    kcol = np.arange(GRID_W)
    valid = (kcol[None, :] >= cstart[:, None]) & (kcol[None, :] < cstart[:, None] + WIN_C)
    col_off = kcol[None, :] - cols[:, None] + (WIN_C - 1)
    pad_r, pad_c = -n_row_off % 8, -n_col_off % 8
    assert pad_r >= 1
    onehot = (np.arange(n_col_off + pad_c)[:, None, None] == col_off[None]) & valid[None]
    onehot = jnp.asarray(onehot.reshape(n_col_off + pad_c, GRID_W * GRID_W), dtype=F32)
    mask = jnp.asarray(np.where(valid, 0.0, MASKED).reshape(1, GRID_W * GRID_W), dtype=F32)
    heads = depth * NA_HEADS
    rpb_p = jnp.pad(rpb.reshape(heads, n_row_off, n_col_off), ((0, 0), (0, pad_r), (0, pad_c)))
    toep = pl.pallas_call(
        _bias_kernel,
        grid=(heads,),
        in_specs=[pl.BlockSpec((1,) + rpb_p.shape[1:], lambda h: (h, 0, 0)),
                  pl.BlockSpec(onehot.shape, lambda h: (0, 0)),
                  pl.BlockSpec(mask.shape, lambda h: (0, 0))],
        out_specs=pl.BlockSpec((1, n_row_off + pad_r, GRID_W * GRID_W), lambda h: (h, 0, 0)),
        out_shape=jax.ShapeDtypeStruct((heads, n_row_off + pad_r, GRID_W * GRID_W), F32),
        compiler_params=_params(("arbitrary",)),
        name="bias",
    )(rpb_p, onehot, mask)
    groups = NA_HEADS // QUAD
    toep = toep.reshape(depth, groups, QUAD, n_row_off + pad_r, GRID_W, GRID_W)
    table = jnp.transpose(toep, (0, 1, 3, 5, 2, 4)).reshape(depth, groups, -1, GRID_W, QUAD * GRID_W)
    return jnp.where((jnp.arange(table.shape[2]) < n_row_off)[None, None, :, None, None], table, MASKED * LOG2E)


def _stack_group(q):
    return jnp.concatenate([_stack_heads(q[:, :LANES]), _stack_heads(q[:, LANES:])], axis=0)


def _unstack_group_t(o, rows):
    pairs = [jnp.concatenate([o[:, (2 * j) * rows:(2 * j + 1) * rows],
                              o[:, (2 * j + 1) * rows:(2 * j + 2) * rows]], axis=0).T for j in range(2)]
    return jnp.concatenate(pairs, axis=1)


def _scores_t(qs_t, k):
    return _dot(k, qs_t)


def _attend_t(s, vt, state):
    m_chunk = jnp.max(s, axis=0, keepdims=True)
    if state is None:
        return m_chunk, _pv_t(vt, s, m_chunk)
    m_old, acc = state
    m = jnp.maximum(m_old, m_chunk)
    return m, jnp.exp2(m_old - m) * acc + _pv_t(vt, s, m)


def _gqa_kernel(q_ref, k_ref, vt_ref, kc_ref, vct_ref, o_ref, qt_ref, *, chunks, tq, tk):
    qt_ref[...] = _stack_group(q_ref[0]).T
    qs = qt_ref[...]
    pending, state = [(_scores_t(qs, kc_ref[0]), vct_ref[0, 0])], None
    for c in range(chunks):
        pending.append((_scores_t(qs, k_ref[0, c * tk:(c + 1) * tk, :]), vt_ref[0, c]))
        if len(pending) > GQA_AHEAD:
            state = _attend_t(*pending.pop(0), state)
    for s, vt in pending:
        state = _attend_t(s, vt, state)
    _, acc = state
    o_ref[0] = _unstack_group_t(_normalise_t(acc, HEAD_DIM), tq).astype(BF16)


def _gqa_attention(qb, kbd, vbt, ckbd, cvbt):
    batch, seq, _ = qb.shape
    ctx_len = ckbd.shape[1]
    tq, tk = GQA_Q_ROWS, vbt.shape[3]
    kspec = lambda n: pl.BlockSpec((1, n, LANES), lambda b, g, i: (b, 0, g))
    vspec = lambda a: pl.BlockSpec((1, a.shape[1], HEAD_DIM, a.shape[3]), lambda b, g, i: (b, 0, g, 0))
    qspec = pl.BlockSpec((1, tq, 2 * LANES), lambda b, g, i: (b, i, g))
    return pl.pallas_call(
        functools.partial(_gqa_kernel, chunks=seq // tk, tq=tq, tk=tk),
        grid=(batch, GQA_KV_HEADS, seq // tq),
        in_specs=[qspec, kspec(seq), vspec(vbt), kspec(ctx_len), vspec(cvbt)],
        out_specs=qspec,
        out_shape=jax.ShapeDtypeStruct((batch, seq, GQA_WIDTH), BF16),
        scratch_shapes=[pltpu.VMEM((LANES, (GQA_Q_HEADS // GQA_KV_HEADS) * tq), BF16)],
        compiler_params=_params(("arbitrary", "arbitrary", "arbitrary")),
        name="gqa",
    )(qb, kbd, vbt, ckbd, cvbt)


def _ctx_kernel(qa_ref, ka_ref, vat_ref, qb_ref, kb_ref, vbt_ref, oa_ref, ob_ref):
    n = qa_ref.shape[1]
    width = QUAD * HEAD_DIM
    for g in range(NA_HEADS // QUAD):
        sl = slice(g * width, (g + 1) * width)
        vt = jnp.concatenate([vat_ref[0, j, sl, :] for j in range(vat_ref.shape[1])], axis=1)
        o = _softmax_pv_t([_dot_t(ka_ref[0, :, sl], _stack_quad(qa_ref[0, :, sl]))], [vt])
        oa_ref[0, :, sl] = _unstack_quad_t(o, n).astype(BF16)
    for g in range(GQA_KV_HEADS):
        sl = slice(g * 2 * LANES, (g + 1) * 2 * LANES)
        s = _scores_t(_stack_group(qb_ref[0, :, sl]).T, kb_ref[0, :, g * LANES:(g + 1) * LANES])
        o = _softmax_pv_t([s], [vbt_ref[0, 0, g * HEAD_DIM:(g + 1) * HEAD_DIM, :]])
        ob_ref[0, :, sl] = _unstack_group_t(o, n).astype(BF16)


def _context_attention(cqa, cka, cvat, cqb, ckbd, cvbt):
    batch, n, _ = cqa.shape
    spec = lambda a: pl.BlockSpec((1,) + a.shape[1:], lambda b: (b,) + (0,) * (a.ndim - 1))
    args = (cqa, cka, cvat, cqb, ckbd, cvbt)
    return pl.pallas_call(
        _ctx_kernel,
        grid=(batch,),
        in_specs=[spec(a) for a in args],
        out_specs=[spec(cqa), spec(cqb)],
        out_shape=[jax.ShapeDtypeStruct(cqa.shape, BF16), jax.ShapeDtypeStruct(cqb.shape, BF16)],
        compiler_params=_params(("arbitrary",)),
        name="ctx",
    )(*args)


def _merge_kernel(a_ref, b_ref, za_ref, zb_ref, ga_ref, gb_ref, x_ref, gate_ref,
                  woa_ref, wob_ref, wout_ref, o_ref):
    o_a = _dot(a_ref[0] * za_ref[0], woa_ref[...])
    o_b = _dot(b_ref[0] * zb_ref[0], wob_ref[...])
    merged = ga_ref[0].astype(F32) * o_a + gb_ref[0].astype(F32) * o_b
    o_ref[0] = x_ref[0] + gate_ref[0] * _dot(merged.astype(BF16), wout_ref[...])


def _merge(a_att, b_att, za, zb, ga, gb, x, gate, w_o_a, w_o_b, w_out, layer, rows):
    groups, total, _ = x.shape
    act = lambda w: pl.BlockSpec((1, rows, w), lambda g, i: (g, i, 0))
    const = lambda a: pl.BlockSpec((None,) + a.shape[1:], lambda g, i: (layer, 0, 0))
    return pl.pallas_call(
        _merge_kernel,
        grid=(groups, total // rows),
        in_specs=[act(NA_WIDTH), act(GQA_WIDTH), act(NA_WIDTH), act(GQA_WIDTH), act(D_MODEL), act(D_MODEL),
                  act(D_MODEL), pl.BlockSpec((1, 1, D_MODEL), lambda g, i: (g, 0, 0)),
                  const(w_o_a), const(w_o_b), const(w_out)],
        out_specs=act(D_MODEL),
        out_shape=jax.ShapeDtypeStruct(x.shape, F32),
        compiler_params=_params(("arbitrary", "arbitrary")),
        name="merge",
    )(a_att, b_att, za, zb, ga, gb, x, gate, w_o_a, w_o_b, w_out)


def _rope_tables(seq):
    t = jnp.arange(seq, dtype=jnp.int32)
    inv = 1.0 / (ROPE_THETA ** (jnp.arange(ROT_HALF, dtype=F32) / ROT_HALF))
    ang_r = (t // GRID_W).astype(F32)[:, None] * inv[None, :]
    ang_c = (t % GRID_W).astype(F32)[:, None] * inv[None, :]
    cos = jnp.concatenate([jnp.cos(ang_r)] * 2 + [jnp.cos(ang_c)] * 2, axis=1)
    sin = jnp.concatenate([-jnp.sin(ang_r), jnp.sin(ang_r), -jnp.sin(ang_c), jnp.sin(ang_c)], axis=1)
    reps = LANES // HEAD_DIM
    return jnp.tile(cos, (1, reps)), jnp.tile(sin, (1, reps))


def _group_sum_matrix():
    idx = np.arange(MXU_COLS) // HEAD_DIM
    return jnp.asarray(idx[:, None] == idx[None, :], dtype=BF16)


def kernel(x, c, ctx, c_ctx, w_ada, b_ada, norm_g, w_in, q_norm_a, k_norm_a, q_norm_b, k_norm_b,
           rpb, w_o_a, w_o_b, w_out):
    batch, seq, _ = x.shape
    ctx_len = ctx.shape[1]
    depth = w_ada.shape[0]
    rows = seq // GRID_W
    assert seq % GRID_W == 0 and rows >= WIN_R and seq % PROJ_ROWS == 0 and seq % GQA_Q_ROWS == 0

    pad = -(batch + 1) % 8
    c_rows = jnp.concatenate([c, c_ctx[None, :], jnp.zeros((pad, D_MODEL), F32)], axis=0)
    mod = _modulation(c_rows, w_ada, b_ada)

    cos_x, sin_x = _rope_tables(seq)
    cos_c = jnp.ones((ctx_len, LANES), F32)
    sin_c = jnp.zeros((ctx_len, LANES), F32)
    gsum = _group_sum_matrix()
    tables = _bias_tables(rpb)
    w_in, w_o_a, w_o_b, w_out = (w.astype(BF16) for w in (w_in, w_o_a, w_o_b, w_out))

    for l in range(depth):
        update_ctx = l < depth - 1
        shift, scale, gate = (mod[l, :, i * D_MODEL:(i + 1) * D_MODEL] for i in range(3))
        per_batch = lambda m: m[:batch, None, :]
        for_ctx = lambda m: jnp.broadcast_to(m[batch][None, None, :], (batch, 1, D_MODEL))
        tile = lambda g, n: jnp.tile(g, n)
        gains = jnp.stack([tile(k_norm_a[l], NA_HEADS), tile(k_norm_b[l], NA_HEADS),
                           tile(q_norm_a[l], NA_HEADS) * (QK_SCALE * LOG2E),
                           tile(q_norm_b[l], GQA_Q_HEADS) * (QK_SCALE * LOG2E)])
        g_row = norm_g[l][None, :]

        ka, vat, kbd, vbt, qa, qb, za, zb, ga, gb = _project(
            x, per_batch(shift), per_batch(scale), g_row, w_in, l, gsum, gains, cos_x, sin_x,
            len(SEG_WIDTHS), PROJ_ROWS)
        nseg_c = len(SEG_WIDTHS) if update_ctx else KV_SEGS
        pc = _project(ctx, for_ctx(shift), for_ctx(scale), g_row, w_in, l, gsum, gains, cos_c, sin_c,
                      nseg_c, ctx_len)
        cka, cvat, ckbd, cvbt = pc[:KV_SEGS]

        a_att = _neighbourhood_attention(qa, ka, vat, cka, cvat, tables, l)
        b_att = _gqa_attention(qb, kbd, vbt, ckbd, cvbt)
        x_new = _merge(a_att, b_att, za, zb, ga, gb, x, per_batch(gate), w_o_a, w_o_b, w_out, l, MERGE_ROWS)

        if update_ctx:
            _, _, _, _, cqa, cqb, cza, czb, cga, cgb = pc
            c_a, c_b = _context_attention(cqa, cka, cvat, cqb, ckbd, cvbt)
            ctx = _merge(c_a, c_b, cza, czb, cga, cgb, ctx, for_ctx(gate), w_o_a, w_o_b, w_out, l, ctx_len)
        x = x_new
    return x
```

```python
import functools

import numpy as np
import jax
import jax.numpy as jnp
from jax import lax
from jax.experimental import pallas as pl
from jax.experimental.pallas import tpu as pltpu

D_MODEL = 1024
GRID_W = 64
HEAD_DIM = 64
NA_HEADS = 8
NA_WIDTH = NA_HEADS * HEAD_DIM
WIN_R = 8
WIN_C = 16
GQA_Q_HEADS = 8
GQA_KV_HEADS = 2
GQA_WIDTH = GQA_Q_HEADS * HEAD_DIM
GQA_KV_WIDTH = GQA_KV_HEADS * HEAD_DIM
ROPE_THETA = 10000.0
ROT_AXIS = HEAD_DIM // 2
ROT_HALF = ROT_AXIS // 2
EPS = 1e-6
QK_SCALE = HEAD_DIM ** -0.5
LOG2E = 1.4426950408889634

LANES = 128
MXU_COLS = 256
MASKED = -1e30
VMEM_LIMIT = 56 * 1024 * 1024

PROJ_ROWS = 512
MERGE_ROWS = 1024
GQA_Q_ROWS = 256
GQA_Q_BLOCKS = 2
GQA_K_ROWS = 256
GQA_AHEAD = 1
QUAD = 4
ONES_ROWS = 16
NA_SLABS = WIN_R // 2 + 1
NA_AHEAD = 4
NA_UNROLL = 8

F32 = jnp.float32
BF16 = jnp.bfloat16

SEG_KA, SEG_VA, SEG_KB, SEG_VB, SEG_QA, SEG_QB, SEG_ZA, SEG_ZB, SEG_GA, SEG_GB = range(10)
SEG_WIDTHS = (NA_WIDTH, NA_WIDTH, GQA_KV_WIDTH, GQA_KV_WIDTH, NA_WIDTH, GQA_WIDTH,
              NA_WIDTH, GQA_WIDTH, D_MODEL, D_MODEL)
SEG_OFFSETS = tuple(int(v) for v in np.cumsum((0,) + SEG_WIDTHS))
SEG_OUT_WIDTHS = tuple(2 * w if s == SEG_KB else w for s, w in enumerate(SEG_WIDTHS))
KV_SEGS = 4
SEG_ISSUE_RANK = {SEG_GA: 0, SEG_GB: 1, SEG_ZA: 2, SEG_ZB: 3, SEG_QB: 4, SEG_KB: 5, SEG_QA: 6, SEG_KA: 7,
                  SEG_VB: 8, SEG_VA: 9}
SEG_GAIN_ROW = {SEG_KA: 0, SEG_KB: 1, SEG_QA: 2, SEG_QB: 3}
SEG_ROPE = (SEG_KB, SEG_QB)
SEG_T_SLAB = {SEG_VA: LANES, SEG_VB: GQA_K_ROWS}


def _dot(a, b):
    return jnp.dot(a, b, preferred_element_type=F32)


def _dot_t(a, b):
    return lax.dot_general(a, b, (((1,), (1,)), ((), ())), preferred_element_type=F32)


def _params(semantics):
    return pltpu.CompilerParams(dimension_semantics=semantics, vmem_limit_bytes=VMEM_LIMIT)


def _mod_kernel(c_ref, w_ref, b_ref, o_ref):
    c = c_ref[...]
    a = c * jax.nn.sigmoid(c)
    o_ref[0] = jnp.dot(a, w_ref[0], preferred_element_type=F32,
                       precision=lax.Precision.HIGHEST) + b_ref[0]


def _modulation(c_rows, w_ada, b_ada):
    depth = w_ada.shape[0]
    rows = c_rows.shape[0]
    ncol = w_ada.shape[2] // D_MODEL
    return pl.pallas_call(
        _mod_kernel,
        grid=(depth, ncol),
        in_specs=[
            pl.BlockSpec((rows, D_MODEL), lambda l, j: (0, 0)),
            pl.BlockSpec((1, D_MODEL, D_MODEL), lambda l, j: (l, 0, j)),
            pl.BlockSpec((1, 1, D_MODEL), lambda l, j: (l, 0, j)),
        ],
        out_specs=pl.BlockSpec((1, rows, D_MODEL), lambda l, j: (l, 0, j)),
        out_shape=jax.ShapeDtypeStruct((depth, rows, ncol * D_MODEL), F32),
        compiler_params=_params(("arbitrary", "arbitrary")),
        name="mod",
    )(c_rows, w_ada, b_ada.reshape(depth, 1, -1))


def _head_norm(p, gsum, gain):
    sq = (p * p).astype(BF16)
    cols = min(MXU_COLS, p.shape[1])
    parts = []
    for j in range(p.shape[1] // cols):
        sl = slice(j * cols, (j + 1) * cols)
        ss = _dot(sq[:, sl], gsum[:cols, :cols])
        parts.append(p[:, sl] * lax.rsqrt(ss * (1.0 / HEAD_DIM) + EPS))
    y = parts[0] if len(parts) == 1 else jnp.concatenate(parts, axis=1)
    return y * gain


def _rope(y, cos, sin):
    width = y.shape[1]
    lane = lax.broadcasted_iota(jnp.int32, y.shape, 1)
    ahead = pltpu.roll(y, width - ROT_HALF, axis=1)
    behind = pltpu.roll(y, ROT_HALF, axis=1)
    partner = jnp.where((lane % ROT_AXIS) < ROT_HALF, ahead, behind)
    reps = width // LANES
    cos = cos if reps == 1 else jnp.concatenate([cos] * reps, axis=1)
    sin = sin if reps == 1 else jnp.concatenate([sin] * reps, axis=1)
    return y * cos + partner * sin


def _duplicate_heads(y):
    lo = lax.broadcasted_iota(jnp.int32, y.shape, 1) < HEAD_DIM
    swapped = pltpu.roll(y, HEAD_DIM, axis=1)
    return jnp.concatenate([jnp.where(lo, y, swapped), jnp.where(lo, swapped, y)], axis=1)


def _proj_kernel(x_ref, shift_ref, scale_ref, g_ref, w_ref, gsum_ref, gain_ref, cos_ref, sin_ref,
                 *out_refs, nseg):
    x = x_ref[0]
    ms = jnp.mean(x * x, axis=-1, keepdims=True)
    gmod = g_ref[...] * (1.0 + scale_ref[0])
    h = (x * lax.rsqrt(ms + EPS) * gmod + shift_ref[0]).astype(BF16)
    gsum = gsum_ref[...]
    for seg in sorted(range(nseg), key=SEG_ISSUE_RANK.__getitem__):
        width = SEG_WIDTHS[seg]
        p = _dot(h, w_ref[:, SEG_OFFSETS[seg]:SEG_OFFSETS[seg + 1]])
        if seg in SEG_GAIN_ROW:
            row = SEG_GAIN_ROW[seg]
            p = _head_norm(p, gsum, gain_ref[row:row + 1, :width])
        if seg in SEG_ROPE:
            p = _rope(p, cos_ref[...], sin_ref[...])
        if seg == SEG_KB:
            p = _duplicate_heads(p)
        if seg in (SEG_ZA, SEG_ZB):
            p = p * jax.nn.sigmoid(p)
        if seg in (SEG_GA, SEG_GB):
            p = jax.nn.sigmoid(p)
        if seg in SEG_T_SLAB:
            pt = p.T.astype(BF16)
            slab = min(SEG_T_SLAB[seg], pt.shape[1])
            for j in range(pt.shape[1] // slab):
                out_refs[seg][0, j] = pt[:, j * slab:(j + 1) * slab]
        else:
            out_refs[seg][0] = p.astype(BF16)


def _project(x, shift, scale, norm_g, w_all, layer, gsum, gains, cos, sin, nseg, rows):
    groups, total, _ = x.shape
    ncols = SEG_OFFSETS[nseg]
    const = lambda g, i: (0, 0)

    def out_layout(s):
        width = SEG_OUT_WIDTHS[s]
        if s not in SEG_T_SLAB:
            return (groups, total, width), pl.BlockSpec((1, rows, width), lambda g, i: (g, i, 0))
        slab = min(SEG_T_SLAB[s], rows)
        return ((groups, total // slab, width, slab),
                pl.BlockSpec((1, rows // slab, width, slab), lambda g, i: (g, i, 0, 0)))

    layouts = [out_layout(s) for s in range(nseg)]
    return pl.pallas_call(
        functools.partial(_proj_kernel, nseg=nseg),
        grid=(groups, total // rows),
        in_specs=[
            pl.BlockSpec((1, rows, D_MODEL), lambda g, i: (g, i, 0)),
            pl.BlockSpec((1, 1, D_MODEL), lambda g, i: (g, 0, 0)),
            pl.BlockSpec((1, 1, D_MODEL), lambda g, i: (g, 0, 0)),
            pl.BlockSpec((1, D_MODEL), const),
            pl.BlockSpec((None, D_MODEL, ncols), lambda g, i: (layer, 0, 0), pipeline_mode=pl.Buffered(1)),
            pl.BlockSpec((MXU_COLS, MXU_COLS), const),
            pl.BlockSpec(gains.shape, const),
            pl.BlockSpec((rows, LANES), lambda g, i: (i, 0)),
            pl.BlockSpec((rows, LANES), lambda g, i: (i, 0)),
        ],
        out_specs=[spec for _, spec in layouts],
        out_shape=[jax.ShapeDtypeStruct(shape, BF16) for shape, _ in layouts],
        compiler_params=_params(("arbitrary", "arbitrary")),
        name="proj",
    )(x, shift, scale, norm_g, w_all, gsum, gains, cos, sin)


def _stack_heads(pair):
    lo = lax.broadcasted_iota(jnp.int32, pair.shape, 1) < HEAD_DIM
    zero = jnp.zeros_like(pair)
    return jnp.concatenate([jnp.where(lo, pair, zero), jnp.where(lo, zero, pair)], axis=0)


def _stack_quad(q):
    head = lax.broadcasted_iota(jnp.int32, q.shape, 1) // HEAD_DIM
    zero = jnp.zeros_like(q)
    return jnp.concatenate([jnp.where(head == h, q, zero) for h in range(QUAD)], axis=0)


def _unstack_quad_t(o, rows):
    ot = o.T
    head = lax.broadcasted_iota(jnp.int32, (rows, QUAD * HEAD_DIM), 1) // HEAD_DIM
    out = ot[:rows]
    for h in range(1, QUAD):
        out = jnp.where(head == h, ot[h * rows:(h + 1) * rows], out)
    return out


def _pv_t(vt, s, m):
    lhs = jnp.concatenate([vt, jnp.ones((ONES_ROWS, vt.shape[1]), vt.dtype)], axis=0)
    return _dot(lhs, jnp.exp2((s - m).astype(BF16)))


def _normalise_t(acc, dims):
    return acc[:dims] * (1.0 / acc[dims:dims + 1])


def _softmax_pv_t(logits, values_t):
    m = functools.reduce(jnp.maximum, [jnp.max(s, axis=0, keepdims=True) for s in logits])
    acc = functools.reduce(jnp.add, [_pv_t(vt, s, m) for vt, s in zip(values_t, logits)])
    return _normalise_t(acc, values_t[0].shape[0])


def _na_window(r, rows):
    start = jnp.clip(r - WIN_R // 2, 0, rows - WIN_R)
    slab0 = jnp.minimum(start // 2, rows // 2 - NA_SLABS)
    first = 2 * slab0
    blocks = []
    for w in range(2 * NA_SLABS):
        key_row = first + w
        valid = (key_row >= start) & (key_row < start + WIN_R)
        blocks.append(jnp.where(valid, key_row - r + (WIN_R - 1), 2 * WIN_R - 1))
    return slab0, blocks


def _na_kernel(q_ref, k_ref, vt_ref, kc_ref, vct_ref, t_ref, o_ref, qt_ref, *, rows, unroll):
    kc = kc_ref[0]
    vct = jnp.concatenate([vct_ref[0, j] for j in range(vct_ref.shape[1])], axis=1)
    win_keys = NA_SLABS * LANES

    def logits(r, slot):
        slab0, blocks = _na_window(r, rows)
        qt_ref[slot] = _stack_quad(q_ref[0, pl.ds(pl.multiple_of(r * GRID_W, GRID_W), GRID_W), :]).T
        qs_t = qt_ref[slot]
        kw = k_ref[0, pl.ds(pl.multiple_of(slab0 * LANES, LANES), win_keys), :]
        bias = jnp.concatenate([t_ref[b] for b in blocks], axis=0)
        return _dot(kw, qs_t) + bias, _dot(kc, qs_t), slab0

    def finish(r, s_win, s_ctx, slab0):
        vt = vt_ref[0, pl.ds(slab0, NA_SLABS)]
        vtw = jnp.concatenate([vt[j] for j in range(NA_SLABS)], axis=1)
        o = _softmax_pv_t([s_win, s_ctx], [vtw, vct])
        o_ref[0, pl.ds(pl.multiple_of(r * GRID_W, GRID_W), GRID_W), :] = _unstack_quad_t(o, GRID_W).astype(BF16)

    def body(i, carry):
        pending = []
        for e in range(unroll):
            pending.append((i * unroll + e,) + logits(i * unroll + e, e))
            if len(pending) > NA_AHEAD:
                finish(*pending.pop(0))
        for args in pending:
            finish(*args)
        return carry

    lax.fori_loop(0, rows // unroll, body, 0)


def _neighbourhood_attention(qa, ka, vat, cka, cvat, tables, layer):
    batch, seq, _ = qa.shape
    ctx_len = cka.shape[1]
    rows = seq // GRID_W
    width = QUAD * HEAD_DIM
    blk = lambda n: pl.BlockSpec((1, n, width), lambda b, j: (b, 0, j))
    vblk = lambda a: pl.BlockSpec((1, a.shape[1], width, LANES), lambda b, j: (b, 0, j, 0))
    return pl.pallas_call(
        functools.partial(_na_kernel, rows=rows, unroll=NA_UNROLL),
        grid=(batch, NA_HEADS // QUAD),
        in_specs=[blk(seq), blk(seq), vblk(vat), blk(ctx_len), vblk(cvat),
                  pl.BlockSpec((None, None) + tables.shape[2:], lambda b, j: (layer, j, 0, 0, 0))],
        out_specs=blk(seq),
        out_shape=jax.ShapeDtypeStruct((batch, seq, NA_WIDTH), BF16),
        scratch_shapes=[pltpu.VMEM((NA_UNROLL, width, QUAD * GRID_W), BF16)],
        compiler_params=_params(("arbitrary", "arbitrary")),
        name="na",
    )(qa, ka, vat, cka, cvat, tables)


def _bias_kernel(rpb_ref, onehot_ref, mask_ref, o_ref):
    o_ref[0] = (jnp.dot(rpb_ref[0], onehot_ref[...], preferred_element_type=F32,
                        precision=lax.Precision.HIGHEST) + mask_ref[...]) * LOG2E


def _bias_tables(rpb):
    depth = rpb.shape[0]
    n_row_off, n_col_off = 2 * WIN_R - 1, 2 * WIN_C - 1
    cols = np.arange(GRID_W)
    cstart = np.clip(cols - WIN_C // 2, 0, GRID_W - WIN_C)
    kcol = np.arange(GRID_W)
    valid = (kcol[None, :] >= cstart[:, None]) & (kcol[None, :] < cstart[:, None] + WIN_C)
    col_off = kcol[None, :] - cols[:, None] + (WIN_C - 1)
    pad_r, pad_c = -n_row_off % 8, -n_col_off % 8
    assert pad_r >= 1
    onehot = (np.arange(n_col_off + pad_c)[:, None, None] == col_off[None]) & valid[None]
    onehot = jnp.asarray(onehot.reshape(n_col_off + pad_c, GRID_W * GRID_W), dtype=F32)
    mask = jnp.asarray(np.where(valid, 0.0, MASKED).reshape(1, GRID_W * GRID_W), dtype=F32)
    heads = depth * NA_HEADS
    rpb_p = jnp.pad(rpb.reshape(heads, n_row_off, n_col_off), ((0, 0), (0, pad_r), (0, pad_c)))
    toep = pl.pallas_call(
        _bias_kernel,
        grid=(heads,),
        in_specs=[pl.BlockSpec((1,) + rpb_p.shape[1:], lambda h: (h, 0, 0)),
                  pl.BlockSpec(onehot.shape, lambda h: (0, 0)),
                  pl.BlockSpec(mask.shape, lambda h: (0, 0))],
        out_specs=pl.BlockSpec((1, n_row_off + pad_r, GRID_W * GRID_W), lambda h: (h, 0, 0)),
        out_shape=jax.ShapeDtypeStruct((heads, n_row_off + pad_r, GRID_W * GRID_W), F32),
        compiler_params=_params(("arbitrary",)),
        name="bias",
    )(rpb_p, onehot, mask)
    groups = NA_HEADS // QUAD
    toep = toep.reshape(depth, groups, QUAD, n_row_off + pad_r, GRID_W, GRID_W)
    table = jnp.transpose(toep, (0, 1, 3, 5, 2, 4)).reshape(depth, groups, -1, GRID_W, QUAD * GRID_W)
    return jnp.where((jnp.arange(table.shape[2]) < n_row_off)[None, None, :, None, None], table, MASKED * LOG2E)


def _stack_group(q):
    return jnp.concatenate([_stack_heads(q[:, :LANES]), _stack_heads(q[:, LANES:])], axis=0)


def _unstack_group_t(o, rows):
    pairs = [jnp.concatenate([o[:, (2 * j) * rows:(2 * j + 1) * rows],
                              o[:, (2 * j + 1) * rows:(2 * j + 2) * rows]], axis=0).T for j in range(2)]
    return jnp.concatenate(pairs, axis=1)


def _scores_t(qs_t, k):
    return _dot(k, qs_t)


def _attend_t(s, vt, state):
    m_chunk = jnp.max(s, axis=0, keepdims=True)
    if state is None:
        return m_chunk, _pv_t(vt, s, m_chunk)
    m_old, acc = state
    m = jnp.maximum(m_old, m_chunk)
    return m, jnp.exp2(m_old - m) * acc + _pv_t(vt, s, m)


def _gqa_kernel(q_ref, k_ref, vt_ref, kc_ref, vct_ref, o_ref, qt_ref, *, chunks, tq, tk, blocks):
    for j in range(blocks):
        qt_ref[j] = _stack_group(q_ref[0, j * tq:(j + 1) * tq, :]).T
    pending = []
    states = [None] * blocks

    def drain(limit):
        while len(pending) > limit:
            j, s, vt = pending.pop(0)
            states[j] = _attend_t(s, vt, states[j])

    for j in range(blocks):
        qs = qt_ref[j]
        pending.append((j, _scores_t(qs, kc_ref[0]), vct_ref[0, 0]))
        drain(GQA_AHEAD)
        for c in range(chunks):
            pending.append((j, _scores_t(qs, k_ref[0, c * tk:(c + 1) * tk, :]), vt_ref[0, c]))
            drain(GQA_AHEAD)
    drain(0)
    for j in range(blocks):
        _, acc = states[j]
        o_ref[0, j * tq:(j + 1) * tq, :] = _unstack_group_t(_normalise_t(acc, HEAD_DIM), tq).astype(BF16)


def _gqa_attention(qb, kbd, vbt, ckbd, cvbt):
    batch, seq, _ = qb.shape
    ctx_len = ckbd.shape[1]
    tq, tk, blocks = GQA_Q_ROWS, vbt.shape[3], GQA_Q_BLOCKS
    kspec = lambda n: pl.BlockSpec((1, n, LANES), lambda b, g, i: (b, 0, g))
    vspec = lambda a: pl.BlockSpec((1, a.shape[1], HEAD_DIM, a.shape[3]), lambda b, g, i: (b, 0, g, 0))
    qspec = pl.BlockSpec((1, blocks * tq, 2 * LANES), lambda b, g, i: (b, i, g))
    return pl.pallas_call(
        functools.partial(_gqa_kernel, chunks=seq // tk, tq=tq, tk=tk, blocks=blocks),
        grid=(batch, GQA_KV_HEADS, seq // (blocks * tq)),
        in_specs=[qspec, kspec(seq), vspec(vbt), kspec(ctx_len), vspec(cvbt)],
        out_specs=qspec,
        out_shape=jax.ShapeDtypeStruct((batch, seq, GQA_WIDTH), BF16),
        scratch_shapes=[pltpu.VMEM((blocks, LANES, (GQA_Q_HEADS // GQA_KV_HEADS) * tq), BF16)],
        compiler_params=_params(("arbitrary", "arbitrary", "arbitrary")),
        name="gqa",
    )(qb, kbd, vbt, ckbd, cvbt)


def _ctx_kernel(qa_ref, ka_ref, vat_ref, qb_ref, kb_ref, vbt_ref, oa_ref, ob_ref):
    n = qa_ref.shape[1]
    width = QUAD * HEAD_DIM
    for g in range(NA_HEADS // QUAD):
        sl = slice(g * width, (g + 1) * width)
        vt = jnp.concatenate([vat_ref[0, j, sl, :] for j in range(vat_ref.shape[1])], axis=1)
        o = _softmax_pv_t([_dot_t(ka_ref[0, :, sl], _stack_quad(qa_ref[0, :, sl]))], [vt])
        oa_ref[0, :, sl] = _unstack_quad_t(o, n).astype(BF16)
    for g in range(GQA_KV_HEADS):
        sl = slice(g * 2 * LANES, (g + 1) * 2 * LANES)
        s = _scores_t(_stack_group(qb_ref[0, :, sl]).T, kb_ref[0, :, g * LANES:(g + 1) * LANES])
        o = _softmax_pv_t([s], [vbt_ref[0, 0, g * HEAD_DIM:(g + 1) * HEAD_DIM, :]])
        ob_ref[0, :, sl] = _unstack_group_t(o, n).astype(BF16)


def _context_attention(cqa, cka, cvat, cqb, ckbd, cvbt):
    batch, n, _ = cqa.shape
    spec = lambda a: pl.BlockSpec((1,) + a.shape[1:], lambda b: (b,) + (0,) * (a.ndim - 1))
    args = (cqa, cka, cvat, cqb, ckbd, cvbt)
    return pl.pallas_call(
        _ctx_kernel,
        grid=(batch,),
        in_specs=[spec(a) for a in args],
        out_specs=[spec(cqa), spec(cqb)],
        out_shape=[jax.ShapeDtypeStruct(cqa.shape, BF16), jax.ShapeDtypeStruct(cqb.shape, BF16)],
        compiler_params=_params(("arbitrary",)),
        name="ctx",
    )(*args)


def _merge_kernel(a_ref, b_ref, za_ref, zb_ref, ga_ref, gb_ref, x_ref, gate_ref,
                  woa_ref, wob_ref, wout_ref, o_ref):
    o_a = _dot(a_ref[0] * za_ref[0], woa_ref[...])
    o_b = _dot(b_ref[0] * zb_ref[0], wob_ref[...])
    merged = ga_ref[0].astype(F32) * o_a + gb_ref[0].astype(F32) * o_b
    o_ref[0] = x_ref[0] + gate_ref[0] * _dot(merged.astype(BF16), wout_ref[...])


def _merge(a_att, b_att, za, zb, ga, gb, x, gate, w_o_a, w_o_b, w_out, layer, rows):
    groups, total, _ = x.shape
    act = lambda w: pl.BlockSpec((1, rows, w), lambda g, i: (g, i, 0))
    const = lambda a: pl.BlockSpec((None,) + a.shape[1:], lambda g, i: (layer, 0, 0))
    return pl.pallas_call(
        _merge_kernel,
        grid=(groups, total // rows),
        in_specs=[act(NA_WIDTH), act(GQA_WIDTH), act(NA_WIDTH), act(GQA_WIDTH), act(D_MODEL), act(D_MODEL),
                  act(D_MODEL), pl.BlockSpec((1, 1, D_MODEL), lambda g, i: (g, 0, 0)),
                  const(w_o_a), const(w_o_b), const(w_out)],
        out_specs=act(D_MODEL),
        out_shape=jax.ShapeDtypeStruct(x.shape, F32),
        compiler_params=_params(("arbitrary", "arbitrary")),
        name="merge",
    )(a_att, b_att, za, zb, ga, gb, x, gate, w_o_a, w_o_b, w_out)


def _rope_tables(seq):
    t = jnp.arange(seq, dtype=jnp.int32)
    inv = 1.0 / (ROPE_THETA ** (jnp.arange(ROT_HALF, dtype=F32) / ROT_HALF))
    ang_r = (t // GRID_W).astype(F32)[:, None] * inv[None, :]
    ang_c = (t % GRID_W).astype(F32)[:, None] * inv[None, :]
    cos = jnp.concatenate([jnp.cos(ang_r)] * 2 + [jnp.cos(ang_c)] * 2, axis=1)
    sin = jnp.concatenate([-jnp.sin(ang_r), jnp.sin(ang_r), -jnp.sin(ang_c), jnp.sin(ang_c)], axis=1)
    reps = LANES // HEAD_DIM
    return jnp.tile(cos, (1, reps)), jnp.tile(sin, (1, reps))


def _group_sum_matrix():
    idx = np.arange(MXU_COLS) // HEAD_DIM
    return jnp.asarray(idx[:, None] == idx[None, :], dtype=BF16)


def kernel(x, c, ctx, c_ctx, w_ada, b_ada, norm_g, w_in, q_norm_a, k_norm_a, q_norm_b, k_norm_b,
           rpb, w_o_a, w_o_b, w_out):
    batch, seq, _ = x.shape
    ctx_len = ctx.shape[1]
    depth = w_ada.shape[0]
    rows = seq // GRID_W
    assert seq % GRID_W == 0 and rows >= WIN_R and seq % PROJ_ROWS == 0 and seq % GQA_Q_ROWS == 0

    pad = -(batch + 1) % 8
    c_rows = jnp.concatenate([c, c_ctx[None, :], jnp.zeros((pad, D_MODEL), F32)], axis=0)
    mod = _modulation(c_rows, w_ada, b_ada)

    cos_x, sin_x = _rope_tables(seq)
    ctx_rows = batch * ctx_len
    ctx = ctx.reshape(1, ctx_rows, D_MODEL)
    cos_c = jnp.ones((ctx_rows, LANES), F32)
    sin_c = jnp.zeros((ctx_rows, LANES), F32)
    flat = lambda a: a.reshape((1, ctx_rows) + a.shape[2:])
    per_batch_ctx = lambda a: a.reshape((batch, a.shape[1] // batch) + a.shape[2:])
    gsum = _group_sum_matrix()
    tables = _bias_tables(rpb)
    w_in, w_o_a, w_o_b, w_out = (w.astype(BF16) for w in (w_in, w_o_a, w_o_b, w_out))

    for l in range(depth):
        update_ctx = l < depth - 1
        shift, scale, gate = (mod[l, :, i * D_MODEL:(i + 1) * D_MODEL] for i in range(3))
        per_batch = lambda m: m[:batch, None, :]
        for_ctx = lambda m: m[batch][None, None, :]
        tile = lambda g, n: jnp.tile(g, n)
        gains = jnp.stack([tile(k_norm_a[l], NA_HEADS), tile(k_norm_b[l], NA_HEADS),
                           tile(q_norm_a[l], NA_HEADS) * (QK_SCALE * LOG2E),
                           tile(q_norm_b[l], GQA_Q_HEADS) * (QK_SCALE * LOG2E)])
        g_row = norm_g[l][None, :]

        ka, vat, kbd, vbt, qa, qb, za, zb, ga, gb = _project(
            x, per_batch(shift), per_batch(scale), g_row, w_in, l, gsum, gains, cos_x, sin_x,
            len(SEG_WIDTHS), PROJ_ROWS)
        nseg_c = len(SEG_WIDTHS) if update_ctx else KV_SEGS
        pc = _project(ctx, for_ctx(shift), for_ctx(scale), g_row, w_in, l, gsum, gains, cos_c, sin_c,
                      nseg_c, min(PROJ_ROWS, ctx_rows))
        pc = [per_batch_ctx(a) for a in pc]
        cka, cvat, ckbd, cvbt = pc[:KV_SEGS]

        a_att = _neighbourhood_attention(qa, ka, vat, cka, cvat, tables, l)
        b_att = _gqa_attention(qb, kbd, vbt, ckbd, cvbt)
        x_new = _merge(a_att, b_att, za, zb, ga, gb, x, per_batch(gate), w_o_a, w_o_b, w_out, l, MERGE_ROWS)

        if update_ctx:
            _, _, _, _, cqa, cqb, cza, czb, cga, cgb = pc
            c_a, c_b = _context_attention(cqa, cka, cvat, cqb, ckbd, cvbt)
            ctx = _merge(*(flat(a) for a in (c_a, c_b, cza, czb, cga, cgb)), ctx, for_ctx(gate),
                         w_o_a, w_o_b, w_out, l, min(MERGE_ROWS, ctx_rows))
        x = x_new
    return x
```

```python
import functools

import numpy as np
import jax
import jax.numpy as jnp
from jax import lax
from jax.experimental import pallas as pl
from jax.experimental.pallas import tpu as pltpu

D_MODEL = 1024
GRID_W = 64
HEAD_DIM = 64
NA_HEADS = 8
NA_WIDTH = NA_HEADS * HEAD_DIM
WIN_R = 8
WIN_C = 16
GQA_Q_HEADS = 8
GQA_KV_HEADS = 2
GQA_WIDTH = GQA_Q_HEADS * HEAD_DIM
GQA_KV_WIDTH = GQA_KV_HEADS * HEAD_DIM
ROPE_THETA = 10000.0
ROT_AXIS = HEAD_DIM // 2
ROT_HALF = ROT_AXIS // 2
EPS = 1e-6
QK_SCALE = HEAD_DIM ** -0.5
LOG2E = 1.4426950408889634

LANES = 128
MXU_COLS = 256
MASKED = -1e30
VMEM_LIMIT = 56 * 1024 * 1024

PROJ_ROWS = 512
MERGE_ROWS = 1024
GQA_Q_ROWS = 256
GQA_Q_BLOCKS = 2
GQA_K_ROWS = 256
GQA_SUB_ROWS = 128
GQA_AHEAD = 1
QUAD = 4
ONES_ROWS = 16
NA_SLABS = WIN_R // 2 + 1
NA_SUB_ROWS = 128
NA_AHEAD = 4
NA_UNROLL = 8

F32 = jnp.float32
BF16 = jnp.bfloat16

SEG_KA, SEG_VA, SEG_KB, SEG_VB, SEG_QA, SEG_QB, SEG_ZA, SEG_ZB, SEG_GA, SEG_GB = range(10)
SEG_WIDTHS = (NA_WIDTH, NA_WIDTH, GQA_KV_WIDTH, GQA_KV_WIDTH, NA_WIDTH, GQA_WIDTH,
              NA_WIDTH, GQA_WIDTH, D_MODEL, D_MODEL)
SEG_OFFSETS = tuple(int(v) for v in np.cumsum((0,) + SEG_WIDTHS))
SEG_OUT_WIDTHS = tuple(2 * w if s == SEG_KB else w for s, w in enumerate(SEG_WIDTHS))
KV_SEGS = 4
SEG_ISSUE_RANK = {SEG_GA: 0, SEG_GB: 1, SEG_ZA: 2, SEG_ZB: 3, SEG_QB: 4, SEG_KB: 5, SEG_QA: 6, SEG_KA: 7,
                  SEG_VB: 8, SEG_VA: 9}
SEG_GAIN_ROW = {SEG_KA: 0, SEG_KB: 1, SEG_QA: 2, SEG_QB: 3}
SEG_ROPE = (SEG_KB, SEG_QB)
SEG_T_SLAB = {SEG_VA: LANES, SEG_VB: GQA_K_ROWS}


def _dot(a, b):
    return jnp.dot(a, b, preferred_element_type=F32)


def _dot_t(a, b):
    return lax.dot_general(a, b, (((1,), (1,)), ((), ())), preferred_element_type=F32)


def _params(semantics):
    return pltpu.CompilerParams(dimension_semantics=semantics, vmem_limit_bytes=VMEM_LIMIT)


def _mod_kernel(c_ref, w_ref, b_ref, o_ref):
    c = c_ref[...]
    a = c * jax.nn.sigmoid(c)
    o_ref[0] = jnp.dot(a, w_ref[0], preferred_element_type=F32,
                       precision=lax.Precision.HIGHEST) + b_ref[0]


def _modulation(c_rows, w_ada, b_ada):
    depth = w_ada.shape[0]
    rows = c_rows.shape[0]
    ncol = w_ada.shape[2] // D_MODEL
    return pl.pallas_call(
        _mod_kernel,
        grid=(depth, ncol),
        in_specs=[
            pl.BlockSpec((rows, D_MODEL), lambda l, j: (0, 0)),
            pl.BlockSpec((1, D_MODEL, D_MODEL), lambda l, j: (l, 0, j)),
            pl.BlockSpec((1, 1, D_MODEL), lambda l, j: (l, 0, j)),
        ],
        out_specs=pl.BlockSpec((1, rows, D_MODEL), lambda l, j: (l, 0, j)),
        out_shape=jax.ShapeDtypeStruct((depth, rows, ncol * D_MODEL), F32),
        compiler_params=_params(("arbitrary", "arbitrary")),
        name="mod",
    )(c_rows, w_ada, b_ada.reshape(depth, 1, -1))


def _head_norm(p, gsum, gain):
    sq = (p * p).astype(BF16)
    cols = min(MXU_COLS, p.shape[1])
    parts = []
    for j in range(p.shape[1] // cols):
        sl = slice(j * cols, (j + 1) * cols)
        ss = _dot(sq[:, sl], gsum[:cols, :cols])
        parts.append(p[:, sl] * lax.rsqrt(ss * (1.0 / HEAD_DIM) + EPS))
    y = parts[0] if len(parts) == 1 else jnp.concatenate(parts, axis=1)
    return y * gain


def _rope(y, cos, sin):
    width = y.shape[1]
    lane = lax.broadcasted_iota(jnp.int32, y.shape, 1)
    ahead = pltpu.roll(y, width - ROT_HALF, axis=1)
    behind = pltpu.roll(y, ROT_HALF, axis=1)
    partner = jnp.where((lane % ROT_AXIS) < ROT_HALF, ahead, behind)
    reps = width // LANES
    cos = cos if reps == 1 else jnp.concatenate([cos] * reps, axis=1)
    sin = sin if reps == 1 else jnp.concatenate([sin] * reps, axis=1)
    return y * cos + partner * sin


def _duplicate_heads(y):
    lo = lax.broadcasted_iota(jnp.int32, y.shape, 1) < HEAD_DIM
    swapped = pltpu.roll(y, HEAD_DIM, axis=1)
    return jnp.concatenate([jnp.where(lo, y, swapped), jnp.where(lo, swapped, y)], axis=1)


def _proj_kernel(x_ref, shift_ref, scale_ref, g_ref, w_ref, gsum_ref, gain_ref, cos_ref, sin_ref,
                 *out_refs, nseg, max_slab):
    x = x_ref[0]
    ms = jnp.mean(x * x, axis=-1, keepdims=True)
    gmod = g_ref[...] * (1.0 + scale_ref[0])
    h = (x * lax.rsqrt(ms + EPS) * gmod + shift_ref[0]).astype(BF16)
    gsum = gsum_ref[...]
    for seg in sorted(range(nseg), key=SEG_ISSUE_RANK.__getitem__):
        width = SEG_WIDTHS[seg]
        p = _dot(h, w_ref[:, SEG_OFFSETS[seg]:SEG_OFFSETS[seg + 1]])
        if seg in SEG_GAIN_ROW:
            row = SEG_GAIN_ROW[seg]
            p = _head_norm(p, gsum, gain_ref[row:row + 1, :width])
        if seg in SEG_ROPE:
            p = _rope(p, cos_ref[...], sin_ref[...])
        if seg == SEG_KB:
            p = _duplicate_heads(p)
        if seg in (SEG_ZA, SEG_ZB):
            p = p * jax.nn.sigmoid(p)
        if seg in (SEG_GA, SEG_GB):
            p = jax.nn.sigmoid(p)
        if seg in SEG_T_SLAB:
            pt = p.T.astype(BF16)
            slab = min(SEG_T_SLAB[seg], max_slab)
            for j in range(pt.shape[1] // slab):
                out_refs[seg][0, j] = pt[:, j * slab:(j + 1) * slab]
        else:
            out_refs[seg][0] = p.astype(BF16)


def _project(x, shift, scale, norm_g, w_all, layer, gsum, gains, cos, sin, nseg, rows, max_slab):
    groups, total, _ = x.shape
    ncols = SEG_OFFSETS[nseg]
    const = lambda g, i: (0, 0)

    def out_layout(s):
        width = SEG_OUT_WIDTHS[s]
        if s not in SEG_T_SLAB:
            return (groups, total, width), pl.BlockSpec((1, rows, width), lambda g, i: (g, i, 0))
        slab = min(SEG_T_SLAB[s], max_slab)
        return ((groups, total // slab, width, slab),
                pl.BlockSpec((1, rows // slab, width, slab), lambda g, i: (g, i, 0, 0)))

    layouts = [out_layout(s) for s in range(nseg)]
    return pl.pallas_call(
        functools.partial(_proj_kernel, nseg=nseg, max_slab=max_slab),
        grid=(groups, total // rows),
        in_specs=[
            pl.BlockSpec((1, rows, D_MODEL), lambda g, i: (g, i, 0)),
            pl.BlockSpec((1, 1, D_MODEL), lambda g, i: (g, 0, 0)),
            pl.BlockSpec((1, 1, D_MODEL), lambda g, i: (g, 0, 0)),
            pl.BlockSpec((1, D_MODEL), const),
            pl.BlockSpec((None, D_MODEL, ncols), lambda g, i: (layer, 0, 0), pipeline_mode=pl.Buffered(1)),
            pl.BlockSpec((MXU_COLS, MXU_COLS), const),
            pl.BlockSpec(gains.shape, const),
            pl.BlockSpec((rows, LANES), lambda g, i: (i, 0)),
            pl.BlockSpec((rows, LANES), lambda g, i: (i, 0)),
        ],
        out_specs=[spec for _, spec in layouts],
        out_shape=[jax.ShapeDtypeStruct(shape, BF16) for shape, _ in layouts],
        compiler_params=_params(("arbitrary", "arbitrary")),
        name="proj",
    )(x, shift, scale, norm_g, w_all, gsum, gains, cos, sin)


def _stack_heads(pair):
    lo = lax.broadcasted_iota(jnp.int32, pair.shape, 1) < HEAD_DIM
    zero = jnp.zeros_like(pair)
    return jnp.concatenate([jnp.where(lo, pair, zero), jnp.where(lo, zero, pair)], axis=0)


def _stack_quad(q):
    head = lax.broadcasted_iota(jnp.int32, q.shape, 1) // HEAD_DIM
    zero = jnp.zeros_like(q)
    return jnp.concatenate([jnp.where(head == h, q, zero) for h in range(QUAD)], axis=0)


def _unstack_quad_t(o, rows):
    ot = o.T
    head = lax.broadcasted_iota(jnp.int32, (rows, QUAD * HEAD_DIM), 1) // HEAD_DIM
    out = ot[:rows]
    for h in range(1, QUAD):
        out = jnp.where(head == h, ot[h * rows:(h + 1) * rows], out)
    return out


def _pv_t(vt, s, m):
    lhs = jnp.concatenate([vt, jnp.ones((ONES_ROWS, vt.shape[1]), vt.dtype)], axis=0)
    return _dot(lhs, jnp.exp2((s - m).astype(BF16)))


def _normalise_t(acc, dims):
    return acc[:dims] * (1.0 / acc[dims:dims + 1])


def _softmax_pv_t(logits, values_t):
    m = functools.reduce(jnp.maximum, [jnp.max(s, axis=0, keepdims=True) for s in logits])
    acc = functools.reduce(jnp.add, [_pv_t(vt, s, m) for vt, s in zip(values_t, logits)])
    return _normalise_t(acc, values_t[0].shape[0])


def _na_window(r, rows):
    start = jnp.clip(r - WIN_R // 2, 0, rows - WIN_R)
    slab0 = jnp.minimum(start // 2, rows // 2 - NA_SLABS)
    first = 2 * slab0
    blocks = []
    for w in range(2 * NA_SLABS):
        key_row = first + w
        valid = (key_row >= start) & (key_row < start + WIN_R)
        blocks.append(jnp.where(valid, key_row - r + (WIN_R - 1), 2 * WIN_R - 1))
    return slab0, blocks


def _na_kernel(q_ref, k_ref, vt_ref, kc_ref, vct_ref, t_ref, o_ref, qt_ref, *, rows, unroll):
    kc = kc_ref[0]
    vct = jnp.concatenate([vct_ref[0, j] for j in range(vct_ref.shape[1])], axis=1)
    win_keys = NA_SLABS * LANES

    def logits(r, slot):
        slab0, blocks = _na_window(r, rows)
        qt_ref[slot] = _stack_quad(q_ref[0, pl.ds(pl.multiple_of(r * GRID_W, GRID_W), GRID_W), :]).T
        qs_t = qt_ref[slot]
        kw = k_ref[0, pl.ds(pl.multiple_of(slab0 * LANES, LANES), win_keys), :]
        bias = jnp.concatenate([t_ref[b] for b in blocks], axis=0)
        return _dot(kw, qs_t) + bias, _dot(kc, qs_t), slab0

    def finish(r, s_win, s_ctx, slab0):
        vt = vt_ref[0, pl.ds(slab0, NA_SLABS)]
        vtw = jnp.concatenate([vt[j] for j in range(NA_SLABS)], axis=1)
        sub = NA_SUB_ROWS
        parts = [s[i:i + sub] for s in (s_ctx, s_win) for i in range(0, s.shape[0], sub)]
        _, acc = _attend_t(parts, jnp.concatenate([vct, vtw], axis=1), None)
        o = _normalise_t(acc, vct.shape[0])
        o_ref[0, pl.ds(pl.multiple_of(r * GRID_W, GRID_W), GRID_W), :] = _unstack_quad_t(o, GRID_W).astype(BF16)

    def body(i, carry):
        pending = []
        for e in range(unroll):
            pending.append((i * unroll + e,) + logits(i * unroll + e, e))
            if len(pending) > NA_AHEAD:
                finish(*pending.pop(0))
        for args in pending:
            finish(*args)
        return carry

    lax.fori_loop(0, rows // unroll, body, 0)


def _neighbourhood_attention(qa, ka, vat, cka, cvat, tables, layer):
    batch, seq, _ = qa.shape
    ctx_len = cka.shape[1]
    rows = seq // GRID_W
    width = QUAD * HEAD_DIM
    blk = lambda n: pl.BlockSpec((1, n, width), lambda b, j: (b, 0, j))
    vblk = lambda a: pl.BlockSpec((1, a.shape[1], width, LANES), lambda b, j: (b, 0, j, 0))
    return pl.pallas_call(
        functools.partial(_na_kernel, rows=rows, unroll=NA_UNROLL),
        grid=(batch, NA_HEADS // QUAD),
        in_specs=[blk(seq), blk(seq), vblk(vat), blk(ctx_len), vblk(cvat),
                  pl.BlockSpec((None, None) + tables.shape[2:], lambda b, j: (layer, j, 0, 0, 0))],
        out_specs=blk(seq),
        out_shape=jax.ShapeDtypeStruct((batch, seq, NA_WIDTH), BF16),
        scratch_shapes=[pltpu.VMEM((NA_UNROLL, width, QUAD * GRID_W), BF16)],
        compiler_params=_params(("arbitrary", "arbitrary")),
        name="na",
    )(qa, ka, vat, cka, cvat, tables)


def _bias_kernel(rpb_ref, onehot_ref, mask_ref, o_ref):
    o_ref[0] = (jnp.dot(rpb_ref[0], onehot_ref[...], preferred_element_type=F32,
                        precision=lax.Precision.HIGHEST) + mask_ref[...]) * LOG2E


def _bias_tables(rpb):
    depth = rpb.shape[0]
    n_row_off, n_col_off = 2 * WIN_R - 1, 2 * WIN_C - 1
    cols = np.arange(GRID_W)
    cstart = np.clip(cols - WIN_C // 2, 0, GRID_W - WIN_C)
    kcol = np.arange(GRID_W)
    valid = (kcol[None, :] >= cstart[:, None]) & (kcol[None, :] < cstart[:, None] + WIN_C)
    col_off = kcol[None, :] - cols[:, None] + (WIN_C - 1)
    pad_r, pad_c = -n_row_off % 8, -n_col_off % 8
    assert pad_r >= 1
    onehot = (np.arange(n_col_off + pad_c)[:, None, None] == col_off[None]) & valid[None]
    onehot = jnp.asarray(onehot.reshape(n_col_off + pad_c, GRID_W * GRID_W), dtype=F32)
    mask = jnp.asarray(np.where(valid, 0.0, MASKED).reshape(1, GRID_W * GRID_W), dtype=F32)
    heads = depth * NA_HEADS
    rpb_p = jnp.pad(rpb.reshape(heads, n_row_off, n_col_off), ((0, 0), (0, pad_r), (0, pad_c)))
    toep = pl.pallas_call(
        _bias_kernel,
        grid=(heads,),
        in_specs=[pl.BlockSpec((1,) + rpb_p.shape[1:], lambda h: (h, 0, 0)),
                  pl.BlockSpec(onehot.shape, lambda h: (0, 0)),
                  pl.BlockSpec(mask.shape, lambda h: (0, 0))],
        out_specs=pl.BlockSpec((1, n_row_off + pad_r, GRID_W * GRID_W), lambda h: (h, 0, 0)),
        out_shape=jax.ShapeDtypeStruct((heads, n_row_off + pad_r, GRID_W * GRID_W), F32),
        compiler_params=_params(("arbitrary",)),
        name="bias",
    )(rpb_p, onehot, mask)
    groups = NA_HEADS // QUAD
    toep = toep.reshape(depth, groups, QUAD, n_row_off + pad_r, GRID_W, GRID_W)
    table = jnp.transpose(toep, (0, 1, 3, 5, 2, 4)).reshape(depth, groups, -1, GRID_W, QUAD * GRID_W)
    return jnp.where((jnp.arange(table.shape[2]) < n_row_off)[None, None, :, None, None], table, MASKED * LOG2E)


def _stack_group(q):
    return jnp.concatenate([_stack_heads(q[:, :LANES]), _stack_heads(q[:, LANES:])], axis=0)


def _unstack_group_t(o, rows):
    pairs = [jnp.concatenate([o[:, (2 * j) * rows:(2 * j + 1) * rows],
                              o[:, (2 * j + 1) * rows:(2 * j + 2) * rows]], axis=0).T for j in range(2)]
    return jnp.concatenate(pairs, axis=1)


def _scores_t(qs_t, k):
    return _dot(k, qs_t)


def _attend_t(parts, vt, state):
    m_old = None if state is None else state[0]
    m, ps, ms = m_old, [], []
    for s in parts:
        m_part = jnp.max(s, axis=0, keepdims=True)
        m = m_part if m is None else jnp.maximum(m, m_part)
        ps.append(jnp.exp2((s - m).astype(BF16)))
        ms.append(m)
    ps = [p * jnp.exp2(m_i - m).astype(BF16) for p, m_i in zip(ps[:-1], ms[:-1])] + ps[-1:]
    lhs = jnp.concatenate([vt, jnp.ones((ONES_ROWS, vt.shape[1]), vt.dtype)], axis=0)
    pv = _dot(lhs, ps[0] if len(ps) == 1 else jnp.concatenate(ps, axis=0))
    if state is None:
        return m, pv
    return m, jnp.exp2(m_old - m) * state[1] + pv


def _gqa_kernel(q_ref, k_ref, vt_ref, kc_ref, vct_ref, o_ref, qt_ref, *, chunks, tq, tk, blocks):
    for j in range(blocks):
        qt_ref[j] = _stack_group(q_ref[0, j * tq:(j + 1) * tq, :]).T
    pending = []
    states = [None] * blocks

    def drain(limit):
        while len(pending) > limit:
            j, parts, vt = pending.pop(0)
            states[j] = _attend_t(parts, vt, states[j])

    def logits(qs, keys, c, rows):
        sub = min(GQA_SUB_ROWS, rows)
        return [_scores_t(qs, keys[0, c * rows + i * sub:c * rows + (i + 1) * sub, :]) for i in range(rows // sub)]

    for j in range(blocks):
        qs = qt_ref[j]
        for c in range(vct_ref.shape[1]):
            pending.append((j, logits(qs, kc_ref, c, vct_ref.shape[3]), vct_ref[0, c]))
            drain(GQA_AHEAD)
        for c in range(chunks):
            pending.append((j, logits(qs, k_ref, c, tk), vt_ref[0, c]))
            drain(GQA_AHEAD)
    drain(0)
    for j in range(blocks):
        _, acc = states[j]
        o_ref[0, j * tq:(j + 1) * tq, :] = _unstack_group_t(_normalise_t(acc, HEAD_DIM), tq).astype(BF16)


def _gqa_attention(qb, kbd, vbt, ckbd, cvbt):
    batch, seq, _ = qb.shape
    ctx_len = ckbd.shape[1]
    tq, tk, blocks = GQA_Q_ROWS, vbt.shape[3], GQA_Q_BLOCKS
    kspec = lambda n: pl.BlockSpec((1, n, LANES), lambda b, g, i: (b, 0, g))
    vspec = lambda a: pl.BlockSpec((1, a.shape[1], HEAD_DIM, a.shape[3]), lambda b, g, i: (b, 0, g, 0))
    qspec = pl.BlockSpec((1, blocks * tq, 2 * LANES), lambda b, g, i: (b, i, g))
    return pl.pallas_call(
        functools.partial(_gqa_kernel, chunks=seq // tk, tq=tq, tk=tk, blocks=blocks),
        grid=(batch, GQA_KV_HEADS, seq // (blocks * tq)),
        in_specs=[qspec, kspec(seq), vspec(vbt), kspec(ctx_len), vspec(cvbt)],
        out_specs=qspec,
        out_shape=jax.ShapeDtypeStruct((batch, seq, GQA_WIDTH), BF16),
        scratch_shapes=[pltpu.VMEM((blocks, LANES, (GQA_Q_HEADS // GQA_KV_HEADS) * tq), BF16)],
        compiler_params=_params(("arbitrary", "arbitrary", "arbitrary")),
        name="gqa",
    )(qb, kbd, vbt, ckbd, cvbt)


def _ctx_kernel(qa_ref, ka_ref, vat_ref, qb_ref, kb_ref, vbt_ref, oa_ref, ob_ref):
    n = qa_ref.shape[1]
    width = QUAD * HEAD_DIM
    for g in range(NA_HEADS // QUAD):
        sl = slice(g * width, (g + 1) * width)
        vt = jnp.concatenate([vat_ref[0, j, sl, :] for j in range(vat_ref.shape[1])], axis=1)
        o = _softmax_pv_t([_dot_t(ka_ref[0, :, sl], _stack_quad(qa_ref[0, :, sl]))], [vt])
        oa_ref[0, :, sl] = _unstack_quad_t(o, n).astype(BF16)
    for g in range(GQA_KV_HEADS):
        sl = slice(g * 2 * LANES, (g + 1) * 2 * LANES)
        s = _scores_t(_stack_group(qb_ref[0, :, sl]).T, kb_ref[0, :, g * LANES:(g + 1) * LANES])
        vt = jnp.concatenate([vbt_ref[0, j, g * HEAD_DIM:(g + 1) * HEAD_DIM, :] for j in range(vbt_ref.shape[1])],
                             axis=1)
        o = _softmax_pv_t([s], [vt])
        ob_ref[0, :, sl] = _unstack_group_t(o, n).astype(BF16)


def _context_attention(cqa, cka, cvat, cqb, ckbd, cvbt):
    batch, n, _ = cqa.shape
    spec = lambda a: pl.BlockSpec((1,) + a.shape[1:], lambda b: (b,) + (0,) * (a.ndim - 1))
    args = (cqa, cka, cvat, cqb, ckbd, cvbt)
    return pl.pallas_call(
        _ctx_kernel,
        grid=(batch,),
        in_specs=[spec(a) for a in args],
        out_specs=[spec(cqa), spec(cqb)],
        out_shape=[jax.ShapeDtypeStruct(cqa.shape, BF16), jax.ShapeDtypeStruct(cqb.shape, BF16)],
        compiler_params=_params(("arbitrary",)),
        name="ctx",
    )(*args)


def _merge_kernel(a_ref, b_ref, za_ref, zb_ref, ga_ref, gb_ref, x_ref, gate_ref,
                  woa_ref, wob_ref, wout_ref, o_ref):
    o_a = _dot(a_ref[0] * za_ref[0], woa_ref[...])
    o_b = _dot(b_ref[0] * zb_ref[0], wob_ref[...])
    merged = ga_ref[0].astype(F32) * o_a + gb_ref[0].astype(F32) * o_b
    o_ref[0] = x_ref[0] + gate_ref[0] * _dot(merged.astype(BF16), wout_ref[...])


def _merge(a_att, b_att, za, zb, ga, gb, x, gate, w_o_a, w_o_b, w_out, layer, rows):
    groups, total, _ = x.shape
    act = lambda w: pl.BlockSpec((1, rows, w), lambda g, i: (g, i, 0))
    const = lambda a: pl.BlockSpec((None,) + a.shape[1:], lambda g, i: (layer, 0, 0))
    return pl.pallas_call(
        _merge_kernel,
        grid=(groups, total // rows),
        in_specs=[act(NA_WIDTH), act(GQA_WIDTH), act(NA_WIDTH), act(GQA_WIDTH), act(D_MODEL), act(D_MODEL),
                  act(D_MODEL), pl.BlockSpec((1, 1, D_MODEL), lambda g, i: (g, 0, 0)),
                  const(w_o_a), const(w_o_b), const(w_out)],
        out_specs=act(D_MODEL),
        out_shape=jax.ShapeDtypeStruct(x.shape, F32),
        compiler_params=_params(("arbitrary", "arbitrary")),
        name="merge",
    )(a_att, b_att, za, zb, ga, gb, x, gate, w_o_a, w_o_b, w_out)


def _rope_tables(seq):
    t = jnp.arange(seq, dtype=jnp.int32)
    inv = 1.0 / (ROPE_THETA ** (jnp.arange(ROT_HALF, dtype=F32) / ROT_HALF))
    ang_r = (t // GRID_W).astype(F32)[:, None] * inv[None, :]
    ang_c = (t % GRID_W).astype(F32)[:, None] * inv[None, :]
    cos = jnp.concatenate([jnp.cos(ang_r)] * 2 + [jnp.cos(ang_c)] * 2, axis=1)
    sin = jnp.concatenate([-jnp.sin(ang_r), jnp.sin(ang_r), -jnp.sin(ang_c), jnp.sin(ang_c)], axis=1)
    reps = LANES // HEAD_DIM
    return jnp.tile(cos, (1, reps)), jnp.tile(sin, (1, reps))


def _group_sum_matrix():
    idx = np.arange(MXU_COLS) // HEAD_DIM
    return jnp.asarray(idx[:, None] == idx[None, :], dtype=BF16)


def kernel(x, c, ctx, c_ctx, w_ada, b_ada, norm_g, w_in, q_norm_a, k_norm_a, q_norm_b, k_norm_b,
           rpb, w_o_a, w_o_b, w_out):
    batch, seq, _ = x.shape
    ctx_len = ctx.shape[1]
    depth = w_ada.shape[0]
    rows = seq // GRID_W
    assert seq % GRID_W == 0 and rows >= WIN_R and seq % PROJ_ROWS == 0 and seq % GQA_Q_ROWS == 0

    pad = -(batch + 1) % 8
    c_rows = jnp.concatenate([c, c_ctx[None, :], jnp.zeros((pad, D_MODEL), F32)], axis=0)
    mod = _modulation(c_rows, w_ada, b_ada)

    cos_x, sin_x = _rope_tables(seq)
    ctx_rows = batch * ctx_len
    ctx = ctx.reshape(1, ctx_rows, D_MODEL)
    cos_c = jnp.ones((ctx_rows, LANES), F32)
    sin_c = jnp.zeros((ctx_rows, LANES), F32)
    flat = lambda a: a.reshape((1, ctx_rows) + a.shape[2:])
    per_batch_ctx = lambda a: a.reshape((batch, a.shape[1] // batch) + a.shape[2:])
    gsum = _group_sum_matrix()
    tables = _bias_tables(rpb)
    w_in, w_o_a, w_o_b, w_out = (w.astype(BF16) for w in (w_in, w_o_a, w_o_b, w_out))

    for l in range(depth):
        update_ctx = l < depth - 1
        shift, scale, gate = (mod[l, :, i * D_MODEL:(i + 1) * D_MODEL] for i in range(3))
        per_batch = lambda m: m[:batch, None, :]
        for_ctx = lambda m: m[batch][None, None, :]
        tile = lambda g, n: jnp.tile(g, n)
        gains = jnp.stack([tile(k_norm_a[l], NA_HEADS), tile(k_norm_b[l], NA_HEADS),
                           tile(q_norm_a[l], NA_HEADS) * (QK_SCALE * LOG2E),
                           tile(q_norm_b[l], GQA_Q_HEADS) * (QK_SCALE * LOG2E)])
        g_row = norm_g[l][None, :]

        ka, vat, kbd, vbt, qa, qb, za, zb, ga, gb = _project(
            x, per_batch(shift), per_batch(scale), g_row, w_in, l, gsum, gains, cos_x, sin_x,
            len(SEG_WIDTHS), PROJ_ROWS, PROJ_ROWS)
        nseg_c = len(SEG_WIDTHS) if update_ctx else KV_SEGS
        pc = _project(ctx, for_ctx(shift), for_ctx(scale), g_row, w_in, l, gsum, gains, cos_c, sin_c,
                      nseg_c, min(PROJ_ROWS, ctx_rows), ctx_len)
        pc = [per_batch_ctx(a) for a in pc]
        cka, cvat, ckbd, cvbt = pc[:KV_SEGS]

        a_att = _neighbourhood_attention(qa, ka, vat, cka, cvat, tables, l)
        b_att = _gqa_attention(qb, kbd, vbt, ckbd, cvbt)
        x_new = _merge(a_att, b_att, za, zb, ga, gb, x, per_batch(gate), w_o_a, w_o_b, w_out, l, MERGE_ROWS)

        if update_ctx:
            _, _, _, _, cqa, cqb, cza, czb, cga, cgb = pc
            c_a, c_b = _context_attention(cqa, cka, cvat, cqb, ckbd, cvbt)
            ctx = _merge(*(flat(a) for a in (c_a, c_b, cza, czb, cga, cgb)), ctx, for_ctx(gate),
                         w_o_a, w_o_b, w_out, l, min(MERGE_ROWS, ctx_rows))
        x = x_new
    return x
```

```python
import functools

import numpy as np
import jax
import jax.numpy as jnp
from jax import lax
from jax.experimental import pallas as pl
from jax.experimental.pallas import tpu as pltpu

D_MODEL = 1024
GRID_W = 64
HEAD_DIM = 64
NA_HEADS = 8
NA_WIDTH = NA_HEADS * HEAD_DIM
WIN_R = 8
WIN_C = 16
GQA_Q_HEADS = 8
GQA_KV_HEADS = 2
GQA_WIDTH = GQA_Q_HEADS * HEAD_DIM
GQA_KV_WIDTH = GQA_KV_HEADS * HEAD_DIM
ROPE_THETA = 10000.0
ROT_AXIS = HEAD_DIM // 2
ROT_HALF = ROT_AXIS // 2
EPS = 1e-6
QK_SCALE = HEAD_DIM ** -0.5
LOG2E = 1.4426950408889634

LANES = 128
MXU_COLS = 256
MASKED = -1e30
VMEM_LIMIT = 56 * 1024 * 1024

PROJ_ROWS = 512
MERGE_ROWS = 1024
GQA_Q_ROWS = 256
GQA_Q_BLOCKS = 2
GQA_K_ROWS = 256
GQA_SUB_ROWS = 128
GQA_AHEAD = 1
QUAD = 4
ONES_ROWS = 16
NA_SUB_ROWS = 128
NA_AHEAD = 4
NA_UNROLL = 8

F32 = jnp.float32
BF16 = jnp.bfloat16

SEG_KA, SEG_VA, SEG_KB, SEG_VB, SEG_QA, SEG_QB, SEG_ZA, SEG_ZB, SEG_GA, SEG_GB = range(10)
SEG_WIDTHS = (NA_WIDTH, NA_WIDTH, GQA_KV_WIDTH, GQA_KV_WIDTH, NA_WIDTH, GQA_WIDTH,
              NA_WIDTH, GQA_WIDTH, D_MODEL, D_MODEL)
SEG_OFFSETS = tuple(int(v) for v in np.cumsum((0,) + SEG_WIDTHS))
SEG_OUT_WIDTHS = tuple(2 * w if s == SEG_KB else w for s, w in enumerate(SEG_WIDTHS))
KV_SEGS = 4
SEG_ISSUE_RANK = {SEG_GA: 0, SEG_GB: 1, SEG_ZA: 2, SEG_ZB: 3, SEG_QB: 4, SEG_KB: 5, SEG_QA: 6, SEG_KA: 7,
                  SEG_VB: 8, SEG_VA: 9}
SEG_GAIN_ROW = {SEG_KA: 0, SEG_KB: 1, SEG_QA: 2, SEG_QB: 3}
SEG_ROPE = (SEG_KB, SEG_QB)
SEG_T_SLAB = {SEG_VA: LANES, SEG_VB: GQA_K_ROWS}


def _dot(a, b):
    return jnp.dot(a, b, preferred_element_type=F32)


def _dot_t(a, b):
    return lax.dot_general(a, b, (((1,), (1,)), ((), ())), preferred_element_type=F32)


def _params(semantics):
    return pltpu.CompilerParams(dimension_semantics=semantics, vmem_limit_bytes=VMEM_LIMIT)


def _mod_kernel(c_ref, w_ref, b_ref, o_ref):
    c = c_ref[...]
    a = c * jax.nn.sigmoid(c)
    o_ref[0] = jnp.dot(a, w_ref[0], preferred_element_type=F32,
                       precision=lax.Precision.HIGHEST) + b_ref[0]


def _modulation(c_rows, w_ada, b_ada):
    depth = w_ada.shape[0]
    rows = c_rows.shape[0]
    ncol = w_ada.shape[2] // D_MODEL
    return pl.pallas_call(
        _mod_kernel,
        grid=(depth, ncol),
        in_specs=[
            pl.BlockSpec((rows, D_MODEL), lambda l, j: (0, 0)),
            pl.BlockSpec((1, D_MODEL, D_MODEL), lambda l, j: (l, 0, j)),
            pl.BlockSpec((1, 1, D_MODEL), lambda l, j: (l, 0, j)),
        ],
        out_specs=pl.BlockSpec((1, rows, D_MODEL), lambda l, j: (l, 0, j)),
        out_shape=jax.ShapeDtypeStruct((depth, rows, ncol * D_MODEL), F32),
        compiler_params=_params(("arbitrary", "arbitrary")),
        name="mod",
    )(c_rows, w_ada, b_ada.reshape(depth, 1, -1))


def _head_norm(p, gsum, gain):
    sq = (p * p).astype(BF16)
    cols = min(MXU_COLS, p.shape[1])
    parts = []
    for j in range(p.shape[1] // cols):
        sl = slice(j * cols, (j + 1) * cols)
        ss = _dot(sq[:, sl], gsum[:cols, :cols])
        parts.append(p[:, sl] * lax.rsqrt(ss * (1.0 / HEAD_DIM) + EPS))
    y = parts[0] if len(parts) == 1 else jnp.concatenate(parts, axis=1)
    return y * gain


def _rope(y, cos, sin):
    width = y.shape[1]
    lane = lax.broadcasted_iota(jnp.int32, y.shape, 1)
    ahead = pltpu.roll(y, width - ROT_HALF, axis=1)
    behind = pltpu.roll(y, ROT_HALF, axis=1)
    partner = jnp.where((lane % ROT_AXIS) < ROT_HALF, ahead, behind)
    reps = width // LANES
    cos = cos if reps == 1 else jnp.concatenate([cos] * reps, axis=1)
    sin = sin if reps == 1 else jnp.concatenate([sin] * reps, axis=1)
    return y * cos + partner * sin


def _duplicate_heads(y):
    lo = lax.broadcasted_iota(jnp.int32, y.shape, 1) < HEAD_DIM
    swapped = pltpu.roll(y, HEAD_DIM, axis=1)
    return jnp.concatenate([jnp.where(lo, y, swapped), jnp.where(lo, swapped, y)], axis=1)


def _proj_kernel(x_ref, shift_ref, scale_ref, g_ref, w_ref, gsum_ref, gain_ref, cos_ref, sin_ref,
                 *out_refs, nseg, max_slab):
    x = x_ref[0]
    ms = jnp.mean(x * x, axis=-1, keepdims=True)
    gmod = g_ref[...] * (1.0 + scale_ref[0])
    h = (x * lax.rsqrt(ms + EPS) * gmod + shift_ref[0]).astype(BF16)
    gsum = gsum_ref[...]
    for seg in sorted(range(nseg), key=SEG_ISSUE_RANK.__getitem__):
        width = SEG_WIDTHS[seg]
        p = _dot(h, w_ref[:, SEG_OFFSETS[seg]:SEG_OFFSETS[seg + 1]])
        if seg in SEG_GAIN_ROW:
            row = SEG_GAIN_ROW[seg]
            p = _head_norm(p, gsum, gain_ref[row:row + 1, :width])
        if seg in SEG_ROPE:
            p = _rope(p, cos_ref[...], sin_ref[...])
        if seg == SEG_KB:
            p = _duplicate_heads(p)
        if seg in (SEG_ZA, SEG_ZB):
            p = p * jax.nn.sigmoid(p)
        if seg in (SEG_GA, SEG_GB):
            p = jax.nn.sigmoid(p)
        if seg in SEG_T_SLAB:
            pt = p.T.astype(BF16)
            slab = min(SEG_T_SLAB[seg], max_slab)
            for j in range(pt.shape[1] // slab):
                out_refs[seg][0, j] = pt[:, j * slab:(j + 1) * slab]
        else:
            out_refs[seg][0] = p.astype(BF16)


def _project(x, shift, scale, norm_g, w_all, layer, gsum, gains, cos, sin, nseg, rows, max_slab):
    groups, total, _ = x.shape
    ncols = SEG_OFFSETS[nseg]
    const = lambda g, i: (0, 0)

    def out_layout(s):
        width = SEG_OUT_WIDTHS[s]
        if s not in SEG_T_SLAB:
            return (groups, total, width), pl.BlockSpec((1, rows, width), lambda g, i: (g, i, 0))
        slab = min(SEG_T_SLAB[s], max_slab)
        return ((groups, total // slab, width, slab),
                pl.BlockSpec((1, rows // slab, width, slab), lambda g, i: (g, i, 0, 0)))

    layouts = [out_layout(s) for s in range(nseg)]
    return pl.pallas_call(
        functools.partial(_proj_kernel, nseg=nseg, max_slab=max_slab),
        grid=(groups, total // rows),
        in_specs=[
            pl.BlockSpec((1, rows, D_MODEL), lambda g, i: (g, i, 0)),
            pl.BlockSpec((1, 1, D_MODEL), lambda g, i: (g, 0, 0)),
            pl.BlockSpec((1, 1, D_MODEL), lambda g, i: (g, 0, 0)),
            pl.BlockSpec((1, D_MODEL), const),
            pl.BlockSpec((None, D_MODEL, ncols), lambda g, i: (layer, 0, 0), pipeline_mode=pl.Buffered(1)),
            pl.BlockSpec((MXU_COLS, MXU_COLS), const),
            pl.BlockSpec(gains.shape, const),
            pl.BlockSpec((rows, LANES), lambda g, i: (i, 0)),
            pl.BlockSpec((rows, LANES), lambda g, i: (i, 0)),
        ],
        out_specs=[spec for _, spec in layouts],
        out_shape=[jax.ShapeDtypeStruct(shape, BF16) for shape, _ in layouts],
        compiler_params=_params(("arbitrary", "arbitrary")),
        name="proj",
    )(x, shift, scale, norm_g, w_all, gsum, gains, cos, sin)


def _stack_heads(pair):
    lo = lax.broadcasted_iota(jnp.int32, pair.shape, 1) < HEAD_DIM
    zero = jnp.zeros_like(pair)
    return jnp.concatenate([jnp.where(lo, pair, zero), jnp.where(lo, zero, pair)], axis=0)


def _stack_quad(q):
    head = lax.broadcasted_iota(jnp.int32, q.shape, 1) // HEAD_DIM
    zero = jnp.zeros_like(q)
    return jnp.concatenate([jnp.where(head == h, q, zero) for h in range(QUAD)], axis=0)


def _unstack_quad_t(o, rows):
    ot = o.T
    head = lax.broadcasted_iota(jnp.int32, (rows, QUAD * HEAD_DIM), 1) // HEAD_DIM
    out = ot[:rows]
    for h in range(1, QUAD):
        out = jnp.where(head == h, ot[h * rows:(h + 1) * rows], out)
    return out


def _pv_t(vt, s, m):
    lhs = jnp.concatenate([vt, jnp.ones((ONES_ROWS, vt.shape[1]), vt.dtype)], axis=0)
    return _dot(lhs, jnp.exp2((s - m).astype(BF16)))


def _normalise_t(acc, dims):
    return acc[:dims] * (1.0 / acc[dims:dims + 1])


def _softmax_pv_t(logits, values_t):
    m = functools.reduce(jnp.maximum, [jnp.max(s, axis=0, keepdims=True) for s in logits])
    acc = functools.reduce(jnp.add, [_pv_t(vt, s, m) for vt, s in zip(values_t, logits)])
    return _normalise_t(acc, values_t[0].shape[0])


def _na_window(r, rows, slabs):
    start = jnp.clip(r - WIN_R // 2, 0, rows - WIN_R)
    slab0 = jnp.minimum(start // 2, rows // 2 - slabs)
    first = 2 * slab0
    blocks = []
    for w in range(2 * slabs):
        key_row = first + w
        valid = (key_row >= start) & (key_row < start + WIN_R)
        blocks.append(jnp.where(valid, key_row - r + (WIN_R - 1), 2 * WIN_R - 1))
    return slab0, blocks


def _na_kernel(q_ref, k_ref, vt_ref, kc_ref, vct_ref, t_ref, o_ref, qt_ref, *, rows, unroll):
    kc = kc_ref[0]
    vct = jnp.concatenate([vct_ref[0, j] for j in range(vct_ref.shape[1])], axis=1)
    assert unroll % 2 == 0 and rows % 2 == 0 and WIN_R % 2 == 0

    def logits(r, slot):
        slabs = WIN_R // 2 + slot % 2
        slab0, blocks = _na_window(r, rows, slabs)
        qt_ref[slot] = _stack_quad(q_ref[0, pl.ds(pl.multiple_of(r * GRID_W, GRID_W), GRID_W), :]).T
        qs_t = qt_ref[slot]
        kw = k_ref[0, pl.ds(pl.multiple_of(slab0 * LANES, LANES), slabs * LANES), :]
        bias = jnp.concatenate([t_ref[b] for b in blocks], axis=0)
        return _dot(kw, qs_t) + bias, _dot(kc, qs_t), slab0

    def finish(r, s_win, s_ctx, slab0):
        slabs = s_win.shape[0] // LANES
        vt = vt_ref[0, pl.ds(slab0, slabs)]
        vtw = jnp.concatenate([vt[j] for j in range(slabs)], axis=1)
        sub = NA_SUB_ROWS
        parts = [s[i:i + sub] for s in (s_ctx, s_win) for i in range(0, s.shape[0], sub)]
        _, acc = _attend_t(parts, jnp.concatenate([vct, vtw], axis=1), None)
        o = _normalise_t(acc, vct.shape[0])
        o_ref[0, pl.ds(pl.multiple_of(r * GRID_W, GRID_W), GRID_W), :] = _unstack_quad_t(o, GRID_W).astype(BF16)

    def body(i, carry):
        pending = []
        for e in range(unroll):
            pending.append((i * unroll + e,) + logits(i * unroll + e, e))
            if len(pending) > NA_AHEAD:
                finish(*pending.pop(0))
        for args in pending:
            finish(*args)
        return carry

    lax.fori_loop(0, rows // unroll, body, 0)


def _neighbourhood_attention(qa, ka, vat, cka, cvat, tables, layer):
    batch, seq, _ = qa.shape
    ctx_len = cka.shape[1]
    rows = seq // GRID_W
    width = QUAD * HEAD_DIM
    blk = lambda n: pl.BlockSpec((1, n, width), lambda b, j: (b, 0, j))
    vblk = lambda a: pl.BlockSpec((1, a.shape[1], width, LANES), lambda b, j: (b, 0, j, 0))
    return pl.pallas_call(
        functools.partial(_na_kernel, rows=rows, unroll=NA_UNROLL),
        grid=(batch, NA_HEADS // QUAD),
        in_specs=[blk(seq), blk(seq), vblk(vat), blk(ctx_len), vblk(cvat),
                  pl.BlockSpec((None, None) + tables.shape[2:], lambda b, j: (layer, j, 0, 0, 0))],
        out_specs=blk(seq),
        out_shape=jax.ShapeDtypeStruct((batch, seq, NA_WIDTH), BF16),
        scratch_shapes=[pltpu.VMEM((NA_UNROLL, width, QUAD * GRID_W), BF16)],
        compiler_params=_params(("arbitrary", "arbitrary")),
        name="na",
    )(qa, ka, vat, cka, cvat, tables)


def _bias_kernel(rpb_ref, onehot_ref, mask_ref, o_ref):
    o_ref[0] = (jnp.dot(rpb_ref[0], onehot_ref[...], preferred_element_type=F32,
                        precision=lax.Precision.HIGHEST) + mask_ref[...]) * LOG2E


def _bias_tables(rpb):
    depth = rpb.shape[0]
    n_row_off, n_col_off = 2 * WIN_R - 1, 2 * WIN_C - 1
    cols = np.arange(GRID_W)
    cstart = np.clip(cols - WIN_C // 2, 0, GRID_W - WIN_C)
    kcol = np.arange(GRID_W)
    valid = (kcol[None, :] >= cstart[:, None]) & (kcol[None, :] < cstart[:, None] + WIN_C)
    col_off = kcol[None, :] - cols[:, None] + (WIN_C - 1)
    pad_r, pad_c = -n_row_off % 8, -n_col_off % 8
    assert pad_r >= 1
    onehot = (np.arange(n_col_off + pad_c)[:, None, None] == col_off[None]) & valid[None]
    onehot = jnp.asarray(onehot.reshape(n_col_off + pad_c, GRID_W * GRID_W), dtype=F32)
    mask = jnp.asarray(np.where(valid, 0.0, MASKED).reshape(1, GRID_W * GRID_W), dtype=F32)
    heads = depth * NA_HEADS
    rpb_p = jnp.pad(rpb.reshape(heads, n_row_off, n_col_off), ((0, 0), (0, pad_r), (0, pad_c)))
    toep = pl.pallas_call(
        _bias_kernel,
        grid=(heads,),
        in_specs=[pl.BlockSpec((1,) + rpb_p.shape[1:], lambda h: (h, 0, 0)),
                  pl.BlockSpec(onehot.shape, lambda h: (0, 0)),
                  pl.BlockSpec(mask.shape, lambda h: (0, 0))],
        out_specs=pl.BlockSpec((1, n_row_off + pad_r, GRID_W * GRID_W), lambda h: (h, 0, 0)),
        out_shape=jax.ShapeDtypeStruct((heads, n_row_off + pad_r, GRID_W * GRID_W), F32),
        compiler_params=_params(("arbitrary",)),
        name="bias",
    )(rpb_p, onehot, mask)
    groups = NA_HEADS // QUAD
    toep = toep.reshape(depth, groups, QUAD, n_row_off + pad_r, GRID_W, GRID_W)
    table = jnp.transpose(toep, (0, 1, 3, 5, 2, 4)).reshape(depth, groups, -1, GRID_W, QUAD * GRID_W)
    return jnp.where((jnp.arange(table.shape[2]) < n_row_off)[None, None, :, None, None], table, MASKED * LOG2E)


def _stack_group(q):
    return jnp.concatenate([_stack_heads(q[:, :LANES]), _stack_heads(q[:, LANES:])], axis=0)


def _unstack_group_t(o, rows):
    pairs = [jnp.concatenate([o[:, (2 * j) * rows:(2 * j + 1) * rows],
                              o[:, (2 * j + 1) * rows:(2 * j + 2) * rows]], axis=0).T for j in range(2)]
    return jnp.concatenate(pairs, axis=1)


def _scores_t(qs_t, k):
    return _dot(k, qs_t)


def _attend_t(parts, vt, state):
    m_old = None if state is None else state[0]
    m, ps, ms = m_old, [], []
    for s in parts:
        m_part = jnp.max(s, axis=0, keepdims=True)
        m = m_part if m is None else jnp.maximum(m, m_part)
        ps.append(jnp.exp2((s - m).astype(BF16)))
        ms.append(m)
    ps = [p * jnp.exp2(m_i - m).astype(BF16) for p, m_i in zip(ps[:-1], ms[:-1])] + ps[-1:]
    lhs = jnp.concatenate([vt, jnp.ones((ONES_ROWS, vt.shape[1]), vt.dtype)], axis=0)
    pv = _dot(lhs, ps[0] if len(ps) == 1 else jnp.concatenate(ps, axis=0))
    if state is None:
        return m, pv
    return m, jnp.exp2(m_old - m) * state[1] + pv


def _gqa_kernel(q_ref, k_ref, vt_ref, kc_ref, vct_ref, o_ref, qt_ref, *, chunks, tq, tk, blocks):
    for j in range(blocks):
        qt_ref[j] = _stack_group(q_ref[0, j * tq:(j + 1) * tq, :]).T
    pending = []
    states = [None] * blocks

    def drain(limit):
        while len(pending) > limit:
            j, parts, vt = pending.pop(0)
            states[j] = _attend_t(parts, vt, states[j])

    def logits(qs, keys, c, rows):
        sub = min(GQA_SUB_ROWS, rows)
        return [_scores_t(qs, keys[0, c * rows + i * sub:c * rows + (i + 1) * sub, :]) for i in range(rows // sub)]

    for j in range(blocks):
        qs = qt_ref[j]
        for c in range(vct_ref.shape[1]):
            pending.append((j, logits(qs, kc_ref, c, vct_ref.shape[3]), vct_ref[0, c]))
            drain(GQA_AHEAD)
        for c in range(chunks):
            pending.append((j, logits(qs, k_ref, c, tk), vt_ref[0, c]))
            drain(GQA_AHEAD)
    drain(0)
    for j in range(blocks):
        _, acc = states[j]
        o_ref[0, j * tq:(j + 1) * tq, :] = _unstack_group_t(_normalise_t(acc, HEAD_DIM), tq).astype(BF16)


def _gqa_attention(qb, kbd, vbt, ckbd, cvbt):
    batch, seq, _ = qb.shape
    ctx_len = ckbd.shape[1]
    tq, tk, blocks = GQA_Q_ROWS, vbt.shape[3], GQA_Q_BLOCKS
    kspec = lambda n: pl.BlockSpec((1, n, LANES), lambda b, g, i: (b, 0, g))
    vspec = lambda a: pl.BlockSpec((1, a.shape[1], HEAD_DIM, a.shape[3]), lambda b, g, i: (b, 0, g, 0))
    qspec = pl.BlockSpec((1, blocks * tq, 2 * LANES), lambda b, g, i: (b, i, g))
    return pl.pallas_call(
        functools.partial(_gqa_kernel, chunks=seq // tk, tq=tq, tk=tk, blocks=blocks),
        grid=(batch, GQA_KV_HEADS, seq // (blocks * tq)),
        in_specs=[qspec, kspec(seq), vspec(vbt), kspec(ctx_len), vspec(cvbt)],
        out_specs=qspec,
        out_shape=jax.ShapeDtypeStruct((batch, seq, GQA_WIDTH), BF16),
        scratch_shapes=[pltpu.VMEM((blocks, LANES, (GQA_Q_HEADS // GQA_KV_HEADS) * tq), BF16)],
        compiler_params=_params(("arbitrary", "arbitrary", "arbitrary")),
        name="gqa",
    )(qb, kbd, vbt, ckbd, cvbt)


def _ctx_kernel(qa_ref, ka_ref, vat_ref, qb_ref, kb_ref, vbt_ref, oa_ref, ob_ref):
    n = qa_ref.shape[1]
    width = QUAD * HEAD_DIM
    for g in range(NA_HEADS // QUAD):
        sl = slice(g * width, (g + 1) * width)
        vt = jnp.concatenate([vat_ref[0, j, sl, :] for j in range(vat_ref.shape[1])], axis=1)
        o = _softmax_pv_t([_dot_t(ka_ref[0, :, sl], _stack_quad(qa_ref[0, :, sl]))], [vt])
        oa_ref[0, :, sl] = _unstack_quad_t(o, n).astype(BF16)
    for g in range(GQA_KV_HEADS):
        sl = slice(g * 2 * LANES, (g + 1) * 2 * LANES)
        s = _scores_t(_stack_group(qb_ref[0, :, sl]).T, kb_ref[0, :, g * LANES:(g + 1) * LANES])
        vt = jnp.concatenate([vbt_ref[0, j, g * HEAD_DIM:(g + 1) * HEAD_DIM, :] for j in range(vbt_ref.shape[1])],
                             axis=1)
        o = _softmax_pv_t([s], [vt])
        ob_ref[0, :, sl] = _unstack_group_t(o, n).astype(BF16)


def _context_attention(cqa, cka, cvat, cqb, ckbd, cvbt):
    batch, n, _ = cqa.shape
    spec = lambda a: pl.BlockSpec((1,) + a.shape[1:], lambda b: (b,) + (0,) * (a.ndim - 1))
    args = (cqa, cka, cvat, cqb, ckbd, cvbt)
    return pl.pallas_call(
        _ctx_kernel,
        grid=(batch,),
        in_specs=[spec(a) for a in args],
        out_specs=[spec(cqa), spec(cqb)],
        out_shape=[jax.ShapeDtypeStruct(cqa.shape, BF16), jax.ShapeDtypeStruct(cqb.shape, BF16)],
        compiler_params=_params(("arbitrary",)),
        name="ctx",
    )(*args)


def _merge_kernel(a_ref, b_ref, za_ref, zb_ref, ga_ref, gb_ref, x_ref, gate_ref,
                  woa_ref, wob_ref, wout_ref, o_ref):
    o_a = _dot(a_ref[0] * za_ref[0], woa_ref[...])
    o_b = _dot(b_ref[0] * zb_ref[0], wob_ref[...])
    merged = ga_ref[0].astype(F32) * o_a + gb_ref[0].astype(F32) * o_b
    o_ref[0] = x_ref[0] + gate_ref[0] * _dot(merged.astype(BF16), wout_ref[...])


def _merge(a_att, b_att, za, zb, ga, gb, x, gate, w_o_a, w_o_b, w_out, layer, rows):
    groups, total, _ = x.shape
    act = lambda w: pl.BlockSpec((1, rows, w), lambda g, i: (g, i, 0))
    const = lambda a: pl.BlockSpec((None,) + a.shape[1:], lambda g, i: (layer, 0, 0))
    return pl.pallas_call(
        _merge_kernel,
        grid=(groups, total // rows),
        in_specs=[act(NA_WIDTH), act(GQA_WIDTH), act(NA_WIDTH), act(GQA_WIDTH), act(D_MODEL), act(D_MODEL),
                  act(D_MODEL), pl.BlockSpec((1, 1, D_MODEL), lambda g, i: (g, 0, 0)),
                  const(w_o_a), const(w_o_b), const(w_out)],
        out_specs=act(D_MODEL),
        out_shape=jax.ShapeDtypeStruct(x.shape, F32),
        compiler_params=_params(("arbitrary", "arbitrary")),
        name="merge",
    )(a_att, b_att, za, zb, ga, gb, x, gate, w_o_a, w_o_b, w_out)


def _rope_tables(seq):
    t = jnp.arange(seq, dtype=jnp.int32)
    inv = 1.0 / (ROPE_THETA ** (jnp.arange(ROT_HALF, dtype=F32) / ROT_HALF))
    ang_r = (t // GRID_W).astype(F32)[:, None] * inv[None, :]
    ang_c = (t % GRID_W).astype(F32)[:, None] * inv[None, :]
    cos = jnp.concatenate([jnp.cos(ang_r)] * 2 + [jnp.cos(ang_c)] * 2, axis=1)
    sin = jnp.concatenate([-jnp.sin(ang_r), jnp.sin(ang_r), -jnp.sin(ang_c), jnp.sin(ang_c)], axis=1)
    reps = LANES // HEAD_DIM
    return jnp.tile(cos, (1, reps)), jnp.tile(sin, (1, reps))


def _group_sum_matrix():
    idx = np.arange(MXU_COLS) // HEAD_DIM
    return jnp.asarray(idx[:, None] == idx[None, :], dtype=BF16)


def kernel(x, c, ctx, c_ctx, w_ada, b_ada, norm_g, w_in, q_norm_a, k_norm_a, q_norm_b, k_norm_b,
           rpb, w_o_a, w_o_b, w_out):
    batch, seq, _ = x.shape
    ctx_len = ctx.shape[1]
    depth = w_ada.shape[0]
    rows = seq // GRID_W
    assert seq % GRID_W == 0 and rows >= WIN_R and seq % PROJ_ROWS == 0 and seq % GQA_Q_ROWS == 0

    pad = -(batch + 1) % 8
    c_rows = jnp.concatenate([c, c_ctx[None, :], jnp.zeros((pad, D_MODEL), F32)], axis=0)
    mod = _modulation(c_rows, w_ada, b_ada)

    cos_x, sin_x = _rope_tables(seq)
    ctx_rows = batch * ctx_len
    ctx = ctx.reshape(1, ctx_rows, D_MODEL)
    cos_c = jnp.ones((ctx_rows, LANES), F32)
    sin_c = jnp.zeros((ctx_rows, LANES), F32)
    flat = lambda a: a.reshape((1, ctx_rows) + a.shape[2:])
    per_batch_ctx = lambda a: a.reshape((batch, a.shape[1] // batch) + a.shape[2:])
    gsum = _group_sum_matrix()
    tables = _bias_tables(rpb)
    w_in, w_o_a, w_o_b, w_out = (w.astype(BF16) for w in (w_in, w_o_a, w_o_b, w_out))

    for l in range(depth):
        update_ctx = l < depth - 1
        shift, scale, gate = (mod[l, :, i * D_MODEL:(i + 1) * D_MODEL] for i in range(3))
        per_batch = lambda m: m[:batch, None, :]
        for_ctx = lambda m: m[batch][None, None, :]
        tile = lambda g, n: jnp.tile(g, n)
        gains = jnp.stack([tile(k_norm_a[l], NA_HEADS), tile(k_norm_b[l], NA_HEADS),
                           tile(q_norm_a[l], NA_HEADS) * (QK_SCALE * LOG2E),
                           tile(q_norm_b[l], GQA_Q_HEADS) * (QK_SCALE * LOG2E)])
        g_row = norm_g[l][None, :]

        ka, vat, kbd, vbt, qa, qb, za, zb, ga, gb = _project(
            x, per_batch(shift), per_batch(scale), g_row, w_in, l, gsum, gains, cos_x, sin_x,
            len(SEG_WIDTHS), PROJ_ROWS, PROJ_ROWS)
        nseg_c = len(SEG_WIDTHS) if update_ctx else KV_SEGS
        pc = _project(ctx, for_ctx(shift), for_ctx(scale), g_row, w_in, l, gsum, gains, cos_c, sin_c,
                      nseg_c, min(PROJ_ROWS, ctx_rows), ctx_len)
        pc = [per_batch_ctx(a) for a in pc]
        cka, cvat, ckbd, cvbt = pc[:KV_SEGS]

        a_att = _neighbourhood_attention(qa, ka, vat, cka, cvat, tables, l)
        b_att = _gqa_attention(qb, kbd, vbt, ckbd, cvbt)
        x_new = _merge(a_att, b_att, za, zb, ga, gb, x, per_batch(gate), w_o_a, w_o_b, w_out, l, MERGE_ROWS)

        if update_ctx:
            _, _, _, _, cqa, cqb, cza, czb, cga, cgb = pc
            c_a, c_b = _context_attention(cqa, cka, cvat, cqb, ckbd, cvbt)
            ctx = _merge(*(flat(a) for a in (c_a, c_b, cza, czb, cga, cgb)), ctx, for_ctx(gate),
                         w_o_a, w_o_b, w_out, l, min(MERGE_ROWS, ctx_rows))
        x = x_new
    return x
```

```python
import functools

import numpy as np
import jax
import jax.numpy as jnp
from jax import lax
from jax.experimental import pallas as pl
from jax.experimental.pallas import tpu as pltpu

D_MODEL = 1024
GRID_W = 64
HEAD_DIM = 64
NA_HEADS = 8
NA_WIDTH = NA_HEADS * HEAD_DIM
WIN_R = 8
WIN_C = 16
GQA_Q_HEADS = 8
GQA_KV_HEADS = 2
GQA_WIDTH = GQA_Q_HEADS * HEAD_DIM
GQA_KV_WIDTH = GQA_KV_HEADS * HEAD_DIM
ROPE_THETA = 10000.0
ROT_AXIS = HEAD_DIM // 2
ROT_HALF = ROT_AXIS // 2
EPS = 1e-6
QK_SCALE = HEAD_DIM ** -0.5
LOG2E = 1.4426950408889634

LANES = 128
MXU_COLS = 256
MASKED = -1e30
VMEM_LIMIT = 56 * 1024 * 1024

PROJ_ROWS = 512
MERGE_ROWS = 1024
GQA_Q_ROWS = 256
GQA_Q_BLOCKS = 4
GQA_K_ROWS = 256
GQA_SUB_ROWS = 128
GQA_AHEAD = 1
QUAD = 4
ONES_ROWS = 16
NA_SUB_ROWS = 128
NA_AHEAD = 6
NA_UNROLL = 8

F32 = jnp.float32
BF16 = jnp.bfloat16

SEG_KA, SEG_VA, SEG_KB, SEG_VB, SEG_QA, SEG_QB, SEG_ZA, SEG_ZB, SEG_GA, SEG_GB = range(10)
SEG_WIDTHS = (NA_WIDTH, NA_WIDTH, GQA_KV_WIDTH, GQA_KV_WIDTH, NA_WIDTH, GQA_WIDTH,
              NA_WIDTH, GQA_WIDTH, D_MODEL, D_MODEL)
SEG_OFFSETS = tuple(int(v) for v in np.cumsum((0,) + SEG_WIDTHS))
SEG_OUT_WIDTHS = tuple(2 * w if s == SEG_KB else w for s, w in enumerate(SEG_WIDTHS))
KV_SEGS = 4
SEG_ISSUE_RANK = {SEG_GA: 0, SEG_GB: 1, SEG_ZA: 2, SEG_ZB: 3, SEG_QB: 4, SEG_KB: 5, SEG_QA: 6, SEG_KA: 7,
                  SEG_VB: 8, SEG_VA: 9}
SEG_GAIN_ROW = {SEG_KA: 0, SEG_KB: 1, SEG_QA: 2, SEG_QB: 3}
SEG_ROPE = (SEG_KB, SEG_QB)
SEG_T_SLAB = {SEG_VA: LANES, SEG_VB: GQA_K_ROWS}


def _dot(a, b):
    return jnp.dot(a, b, preferred_element_type=F32)


def _dot_t(a, b):
    return lax.dot_general(a, b, (((1,), (1,)), ((), ())), preferred_element_type=F32)


def _params(semantics):
    return pltpu.CompilerParams(dimension_semantics=semantics, vmem_limit_bytes=VMEM_LIMIT)


def _mod_kernel(c_ref, w_ref, b_ref, o_ref):
    c = c_ref[...]
    a = c * jax.nn.sigmoid(c)
    o_ref[0] = jnp.dot(a, w_ref[0], preferred_element_type=F32,
                       precision=lax.Precision.HIGHEST) + b_ref[0]


def _modulation(c_rows, w_ada, b_ada):
    depth = w_ada.shape[0]
    rows = c_rows.shape[0]
    ncol = w_ada.shape[2] // D_MODEL
    return pl.pallas_call(
        _mod_kernel,
        grid=(depth, ncol),
        in_specs=[
            pl.BlockSpec((rows, D_MODEL), lambda l, j: (0, 0)),
            pl.BlockSpec((1, D_MODEL, D_MODEL), lambda l, j: (l, 0, j)),
            pl.BlockSpec((1, 1, D_MODEL), lambda l, j: (l, 0, j)),
        ],
        out_specs=pl.BlockSpec((1, rows, D_MODEL), lambda l, j: (l, 0, j)),
        out_shape=jax.ShapeDtypeStruct((depth, rows, ncol * D_MODEL), F32),
        compiler_params=_params(("arbitrary", "arbitrary")),
        name="mod",
    )(c_rows, w_ada, b_ada.reshape(depth, 1, -1))


def _head_norm(p, gsum, gain):
    sq = (p * p).astype(BF16)
    cols = min(MXU_COLS, p.shape[1])
    parts = []
    for j in range(p.shape[1] // cols):
        sl = slice(j * cols, (j + 1) * cols)
        ss = _dot(sq[:, sl], gsum[:cols, :cols])
        parts.append(p[:, sl] * lax.rsqrt(ss * (1.0 / HEAD_DIM) + EPS))
    y = parts[0] if len(parts) == 1 else jnp.concatenate(parts, axis=1)
    return y * gain


def _rope(y, cos, sin):
    width = y.shape[1]
    lane = lax.broadcasted_iota(jnp.int32, y.shape, 1)
    ahead = pltpu.roll(y, width - ROT_HALF, axis=1)
    behind = pltpu.roll(y, ROT_HALF, axis=1)
    partner = jnp.where((lane % ROT_AXIS) < ROT_HALF, ahead, behind)
    reps = width // LANES
    cos = cos if reps == 1 else jnp.concatenate([cos] * reps, axis=1)
    sin = sin if reps == 1 else jnp.concatenate([sin] * reps, axis=1)
    return y * cos + partner * sin


def _duplicate_heads(y):
    lo = lax.broadcasted_iota(jnp.int32, y.shape, 1) < HEAD_DIM
    swapped = pltpu.roll(y, HEAD_DIM, axis=1)
    return jnp.concatenate([jnp.where(lo, y, swapped), jnp.where(lo, swapped, y)], axis=1)


def _proj_kernel(x_ref, shift_ref, scale_ref, g_ref, w_ref, gsum_ref, gain_ref, cos_ref, sin_ref,
                 *out_refs, nseg, max_slab):
    x = x_ref[0]
    ms = jnp.mean(x * x, axis=-1, keepdims=True)
    gmod = g_ref[...] * (1.0 + scale_ref[0])
    h = (x * lax.rsqrt(ms + EPS) * gmod + shift_ref[0]).astype(BF16)
    gsum = gsum_ref[...]
    for seg in sorted(range(nseg), key=SEG_ISSUE_RANK.__getitem__):
        width = SEG_WIDTHS[seg]
        p = _dot(h, w_ref[:, SEG_OFFSETS[seg]:SEG_OFFSETS[seg + 1]])
        if seg in SEG_GAIN_ROW:
            row = SEG_GAIN_ROW[seg]
            p = _head_norm(p, gsum, gain_ref[row:row + 1, :width])
        if seg in SEG_ROPE:
            p = _rope(p, cos_ref[...], sin_ref[...])
        if seg == SEG_KB:
            p = _duplicate_heads(p)
        if seg in (SEG_ZA, SEG_ZB):
            p = p * jax.nn.sigmoid(p)
        if seg in (SEG_GA, SEG_GB):
            p = jax.nn.sigmoid(p)
        if seg in SEG_T_SLAB:
            pt = p.T.astype(BF16)
            slab = min(SEG_T_SLAB[seg], max_slab)
            for j in range(pt.shape[1] // slab):
                out_refs[seg][0, j] = pt[:, j * slab:(j + 1) * slab]
        else:
            out_refs[seg][0] = p.astype(BF16)


def _project(x, shift, scale, norm_g, w_all, layer, gsum, gains, cos, sin, nseg, rows, max_slab):
    groups, total, _ = x.shape
    ncols = SEG_OFFSETS[nseg]
    const = lambda g, i: (0, 0)

    def out_layout(s):
        width = SEG_OUT_WIDTHS[s]
        if s not in SEG_T_SLAB:
            return (groups, total, width), pl.BlockSpec((1, rows, width), lambda g, i: (g, i, 0))
        slab = min(SEG_T_SLAB[s], max_slab)
        return ((groups, total // slab, width, slab),
                pl.BlockSpec((1, rows // slab, width, slab), lambda g, i: (g, i, 0, 0)))

    layouts = [out_layout(s) for s in range(nseg)]
    return pl.pallas_call(
        functools.partial(_proj_kernel, nseg=nseg, max_slab=max_slab),
        grid=(groups, total // rows),
        in_specs=[
            pl.BlockSpec((1, rows, D_MODEL), lambda g, i: (g, i, 0)),
            pl.BlockSpec((1, 1, D_MODEL), lambda g, i: (g, 0, 0)),
            pl.BlockSpec((1, 1, D_MODEL), lambda g, i: (g, 0, 0)),
            pl.BlockSpec((1, D_MODEL), const),
            pl.BlockSpec((None, D_MODEL, ncols), lambda g, i: (layer, 0, 0), pipeline_mode=pl.Buffered(1)),
            pl.BlockSpec((MXU_COLS, MXU_COLS), const),
            pl.BlockSpec(gains.shape, const),
            pl.BlockSpec((rows, LANES), lambda g, i: (i, 0)),
            pl.BlockSpec((rows, LANES), lambda g, i: (i, 0)),
        ],
        out_specs=[spec for _, spec in layouts],
        out_shape=[jax.ShapeDtypeStruct(shape, BF16) for shape, _ in layouts],
        compiler_params=_params(("arbitrary", "arbitrary")),
        name="proj",
    )(x, shift, scale, norm_g, w_all, gsum, gains, cos, sin)


def _stack_heads(pair):
    lo = lax.broadcasted_iota(jnp.int32, pair.shape, 1) < HEAD_DIM
    zero = jnp.zeros_like(pair)
    return jnp.concatenate([jnp.where(lo, pair, zero), jnp.where(lo, zero, pair)], axis=0)


def _stack_quad(q):
    head = lax.broadcasted_iota(jnp.int32, q.shape, 1) // HEAD_DIM
    zero = jnp.zeros_like(q)
    return jnp.concatenate([jnp.where(head == h, q, zero) for h in range(QUAD)], axis=0)


def _unstack_quad_t(o, rows):
    ot = o.T
    head = lax.broadcasted_iota(jnp.int32, (rows, QUAD * HEAD_DIM), 1) // HEAD_DIM
    out = ot[:rows]
    for h in range(1, QUAD):
        out = jnp.where(head == h, ot[h * rows:(h + 1) * rows], out)
    return out


def _pv_t(vt, s, m):
    lhs = jnp.concatenate([vt, jnp.ones((ONES_ROWS, vt.shape[1]), vt.dtype)], axis=0)
    return _dot(lhs, jnp.exp2((s - m).astype(BF16)))


def _normalise_t(acc, dims):
    return acc[:dims] * (1.0 / acc[dims:dims + 1])


def _softmax_pv_t(logits, values_t):
    m = functools.reduce(jnp.maximum, [jnp.max(s, axis=0, keepdims=True) for s in logits])
    acc = functools.reduce(jnp.add, [_pv_t(vt, s, m) for vt, s in zip(values_t, logits)])
    return _normalise_t(acc, values_t[0].shape[0])


def _na_window(r, rows, slabs):
    start = jnp.clip(r - WIN_R // 2, 0, rows - WIN_R)
    slab0 = jnp.minimum(start // 2, rows // 2 - slabs)
    first = 2 * slab0
    blocks = []
    for w in range(2 * slabs):
        key_row = first + w
        valid = (key_row >= start) & (key_row < start + WIN_R)
        blocks.append(jnp.where(valid, key_row - r + (WIN_R - 1), 2 * WIN_R - 1))
    return slab0, blocks


def _na_kernel(q_ref, k_ref, vt_ref, kc_ref, vct_ref, t_ref, o_ref, qt_ref, *, rows, unroll):
    kc = kc_ref[0]
    vct = jnp.concatenate([vct_ref[0, j] for j in range(vct_ref.shape[1])], axis=1)
    assert unroll % 2 == 0 and rows % 2 == 0 and WIN_R % 2 == 0

    def logits(r, slot):
        slabs = WIN_R // 2 + slot % 2
        slab0, blocks = _na_window(r, rows, slabs)
        qt_ref[slot] = _stack_quad(q_ref[0, pl.ds(pl.multiple_of(r * GRID_W, GRID_W), GRID_W), :]).T
        qs_t = qt_ref[slot]
        kw = k_ref[0, pl.ds(pl.multiple_of(slab0 * LANES, LANES), slabs * LANES), :]
        bias = jnp.concatenate([t_ref[b] for b in blocks], axis=0)
        return _dot(kw, qs_t) + bias, _dot(kc, qs_t), slab0

    def finish(r, s_win, s_ctx, slab0):
        slabs = s_win.shape[0] // LANES
        vt = vt_ref[0, pl.ds(slab0, slabs)]
        vtw = jnp.concatenate([vt[j] for j in range(slabs)], axis=1)
        sub = NA_SUB_ROWS
        parts = [s[i:i + sub] for s in (s_ctx, s_win) for i in range(0, s.shape[0], sub)]
        _, acc = _attend_t(parts, jnp.concatenate([vct, vtw], axis=1), None)
        o = _normalise_t(acc, vct.shape[0])
        o_ref[0, pl.ds(pl.multiple_of(r * GRID_W, GRID_W), GRID_W), :] = _unstack_quad_t(o, GRID_W).astype(BF16)

    def body(i, carry):
        pending = []
        for e in range(unroll):
            pending.append((i * unroll + e,) + logits(i * unroll + e, e))
            if len(pending) > NA_AHEAD:
                finish(*pending.pop(0))
        for args in pending:
            finish(*args)
        return carry

    lax.fori_loop(0, rows // unroll, body, 0)


def _neighbourhood_attention(qa, ka, vat, cka, cvat, tables, layer):
    batch, seq, _ = qa.shape
    ctx_len = cka.shape[1]
    rows = seq // GRID_W
    width = QUAD * HEAD_DIM
    blk = lambda n: pl.BlockSpec((1, n, width), lambda b, j: (b, 0, j))
    vblk = lambda a: pl.BlockSpec((1, a.shape[1], width, LANES), lambda b, j: (b, 0, j, 0))
    return pl.pallas_call(
        functools.partial(_na_kernel, rows=rows, unroll=NA_UNROLL),
        grid=(batch, NA_HEADS // QUAD),
        in_specs=[blk(seq), blk(seq), vblk(vat), blk(ctx_len), vblk(cvat),
                  pl.BlockSpec((None, None) + tables.shape[2:], lambda b, j: (layer, j, 0, 0, 0))],
        out_specs=blk(seq),
        out_shape=jax.ShapeDtypeStruct((batch, seq, NA_WIDTH), BF16),
        scratch_shapes=[pltpu.VMEM((NA_UNROLL, width, QUAD * GRID_W), BF16)],
        compiler_params=_params(("arbitrary", "arbitrary")),
        name="na",
    )(qa, ka, vat, cka, cvat, tables)


def _bias_kernel(rpb_ref, onehot_ref, mask_ref, o_ref):
    o_ref[0] = (jnp.dot(rpb_ref[0], onehot_ref[...], preferred_element_type=F32,
                        precision=lax.Precision.HIGHEST) + mask_ref[...]) * LOG2E


def _bias_tables(rpb):
    depth = rpb.shape[0]
    n_row_off, n_col_off = 2 * WIN_R - 1, 2 * WIN_C - 1
    cols = np.arange(GRID_W)
    cstart = np.clip(cols - WIN_C // 2, 0, GRID_W - WIN_C)
    kcol = np.arange(GRID_W)
    valid = (kcol[None, :] >= cstart[:, None]) & (kcol[None, :] < cstart[:, None] + WIN_C)
    col_off = kcol[None, :] - cols[:, None] + (WIN_C - 1)
    pad_r, pad_c = -n_row_off % 8, -n_col_off % 8
    assert pad_r >= 1
    onehot = (np.arange(n_col_off + pad_c)[:, None, None] == col_off.T[None]) & valid.T[None]
    onehot = jnp.asarray(onehot.reshape(n_col_off + pad_c, GRID_W * GRID_W), dtype=F32)
    mask = jnp.asarray(np.where(valid.T, 0.0, MASKED).reshape(1, GRID_W * GRID_W), dtype=F32)
    heads = depth * NA_HEADS
    rpb_p = jnp.pad(rpb.reshape(heads, n_row_off, n_col_off), ((0, 0), (0, pad_r), (0, pad_c)))
    toep = pl.pallas_call(
        _bias_kernel,
        grid=(heads,),
        in_specs=[pl.BlockSpec((1,) + rpb_p.shape[1:], lambda h: (h, 0, 0)),
                  pl.BlockSpec(onehot.shape, lambda h: (0, 0)),
                  pl.BlockSpec(mask.shape, lambda h: (0, 0))],
        out_specs=pl.BlockSpec((1, n_row_off + pad_r, GRID_W * GRID_W), lambda h: (h, 0, 0)),
        out_shape=jax.ShapeDtypeStruct((heads, n_row_off + pad_r, GRID_W * GRID_W), F32),
        compiler_params=_params(("arbitrary",)),
        name="bias",
    )(rpb_p, onehot, mask)
    groups = NA_HEADS // QUAD
    toep = toep.reshape(depth, groups, QUAD, n_row_off + pad_r, GRID_W, GRID_W)
    table = jnp.transpose(toep, (0, 1, 3, 4, 2, 5)).reshape(depth, groups, -1, GRID_W, QUAD * GRID_W)
    return jnp.where((jnp.arange(table.shape[2]) < n_row_off)[None, None, :, None, None], table, MASKED * LOG2E)


def _stack_group(q):
    return jnp.concatenate([_stack_heads(q[:, :LANES]), _stack_heads(q[:, LANES:])], axis=0)


def _unstack_group_t(o, rows):
    pairs = [jnp.concatenate([o[:, (2 * j) * rows:(2 * j + 1) * rows],
                              o[:, (2 * j + 1) * rows:(2 * j + 2) * rows]], axis=0).T for j in range(2)]
    return jnp.concatenate(pairs, axis=1)


def _scores_t(qs_t, k):
    return _dot(k, qs_t)


def _attend_t(parts, vt, state):
    m_old = None if state is None else state[0]
    m, ps, ms = m_old, [], []
    for s in parts:
        m_part = jnp.max(s, axis=0, keepdims=True)
        m = m_part if m is None else jnp.maximum(m, m_part)
        ps.append(jnp.exp2((s - m).astype(BF16)))
        ms.append(m)
    ps = [p * jnp.exp2(m_i - m).astype(BF16) for p, m_i in zip(ps[:-1], ms[:-1])] + ps[-1:]
    lhs = jnp.concatenate([vt, jnp.ones((ONES_ROWS, vt.shape[1]), vt.dtype)], axis=0)
    pv = _dot(lhs, ps[0] if len(ps) == 1 else jnp.concatenate(ps, axis=0))
    if state is None:
        return m, pv
    return m, jnp.exp2(m_old - m) * state[1] + pv


def _gqa_kernel(q_ref, k_ref, vt_ref, kc_ref, vct_ref, o_ref, qt_ref, *, chunks, tq, tk, blocks):
    for j in range(blocks):
        qt_ref[j] = _stack_group(q_ref[0, j * tq:(j + 1) * tq, :]).T
    pending = []
    states = [None] * blocks

    def drain(limit):
        while len(pending) > limit:
            j, parts, vt = pending.pop(0)
            states[j] = _attend_t(parts, vt, states[j])

    def logits(qs, keys, c, rows):
        sub = min(GQA_SUB_ROWS, rows)
        return [_scores_t(qs, keys[0, c * rows + i * sub:c * rows + (i + 1) * sub, :]) for i in range(rows // sub)]

    for j in range(blocks):
        qs = qt_ref[j]
        for c in range(vct_ref.shape[1]):
            pending.append((j, logits(qs, kc_ref, c, vct_ref.shape[3]), vct_ref[0, c]))
            drain(GQA_AHEAD)
        for c in range(chunks):
            pending.append((j, logits(qs, k_ref, c, tk), vt_ref[0, c]))
            drain(GQA_AHEAD)
    drain(0)
    for j in range(blocks):
        _, acc = states[j]
        o_ref[0, j * tq:(j + 1) * tq, :] = _unstack_group_t(_normalise_t(acc, HEAD_DIM), tq).astype(BF16)


def _gqa_attention(qb, kbd, vbt, ckbd, cvbt):
    batch, seq, _ = qb.shape
    ctx_len = ckbd.shape[1]
    tq, tk, blocks = GQA_Q_ROWS, vbt.shape[3], GQA_Q_BLOCKS
    kspec = lambda n: pl.BlockSpec((1, n, LANES), lambda b, g, i: (b, 0, g))
    vspec = lambda a: pl.BlockSpec((1, a.shape[1], HEAD_DIM, a.shape[3]), lambda b, g, i: (b, 0, g, 0))
    qspec = pl.BlockSpec((1, blocks * tq, 2 * LANES), lambda b, g, i: (b, i, g))
    return pl.pallas_call(
        functools.partial(_gqa_kernel, chunks=seq // tk, tq=tq, tk=tk, blocks=blocks),
        grid=(batch, GQA_KV_HEADS, seq // (blocks * tq)),
        in_specs=[qspec, kspec(seq), vspec(vbt), kspec(ctx_len), vspec(cvbt)],
        out_specs=qspec,
        out_shape=jax.ShapeDtypeStruct((batch, seq, GQA_WIDTH), BF16),
        scratch_shapes=[pltpu.VMEM((blocks, LANES, (GQA_Q_HEADS // GQA_KV_HEADS) * tq), BF16)],
        compiler_params=_params(("arbitrary", "arbitrary", "arbitrary")),
        name="gqa",
    )(qb, kbd, vbt, ckbd, cvbt)


def _ctx_kernel(qa_ref, ka_ref, vat_ref, qb_ref, kb_ref, vbt_ref, oa_ref, ob_ref):
    n = qa_ref.shape[1]
    width = QUAD * HEAD_DIM
    for g in range(NA_HEADS // QUAD):
        sl = slice(g * width, (g + 1) * width)
        vt = jnp.concatenate([vat_ref[0, j, sl, :] for j in range(vat_ref.shape[1])], axis=1)
        o = _softmax_pv_t([_dot_t(ka_ref[0, :, sl], _stack_quad(qa_ref[0, :, sl]))], [vt])
        oa_ref[0, :, sl] = _unstack_quad_t(o, n).astype(BF16)
    for g in range(GQA_KV_HEADS):
        sl = slice(g * 2 * LANES, (g + 1) * 2 * LANES)
        s = _scores_t(_stack_group(qb_ref[0, :, sl]).T, kb_ref[0, :, g * LANES:(g + 1) * LANES])
        vt = jnp.concatenate([vbt_ref[0, j, g * HEAD_DIM:(g + 1) * HEAD_DIM, :] for j in range(vbt_ref.shape[1])],
                             axis=1)
        o = _softmax_pv_t([s], [vt])
        ob_ref[0, :, sl] = _unstack_group_t(o, n).astype(BF16)


def _context_attention(cqa, cka, cvat, cqb, ckbd, cvbt):
    batch, n, _ = cqa.shape
    spec = lambda a: pl.BlockSpec((1,) + a.shape[1:], lambda b: (b,) + (0,) * (a.ndim - 1))
    args = (cqa, cka, cvat, cqb, ckbd, cvbt)
    return pl.pallas_call(
        _ctx_kernel,
        grid=(batch,),
        in_specs=[spec(a) for a in args],
        out_specs=[spec(cqa), spec(cqb)],
        out_shape=[jax.ShapeDtypeStruct(cqa.shape, BF16), jax.ShapeDtypeStruct(cqb.shape, BF16)],
        compiler_params=_params(("arbitrary",)),
        name="ctx",
    )(*args)


def _merge_kernel(a_ref, b_ref, za_ref, zb_ref, ga_ref, gb_ref, x_ref, gate_ref,
                  woa_ref, wob_ref, wout_ref, o_ref):
    o_a = _dot(a_ref[0] * za_ref[0], woa_ref[...])
    o_b = _dot(b_ref[0] * zb_ref[0], wob_ref[...])
    merged = ga_ref[0].astype(F32) * o_a + gb_ref[0].astype(F32) * o_b
    o_ref[0] = x_ref[0] + gate_ref[0] * _dot(merged.astype(BF16), wout_ref[...])


def _merge(a_att, b_att, za, zb, ga, gb, x, gate, w_o_a, w_o_b, w_out, layer, rows):
    groups, total, _ = x.shape
    act = lambda w: pl.BlockSpec((1, rows, w), lambda g, i: (g, i, 0))
    const = lambda a: pl.BlockSpec((None,) + a.shape[1:], lambda g, i: (layer, 0, 0))
    return pl.pallas_call(
        _merge_kernel,
        grid=(groups, total // rows),
        in_specs=[act(NA_WIDTH), act(GQA_WIDTH), act(NA_WIDTH), act(GQA_WIDTH), act(D_MODEL), act(D_MODEL),
                  act(D_MODEL), pl.BlockSpec((1, 1, D_MODEL), lambda g, i: (g, 0, 0)),
                  const(w_o_a), const(w_o_b), const(w_out)],
        out_specs=act(D_MODEL),
        out_shape=jax.ShapeDtypeStruct(x.shape, F32),
        compiler_params=_params(("arbitrary", "arbitrary")),
        name="merge",
    )(a_att, b_att, za, zb, ga, gb, x, gate, w_o_a, w_o_b, w_out)


def _rope_tables(seq):
    t = jnp.arange(seq, dtype=jnp.int32)
    inv = 1.0 / (ROPE_THETA ** (jnp.arange(ROT_HALF, dtype=F32) / ROT_HALF))
    ang_r = (t // GRID_W).astype(F32)[:, None] * inv[None, :]
    ang_c = (t % GRID_W).astype(F32)[:, None] * inv[None, :]
    cos = jnp.concatenate([jnp.cos(ang_r)] * 2 + [jnp.cos(ang_c)] * 2, axis=1)
    sin = jnp.concatenate([-jnp.sin(ang_r), jnp.sin(ang_r), -jnp.sin(ang_c), jnp.sin(ang_c)], axis=1)
    reps = LANES // HEAD_DIM
    return jnp.tile(cos, (1, reps)), jnp.tile(sin, (1, reps))


def _group_sum_matrix():
    idx = np.arange(MXU_COLS) // HEAD_DIM
    return jnp.asarray(idx[:, None] == idx[None, :], dtype=BF16)


def kernel(x, c, ctx, c_ctx, w_ada, b_ada, norm_g, w_in, q_norm_a, k_norm_a, q_norm_b, k_norm_b,
           rpb, w_o_a, w_o_b, w_out):
    batch, seq, _ = x.shape
    ctx_len = ctx.shape[1]
    depth = w_ada.shape[0]
    rows = seq // GRID_W
    assert seq % GRID_W == 0 and rows >= WIN_R and seq % PROJ_ROWS == 0 and seq % GQA_Q_ROWS == 0

    pad = -(batch + 1) % 8
    c_rows = jnp.concatenate([c, c_ctx[None, :], jnp.zeros((pad, D_MODEL), F32)], axis=0)
    mod = _modulation(c_rows, w_ada, b_ada)

    cos_x, sin_x = _rope_tables(seq)
    ctx_rows = batch * ctx_len
    ctx = ctx.reshape(1, ctx_rows, D_MODEL)
    cos_c = jnp.ones((ctx_rows, LANES), F32)
    sin_c = jnp.zeros((ctx_rows, LANES), F32)
    flat = lambda a: a.reshape((1, ctx_rows) + a.shape[2:])
    per_batch_ctx = lambda a: a.reshape((batch, a.shape[1] // batch) + a.shape[2:])
    gsum = _group_sum_matrix()
    tables = _bias_tables(rpb)
    w_in, w_o_a, w_o_b, w_out = (w.astype(BF16) for w in (w_in, w_o_a, w_o_b, w_out))

    for l in range(depth):
        update_ctx = l < depth - 1
        shift, scale, gate = (mod[l, :, i * D_MODEL:(i + 1) * D_MODEL] for i in range(3))
        per_batch = lambda m: m[:batch, None, :]
        for_ctx = lambda m: m[batch][None, None, :]
        tile = lambda g, n: jnp.tile(g, n)
        gains = jnp.stack([tile(k_norm_a[l], NA_HEADS), tile(k_norm_b[l], NA_HEADS),
                           tile(q_norm_a[l], NA_HEADS) * (QK_SCALE * LOG2E),
                           tile(q_norm_b[l], GQA_Q_HEADS) * (QK_SCALE * LOG2E)])
        g_row = norm_g[l][None, :]

        ka, vat, kbd, vbt, qa, qb, za, zb, ga, gb = _project(
            x, per_batch(shift), per_batch(scale), g_row, w_in, l, gsum, gains, cos_x, sin_x,
            len(SEG_WIDTHS), PROJ_ROWS, PROJ_ROWS)
        nseg_c = len(SEG_WIDTHS) if update_ctx else KV_SEGS
        pc = _project(ctx, for_ctx(shift), for_ctx(scale), g_row, w_in, l, gsum, gains, cos_c, sin_c,
                      nseg_c, min(PROJ_ROWS, ctx_rows), ctx_len)
        pc = [per_batch_ctx(a) for a in pc]
        cka, cvat, ckbd, cvbt = pc[:KV_SEGS]

        a_att = _neighbourhood_attention(qa, ka, vat, cka, cvat, tables, l)
        b_att = _gqa_attention(qb, kbd, vbt, ckbd, cvbt)
        x_new = _merge(a_att, b_att, za, zb, ga, gb, x, per_batch(gate), w_o_a, w_o_b, w_out, l, MERGE_ROWS)

        if update_ctx:
            _, _, _, _, cqa, cqb, cza, czb, cga, cgb = pc
            c_a, c_b = _context_attention(cqa, cka, cvat, cqb, ckbd, cvbt)
            ctx = _merge(*(flat(a) for a in (c_a, c_b, cza, czb, cga, cgb)), ctx, for_ctx(gate),
                         w_o_a, w_o_b, w_out, l, min(MERGE_ROWS, ctx_rows))
        x = x_new
    return x
```

```python
import functools

import numpy as np
import jax
import jax.numpy as jnp
from jax import lax
from jax.experimental import pallas as pl
from jax.experimental.pallas import tpu as pltpu

D_MODEL = 1024
GRID_W = 64
HEAD_DIM = 64
NA_HEADS = 8
NA_WIDTH = NA_HEADS * HEAD_DIM
WIN_R = 8
WIN_C = 16
GQA_Q_HEADS = 8
GQA_KV_HEADS = 2
GQA_WIDTH = GQA_Q_HEADS * HEAD_DIM
GQA_KV_WIDTH = GQA_KV_HEADS * HEAD_DIM
ROPE_THETA = 10000.0
ROT_AXIS = HEAD_DIM // 2
ROT_HALF = ROT_AXIS // 2
EPS = 1e-6
QK_SCALE = HEAD_DIM ** -0.5
LOG2E = 1.4426950408889634

LANES = 128
MXU_COLS = 256
MASKED = -1e30
VMEM_LIMIT = 56 * 1024 * 1024

PROJ_ROWS = 1024
MERGE_ROWS = 512
GQA_Q_ROWS = 256
GQA_Q_BLOCKS = 4
GQA_K_ROWS = 256
GQA_SUB_ROWS = 128
GQA_AHEAD = 1
QUAD = 4
ONES_ROWS = 16
NA_SUB_ROWS = 128
NA_AHEAD = 6
NA_UNROLL = 8

F32 = jnp.float32
BF16 = jnp.bfloat16

SEG_KA, SEG_VA, SEG_KB, SEG_VB, SEG_QA, SEG_QB = range(6)
SEG_WIDTHS = (NA_WIDTH, NA_WIDTH, GQA_KV_WIDTH, GQA_KV_WIDTH, NA_WIDTH, GQA_WIDTH)
SEG_OFFSETS = tuple(int(v) for v in np.cumsum((0,) + SEG_WIDTHS))
GATE_WIDTHS = (NA_WIDTH, GQA_WIDTH, D_MODEL, D_MODEL)
GATE_OFFSETS = tuple(int(v) for v in np.cumsum((0,) + GATE_WIDTHS))
SEG_OUT_WIDTHS = tuple(2 * w if s == SEG_KB else w for s, w in enumerate(SEG_WIDTHS))
KV_SEGS = 4
SEG_ISSUE_RANK = {SEG_QB: 0, SEG_KB: 1, SEG_QA: 2, SEG_KA: 3, SEG_VB: 4, SEG_VA: 5}
SEG_GAIN_ROW = {SEG_KA: 0, SEG_KB: 1, SEG_QA: 2, SEG_QB: 3}
SEG_ROPE = (SEG_KB, SEG_QB)
SEG_T_SLAB = {SEG_VA: LANES, SEG_VB: GQA_K_ROWS}


def _dot(a, b):
    return jnp.dot(a, b, preferred_element_type=F32)


def _dot_t(a, b):
    return lax.dot_general(a, b, (((1,), (1,)), ((), ())), preferred_element_type=F32)


def _params(semantics):
    return pltpu.CompilerParams(dimension_semantics=semantics, vmem_limit_bytes=VMEM_LIMIT)


def _mod_kernel(c_ref, w_ref, b_ref, o_ref):
    c = c_ref[...]
    a = c * jax.nn.sigmoid(c)
    o_ref[0] = jnp.dot(a, w_ref[0], preferred_element_type=F32,
                       precision=lax.Precision.HIGHEST) + b_ref[0]


def _modulation(c_rows, w_ada, b_ada):
    depth = w_ada.shape[0]
    rows = c_rows.shape[0]
    ncol = w_ada.shape[2] // D_MODEL
    return pl.pallas_call(
        _mod_kernel,
        grid=(depth, ncol),
        in_specs=[
            pl.BlockSpec((rows, D_MODEL), lambda l, j: (0, 0)),
            pl.BlockSpec((1, D_MODEL, D_MODEL), lambda l, j: (l, 0, j)),
            pl.BlockSpec((1, 1, D_MODEL), lambda l, j: (l, 0, j)),
        ],
        out_specs=pl.BlockSpec((1, rows, D_MODEL), lambda l, j: (l, 0, j)),
        out_shape=jax.ShapeDtypeStruct((depth, rows, ncol * D_MODEL), F32),
        compiler_params=_params(("arbitrary", "arbitrary")),
        name="mod",
    )(c_rows, w_ada, b_ada.reshape(depth, 1, -1))


def _head_norm(p, gsum, gain):
    sq = (p * p).astype(BF16)
    cols = min(MXU_COLS, p.shape[1])
    parts = []
    for j in range(p.shape[1] // cols):
        sl = slice(j * cols, (j + 1) * cols)
        ss = _dot(sq[:, sl], gsum[:cols, :cols])
        parts.append(p[:, sl] * lax.rsqrt(ss * (1.0 / HEAD_DIM) + EPS))
    y = parts[0] if len(parts) == 1 else jnp.concatenate(parts, axis=1)
    return y * gain


def _rope(y, cos, sin):
    width = y.shape[1]
    lane = lax.broadcasted_iota(jnp.int32, y.shape, 1)
    ahead = pltpu.roll(y, width - ROT_HALF, axis=1)
    behind = pltpu.roll(y, ROT_HALF, axis=1)
    partner = jnp.where((lane % ROT_AXIS) < ROT_HALF, ahead, behind)
    reps = width // LANES
    cos = cos if reps == 1 else jnp.concatenate([cos] * reps, axis=1)
    sin = sin if reps == 1 else jnp.concatenate([sin] * reps, axis=1)
    return y * cos + partner * sin


def _duplicate_heads(y):
    lo = lax.broadcasted_iota(jnp.int32, y.shape, 1) < HEAD_DIM
    swapped = pltpu.roll(y, HEAD_DIM, axis=1)
    return jnp.concatenate([jnp.where(lo, y, swapped), jnp.where(lo, swapped, y)], axis=1)


def _modulated_norm(x, g_ref, scale_ref, shift_ref):
    ms = jnp.mean(x * x, axis=-1, keepdims=True)
    gmod = g_ref[...] * (1.0 + scale_ref[0])
    return (x * lax.rsqrt(ms + EPS) * gmod + shift_ref[0]).astype(BF16)


def _proj_kernel(x_ref, shift_ref, scale_ref, g_ref, w_ref, gsum_ref, gain_ref, cos_ref, sin_ref,
                 *out_refs, nseg, max_slab):
    h = _modulated_norm(x_ref[0], g_ref, scale_ref, shift_ref)
    gsum = gsum_ref[...]
    for seg in sorted(range(nseg), key=SEG_ISSUE_RANK.__getitem__):
        width = SEG_WIDTHS[seg]
        p = _dot(h, w_ref[:, SEG_OFFSETS[seg]:SEG_OFFSETS[seg + 1]])
        if seg in SEG_GAIN_ROW:
            row = SEG_GAIN_ROW[seg]
            p = _head_norm(p, gsum, gain_ref[row:row + 1, :width])
        if seg in SEG_ROPE:
            p = _rope(p, cos_ref[...], sin_ref[...])
        if seg == SEG_KB:
            p = _duplicate_heads(p)
        if seg in SEG_T_SLAB:
            pt = p.T.astype(BF16)
            slab = min(SEG_T_SLAB[seg], max_slab)
            for j in range(pt.shape[1] // slab):
                out_refs[seg][0, j] = pt[:, j * slab:(j + 1) * slab]
        else:
            out_refs[seg][0] = p.astype(BF16)


def _project(x, shift, scale, norm_g, w_all, layer, gsum, gains, cos, sin, nseg, rows, max_slab):
    groups, total, _ = x.shape
    ncols = SEG_OFFSETS[nseg]
    const = lambda g, i: (0, 0)

    def out_layout(s):
        width = SEG_OUT_WIDTHS[s]
        if s not in SEG_T_SLAB:
            return (groups, total, width), pl.BlockSpec((1, rows, width), lambda g, i: (g, i, 0))
        slab = min(SEG_T_SLAB[s], max_slab)
        return ((groups, total // slab, width, slab),
                pl.BlockSpec((1, rows // slab, width, slab), lambda g, i: (g, i, 0, 0)))

    layouts = [out_layout(s) for s in range(nseg)]
    return pl.pallas_call(
        functools.partial(_proj_kernel, nseg=nseg, max_slab=max_slab),
        grid=(groups, total // rows),
        in_specs=[
            pl.BlockSpec((1, rows, D_MODEL), lambda g, i: (g, i, 0)),
            pl.BlockSpec((1, 1, D_MODEL), lambda g, i: (g, 0, 0)),
            pl.BlockSpec((1, 1, D_MODEL), lambda g, i: (g, 0, 0)),
            pl.BlockSpec((1, D_MODEL), const),
            pl.BlockSpec((None, D_MODEL, ncols), lambda g, i: (layer, 0, 0), pipeline_mode=pl.Buffered(1)),
            pl.BlockSpec((MXU_COLS, MXU_COLS), const),
            pl.BlockSpec(gains.shape, const),
            pl.BlockSpec((rows, LANES), lambda g, i: (i, 0)),
            pl.BlockSpec((rows, LANES), lambda g, i: (i, 0)),
        ],
        out_specs=[spec for _, spec in layouts],
        out_shape=[jax.ShapeDtypeStruct(shape, BF16) for shape, _ in layouts],
        compiler_params=_params(("arbitrary", "arbitrary")),
        name="proj",
    )(x, shift, scale, norm_g, w_all, gsum, gains, cos, sin)


def _stack_heads(pair):
    lo = lax.broadcasted_iota(jnp.int32, pair.shape, 1) < HEAD_DIM
    zero = jnp.zeros_like(pair)
    return jnp.concatenate([jnp.where(lo, pair, zero), jnp.where(lo, zero, pair)], axis=0)


def _stack_quad(q):
    head = lax.broadcasted_iota(jnp.int32, q.shape, 1) // HEAD_DIM
    zero = jnp.zeros_like(q)
    return jnp.concatenate([jnp.where(head == h, q, zero) for h in range(QUAD)], axis=0)


def _unstack_quad_t(o, rows):
    ot = o.T
    head = lax.broadcasted_iota(jnp.int32, (rows, QUAD * HEAD_DIM), 1) // HEAD_DIM
    out = ot[:rows]
    for h in range(1, QUAD):
        out = jnp.where(head == h, ot[h * rows:(h + 1) * rows], out)
    return out


def _pv_t(vt, s, m):
    lhs = jnp.concatenate([vt, jnp.ones((ONES_ROWS, vt.shape[1]), vt.dtype)], axis=0)
    return _dot(lhs, jnp.exp2((s - m).astype(BF16)))


def _normalise_t(acc, dims):
    return acc[:dims] * (1.0 / acc[dims:dims + 1])


def _softmax_pv_t(logits, values_t):
    m = functools.reduce(jnp.maximum, [jnp.max(s, axis=0, keepdims=True) for s in logits])
    acc = functools.reduce(jnp.add, [_pv_t(vt, s, m) for vt, s in zip(values_t, logits)])
    return _normalise_t(acc, values_t[0].shape[0])


def _na_window(r, rows, slabs):
    start = jnp.clip(r - WIN_R // 2, 0, rows - WIN_R)
    slab0 = jnp.minimum(start // 2, rows // 2 - slabs)
    first = 2 * slab0
    blocks = []
    for w in range(2 * slabs):
        key_row = first + w
        valid = (key_row >= start) & (key_row < start + WIN_R)
        blocks.append(jnp.where(valid, key_row - r + (WIN_R - 1), 2 * WIN_R - 1))
    return slab0, blocks


def _na_kernel(q_ref, k_ref, vt_ref, kc_ref, vct_ref, t_ref, o_ref, qt_ref, *, rows, unroll):
    kc = kc_ref[0]
    vct = jnp.concatenate([vct_ref[0, j] for j in range(vct_ref.shape[1])], axis=1)
    assert unroll % 2 == 0 and rows % 2 == 0 and WIN_R % 2 == 0

    def logits(r, slot):
        slabs = WIN_R // 2 + slot % 2
        slab0, blocks = _na_window(r, rows, slabs)
        qt_ref[slot] = _stack_quad(q_ref[0, pl.ds(pl.multiple_of(r * GRID_W, GRID_W), GRID_W), :]).T
        qs_t = qt_ref[slot]
        kw = k_ref[0, pl.ds(pl.multiple_of(slab0 * LANES, LANES), slabs * LANES), :]
        bias = jnp.concatenate([t_ref[b] for b in blocks], axis=0)
        return _dot(kw, qs_t) + bias, _dot(kc, qs_t), slab0

    def finish(r, s_win, s_ctx, slab0):
        slabs = s_win.shape[0] // LANES
        vt = vt_ref[0, pl.ds(slab0, slabs)]
        vtw = jnp.concatenate([vt[j] for j in range(slabs)], axis=1)
        sub = NA_SUB_ROWS
        parts = [s[i:i + sub] for s in (s_ctx, s_win) for i in range(0, s.shape[0], sub)]
        _, acc = _attend_t(parts, jnp.concatenate([vct, vtw], axis=1), None)
        o = _normalise_t(acc, vct.shape[0])
        o_ref[0, pl.ds(pl.multiple_of(r * GRID_W, GRID_W), GRID_W), :] = _unstack_quad_t(o, GRID_W).astype(BF16)

    def body(i, carry):
        pending = []
        for e in range(unroll):
            pending.append((i * unroll + e,) + logits(i * unroll + e, e))
            if len(pending) > NA_AHEAD:
                finish(*pending.pop(0))
        for args in pending:
            finish(*args)
        return carry

    lax.fori_loop(0, rows // unroll, body, 0)


def _neighbourhood_attention(qa, ka, vat, cka, cvat, tables, layer):
    batch, seq, _ = qa.shape
    ctx_len = cka.shape[1]
    rows = seq // GRID_W
    width = QUAD * HEAD_DIM
    blk = lambda n: pl.BlockSpec((1, n, width), lambda b, j: (b, 0, j))
    vblk = lambda a: pl.BlockSpec((1, a.shape[1], width, LANES), lambda b, j: (b, 0, j, 0))
    return pl.pallas_call(
        functools.partial(_na_kernel, rows=rows, unroll=NA_UNROLL),
        grid=(batch, NA_HEADS // QUAD),
        in_specs=[blk(seq), blk(seq), vblk(vat), blk(ctx_len), vblk(cvat),
                  pl.BlockSpec((None, None) + tables.shape[2:], lambda b, j: (layer, j, 0, 0, 0))],
        out_specs=blk(seq),
        out_shape=jax.ShapeDtypeStruct((batch, seq, NA_WIDTH), BF16),
        scratch_shapes=[pltpu.VMEM((NA_UNROLL, width, QUAD * GRID_W), BF16)],
        compiler_params=_params(("arbitrary", "arbitrary")),
        name="na",
    )(qa, ka, vat, cka, cvat, tables)


def _bias_kernel(rpb_ref, onehot_ref, mask_ref, o_ref):
    o_ref[0] = (jnp.dot(rpb_ref[0], onehot_ref[...], preferred_element_type=F32,
                        precision=lax.Precision.HIGHEST) + mask_ref[...]) * LOG2E


def _bias_tables(rpb):
    depth = rpb.shape[0]
    n_row_off, n_col_off = 2 * WIN_R - 1, 2 * WIN_C - 1
    cols = np.arange(GRID_W)
    cstart = np.clip(cols - WIN_C // 2, 0, GRID_W - WIN_C)
    kcol = np.arange(GRID_W)
    valid = (kcol[None, :] >= cstart[:, None]) & (kcol[None, :] < cstart[:, None] + WIN_C)
    col_off = kcol[None, :] - cols[:, None] + (WIN_C - 1)
    pad_r, pad_c = -n_row_off % 8, -n_col_off % 8
    assert pad_r >= 1
    onehot = (np.arange(n_col_off + pad_c)[:, None, None] == col_off.T[None]) & valid.T[None]
    onehot = jnp.asarray(onehot.reshape(n_col_off + pad_c, GRID_W * GRID_W), dtype=F32)
    mask = jnp.asarray(np.where(valid.T, 0.0, MASKED).reshape(1, GRID_W * GRID_W), dtype=F32)
    heads = depth * NA_HEADS
    rpb_p = jnp.pad(rpb.reshape(heads, n_row_off, n_col_off), ((0, 0), (0, pad_r), (0, pad_c)))
    toep = pl.pallas_call(
        _bias_kernel,
        grid=(heads,),
        in_specs=[pl.BlockSpec((1,) + rpb_p.shape[1:], lambda h: (h, 0, 0)),
                  pl.BlockSpec(onehot.shape, lambda h: (0, 0)),
                  pl.BlockSpec(mask.shape, lambda h: (0, 0))],
        out_specs=pl.BlockSpec((1, n_row_off + pad_r, GRID_W * GRID_W), lambda h: (h, 0, 0)),
        out_shape=jax.ShapeDtypeStruct((heads, n_row_off + pad_r, GRID_W * GRID_W), F32),
        compiler_params=_params(("arbitrary",)),
        name="bias",
    )(rpb_p, onehot, mask)
    groups = NA_HEADS // QUAD
    toep = toep.reshape(depth, groups, QUAD, n_row_off + pad_r, GRID_W, GRID_W)
    table = jnp.transpose(toep, (0, 1, 3, 4, 2, 5)).reshape(depth, groups, -1, GRID_W, QUAD * GRID_W)
    return jnp.where((jnp.arange(table.shape[2]) < n_row_off)[None, None, :, None, None], table, MASKED * LOG2E)


def _stack_group(q):
    return jnp.concatenate([_stack_heads(q[:, :LANES]), _stack_heads(q[:, LANES:])], axis=0)


def _unstack_group_t(o, rows):
    pairs = [jnp.concatenate([o[:, (2 * j) * rows:(2 * j + 1) * rows],
                              o[:, (2 * j + 1) * rows:(2 * j + 2) * rows]], axis=0).T for j in range(2)]
    return jnp.concatenate(pairs, axis=1)


def _scores_t(qs_t, k):
    return _dot(k, qs_t)


def _attend_t(parts, vt, state):
    m_old = None if state is None else state[0]
    m, ps, ms = m_old, [], []
    for s in parts:
        m_part = jnp.max(s, axis=0, keepdims=True)
        m = m_part if m is None else jnp.maximum(m, m_part)
        ps.append(jnp.exp2((s - m).astype(BF16)))
        ms.append(m)
    ps = [p * jnp.exp2(m_i - m).astype(BF16) for p, m_i in zip(ps[:-1], ms[:-1])] + ps[-1:]
    lhs = jnp.concatenate([vt, jnp.ones((ONES_ROWS, vt.shape[1]), vt.dtype)], axis=0)
    pv = _dot(lhs, ps[0] if len(ps) == 1 else jnp.concatenate(ps, axis=0))
    if state is None:
        return m, pv
    return m, jnp.exp2(m_old - m) * state[1] + pv


def _gqa_kernel(q_ref, k_ref, vt_ref, kc_ref, vct_ref, o_ref, qt_ref, *, chunks, tq, tk, blocks):
    for j in range(blocks):
        qt_ref[j] = _stack_group(q_ref[0, j * tq:(j + 1) * tq, :]).T
    pending = []
    states = [None] * blocks

    def drain(limit):
        while len(pending) > limit:
            j, parts, vt = pending.pop(0)
            states[j] = _attend_t(parts, vt, states[j])

    def logits(qs, keys, c, rows):
        sub = min(GQA_SUB_ROWS, rows)
        return [_scores_t(qs, keys[0, c * rows + i * sub:c * rows + (i + 1) * sub, :]) for i in range(rows // sub)]

    for j in range(blocks):
        qs = qt_ref[j]
        for c in range(vct_ref.shape[1]):
            pending.append((j, logits(qs, kc_ref, c, vct_ref.shape[3]), vct_ref[0, c]))
            drain(GQA_AHEAD)
        for c in range(chunks):
            pending.append((j, logits(qs, k_ref, c, tk), vt_ref[0, c]))
            drain(GQA_AHEAD)
    drain(0)
    for j in range(blocks):
        _, acc = states[j]
        o_ref[0, j * tq:(j + 1) * tq, :] = _unstack_group_t(_normalise_t(acc, HEAD_DIM), tq).astype(BF16)


def _gqa_attention(qb, kbd, vbt, ckbd, cvbt):
    batch, seq, _ = qb.shape
    ctx_len = ckbd.shape[1]
    tq, tk, blocks = GQA_Q_ROWS, vbt.shape[3], GQA_Q_BLOCKS
    kspec = lambda n: pl.BlockSpec((1, n, LANES), lambda b, g, i: (b, 0, g))
    vspec = lambda a: pl.BlockSpec((1, a.shape[1], HEAD_DIM, a.shape[3]), lambda b, g, i: (b, 0, g, 0))
    qspec = pl.BlockSpec((1, blocks * tq, 2 * LANES), lambda b, g, i: (b, i, g))
    return pl.pallas_call(
        functools.partial(_gqa_kernel, chunks=seq // tk, tq=tq, tk=tk, blocks=blocks),
        grid=(batch, GQA_KV_HEADS, seq // (blocks * tq)),
        in_specs=[qspec, kspec(seq), vspec(vbt), kspec(ctx_len), vspec(cvbt)],
        out_specs=qspec,
        out_shape=jax.ShapeDtypeStruct((batch, seq, GQA_WIDTH), BF16),
        scratch_shapes=[pltpu.VMEM((blocks, LANES, (GQA_Q_HEADS // GQA_KV_HEADS) * tq), BF16)],
        compiler_params=_params(("arbitrary", "arbitrary", "arbitrary")),
        name="gqa",
    )(qb, kbd, vbt, ckbd, cvbt)


def _ctx_kernel(qa_ref, ka_ref, vat_ref, qb_ref, kb_ref, vbt_ref, oa_ref, ob_ref):
    n = qa_ref.shape[1]
    width = QUAD * HEAD_DIM
    for g in range(NA_HEADS // QUAD):
        sl = slice(g * width, (g + 1) * width)
        vt = jnp.concatenate([vat_ref[0, j, sl, :] for j in range(vat_ref.shape[1])], axis=1)
        o = _softmax_pv_t([_dot_t(ka_ref[0, :, sl], _stack_quad(qa_ref[0, :, sl]))], [vt])
        oa_ref[0, :, sl] = _unstack_quad_t(o, n).astype(BF16)
    for g in range(GQA_KV_HEADS):
        sl = slice(g * 2 * LANES, (g + 1) * 2 * LANES)
        s = _scores_t(_stack_group(qb_ref[0, :, sl]).T, kb_ref[0, :, g * LANES:(g + 1) * LANES])
        vt = jnp.concatenate([vbt_ref[0, j, g * HEAD_DIM:(g + 1) * HEAD_DIM, :] for j in range(vbt_ref.shape[1])],
                             axis=1)
        o = _softmax_pv_t([s], [vt])
        ob_ref[0, :, sl] = _unstack_group_t(o, n).astype(BF16)


def _context_attention(cqa, cka, cvat, cqb, ckbd, cvbt):
    batch, n, _ = cqa.shape
    spec = lambda a: pl.BlockSpec((1,) + a.shape[1:], lambda b: (b,) + (0,) * (a.ndim - 1))
    args = (cqa, cka, cvat, cqb, ckbd, cvbt)
    return pl.pallas_call(
        _ctx_kernel,
        grid=(batch,),
        in_specs=[spec(a) for a in args],
        out_specs=[spec(cqa), spec(cqb)],
        out_shape=[jax.ShapeDtypeStruct(cqa.shape, BF16), jax.ShapeDtypeStruct(cqb.shape, BF16)],
        compiler_params=_params(("arbitrary",)),
        name="ctx",
    )(*args)


def _merge_kernel(a_ref, b_ref, x_ref, shift_ref, scale_ref, gate_ref, g_ref, wzg_ref,
                  woa_ref, wob_ref, wout_ref, o_ref):
    x = x_ref[0]
    h = _modulated_norm(x, g_ref, scale_ref, shift_ref)
    z_a, z_b, g_a, g_b = (_dot(h, wzg_ref[:, GATE_OFFSETS[i]:GATE_OFFSETS[i + 1]]) for i in range(4))
    o_a = _dot((a_ref[0].astype(F32) * (z_a * jax.nn.sigmoid(z_a))).astype(BF16), woa_ref[...])
    o_b = _dot((b_ref[0].astype(F32) * (z_b * jax.nn.sigmoid(z_b))).astype(BF16), wob_ref[...])
    merged = jax.nn.sigmoid(g_a) * o_a + jax.nn.sigmoid(g_b) * o_b
    o_ref[0] = x + gate_ref[0] * _dot(merged.astype(BF16), wout_ref[...])


def _merge(a_att, b_att, x, shift, scale, gate, norm_g, w_gates, w_o_a, w_o_b, w_out, layer, rows):
    groups, total, _ = x.shape
    act = lambda w: pl.BlockSpec((1, rows, w), lambda g, i: (g, i, 0))
    mod = pl.BlockSpec((1, 1, D_MODEL), lambda g, i: (g, 0, 0))
    const = lambda a: pl.BlockSpec((None,) + a.shape[1:], lambda g, i: (layer, 0, 0))
    return pl.pallas_call(
        _merge_kernel,
        grid=(groups, total // rows),
        in_specs=[act(NA_WIDTH), act(GQA_WIDTH), act(D_MODEL), mod, mod, mod,
                  pl.BlockSpec((1, D_MODEL), lambda g, i: (0, 0)),
                  const(w_gates), const(w_o_a), const(w_o_b), const(w_out)],
        out_specs=act(D_MODEL),
        out_shape=jax.ShapeDtypeStruct(x.shape, F32),
        compiler_params=_params(("arbitrary", "arbitrary")),
        name="merge",
    )(a_att, b_att, x, shift, scale, gate, norm_g, w_gates, w_o_a, w_o_b, w_out)


def _rope_tables(seq):
    t = jnp.arange(seq, dtype=jnp.int32)
    inv = 1.0 / (ROPE_THETA ** (jnp.arange(ROT_HALF, dtype=F32) / ROT_HALF))
    ang_r = (t // GRID_W).astype(F32)[:, None] * inv[None, :]
    ang_c = (t % GRID_W).astype(F32)[:, None] * inv[None, :]
    cos = jnp.concatenate([jnp.cos(ang_r)] * 2 + [jnp.cos(ang_c)] * 2, axis=1)
    sin = jnp.concatenate([-jnp.sin(ang_r), jnp.sin(ang_r), -jnp.sin(ang_c), jnp.sin(ang_c)], axis=1)
    reps = LANES // HEAD_DIM
    return jnp.tile(cos, (1, reps)), jnp.tile(sin, (1, reps))


def _group_sum_matrix():
    idx = np.arange(MXU_COLS) // HEAD_DIM
    return jnp.asarray(idx[:, None] == idx[None, :], dtype=BF16)


def kernel(x, c, ctx, c_ctx, w_ada, b_ada, norm_g, w_in, q_norm_a, k_norm_a, q_norm_b, k_norm_b,
           rpb, w_o_a, w_o_b, w_out):
    batch, seq, _ = x.shape
    ctx_len = ctx.shape[1]
    depth = w_ada.shape[0]
    rows = seq // GRID_W
    assert seq % GRID_W == 0 and rows >= WIN_R and seq % PROJ_ROWS == 0 and seq % GQA_Q_ROWS == 0

    pad = -(batch + 1) % 8
    c_rows = jnp.concatenate([c, c_ctx[None, :], jnp.zeros((pad, D_MODEL), F32)], axis=0)
    mod = _modulation(c_rows, w_ada, b_ada)

    cos_x, sin_x = _rope_tables(seq)
    ctx_rows = batch * ctx_len
    ctx = ctx.reshape(1, ctx_rows, D_MODEL)
    cos_c = jnp.ones((ctx_rows, LANES), F32)
    sin_c = jnp.zeros((ctx_rows, LANES), F32)
    flat = lambda a: a.reshape((1, ctx_rows) + a.shape[2:])
    per_batch_ctx = lambda a: a.reshape((batch, a.shape[1] // batch) + a.shape[2:])
    gsum = _group_sum_matrix()
    tables = _bias_tables(rpb)
    w_in, w_o_a, w_o_b, w_out = (w.astype(BF16) for w in (w_in, w_o_a, w_o_b, w_out))
    w_gates = w_in[:, :, SEG_OFFSETS[-1]:]

    for l in range(depth):
        update_ctx = l < depth - 1
        shift, scale, gate = (mod[l, :, i * D_MODEL:(i + 1) * D_MODEL] for i in range(3))
        per_batch = lambda m: m[:batch, None, :]
        for_ctx = lambda m: m[batch][None, None, :]
        tile = lambda g, n: jnp.tile(g, n)
        gains = jnp.stack([tile(k_norm_a[l], NA_HEADS), tile(k_norm_b[l], NA_HEADS),
                           tile(q_norm_a[l], NA_HEADS) * (QK_SCALE * LOG2E),
                           tile(q_norm_b[l], GQA_Q_HEADS) * (QK_SCALE * LOG2E)])
        g_row = norm_g[l][None, :]

        ka, vat, kbd, vbt, qa, qb = _project(
            x, per_batch(shift), per_batch(scale), g_row, w_in, l, gsum, gains, cos_x, sin_x,
            len(SEG_WIDTHS), PROJ_ROWS, PROJ_ROWS)
        nseg_c = len(SEG_WIDTHS) if update_ctx else KV_SEGS
        pc = _project(ctx, for_ctx(shift), for_ctx(scale), g_row, w_in, l, gsum, gains, cos_c, sin_c,
                      nseg_c, min(PROJ_ROWS, ctx_rows), ctx_len)
        pc = [per_batch_ctx(a) for a in pc]
        cka, cvat, ckbd, cvbt = pc[:KV_SEGS]

        a_att = _neighbourhood_attention(qa, ka, vat, cka, cvat, tables, l)
        b_att = _gqa_attention(qb, kbd, vbt, ckbd, cvbt)
        x_new = _merge(a_att, b_att, x, per_batch(shift), per_batch(scale), per_batch(gate), g_row,
                       w_gates, w_o_a, w_o_b, w_out, l, MERGE_ROWS)

        if update_ctx:
            cqa, cqb = pc[KV_SEGS:]
            c_a, c_b = _context_attention(cqa, cka, cvat, cqb, ckbd, cvbt)
            ctx = _merge(flat(c_a), flat(c_b), ctx, for_ctx(shift), for_ctx(scale), for_ctx(gate), g_row,
                         w_gates, w_o_a, w_o_b, w_out, l, min(MERGE_ROWS, ctx_rows))
        x = x_new
    return x
```

```python
import functools

import numpy as np
import jax
import jax.numpy as jnp
from jax import lax
from jax.experimental import pallas as pl
from jax.experimental.pallas import tpu as pltpu

D_MODEL = 1024
GRID_W = 64
HEAD_DIM = 64
NA_HEADS = 8
NA_WIDTH = NA_HEADS * HEAD_DIM
WIN_R = 8
WIN_C = 16
GQA_Q_HEADS = 8
GQA_KV_HEADS = 2
GQA_WIDTH = GQA_Q_HEADS * HEAD_DIM
GQA_KV_WIDTH = GQA_KV_HEADS * HEAD_DIM
ROPE_THETA = 10000.0
ROT_AXIS = HEAD_DIM // 2
ROT_HALF = ROT_AXIS // 2
EPS = 1e-6
QK_SCALE = HEAD_DIM ** -0.5
LOG2E = 1.4426950408889634

LANES = 128
MXU_COLS = 256
MASKED = -1e30
VMEM_LIMIT = 56 * 1024 * 1024

PROJ_ROWS = 1024
MERGE_ROWS = 1024
GQA_Q_ROWS = 256
GQA_Q_BLOCKS = 4
GQA_K_ROWS = 256
GQA_SUB_ROWS = 128
GQA_AHEAD = 1
QUAD = 4
ONES_ROWS = 16
NA_SUB_ROWS = 128
NA_AHEAD = 6
NA_UNROLL = 16

F32 = jnp.float32
BF16 = jnp.bfloat16

SEG_KA, SEG_VA, SEG_KB, SEG_VB, SEG_QA, SEG_QB = range(6)
SEG_WIDTHS = (NA_WIDTH, NA_WIDTH, GQA_KV_WIDTH, GQA_KV_WIDTH, NA_WIDTH, GQA_WIDTH)
SEG_OFFSETS = tuple(int(v) for v in np.cumsum((0,) + SEG_WIDTHS))
GATE_WIDTHS = (NA_WIDTH, GQA_WIDTH, D_MODEL, D_MODEL)
GATE_OFFSETS = tuple(int(v) for v in np.cumsum((0,) + GATE_WIDTHS))
SEG_OUT_WIDTHS = tuple(2 * w if s == SEG_KB else w for s, w in enumerate(SEG_WIDTHS))
KV_SEGS = 4
SEG_ISSUE_RANK = {SEG_QB: 0, SEG_KB: 1, SEG_QA: 2, SEG_KA: 3, SEG_VB: 4, SEG_VA: 5}
SEG_GAIN_ROW = {SEG_KA: 0, SEG_KB: 1, SEG_QA: 2, SEG_QB: 3}
SEG_ROPE = (SEG_KB, SEG_QB)
SEG_T_SLAB = {SEG_VA: LANES, SEG_VB: GQA_K_ROWS}


def _dot(a, b):
    return jnp.dot(a, b, preferred_element_type=F32)


def _dot_t(a, b):
    return lax.dot_general(a, b, (((1,), (1,)), ((), ())), preferred_element_type=F32)


def _params(semantics):
    return pltpu.CompilerParams(dimension_semantics=semantics, vmem_limit_bytes=VMEM_LIMIT)


def _mod_kernel(c_ref, w_ref, b_ref, o_ref):
    c = c_ref[...]
    a = c * jax.nn.sigmoid(c)
    o_ref[0] = jnp.dot(a, w_ref[0], preferred_element_type=F32,
                       precision=lax.Precision.HIGHEST) + b_ref[0]


def _modulation(c_rows, w_ada, b_ada):
    depth = w_ada.shape[0]
    rows = c_rows.shape[0]
    ncol = w_ada.shape[2] // D_MODEL
    return pl.pallas_call(
        _mod_kernel,
        grid=(depth, ncol),
        in_specs=[
            pl.BlockSpec((rows, D_MODEL), lambda l, j: (0, 0)),
            pl.BlockSpec((1, D_MODEL, D_MODEL), lambda l, j: (l, 0, j)),
            pl.BlockSpec((1, 1, D_MODEL), lambda l, j: (l, 0, j)),
        ],
        out_specs=pl.BlockSpec((1, rows, D_MODEL), lambda l, j: (l, 0, j)),
        out_shape=jax.ShapeDtypeStruct((depth, rows, ncol * D_MODEL), F32),
        compiler_params=_params(("arbitrary", "arbitrary")),
        name="mod",
    )(c_rows, w_ada, b_ada.reshape(depth, 1, -1))


def _head_norm(p, gsum, gain):
    sq = (p * p).astype(BF16)
    cols = min(MXU_COLS, p.shape[1])
    parts = []
    for j in range(p.shape[1] // cols):
        sl = slice(j * cols, (j + 1) * cols)
        ss = _dot(sq[:, sl], gsum[:cols, :cols])
        parts.append(p[:, sl] * lax.rsqrt(ss * (1.0 / HEAD_DIM) + EPS))
    y = parts[0] if len(parts) == 1 else jnp.concatenate(parts, axis=1)
    return y * gain


def _rope(y, cos, sin):
    width = y.shape[1]
    lane = lax.broadcasted_iota(jnp.int32, y.shape, 1)
    ahead = pltpu.roll(y, width - ROT_HALF, axis=1)
    behind = pltpu.roll(y, ROT_HALF, axis=1)
    partner = jnp.where((lane % ROT_AXIS) < ROT_HALF, ahead, behind)
    reps = width // LANES
    cos = cos if reps == 1 else jnp.concatenate([cos] * reps, axis=1)
    sin = sin if reps == 1 else jnp.concatenate([sin] * reps, axis=1)
    return y * cos + partner * sin


def _duplicate_heads(y):
    lo = lax.broadcasted_iota(jnp.int32, y.shape, 1) < HEAD_DIM
    swapped = pltpu.roll(y, HEAD_DIM, axis=1)
    return jnp.concatenate([jnp.where(lo, y, swapped), jnp.where(lo, swapped, y)], axis=1)


def _modulated_norm(x, g_ref, scale_ref, shift_ref):
    ms = jnp.mean(x * x, axis=-1, keepdims=True)
    gmod = g_ref[...] * (1.0 + scale_ref[0])
    return (x * lax.rsqrt(ms + EPS) * gmod + shift_ref[0]).astype(BF16)


def _proj_kernel(x_ref, shift_ref, scale_ref, g_ref, w_ref, gsum_ref, gain_ref, cos_ref, sin_ref,
                 *out_refs, nseg, max_slab):
    h = _modulated_norm(x_ref[0], g_ref, scale_ref, shift_ref)
    gsum = gsum_ref[...]
    for seg in sorted(range(nseg), key=SEG_ISSUE_RANK.__getitem__):
        width = SEG_WIDTHS[seg]
        p = _dot(h, w_ref[:, SEG_OFFSETS[seg]:SEG_OFFSETS[seg + 1]])
        if seg in SEG_GAIN_ROW:
            row = SEG_GAIN_ROW[seg]
            p = _head_norm(p, gsum, gain_ref[row:row + 1, :width])
        if seg in SEG_ROPE:
            p = _rope(p, cos_ref[...], sin_ref[...])
        if seg == SEG_KB:
            p = _duplicate_heads(p)
        if seg in SEG_T_SLAB:
            pt = p.T.astype(BF16)
            slab = min(SEG_T_SLAB[seg], max_slab)
            for j in range(pt.shape[1] // slab):
                out_refs[seg][0, j] = pt[:, j * slab:(j + 1) * slab]
        else:
            out_refs[seg][0] = p.astype(BF16)


def _project(x, shift, scale, norm_g, w_all, layer, gsum, gains, cos, sin, nseg, rows, max_slab):
    groups, total, _ = x.shape
    ncols = SEG_OFFSETS[nseg]
    const = lambda g, i: (0, 0)

    def out_layout(s):
        width = SEG_OUT_WIDTHS[s]
        if s not in SEG_T_SLAB:
            return (groups, total, width), pl.BlockSpec((1, rows, width), lambda g, i: (g, i, 0))
        slab = min(SEG_T_SLAB[s], max_slab)
        return ((groups, total // slab, width, slab),
                pl.BlockSpec((1, rows // slab, width, slab), lambda g, i: (g, i, 0, 0)))

    layouts = [out_layout(s) for s in range(nseg)]
    return pl.pallas_call(
        functools.partial(_proj_kernel, nseg=nseg, max_slab=max_slab),
        grid=(groups, total // rows),
        in_specs=[
            pl.BlockSpec((1, rows, D_MODEL), lambda g, i: (g, i, 0)),
            pl.BlockSpec((1, 1, D_MODEL), lambda g, i: (g, 0, 0)),
            pl.BlockSpec((1, 1, D_MODEL), lambda g, i: (g, 0, 0)),
            pl.BlockSpec((1, D_MODEL), const),
            pl.BlockSpec((None, D_MODEL, ncols), lambda g, i: (layer, 0, 0), pipeline_mode=pl.Buffered(1)),
            pl.BlockSpec((MXU_COLS, MXU_COLS), const),
            pl.BlockSpec(gains.shape, const),
            pl.BlockSpec((rows, LANES), lambda g, i: (i, 0)),
            pl.BlockSpec((rows, LANES), lambda g, i: (i, 0)),
        ],
        out_specs=[spec for _, spec in layouts],
        out_shape=[jax.ShapeDtypeStruct(shape, BF16) for shape, _ in layouts],
        compiler_params=_params(("arbitrary", "arbitrary")),
        name="proj",
    )(x, shift, scale, norm_g, w_all, gsum, gains, cos, sin)


def _stack_heads(pair):
    lo = lax.broadcasted_iota(jnp.int32, pair.shape, 1) < HEAD_DIM
    zero = jnp.zeros_like(pair)
    return jnp.concatenate([jnp.where(lo, pair, zero), jnp.where(lo, zero, pair)], axis=0)


def _stack_quad(q):
    head = lax.broadcasted_iota(jnp.int32, q.shape, 1) // HEAD_DIM
    zero = jnp.zeros_like(q)
    return jnp.concatenate([jnp.where(head == h, q, zero) for h in range(QUAD)], axis=0)


def _unstack_quad_t(o, rows):
    ot = o.T
    head = lax.broadcasted_iota(jnp.int32, (rows, QUAD * HEAD_DIM), 1) // HEAD_DIM
    out = ot[:rows]
    for h in range(1, QUAD):
        out = jnp.where(head == h, ot[h * rows:(h + 1) * rows], out)
    return out


def _pv_t(vt, s, m):
    lhs = jnp.concatenate([vt, jnp.ones((ONES_ROWS, vt.shape[1]), vt.dtype)], axis=0)
    return _dot(lhs, jnp.exp2((s - m).astype(BF16)))


def _normalise_t(acc, dims):
    return acc[:dims] * (1.0 / acc[dims:dims + 1])


def _softmax_pv_t(logits, values_t):
    m = functools.reduce(jnp.maximum, [jnp.max(s, axis=0, keepdims=True) for s in logits])
    acc = functools.reduce(jnp.add, [_pv_t(vt, s, m) for vt, s in zip(values_t, logits)])
    return _normalise_t(acc, values_t[0].shape[0])


def _na_window(r, rows, slabs):
    start = jnp.clip(r - WIN_R // 2, 0, rows - WIN_R)
    slab0 = jnp.minimum(start // 2, rows // 2 - slabs)
    first = 2 * slab0
    blocks = []
    for w in range(2 * slabs):
        key_row = first + w
        valid = (key_row >= start) & (key_row < start + WIN_R)
        blocks.append(jnp.where(valid, key_row - r + (WIN_R - 1), 2 * WIN_R - 1))
    return slab0, blocks


def _na_kernel(q_ref, k_ref, vt_ref, kc_ref, vct_ref, t_ref, o_ref, qt_ref, *, rows, unroll):
    kc = kc_ref[0]
    vct = jnp.concatenate([vct_ref[0, j] for j in range(vct_ref.shape[1])], axis=1)
    assert unroll % 2 == 0 and rows % 2 == 0 and WIN_R % 2 == 0

    def logits(r, slot):
        slabs = WIN_R // 2 + slot % 2
        slab0, blocks = _na_window(r, rows, slabs)
        qt_ref[slot] = _stack_quad(q_ref[0, pl.ds(pl.multiple_of(r * GRID_W, GRID_W), GRID_W), :]).T
        qs_t = qt_ref[slot]
        kw = k_ref[0, pl.ds(pl.multiple_of(slab0 * LANES, LANES), slabs * LANES), :]
        bias = jnp.concatenate([t_ref[b] for b in blocks], axis=0)
        return _dot(kw, qs_t) + bias, _dot(kc, qs_t), slab0

    def finish(r, s_win, s_ctx, slab0):
        slabs = s_win.shape[0] // LANES
        vt = vt_ref[0, pl.ds(slab0, slabs)]
        vtw = jnp.concatenate([vt[j] for j in range(slabs)], axis=1)
        sub = NA_SUB_ROWS
        parts = [s[i:i + sub] for s in (s_ctx, s_win) for i in range(0, s.shape[0], sub)]
        _, acc = _attend_t(parts, jnp.concatenate([vct, vtw], axis=1), None)
        o = _normalise_t(acc, vct.shape[0])
        o_ref[0, pl.ds(pl.multiple_of(r * GRID_W, GRID_W), GRID_W), :] = _unstack_quad_t(o, GRID_W).astype(BF16)

    def body(i, carry):
        pending = []
        for e in range(unroll):
            pending.append((i * unroll + e,) + logits(i * unroll + e, e))
            if len(pending) > NA_AHEAD:
                finish(*pending.pop(0))
        for args in pending:
            finish(*args)
        return carry

    lax.fori_loop(0, rows // unroll, body, 0)


def _neighbourhood_attention(qa, ka, vat, cka, cvat, tables, layer):
    batch, seq, _ = qa.shape
    ctx_len = cka.shape[1]
    rows = seq // GRID_W
    width = QUAD * HEAD_DIM
    blk = lambda n: pl.BlockSpec((1, n, width), lambda b, j: (b, 0, j))
    vblk = lambda a: pl.BlockSpec((1, a.shape[1], width, LANES), lambda b, j: (b, 0, j, 0))
    return pl.pallas_call(
        functools.partial(_na_kernel, rows=rows, unroll=NA_UNROLL),
        grid=(batch, NA_HEADS // QUAD),
        in_specs=[blk(seq), blk(seq), vblk(vat), blk(ctx_len), vblk(cvat),
                  pl.BlockSpec((None, None) + tables.shape[2:], lambda b, j: (layer, j, 0, 0, 0))],
        out_specs=blk(seq),
        out_shape=jax.ShapeDtypeStruct((batch, seq, NA_WIDTH), BF16),
        scratch_shapes=[pltpu.VMEM((NA_UNROLL, width, QUAD * GRID_W), BF16)],
        compiler_params=_params(("arbitrary", "arbitrary")),
        name="na",
    )(qa, ka, vat, cka, cvat, tables)


def _bias_kernel(rpb_ref, onehot_ref, mask_ref, o_ref):
    o_ref[0] = (jnp.dot(rpb_ref[0], onehot_ref[...], preferred_element_type=F32,
                        precision=lax.Precision.HIGHEST) + mask_ref[...]) * LOG2E


def _bias_tables(rpb):
    depth = rpb.shape[0]
    n_row_off, n_col_off = 2 * WIN_R - 1, 2 * WIN_C - 1
    cols = np.arange(GRID_W)
    cstart = np.clip(cols - WIN_C // 2, 0, GRID_W - WIN_C)
    kcol = np.arange(GRID_W)
    valid = (kcol[None, :] >= cstart[:, None]) & (kcol[None, :] < cstart[:, None] + WIN_C)
    col_off = kcol[None, :] - cols[:, None] + (WIN_C - 1)
    pad_r, pad_c = -n_row_off % 8, -n_col_off % 8
    assert pad_r >= 1
    onehot = (np.arange(n_col_off + pad_c)[:, None, None] == col_off.T[None]) & valid.T[None]
    onehot = jnp.asarray(onehot.reshape(n_col_off + pad_c, GRID_W * GRID_W), dtype=F32)
    mask = jnp.asarray(np.where(valid.T, 0.0, MASKED).reshape(1, GRID_W * GRID_W), dtype=F32)
    heads = depth * NA_HEADS
    rpb_p = jnp.pad(rpb.reshape(heads, n_row_off, n_col_off), ((0, 0), (0, pad_r), (0, pad_c)))
    toep = pl.pallas_call(
        _bias_kernel,
        grid=(heads,),
        in_specs=[pl.BlockSpec((1,) + rpb_p.shape[1:], lambda h: (h, 0, 0)),
                  pl.BlockSpec(onehot.shape, lambda h: (0, 0)),
                  pl.BlockSpec(mask.shape, lambda h: (0, 0))],
        out_specs=pl.BlockSpec((1, n_row_off + pad_r, GRID_W * GRID_W), lambda h: (h, 0, 0)),
        out_shape=jax.ShapeDtypeStruct((heads, n_row_off + pad_r, GRID_W * GRID_W), F32),
        compiler_params=_params(("arbitrary",)),
        name="bias",
    )(rpb_p, onehot, mask)
    groups = NA_HEADS // QUAD
    toep = toep.reshape(depth, groups, QUAD, n_row_off + pad_r, GRID_W, GRID_W)
    table = jnp.transpose(toep, (0, 1, 3, 4, 2, 5)).reshape(depth, groups, -1, GRID_W, QUAD * GRID_W)
    return jnp.where((jnp.arange(table.shape[2]) < n_row_off)[None, None, :, None, None], table, MASKED * LOG2E)


def _stack_group(q):
    return jnp.concatenate([_stack_heads(q[:, :LANES]), _stack_heads(q[:, LANES:])], axis=0)


def _unstack_group_t(o, rows):
    pairs = [jnp.concatenate([o[:, (2 * j) * rows:(2 * j + 1) * rows],
                              o[:, (2 * j + 1) * rows:(2 * j + 2) * rows]], axis=0).T for j in range(2)]
    return jnp.concatenate(pairs, axis=1)


def _scores_t(qs_t, k):
    return _dot(k, qs_t)


def _attend_t(parts, vt, state):
    m_old = None if state is None else state[0]
    m, ps, ms = m_old, [], []
    for s in parts:
        m_part = jnp.max(s, axis=0, keepdims=True)
        m = m_part if m is None else jnp.maximum(m, m_part)
        ps.append(jnp.exp2((s - m).astype(BF16)))
        ms.append(m)
    ps = [p * jnp.exp2(m_i - m).astype(BF16) for p, m_i in zip(ps[:-1], ms[:-1])] + ps[-1:]
    lhs = jnp.concatenate([vt, jnp.ones((ONES_ROWS, vt.shape[1]), vt.dtype)], axis=0)
    pv = _dot(lhs, ps[0] if len(ps) == 1 else jnp.concatenate(ps, axis=0))
    if state is None:
        return m, pv
    return m, jnp.exp2(m_old - m) * state[1] + pv


def _gqa_kernel(q_ref, k_ref, vt_ref, kc_ref, vct_ref, o_ref, qt_ref, *, chunks, tq, tk, blocks):
    for j in range(blocks):
        qt_ref[j] = _stack_group(q_ref[0, j * tq:(j + 1) * tq, :]).T
    pending = []
    states = [None] * blocks

    def drain(limit):
        while len(pending) > limit:
            j, parts, vt = pending.pop(0)
            states[j] = _attend_t(parts, vt, states[j])

    def logits(qs, keys, c, rows):
        sub = min(GQA_SUB_ROWS, rows)
        return [_scores_t(qs, keys[0, c * rows + i * sub:c * rows + (i + 1) * sub, :]) for i in range(rows // sub)]

    for j in range(blocks):
        qs = qt_ref[j]
        for c in range(vct_ref.shape[1]):
            pending.append((j, logits(qs, kc_ref, c, vct_ref.shape[3]), vct_ref[0, c]))
            drain(GQA_AHEAD)
        for c in range(chunks):
            pending.append((j, logits(qs, k_ref, c, tk), vt_ref[0, c]))
            drain(GQA_AHEAD)
    drain(0)
    for j in range(blocks):
        _, acc = states[j]
        o_ref[0, j * tq:(j + 1) * tq, :] = _unstack_group_t(_normalise_t(acc, HEAD_DIM), tq).astype(BF16)


def _gqa_attention(qb, kbd, vbt, ckbd, cvbt):
    batch, seq, _ = qb.shape
    ctx_len = ckbd.shape[1]
    tq, tk, blocks = GQA_Q_ROWS, vbt.shape[3], GQA_Q_BLOCKS
    kspec = lambda n: pl.BlockSpec((1, n, LANES), lambda b, g, i: (b, 0, g))
    vspec = lambda a: pl.BlockSpec((1, a.shape[1], HEAD_DIM, a.shape[3]), lambda b, g, i: (b, 0, g, 0))
    qspec = pl.BlockSpec((1, blocks * tq, 2 * LANES), lambda b, g, i: (b, i, g))
    return pl.pallas_call(
        functools.partial(_gqa_kernel, chunks=seq // tk, tq=tq, tk=tk, blocks=blocks),
        grid=(batch, GQA_KV_HEADS, seq // (blocks * tq)),
        in_specs=[qspec, kspec(seq), vspec(vbt), kspec(ctx_len), vspec(cvbt)],
        out_specs=qspec,
        out_shape=jax.ShapeDtypeStruct((batch, seq, GQA_WIDTH), BF16),
        scratch_shapes=[pltpu.VMEM((blocks, LANES, (GQA_Q_HEADS // GQA_KV_HEADS) * tq), BF16)],
        compiler_params=_params(("arbitrary", "arbitrary", "arbitrary")),
        name="gqa",
    )(qb, kbd, vbt, ckbd, cvbt)


def _ctx_kernel(qa_ref, ka_ref, vat_ref, qb_ref, kb_ref, vbt_ref, oa_ref, ob_ref):
    n = qa_ref.shape[1]
    width = QUAD * HEAD_DIM
    for g in range(NA_HEADS // QUAD):
        sl = slice(g * width, (g + 1) * width)
        vt = jnp.concatenate([vat_ref[0, j, sl, :] for j in range(vat_ref.shape[1])], axis=1)
        o = _softmax_pv_t([_dot_t(ka_ref[0, :, sl], _stack_quad(qa_ref[0, :, sl]))], [vt])
        oa_ref[0, :, sl] = _unstack_quad_t(o, n).astype(BF16)
    for g in range(GQA_KV_HEADS):
        sl = slice(g * 2 * LANES, (g + 1) * 2 * LANES)
        s = _scores_t(_stack_group(qb_ref[0, :, sl]).T, kb_ref[0, :, g * LANES:(g + 1) * LANES])
        vt = jnp.concatenate([vbt_ref[0, j, g * HEAD_DIM:(g + 1) * HEAD_DIM, :] for j in range(vbt_ref.shape[1])],
                             axis=1)
        o = _softmax_pv_t([s], [vt])
        ob_ref[0, :, sl] = _unstack_group_t(o, n).astype(BF16)


def _context_attention(cqa, cka, cvat, cqb, ckbd, cvbt):
    batch, n, _ = cqa.shape
    spec = lambda a: pl.BlockSpec((1,) + a.shape[1:], lambda b: (b,) + (0,) * (a.ndim - 1))
    args = (cqa, cka, cvat, cqb, ckbd, cvbt)
    return pl.pallas_call(
        _ctx_kernel,
        grid=(batch,),
        in_specs=[spec(a) for a in args],
        out_specs=[spec(cqa), spec(cqb)],
        out_shape=[jax.ShapeDtypeStruct(cqa.shape, BF16), jax.ShapeDtypeStruct(cqb.shape, BF16)],
        compiler_params=_params(("arbitrary",)),
        name="ctx",
    )(*args)


def _merge_kernel(a_ref, b_ref, x_ref, shift_ref, scale_ref, gate_ref, g_ref, wzg_ref,
                  woa_ref, wob_ref, wout_ref, o_ref):
    x = x_ref[0]
    h = _modulated_norm(x, g_ref, scale_ref, shift_ref)
    z_a, z_b, g_a, g_b = (_dot(h, wzg_ref[:, GATE_OFFSETS[i]:GATE_OFFSETS[i + 1]]) for i in range(4))
    o_a = _dot((a_ref[0].astype(F32) * (z_a * jax.nn.sigmoid(z_a))).astype(BF16), woa_ref[...])
    o_b = _dot((b_ref[0].astype(F32) * (z_b * jax.nn.sigmoid(z_b))).astype(BF16), wob_ref[...])
    merged = jax.nn.sigmoid(g_a) * o_a + jax.nn.sigmoid(g_b) * o_b
    o_ref[0] = x + gate_ref[0] * _dot(merged.astype(BF16), wout_ref[...])


def _merge(a_att, b_att, x, shift, scale, gate, norm_g, w_gates, w_o_a, w_o_b, w_out, layer, rows):
    groups, total, _ = x.shape
    act = lambda w: pl.BlockSpec((1, rows, w), lambda g, i: (g, i, 0))
    mod = pl.BlockSpec((1, 1, D_MODEL), lambda g, i: (g, 0, 0))
    const = lambda a: pl.BlockSpec((None,) + a.shape[1:], lambda g, i: (layer, 0, 0))
    return pl.pallas_call(
        _merge_kernel,
        grid=(groups, total // rows),
        in_specs=[act(NA_WIDTH), act(GQA_WIDTH), act(D_MODEL), mod, mod, mod,
                  pl.BlockSpec((1, D_MODEL), lambda g, i: (0, 0)),
                  const(w_gates), const(w_o_a), const(w_o_b), const(w_out)],
        out_specs=act(D_MODEL),
        out_shape=jax.ShapeDtypeStruct(x.shape, F32),
        compiler_params=_params(("arbitrary", "arbitrary")),
        name="merge",
    )(a_att, b_att, x, shift, scale, gate, norm_g, w_gates, w_o_a, w_o_b, w_out)


def _rope_tables(seq):
    t = jnp.arange(seq, dtype=jnp.int32)
    inv = 1.0 / (ROPE_THETA ** (jnp.arange(ROT_HALF, dtype=F32) / ROT_HALF))
    ang_r = (t // GRID_W).astype(F32)[:, None] * inv[None, :]
    ang_c = (t % GRID_W).astype(F32)[:, None] * inv[None, :]
    cos = jnp.concatenate([jnp.cos(ang_r)] * 2 + [jnp.cos(ang_c)] * 2, axis=1)
    sin = jnp.concatenate([-jnp.sin(ang_r), jnp.sin(ang_r), -jnp.sin(ang_c), jnp.sin(ang_c)], axis=1)
    reps = LANES // HEAD_DIM
    return jnp.tile(cos, (1, reps)), jnp.tile(sin, (1, reps))


def _group_sum_matrix():
    idx = np.arange(MXU_COLS) // HEAD_DIM
    return jnp.asarray(idx[:, None] == idx[None, :], dtype=BF16)


def kernel(x, c, ctx, c_ctx, w_ada, b_ada, norm_g, w_in, q_norm_a, k_norm_a, q_norm_b, k_norm_b,
           rpb, w_o_a, w_o_b, w_out):
    batch, seq, _ = x.shape
    ctx_len = ctx.shape[1]
    depth = w_ada.shape[0]
    rows = seq // GRID_W
    assert seq % GRID_W == 0 and rows >= WIN_R and seq % PROJ_ROWS == 0 and seq % GQA_Q_ROWS == 0

    pad = -(batch + 1) % 8
    c_rows = jnp.concatenate([c, c_ctx[None, :], jnp.zeros((pad, D_MODEL), F32)], axis=0)
    mod = _modulation(c_rows, w_ada, b_ada)

    cos_x, sin_x = _rope_tables(seq)
    ctx_rows = batch * ctx_len
    ctx = ctx.reshape(1, ctx_rows, D_MODEL)
    cos_c = jnp.ones((ctx_rows, LANES), F32)
    sin_c = jnp.zeros((ctx_rows, LANES), F32)
    flat = lambda a: a.reshape((1, ctx_rows) + a.shape[2:])
    per_batch_ctx = lambda a: a.reshape((batch, a.shape[1] // batch) + a.shape[2:])
    gsum = _group_sum_matrix()
    tables = _bias_tables(rpb)
    w_in, w_o_a, w_o_b, w_out = (w.astype(BF16) for w in (w_in, w_o_a, w_o_b, w_out))
    w_gates = w_in[:, :, SEG_OFFSETS[-1]:]

    for l in range(depth):
        update_ctx = l < depth - 1
        shift, scale, gate = (mod[l, :, i * D_MODEL:(i + 1) * D_MODEL] for i in range(3))
        per_batch = lambda m: m[:batch, None, :]
        for_ctx = lambda m: m[batch][None, None, :]
        tile = lambda g, n: jnp.tile(g, n)
        gains = jnp.stack([tile(k_norm_a[l], NA_HEADS), tile(k_norm_b[l], NA_HEADS),
                           tile(q_norm_a[l], NA_HEADS) * (QK_SCALE * LOG2E),
                           tile(q_norm_b[l], GQA_Q_HEADS) * (QK_SCALE * LOG2E)])
        g_row = norm_g[l][None, :]

        ka, vat, kbd, vbt, qa, qb = _project(
            x, per_batch(shift), per_batch(scale), g_row, w_in, l, gsum, gains, cos_x, sin_x,
            len(SEG_WIDTHS), PROJ_ROWS, PROJ_ROWS)
        nseg_c = len(SEG_WIDTHS) if update_ctx else KV_SEGS
        pc = _project(ctx, for_ctx(shift), for_ctx(scale), g_row, w_in, l, gsum, gains, cos_c, sin_c,
                      nseg_c, min(PROJ_ROWS, ctx_rows), ctx_len)
        pc = [per_batch_ctx(a) for a in pc]
        cka, cvat, ckbd, cvbt = pc[:KV_SEGS]

        a_att = _neighbourhood_attention(qa, ka, vat, cka, cvat, tables, l)
        b_att = _gqa_attention(qb, kbd, vbt, ckbd, cvbt)
        x_new = _merge(a_att, b_att, x, per_batch(shift), per_batch(scale), per_batch(gate), g_row,
                       w_gates, w_o_a, w_o_b, w_out, l, MERGE_ROWS)

        if update_ctx:
            cqa, cqb = pc[KV_SEGS:]
            c_a, c_b = _context_attention(cqa, cka, cvat, cqb, ckbd, cvbt)
            ctx = _merge(flat(c_a), flat(c_b), ctx, for_ctx(shift), for_ctx(scale), for_ctx(gate), g_row,
                         w_gates, w_o_a, w_o_b, w_out, l, min(MERGE_ROWS, ctx_rows))
        x = x_new
    return x
```

```python
import functools

import numpy as np
import jax
import jax.numpy as jnp
from jax import lax
from jax.experimental import pallas as pl
from jax.experimental.pallas import tpu as pltpu

D_MODEL = 1024
GRID_W = 64
HEAD_DIM = 64
NA_HEADS = 8
NA_WIDTH = NA_HEADS * HEAD_DIM
WIN_R = 8
WIN_C = 16
GQA_Q_HEADS = 8
GQA_KV_HEADS = 2
GQA_WIDTH = GQA_Q_HEADS * HEAD_DIM
GQA_KV_WIDTH = GQA_KV_HEADS * HEAD_DIM
ROPE_THETA = 10000.0
ROT_AXIS = HEAD_DIM // 2
ROT_HALF = ROT_AXIS // 2
EPS = 1e-6
QK_SCALE = HEAD_DIM ** -0.5
LOG2E = 1.4426950408889634

LANES = 128
MXU_COLS = 256
MASKED = -1e30
VMEM_LIMIT = 56 * 1024 * 1024

PROJ_ROWS = 1024
MERGE_ROWS = 1024
GQA_Q_ROWS = 256
GQA_Q_BLOCKS = 4
GQA_K_ROWS = 256
GQA_SUB_ROWS = 128
GQA_AHEAD = 1
QUAD = 4
ONES_ROWS = 16
NA_SUB_ROWS = 128
NA_AHEAD = 6
NA_UNROLL = 16

F32 = jnp.float32
BF16 = jnp.bfloat16

SEG_KA, SEG_VA, SEG_KB, SEG_VB, SEG_QA, SEG_QB = range(6)
SEG_WIDTHS = (NA_WIDTH, NA_WIDTH, GQA_KV_WIDTH, GQA_KV_WIDTH, NA_WIDTH, GQA_WIDTH)
SEG_OFFSETS = tuple(int(v) for v in np.cumsum((0,) + SEG_WIDTHS))
GATE_WIDTHS = (NA_WIDTH, GQA_WIDTH, D_MODEL, D_MODEL)
GATE_OFFSETS = tuple(int(v) for v in np.cumsum((0,) + GATE_WIDTHS))
SEG_OUT_WIDTHS = tuple(2 * w if s == SEG_KB else w for s, w in enumerate(SEG_WIDTHS))
KV_SEGS = 4
SEG_ISSUE_RANK = {SEG_QB: 0, SEG_KB: 1, SEG_QA: 2, SEG_KA: 3, SEG_VB: 4, SEG_VA: 5}
SEG_GAIN_ROW = {SEG_KA: 0, SEG_KB: 1, SEG_QA: 2, SEG_QB: 3}
SEG_ROPE = (SEG_KB, SEG_QB)
SEG_T_SLAB = {SEG_VA: LANES, SEG_VB: GQA_K_ROWS}


def _dot(a, b):
    return jnp.dot(a, b, preferred_element_type=F32)


def _dot_t(a, b):
    return lax.dot_general(a, b, (((1,), (1,)), ((), ())), preferred_element_type=F32)


def _params(semantics):
    return pltpu.CompilerParams(dimension_semantics=semantics, vmem_limit_bytes=VMEM_LIMIT)


def _mod_kernel(c_ref, w_ref, b_ref, o_ref):
    c = c_ref[...]
    a = c * jax.nn.sigmoid(c)
    o_ref[0] = jnp.dot(a, w_ref[0], preferred_element_type=F32,
                       precision=lax.Precision.HIGHEST) + b_ref[0]


def _modulation(c_rows, w_ada, b_ada):
    depth = w_ada.shape[0]
    rows = c_rows.shape[0]
    ncol = w_ada.shape[2] // D_MODEL
    return pl.pallas_call(
        _mod_kernel,
        grid=(depth, ncol),
        in_specs=[
            pl.BlockSpec((rows, D_MODEL), lambda l, j: (0, 0)),
            pl.BlockSpec((1, D_MODEL, D_MODEL), lambda l, j: (l, 0, j)),
            pl.BlockSpec((1, 1, D_MODEL), lambda l, j: (l, 0, j)),
        ],
        out_specs=pl.BlockSpec((1, rows, D_MODEL), lambda l, j: (l, 0, j)),
        out_shape=jax.ShapeDtypeStruct((depth, rows, ncol * D_MODEL), F32),
        compiler_params=_params(("arbitrary", "arbitrary")),
        name="mod",
    )(c_rows, w_ada, b_ada.reshape(depth, 1, -1))


def _head_norm(p, gsum, gain):
    sq = (p * p).astype(BF16)
    cols = min(MXU_COLS, p.shape[1])
    parts = []
    for j in range(p.shape[1] // cols):
        sl = slice(j * cols, (j + 1) * cols)
        ss = _dot(sq[:, sl], gsum[:cols, :cols])
        parts.append(p[:, sl] * lax.rsqrt(ss * (1.0 / HEAD_DIM) + EPS))
    y = parts[0] if len(parts) == 1 else jnp.concatenate(parts, axis=1)
    return y * gain


def _rope(y, cos, sin):
    width = y.shape[1]
    lane = lax.broadcasted_iota(jnp.int32, y.shape, 1)
    ahead = pltpu.roll(y, width - ROT_HALF, axis=1)
    behind = pltpu.roll(y, ROT_HALF, axis=1)
    partner = jnp.where((lane % ROT_AXIS) < ROT_HALF, ahead, behind)
    reps = width // LANES
    cos = cos if reps == 1 else jnp.concatenate([cos] * reps, axis=1)
    sin = sin if reps == 1 else jnp.concatenate([sin] * reps, axis=1)
    return y * cos + partner * sin


def _duplicate_heads(y):
    lo = lax.broadcasted_iota(jnp.int32, y.shape, 1) < HEAD_DIM
    swapped = pltpu.roll(y, HEAD_DIM, axis=1)
    return jnp.concatenate([jnp.where(lo, y, swapped), jnp.where(lo, swapped, y)], axis=1)


def _modulated_norm(x, g_ref, scale_ref, shift_ref):
    ms = jnp.mean(x * x, axis=-1, keepdims=True)
    gmod = g_ref[...] * (1.0 + scale_ref[0])
    return (x * lax.rsqrt(ms + EPS) * gmod + shift_ref[0]).astype(BF16)


def _proj_kernel(x_ref, shift_ref, scale_ref, g_ref, w_ref, gsum_ref, gain_ref, cos_ref, sin_ref,
                 *out_refs, nseg, max_slab):
    h = _modulated_norm(x_ref[0], g_ref, scale_ref, shift_ref)
    gsum = gsum_ref[...]
    for seg in sorted(range(nseg), key=SEG_ISSUE_RANK.__getitem__):
        width = SEG_WIDTHS[seg]
        p = _dot(h, w_ref[:, SEG_OFFSETS[seg]:SEG_OFFSETS[seg + 1]])
        if seg in SEG_GAIN_ROW:
            row = SEG_GAIN_ROW[seg]
            p = _head_norm(p, gsum, gain_ref[row:row + 1, :width])
        if seg in SEG_ROPE:
            p = _rope(p, cos_ref[...], sin_ref[...])
        if seg == SEG_KB:
            p = _duplicate_heads(p)
        if seg in SEG_T_SLAB:
            pt = p.T.astype(BF16)
            slab = min(SEG_T_SLAB[seg], max_slab)
            for j in range(pt.shape[1] // slab):
                out_refs[seg][0, j] = pt[:, j * slab:(j + 1) * slab]
        else:
            out_refs[seg][0] = p.astype(BF16)


def _project(x, shift, scale, norm_g, w_all, layer, gsum, gains, cos, sin, nseg, rows, max_slab):
    groups, total, _ = x.shape
    ncols = SEG_OFFSETS[nseg]
    const = lambda g, i: (0, 0)

    def out_layout(s):
        width = SEG_OUT_WIDTHS[s]
        if s not in SEG_T_SLAB:
            return (groups, total, width), pl.BlockSpec((1, rows, width), lambda g, i: (g, i, 0))
        slab = min(SEG_T_SLAB[s], max_slab)
        return ((groups, total // slab, width, slab),
                pl.BlockSpec((1, rows // slab, width, slab), lambda g, i: (g, i, 0, 0)))

    layouts = [out_layout(s) for s in range(nseg)]
    return pl.pallas_call(
        functools.partial(_proj_kernel, nseg=nseg, max_slab=max_slab),
        grid=(groups, total // rows),
        in_specs=[
            pl.BlockSpec((1, rows, D_MODEL), lambda g, i: (g, i, 0)),
            pl.BlockSpec((1, 1, D_MODEL), lambda g, i: (g, 0, 0)),
            pl.BlockSpec((1, 1, D_MODEL), lambda g, i: (g, 0, 0)),
            pl.BlockSpec((1, D_MODEL), const),
            pl.BlockSpec((None, D_MODEL, ncols), lambda g, i: (layer, 0, 0), pipeline_mode=pl.Buffered(1)),
            pl.BlockSpec((MXU_COLS, MXU_COLS), const),
            pl.BlockSpec(gains.shape, const),
            pl.BlockSpec((rows, LANES), lambda g, i: (i, 0)),
            pl.BlockSpec((rows, LANES), lambda g, i: (i, 0)),
        ],
        out_specs=[spec for _, spec in layouts],
        out_shape=[jax.ShapeDtypeStruct(shape, BF16) for shape, _ in layouts],
        compiler_params=_params(("arbitrary", "arbitrary")),
        name="proj",
    )(x, shift, scale, norm_g, w_all, gsum, gains, cos, sin)


def _stack_heads(pair):
    lo = lax.broadcasted_iota(jnp.int32, pair.shape, 1) < HEAD_DIM
    zero = jnp.zeros_like(pair)
    return jnp.concatenate([jnp.where(lo, pair, zero), jnp.where(lo, zero, pair)], axis=0)


def _stack_quad(q):
    head = lax.broadcasted_iota(jnp.int32, q.shape, 1) // HEAD_DIM
    zero = jnp.zeros_like(q)
    return jnp.concatenate([jnp.where(head == h, q, zero) for h in range(QUAD)], axis=0)


def _unstack_quad_t(o, rows):
    ot = o.T
    head = lax.broadcasted_iota(jnp.int32, (rows, QUAD * HEAD_DIM), 1) // HEAD_DIM
    out = ot[:rows]
    for h in range(1, QUAD):
        out = jnp.where(head == h, ot[h * rows:(h + 1) * rows], out)
    return out


def _pv_t(vt, s, m):
    lhs = jnp.concatenate([vt, jnp.ones((ONES_ROWS, vt.shape[1]), vt.dtype)], axis=0)
    return _dot(lhs, jnp.exp2((s - m).astype(BF16)))


def _normalise_t(acc, dims):
    return acc[:dims] * (1.0 / acc[dims:dims + 1])


def _softmax_pv_t(logits, values_t):
    m = functools.reduce(jnp.maximum, [jnp.max(s, axis=0, keepdims=True) for s in logits])
    acc = functools.reduce(jnp.add, [_pv_t(vt, s, m) for vt, s in zip(values_t, logits)])
    return _normalise_t(acc, values_t[0].shape[0])


def _na_window(r, rows, slabs):
    start = jnp.clip(r - WIN_R // 2, 0, rows - WIN_R)
    slab0 = jnp.minimum(start // 2, rows // 2 - slabs)
    first = 2 * slab0
    blocks = []
    for w in range(2 * slabs):
        key_row = first + w
        valid = (key_row >= start) & (key_row < start + WIN_R)
        blocks.append(jnp.where(valid, key_row - r + (WIN_R - 1), 2 * WIN_R - 1))
    return slab0, blocks


def _na_kernel(q_ref, k_ref, vt_ref, kc_ref, vct_ref, t_ref, o_ref, qt_ref, tab_ref, *, rows, unroll):
    n_off = t_ref.shape[1]
    for off in range(n_off - 1):
        tab_ref[off] = jnp.concatenate([t_ref[h, off] for h in range(QUAD)], axis=1)
    tab_ref[n_off - 1] = jnp.full(tab_ref.shape[1:], MASKED * LOG2E, F32)
    kc = kc_ref[0]
    vct = jnp.concatenate([vct_ref[0, j] for j in range(vct_ref.shape[1])], axis=1)
    assert unroll % 2 == 0 and rows % 2 == 0 and WIN_R % 2 == 0

    def logits(r, slot):
        slabs = WIN_R // 2 + slot % 2
        slab0, blocks = _na_window(r, rows, slabs)
        qt_ref[slot] = _stack_quad(q_ref[0, pl.ds(pl.multiple_of(r * GRID_W, GRID_W), GRID_W), :]).T
        qs_t = qt_ref[slot]
        kw = k_ref[0, pl.ds(pl.multiple_of(slab0 * LANES, LANES), slabs * LANES), :]
        bias = jnp.concatenate([tab_ref[b] for b in blocks], axis=0)
        return _dot(kw, qs_t) + bias, _dot(kc, qs_t), slab0

    def finish(r, s_win, s_ctx, slab0):
        slabs = s_win.shape[0] // LANES
        vt = vt_ref[0, pl.ds(slab0, slabs)]
        vtw = jnp.concatenate([vt[j] for j in range(slabs)], axis=1)
        sub = NA_SUB_ROWS
        parts = [s[i:i + sub] for s in (s_ctx, s_win) for i in range(0, s.shape[0], sub)]
        _, acc = _attend_t(parts, jnp.concatenate([vct, vtw], axis=1), None)
        o = _normalise_t(acc, vct.shape[0])
        o_ref[0, pl.ds(pl.multiple_of(r * GRID_W, GRID_W), GRID_W), :] = _unstack_quad_t(o, GRID_W).astype(BF16)

    def body(i, carry):
        pending = []
        for e in range(unroll):
            pending.append((i * unroll + e,) + logits(i * unroll + e, e))
            if len(pending) > NA_AHEAD:
                finish(*pending.pop(0))
        for args in pending:
            finish(*args)
        return carry

    lax.fori_loop(0, rows // unroll, body, 0)


def _neighbourhood_attention(qa, ka, vat, cka, cvat, tables, layer):
    batch, seq, _ = qa.shape
    ctx_len = cka.shape[1]
    rows = seq // GRID_W
    width = QUAD * HEAD_DIM
    blk = lambda n: pl.BlockSpec((1, n, width), lambda b, j: (b, 0, j))
    vblk = lambda a: pl.BlockSpec((1, a.shape[1], width, LANES), lambda b, j: (b, 0, j, 0))
    return pl.pallas_call(
        functools.partial(_na_kernel, rows=rows, unroll=NA_UNROLL),
        grid=(batch, NA_HEADS // QUAD),
        in_specs=[blk(seq), blk(seq), vblk(vat), blk(ctx_len), vblk(cvat),
                  pl.BlockSpec((None, None) + tables.shape[2:], lambda b, j: (layer, j, 0, 0, 0, 0))],
        out_specs=blk(seq),
        out_shape=jax.ShapeDtypeStruct((batch, seq, NA_WIDTH), BF16),
        scratch_shapes=[pltpu.VMEM((NA_UNROLL, width, QUAD * GRID_W), BF16),
                        pltpu.VMEM((tables.shape[3], GRID_W, QUAD * GRID_W), F32)],
        compiler_params=_params(("arbitrary", "arbitrary")),
        name="na",
    )(qa, ka, vat, cka, cvat, tables)


def _bias_kernel(rpb_ref, onehot_ref, mask_ref, o_ref):
    o_ref[0] = (jnp.dot(rpb_ref[0], onehot_ref[...], preferred_element_type=F32,
                        precision=lax.Precision.HIGHEST) + mask_ref[...]) * LOG2E


def _bias_tables(rpb):
    depth = rpb.shape[0]
    n_row_off, n_col_off = 2 * WIN_R - 1, 2 * WIN_C - 1
    cols = np.arange(GRID_W)
    cstart = np.clip(cols - WIN_C // 2, 0, GRID_W - WIN_C)
    kcol = np.arange(GRID_W)
    valid = (kcol[None, :] >= cstart[:, None]) & (kcol[None, :] < cstart[:, None] + WIN_C)
    col_off = kcol[None, :] - cols[:, None] + (WIN_C - 1)
    pad_r, pad_c = -n_row_off % 8, -n_col_off % 8
    assert pad_r >= 1
    onehot = (np.arange(n_col_off + pad_c)[:, None, None] == col_off.T[None]) & valid.T[None]
    onehot = jnp.asarray(onehot.reshape(n_col_off + pad_c, GRID_W * GRID_W), dtype=F32)
    mask = jnp.asarray(np.where(valid.T, 0.0, MASKED).reshape(1, GRID_W * GRID_W), dtype=F32)
    heads = depth * NA_HEADS
    rpb_p = jnp.pad(rpb.reshape(heads, n_row_off, n_col_off), ((0, 0), (0, pad_r), (0, pad_c)))
    toep = pl.pallas_call(
        _bias_kernel,
        grid=(heads,),
        in_specs=[pl.BlockSpec((1,) + rpb_p.shape[1:], lambda h: (h, 0, 0)),
                  pl.BlockSpec(onehot.shape, lambda h: (0, 0)),
                  pl.BlockSpec(mask.shape, lambda h: (0, 0))],
        out_specs=pl.BlockSpec((1, n_row_off + pad_r, GRID_W * GRID_W), lambda h: (h, 0, 0)),
        out_shape=jax.ShapeDtypeStruct((heads, n_row_off + pad_r, GRID_W * GRID_W), F32),
        compiler_params=_params(("arbitrary",)),
        name="bias",
    )(rpb_p, onehot, mask)
    return toep.reshape(depth, NA_HEADS // QUAD, QUAD, n_row_off + pad_r, GRID_W, GRID_W)


def _stack_group(q):
    return jnp.concatenate([_stack_heads(q[:, :LANES]), _stack_heads(q[:, LANES:])], axis=0)


def _unstack_group_t(o, rows):
    pairs = [jnp.concatenate([o[:, (2 * j) * rows:(2 * j + 1) * rows],
                              o[:, (2 * j + 1) * rows:(2 * j + 2) * rows]], axis=0).T for j in range(2)]
    return jnp.concatenate(pairs, axis=1)


def _scores_t(qs_t, k):
    return _dot(k, qs_t)


def _attend_t(parts, vt, state):
    m_old = None if state is None else state[0]
    m, ps, ms = m_old, [], []
    for s in parts:
        m_part = jnp.max(s, axis=0, keepdims=True)
        m = m_part if m is None else jnp.maximum(m, m_part)
        ps.append(jnp.exp2((s - m).astype(BF16)))
        ms.append(m)
    ps = [p * jnp.exp2(m_i - m).astype(BF16) for p, m_i in zip(ps[:-1], ms[:-1])] + ps[-1:]
    lhs = jnp.concatenate([vt, jnp.ones((ONES_ROWS, vt.shape[1]), vt.dtype)], axis=0)
    pv = _dot(lhs, ps[0] if len(ps) == 1 else jnp.concatenate(ps, axis=0))
    if state is None:
        return m, pv
    return m, jnp.exp2(m_old - m) * state[1] + pv


def _gqa_kernel(q_ref, k_ref, vt_ref, kc_ref, vct_ref, o_ref, qt_ref, *, chunks, tq, tk, blocks):
    for j in range(blocks):
        qt_ref[j] = _stack_group(q_ref[0, j * tq:(j + 1) * tq, :]).T
    pending = []
    states = [None] * blocks

    def drain(limit):
        while len(pending) > limit:
            j, parts, vt = pending.pop(0)
            states[j] = _attend_t(parts, vt, states[j])

    def logits(qs, keys, c, rows):
        sub = min(GQA_SUB_ROWS, rows)
        return [_scores_t(qs, keys[0, c * rows + i * sub:c * rows + (i + 1) * sub, :]) for i in range(rows // sub)]

    for j in range(blocks):
        qs = qt_ref[j]
        for c in range(vct_ref.shape[1]):
            pending.append((j, logits(qs, kc_ref, c, vct_ref.shape[3]), vct_ref[0, c]))
            drain(GQA_AHEAD)
        for c in range(chunks):
            pending.append((j, logits(qs, k_ref, c, tk), vt_ref[0, c]))
            drain(GQA_AHEAD)
    drain(0)
    for j in range(blocks):
        _, acc = states[j]
        o_ref[0, j * tq:(j + 1) * tq, :] = _unstack_group_t(_normalise_t(acc, HEAD_DIM), tq).astype(BF16)


def _gqa_attention(qb, kbd, vbt, ckbd, cvbt):
    batch, seq, _ = qb.shape
    ctx_len = ckbd.shape[1]
    tq, tk, blocks = GQA_Q_ROWS, vbt.shape[3], GQA_Q_BLOCKS
    kspec = lambda n: pl.BlockSpec((1, n, LANES), lambda b, g, i: (b, 0, g))
    vspec = lambda a: pl.BlockSpec((1, a.shape[1], HEAD_DIM, a.shape[3]), lambda b, g, i: (b, 0, g, 0))
    qspec = pl.BlockSpec((1, blocks * tq, 2 * LANES), lambda b, g, i: (b, i, g))
    return pl.pallas_call(
        functools.partial(_gqa_kernel, chunks=seq // tk, tq=tq, tk=tk, blocks=blocks),
        grid=(batch, GQA_KV_HEADS, seq // (blocks * tq)),
        in_specs=[qspec, kspec(seq), vspec(vbt), kspec(ctx_len), vspec(cvbt)],
        out_specs=qspec,
        out_shape=jax.ShapeDtypeStruct((batch, seq, GQA_WIDTH), BF16),
        scratch_shapes=[pltpu.VMEM((blocks, LANES, (GQA_Q_HEADS // GQA_KV_HEADS) * tq), BF16)],
        compiler_params=_params(("arbitrary", "arbitrary", "arbitrary")),
        name="gqa",
    )(qb, kbd, vbt, ckbd, cvbt)


def _ctx_kernel(qa_ref, ka_ref, vat_ref, qb_ref, kb_ref, vbt_ref, oa_ref, ob_ref):
    n = qa_ref.shape[1]
    width = QUAD * HEAD_DIM
    for g in range(NA_HEADS // QUAD):
        sl = slice(g * width, (g + 1) * width)
        vt = jnp.concatenate([vat_ref[0, j, sl, :] for j in range(vat_ref.shape[1])], axis=1)
        o = _softmax_pv_t([_dot_t(ka_ref[0, :, sl], _stack_quad(qa_ref[0, :, sl]))], [vt])
        oa_ref[0, :, sl] = _unstack_quad_t(o, n).astype(BF16)
    for g in range(GQA_KV_HEADS):
        sl = slice(g * 2 * LANES, (g + 1) * 2 * LANES)
        s = _scores_t(_stack_group(qb_ref[0, :, sl]).T, kb_ref[0, :, g * LANES:(g + 1) * LANES])
        vt = jnp.concatenate([vbt_ref[0, j, g * HEAD_DIM:(g + 1) * HEAD_DIM, :] for j in range(vbt_ref.shape[1])],
                             axis=1)
        o = _softmax_pv_t([s], [vt])
        ob_ref[0, :, sl] = _unstack_group_t(o, n).astype(BF16)


def _context_attention(cqa, cka, cvat, cqb, ckbd, cvbt):
    batch, n, _ = cqa.shape
    spec = lambda a: pl.BlockSpec((1,) + a.shape[1:], lambda b: (b,) + (0,) * (a.ndim - 1))
    args = (cqa, cka, cvat, cqb, ckbd, cvbt)
    return pl.pallas_call(
        _ctx_kernel,
        grid=(batch,),
        in_specs=[spec(a) for a in args],
        out_specs=[spec(cqa), spec(cqb)],
        out_shape=[jax.ShapeDtypeStruct(cqa.shape, BF16), jax.ShapeDtypeStruct(cqb.shape, BF16)],
        compiler_params=_params(("arbitrary",)),
        name="ctx",
    )(*args)


def _merge_kernel(a_ref, b_ref, x_ref, shift_ref, scale_ref, gate_ref, g_ref, wzg_ref,
                  woa_ref, wob_ref, wout_ref, o_ref):
    x = x_ref[0]
    h = _modulated_norm(x, g_ref, scale_ref, shift_ref)
    z_a, z_b, g_a, g_b = (_dot(h, wzg_ref[:, GATE_OFFSETS[i]:GATE_OFFSETS[i + 1]]) for i in range(4))
    o_a = _dot((a_ref[0].astype(F32) * (z_a * jax.nn.sigmoid(z_a))).astype(BF16), woa_ref[...])
    o_b = _dot((b_ref[0].astype(F32) * (z_b * jax.nn.sigmoid(z_b))).astype(BF16), wob_ref[...])
    merged = jax.nn.sigmoid(g_a) * o_a + jax.nn.sigmoid(g_b) * o_b
    o_ref[0] = x + gate_ref[0] * _dot(merged.astype(BF16), wout_ref[...])


def _merge(a_att, b_att, x, shift, scale, gate, norm_g, w_gates, w_o_a, w_o_b, w_out, layer, rows):
    groups, total, _ = x.shape
    act = lambda w: pl.BlockSpec((1, rows, w), lambda g, i: (g, i, 0))
    mod = pl.BlockSpec((1, 1, D_MODEL), lambda g, i: (g, 0, 0))
    const = lambda a: pl.BlockSpec((None,) + a.shape[1:], lambda g, i: (layer, 0, 0))
    return pl.pallas_call(
        _merge_kernel,
        grid=(groups, total // rows),
        in_specs=[act(NA_WIDTH), act(GQA_WIDTH), act(D_MODEL), mod, mod, mod,
                  pl.BlockSpec((1, D_MODEL), lambda g, i: (0, 0)),
                  const(w_gates), const(w_o_a), const(w_o_b), const(w_out)],
        out_specs=act(D_MODEL),
        out_shape=jax.ShapeDtypeStruct(x.shape, F32),
        compiler_params=_params(("arbitrary", "arbitrary")),
        name="merge",
    )(a_att, b_att, x, shift, scale, gate, norm_g, w_gates, w_o_a, w_o_b, w_out)


def _rope_tables(seq):
    t = jnp.arange(seq, dtype=jnp.int32)
    inv = 1.0 / (ROPE_THETA ** (jnp.arange(ROT_HALF, dtype=F32) / ROT_HALF))
    ang_r = (t // GRID_W).astype(F32)[:, None] * inv[None, :]
    ang_c = (t % GRID_W).astype(F32)[:, None] * inv[None, :]
    cos = jnp.concatenate([jnp.cos(ang_r)] * 2 + [jnp.cos(ang_c)] * 2, axis=1)
    sin = jnp.concatenate([-jnp.sin(ang_r), jnp.sin(ang_r), -jnp.sin(ang_c), jnp.sin(ang_c)], axis=1)
    reps = LANES // HEAD_DIM
    return jnp.tile(cos, (1, reps)), jnp.tile(sin, (1, reps))


def _group_sum_matrix():
    idx = np.arange(MXU_COLS) // HEAD_DIM
    return jnp.asarray(idx[:, None] == idx[None, :], dtype=BF16)


def kernel(x, c, ctx, c_ctx, w_ada, b_ada, norm_g, w_in, q_norm_a, k_norm_a, q_norm_b, k_norm_b,
           rpb, w_o_a, w_o_b, w_out):
    batch, seq, _ = x.shape
    ctx_len = ctx.shape[1]
    depth = w_ada.shape[0]
    rows = seq // GRID_W
    assert seq % GRID_W == 0 and rows >= WIN_R and seq % PROJ_ROWS == 0 and seq % GQA_Q_ROWS == 0

    pad = -(batch + 1) % 8
    c_rows = jnp.concatenate([c, c_ctx[None, :], jnp.zeros((pad, D_MODEL), F32)], axis=0)
    mod = _modulation(c_rows, w_ada, b_ada)

    cos_x, sin_x = _rope_tables(seq)
    ctx_rows = batch * ctx_len
    ctx = ctx.reshape(1, ctx_rows, D_MODEL)
    cos_c = jnp.ones((ctx_rows, LANES), F32)
    sin_c = jnp.zeros((ctx_rows, LANES), F32)
    flat = lambda a: a.reshape((1, ctx_rows) + a.shape[2:])
    per_batch_ctx = lambda a: a.reshape((batch, a.shape[1] // batch) + a.shape[2:])
    gsum = _group_sum_matrix()
    tables = _bias_tables(rpb)
    w_in, w_o_a, w_o_b, w_out = (w.astype(BF16) for w in (w_in, w_o_a, w_o_b, w_out))
    w_gates = w_in[:, :, SEG_OFFSETS[-1]:]

    for l in range(depth):
        update_ctx = l < depth - 1
        shift, scale, gate = (mod[l, :, i * D_MODEL:(i + 1) * D_MODEL] for i in range(3))
        per_batch = lambda m: m[:batch, None, :]
        for_ctx = lambda m: m[batch][None, None, :]
        tile = lambda g, n: jnp.tile(g, n)
        gains = jnp.stack([tile(k_norm_a[l], NA_HEADS), tile(k_norm_b[l], NA_HEADS),
                           tile(q_norm_a[l], NA_HEADS) * (QK_SCALE * LOG2E),
                           tile(q_norm_b[l], GQA_Q_HEADS) * (QK_SCALE * LOG2E)])
        g_row = norm_g[l][None, :]

        ka, vat, kbd, vbt, qa, qb = _project(
            x, per_batch(shift), per_batch(scale), g_row, w_in, l, gsum, gains, cos_x, sin_x,
            len(SEG_WIDTHS), PROJ_ROWS, PROJ_ROWS)
        nseg_c = len(SEG_WIDTHS) if update_ctx else KV_SEGS
        pc = _project(ctx, for_ctx(shift), for_ctx(scale), g_row, w_in, l, gsum, gains, cos_c, sin_c,
                      nseg_c, ctx_len, ctx_len)
        pc = [per_batch_ctx(a) for a in pc]
        cka, cvat, ckbd, cvbt = pc[:KV_SEGS]

        a_att = _neighbourhood_attention(qa, ka, vat, cka, cvat, tables, l)
        b_att = _gqa_attention(qb, kbd, vbt, ckbd, cvbt)
        x_new = _merge(a_att, b_att, x, per_batch(shift), per_batch(scale), per_batch(gate), g_row,
                       w_gates, w_o_a, w_o_b, w_out, l, MERGE_ROWS)

        if update_ctx:
            cqa, cqb = pc[KV_SEGS:]
            c_a, c_b = _context_attention(cqa, cka, cvat, cqb, ckbd, cvbt)
            ctx = _merge(flat(c_a), flat(c_b), ctx, for_ctx(shift), for_ctx(scale), for_ctx(gate), g_row,
                         w_gates, w_o_a, w_o_b, w_out, l, ctx_len)
        x = x_new
    return x
```

```python
import functools

import numpy as np
import jax
import jax.numpy as jnp
from jax import lax
from jax.experimental import pallas as pl
from jax.experimental.pallas import tpu as pltpu

D_MODEL = 1024
GRID_W = 64
HEAD_DIM = 64
NA_HEADS = 8
NA_WIDTH = NA_HEADS * HEAD_DIM
WIN_R = 8
WIN_C = 16
GQA_Q_HEADS = 8
GQA_KV_HEADS = 2
GQA_WIDTH = GQA_Q_HEADS * HEAD_DIM
GQA_KV_WIDTH = GQA_KV_HEADS * HEAD_DIM
ROPE_THETA = 10000.0
ROT_AXIS = HEAD_DIM // 2
ROT_HALF = ROT_AXIS // 2
EPS = 1e-6
QK_SCALE = HEAD_DIM ** -0.5
LOG2E = 1.4426950408889634

LANES = 128
MXU_COLS = 256
MASKED = -1e30
VMEM_LIMIT = 56 * 1024 * 1024

PROJ_ROWS = 1024
MERGE_ROWS = 1024
GQA_Q_ROWS = 256
GQA_Q_BLOCKS = 4
GQA_K_ROWS = 256
GQA_SUB_ROWS = 128
GQA_AHEAD = 1
QUAD = 4
ONES_ROWS = 16
NA_SUB_ROWS = 128
NA_AHEAD = 6
NA_UNROLL = 16

F32 = jnp.float32
BF16 = jnp.bfloat16

SEG_KA, SEG_VA, SEG_KB, SEG_VB, SEG_QA, SEG_QB = range(6)
SEG_WIDTHS = (NA_WIDTH, NA_WIDTH, GQA_KV_WIDTH, GQA_KV_WIDTH, NA_WIDTH, GQA_WIDTH)
SEG_OFFSETS = tuple(int(v) for v in np.cumsum((0,) + SEG_WIDTHS))
GATE_WIDTHS = (NA_WIDTH, GQA_WIDTH, D_MODEL, D_MODEL)
GATE_OFFSETS = tuple(int(v) for v in np.cumsum((SEG_OFFSETS[-1],) + GATE_WIDTHS))
SEG_OUT_WIDTHS = tuple(2 * w if s == SEG_KB else w for s, w in enumerate(SEG_WIDTHS))
KV_SEGS = 4
SEG_ISSUE_RANK = {SEG_QB: 0, SEG_KB: 1, SEG_QA: 2, SEG_KA: 3, SEG_VB: 4, SEG_VA: 5}
SEG_GAIN_ROW = {SEG_KA: 0, SEG_KB: 1, SEG_QA: 2, SEG_QB: 3}
SEG_ROPE = (SEG_KB, SEG_QB)
SEG_T_SLAB = {SEG_VA: LANES, SEG_VB: GQA_K_ROWS}


def _dot(a, b):
    return jnp.dot(a, b, preferred_element_type=F32)


def _dot_t(a, b):
    return lax.dot_general(a, b, (((1,), (1,)), ((), ())), preferred_element_type=F32)


def _params(semantics):
    return pltpu.CompilerParams(dimension_semantics=semantics, vmem_limit_bytes=VMEM_LIMIT)


def _mod_kernel(c_ref, w_ref, b_ref, o_ref):
    c = c_ref[...]
    a = c * jax.nn.sigmoid(c)
    o_ref[0] = jnp.dot(a, w_ref[0], preferred_element_type=F32,
                       precision=lax.Precision.HIGHEST) + b_ref[0]


def _modulation(c_rows, w_ada, b_ada):
    depth = w_ada.shape[0]
    rows = c_rows.shape[0]
    ncol = w_ada.shape[2] // D_MODEL
    return pl.pallas_call(
        _mod_kernel,
        grid=(depth, ncol),
        in_specs=[
            pl.BlockSpec((rows, D_MODEL), lambda l, j: (0, 0)),
            pl.BlockSpec((1, D_MODEL, D_MODEL), lambda l, j: (l, 0, j)),
            pl.BlockSpec((1, 1, D_MODEL), lambda l, j: (l, 0, j)),
        ],
        out_specs=pl.BlockSpec((1, rows, D_MODEL), lambda l, j: (l, 0, j)),
        out_shape=jax.ShapeDtypeStruct((depth, rows, ncol * D_MODEL), F32),
        compiler_params=_params(("arbitrary", "arbitrary")),
        name="mod",
    )(c_rows, w_ada, b_ada.reshape(depth, 1, -1))


def _head_norm(p, gsum, gain):
    sq = (p * p).astype(BF16)
    cols = min(MXU_COLS, p.shape[1])
    parts = []
    for j in range(p.shape[1] // cols):
        sl = slice(j * cols, (j + 1) * cols)
        ss = _dot(sq[:, sl], gsum[:cols, :cols])
        parts.append(p[:, sl] * lax.rsqrt(ss * (1.0 / HEAD_DIM) + EPS))
    y = parts[0] if len(parts) == 1 else jnp.concatenate(parts, axis=1)
    return y * gain


def _rope(y, cos, sin):
    width = y.shape[1]
    lane = lax.broadcasted_iota(jnp.int32, y.shape, 1)
    ahead = pltpu.roll(y, width - ROT_HALF, axis=1)
    behind = pltpu.roll(y, ROT_HALF, axis=1)
    partner = jnp.where((lane % ROT_AXIS) < ROT_HALF, ahead, behind)
    reps = width // LANES
    cos = cos if reps == 1 else jnp.concatenate([cos] * reps, axis=1)
    sin = sin if reps == 1 else jnp.concatenate([sin] * reps, axis=1)
    return y * cos + partner * sin


def _duplicate_heads(y):
    lo = lax.broadcasted_iota(jnp.int32, y.shape, 1) < HEAD_DIM
    swapped = pltpu.roll(y, HEAD_DIM, axis=1)
    return jnp.concatenate([jnp.where(lo, y, swapped), jnp.where(lo, swapped, y)], axis=1)


def _modulated_norm(x, g_ref, scale_ref, shift_ref):
    ms = jnp.mean(x * x, axis=-1, keepdims=True)
    gmod = g_ref[...] * (1.0 + scale_ref[0])
    return (x * lax.rsqrt(ms + EPS) * gmod + shift_ref[0]).astype(BF16)


def _proj_kernel(x_ref, shift_ref, scale_ref, g_ref, w_ref, gsum_ref, gain_ref, cos_ref, sin_ref,
                 *out_refs, nseg, max_slab):
    h = _modulated_norm(x_ref[0], g_ref, scale_ref, shift_ref)
    gsum = gsum_ref[...]
    for seg in sorted(range(nseg), key=SEG_ISSUE_RANK.__getitem__):
        width = SEG_WIDTHS[seg]
        p = _dot(h, w_ref[:, SEG_OFFSETS[seg]:SEG_OFFSETS[seg + 1]])
        if seg in SEG_GAIN_ROW:
            row = SEG_GAIN_ROW[seg]
            p = _head_norm(p, gsum, gain_ref[row:row + 1, :width])
        if seg in SEG_ROPE:
            p = _rope(p, cos_ref[...], sin_ref[...])
        if seg == SEG_KB:
            p = _duplicate_heads(p)
        if seg in SEG_T_SLAB:
            pt = p.T.astype(BF16)
            slab = min(SEG_T_SLAB[seg], max_slab)
            for j in range(pt.shape[1] // slab):
                out_refs[seg][0, j] = pt[:, j * slab:(j + 1) * slab]
        else:
            out_refs[seg][0] = p.astype(BF16)


def _project(x, shift, scale, norm_g, w_all, layer, gsum, gains, cos, sin, nseg, rows, max_slab):
    groups, total, _ = x.shape
    ncols = SEG_OFFSETS[nseg]
    const = lambda g, i: (0, 0)

    def out_layout(s):
        width = SEG_OUT_WIDTHS[s]
        if s not in SEG_T_SLAB:
            return (groups, total, width), pl.BlockSpec((1, rows, width), lambda g, i: (g, i, 0))
        slab = min(SEG_T_SLAB[s], max_slab)
        return ((groups, total // slab, width, slab),
                pl.BlockSpec((1, rows // slab, width, slab), lambda g, i: (g, i, 0, 0)))

    layouts = [out_layout(s) for s in range(nseg)]
    return pl.pallas_call(
        functools.partial(_proj_kernel, nseg=nseg, max_slab=max_slab),
        grid=(groups, total // rows),
        in_specs=[
            pl.BlockSpec((1, rows, D_MODEL), lambda g, i: (g, i, 0)),
            pl.BlockSpec((1, 1, D_MODEL), lambda g, i: (g, 0, 0)),
            pl.BlockSpec((1, 1, D_MODEL), lambda g, i: (g, 0, 0)),
            pl.BlockSpec((1, D_MODEL), const),
            pl.BlockSpec((None, D_MODEL, ncols), lambda g, i: (layer, 0, 0), pipeline_mode=pl.Buffered(1)),
            pl.BlockSpec((MXU_COLS, MXU_COLS), const),
            pl.BlockSpec(gains.shape, const),
            pl.BlockSpec((rows, LANES), lambda g, i: (i, 0)),
            pl.BlockSpec((rows, LANES), lambda g, i: (i, 0)),
        ],
        out_specs=[spec for _, spec in layouts],
        out_shape=[jax.ShapeDtypeStruct(shape, BF16) for shape, _ in layouts],
        compiler_params=_params(("arbitrary", "arbitrary")),
        name="proj",
    )(x, shift, scale, norm_g, w_all, gsum, gains, cos, sin)


def _stack_heads(pair):
    lo = lax.broadcasted_iota(jnp.int32, pair.shape, 1) < HEAD_DIM
    zero = jnp.zeros_like(pair)
    return jnp.concatenate([jnp.where(lo, pair, zero), jnp.where(lo, zero, pair)], axis=0)


def _stack_quad(q):
    head = lax.broadcasted_iota(jnp.int32, q.shape, 1) // HEAD_DIM
    zero = jnp.zeros_like(q)
    return jnp.concatenate([jnp.where(head == h, q, zero) for h in range(QUAD)], axis=0)


def _unstack_quad_t(o, rows):
    ot = o.T
    head = lax.broadcasted_iota(jnp.int32, (rows, QUAD * HEAD_DIM), 1) // HEAD_DIM
    out = ot[:rows]
    for h in range(1, QUAD):
        out = jnp.where(head == h, ot[h * rows:(h + 1) * rows], out)
    return out


def _pv_t(vt, s, m):
    lhs = jnp.concatenate([vt, jnp.ones((ONES_ROWS, vt.shape[1]), vt.dtype)], axis=0)
    return _dot(lhs, jnp.exp2((s - m).astype(BF16)))


def _normalise_t(acc, dims):
    return acc[:dims] * (1.0 / acc[dims:dims + 1])


def _softmax_pv_t(logits, values_t):
    m = functools.reduce(jnp.maximum, [jnp.max(s, axis=0, keepdims=True) for s in logits])
    acc = functools.reduce(jnp.add, [_pv_t(vt, s, m) for vt, s in zip(values_t, logits)])
    return _normalise_t(acc, values_t[0].shape[0])


def _na_window(r, rows, slabs):
    start = jnp.clip(r - WIN_R // 2, 0, rows - WIN_R)
    slab0 = jnp.minimum(start // 2, rows // 2 - slabs)
    first = 2 * slab0
    blocks = []
    for w in range(2 * slabs):
        key_row = first + w
        valid = (key_row >= start) & (key_row < start + WIN_R)
        blocks.append(jnp.where(valid, key_row - r + (WIN_R - 1), 2 * WIN_R - 1))
    return slab0, blocks


def _na_kernel(q_ref, k_ref, vt_ref, kc_ref, vct_ref, t_ref, o_ref, qt_ref, tab_ref, *, rows, unroll):
    n_off = t_ref.shape[1]
    for off in range(n_off - 1):
        tab_ref[off] = jnp.concatenate([t_ref[h, off] for h in range(QUAD)], axis=1)
    tab_ref[n_off - 1] = jnp.full(tab_ref.shape[1:], MASKED * LOG2E, F32)
    kc = kc_ref[0]
    vct = jnp.concatenate([vct_ref[0, j] for j in range(vct_ref.shape[1])], axis=1)
    assert unroll % 2 == 0 and rows % 2 == 0 and WIN_R % 2 == 0

    def logits(r, slot):
        slabs = WIN_R // 2 + slot % 2
        slab0, blocks = _na_window(r, rows, slabs)
        qt_ref[slot] = _stack_quad(q_ref[0, pl.ds(pl.multiple_of(r * GRID_W, GRID_W), GRID_W), :]).T
        qs_t = qt_ref[slot]
        kw = k_ref[0, pl.ds(pl.multiple_of(slab0 * LANES, LANES), slabs * LANES), :]
        bias = jnp.concatenate([tab_ref[b] for b in blocks], axis=0)
        return _dot(kw, qs_t) + bias, _dot(kc, qs_t), slab0

    def finish(r, s_win, s_ctx, slab0):
        slabs = s_win.shape[0] // LANES
        vt = vt_ref[0, pl.ds(slab0, slabs)]
        vtw = jnp.concatenate([vt[j] for j in range(slabs)], axis=1)
        sub = NA_SUB_ROWS
        parts = [s[i:i + sub] for s in (s_ctx, s_win) for i in range(0, s.shape[0], sub)]
        _, acc = _attend_t(parts, jnp.concatenate([vct, vtw], axis=1), None)
        o = _normalise_t(acc, vct.shape[0])
        o_ref[0, pl.ds(pl.multiple_of(r * GRID_W, GRID_W), GRID_W), :] = _unstack_quad_t(o, GRID_W).astype(BF16)

    def body(i, carry):
        pending = []
        for e in range(unroll):
            pending.append((i * unroll + e,) + logits(i * unroll + e, e))
            if len(pending) > NA_AHEAD:
                finish(*pending.pop(0))
        for args in pending:
            finish(*args)
        return carry

    lax.fori_loop(0, rows // unroll, body, 0)


def _neighbourhood_attention(qa, ka, vat, cka, cvat, tables, layer):
    batch, seq, _ = qa.shape
    ctx_len = cka.shape[1]
    rows = seq // GRID_W
    width = QUAD * HEAD_DIM
    blk = lambda n: pl.BlockSpec((1, n, width), lambda b, j: (b, 0, j))
    vblk = lambda a: pl.BlockSpec((1, a.shape[1], width, LANES), lambda b, j: (b, 0, j, 0))
    return pl.pallas_call(
        functools.partial(_na_kernel, rows=rows, unroll=NA_UNROLL),
        grid=(batch, NA_HEADS // QUAD),
        in_specs=[blk(seq), blk(seq), vblk(vat), blk(ctx_len), vblk(cvat),
                  pl.BlockSpec((None, None) + tables.shape[2:], lambda b, j: (layer, j, 0, 0, 0, 0))],
        out_specs=blk(seq),
        out_shape=jax.ShapeDtypeStruct((batch, seq, NA_WIDTH), BF16),
        scratch_shapes=[pltpu.VMEM((NA_UNROLL, width, QUAD * GRID_W), BF16),
                        pltpu.VMEM((tables.shape[3], GRID_W, QUAD * GRID_W), F32)],
        compiler_params=_params(("arbitrary", "arbitrary")),
        name="na",
    )(qa, ka, vat, cka, cvat, tables)


def _bias_kernel(rpb_ref, onehot_ref, mask_ref, o_ref):
    o_ref[0] = (jnp.dot(rpb_ref[0], onehot_ref[...], preferred_element_type=F32,
                        precision=lax.Precision.HIGHEST) + mask_ref[...]) * LOG2E


def _bias_tables(rpb):
    depth = rpb.shape[0]
    n_row_off, n_col_off = 2 * WIN_R - 1, 2 * WIN_C - 1
    cols = np.arange(GRID_W)
    cstart = np.clip(cols - WIN_C // 2, 0, GRID_W - WIN_C)
    kcol = np.arange(GRID_W)
    valid = (kcol[None, :] >= cstart[:, None]) & (kcol[None, :] < cstart[:, None] + WIN_C)
    col_off = kcol[None, :] - cols[:, None] + (WIN_C - 1)
    pad_r, pad_c = -n_row_off % 8, -n_col_off % 8
    assert pad_r >= 1
    onehot = (np.arange(n_col_off + pad_c)[:, None, None] == col_off.T[None]) & valid.T[None]
    onehot = jnp.asarray(onehot.reshape(n_col_off + pad_c, GRID_W * GRID_W), dtype=F32)
    mask = jnp.asarray(np.where(valid.T, 0.0, MASKED).reshape(1, GRID_W * GRID_W), dtype=F32)
    heads = depth * NA_HEADS
    rpb_p = jnp.pad(rpb.reshape(heads, n_row_off, n_col_off), ((0, 0), (0, pad_r), (0, pad_c)))
    toep = pl.pallas_call(
        _bias_kernel,
        grid=(heads,),
        in_specs=[pl.BlockSpec((1,) + rpb_p.shape[1:], lambda h: (h, 0, 0)),
                  pl.BlockSpec(onehot.shape, lambda h: (0, 0)),
                  pl.BlockSpec(mask.shape, lambda h: (0, 0))],
        out_specs=pl.BlockSpec((1, n_row_off + pad_r, GRID_W * GRID_W), lambda h: (h, 0, 0)),
        out_shape=jax.ShapeDtypeStruct((heads, n_row_off + pad_r, GRID_W * GRID_W), F32),
        compiler_params=_params(("arbitrary",)),
        name="bias",
    )(rpb_p, onehot, mask)
    return toep.reshape(depth, NA_HEADS // QUAD, QUAD, n_row_off + pad_r, GRID_W, GRID_W)


def _stack_group(q):
    return jnp.concatenate([_stack_heads(q[:, :LANES]), _stack_heads(q[:, LANES:])], axis=0)


def _unstack_group_t(o, rows):
    pairs = [jnp.concatenate([o[:, (2 * j) * rows:(2 * j + 1) * rows],
                              o[:, (2 * j + 1) * rows:(2 * j + 2) * rows]], axis=0).T for j in range(2)]
    return jnp.concatenate(pairs, axis=1)


def _scores_t(qs_t, k):
    return _dot(k, qs_t)


def _attend_t(parts, vt, state):
    m_old = None if state is None else state[0]
    m, ps, ms = m_old, [], []
    for s in parts:
        m_part = jnp.max(s, axis=0, keepdims=True)
        m = m_part if m is None else jnp.maximum(m, m_part)
        ps.append(jnp.exp2((s - m).astype(BF16)))
        ms.append(m)
    ps = [p * jnp.exp2(m_i - m).astype(BF16) for p, m_i in zip(ps[:-1], ms[:-1])] + ps[-1:]
    lhs = jnp.concatenate([vt, jnp.ones((ONES_ROWS, vt.shape[1]), vt.dtype)], axis=0)
    pv = _dot(lhs, ps[0] if len(ps) == 1 else jnp.concatenate(ps, axis=0))
    if state is None:
        return m, pv
    return m, jnp.exp2(m_old - m) * state[1] + pv


def _gqa_kernel(q_ref, k_ref, vt_ref, kc_ref, vct_ref, o_ref, qt_ref, *, chunks, tq, tk, blocks):
    for j in range(blocks):
        qt_ref[j] = _stack_group(q_ref[0, j * tq:(j + 1) * tq, :]).T
    pending = []
    states = [None] * blocks

    def drain(limit):
        while len(pending) > limit:
            j, parts, vt = pending.pop(0)
            states[j] = _attend_t(parts, vt, states[j])

    def logits(qs, keys, c, rows):
        sub = min(GQA_SUB_ROWS, rows)
        return [_scores_t(qs, keys[0, c * rows + i * sub:c * rows + (i + 1) * sub, :]) for i in range(rows // sub)]

    for j in range(blocks):
        qs = qt_ref[j]
        for c in range(vct_ref.shape[1]):
            pending.append((j, logits(qs, kc_ref, c, vct_ref.shape[3]), vct_ref[0, c]))
            drain(GQA_AHEAD)
        for c in range(chunks):
            pending.append((j, logits(qs, k_ref, c, tk), vt_ref[0, c]))
            drain(GQA_AHEAD)
    drain(0)
    for j in range(blocks):
        _, acc = states[j]
        o_ref[0, j * tq:(j + 1) * tq, :] = _unstack_group_t(_normalise_t(acc, HEAD_DIM), tq).astype(BF16)


def _gqa_attention(qb, kbd, vbt, ckbd, cvbt):
    batch, seq, _ = qb.shape
    ctx_len = ckbd.shape[1]
    tq, tk, blocks = GQA_Q_ROWS, vbt.shape[3], GQA_Q_BLOCKS
    kspec = lambda n: pl.BlockSpec((1, n, LANES), lambda b, g, i: (b, 0, g))
    vspec = lambda a: pl.BlockSpec((1, a.shape[1], HEAD_DIM, a.shape[3]), lambda b, g, i: (b, 0, g, 0))
    qspec = pl.BlockSpec((1, blocks * tq, 2 * LANES), lambda b, g, i: (b, i, g))
    return pl.pallas_call(
        functools.partial(_gqa_kernel, chunks=seq // tk, tq=tq, tk=tk, blocks=blocks),
        grid=(batch, GQA_KV_HEADS, seq // (blocks * tq)),
        in_specs=[qspec, kspec(seq), vspec(vbt), kspec(ctx_len), vspec(cvbt)],
        out_specs=qspec,
        out_shape=jax.ShapeDtypeStruct((batch, seq, GQA_WIDTH), BF16),
        scratch_shapes=[pltpu.VMEM((blocks, LANES, (GQA_Q_HEADS // GQA_KV_HEADS) * tq), BF16)],
        compiler_params=_params(("arbitrary", "arbitrary", "arbitrary")),
        name="gqa",
    )(qb, kbd, vbt, ckbd, cvbt)


def _ctx_kernel(qa_ref, ka_ref, vat_ref, qb_ref, kb_ref, vbt_ref, oa_ref, ob_ref):
    n = qa_ref.shape[1]
    width = QUAD * HEAD_DIM
    for g in range(NA_HEADS // QUAD):
        sl = slice(g * width, (g + 1) * width)
        vt = jnp.concatenate([vat_ref[0, j, sl, :] for j in range(vat_ref.shape[1])], axis=1)
        o = _softmax_pv_t([_dot_t(ka_ref[0, :, sl], _stack_quad(qa_ref[0, :, sl]))], [vt])
        oa_ref[0, :, sl] = _unstack_quad_t(o, n).astype(BF16)
    for g in range(GQA_KV_HEADS):
        sl = slice(g * 2 * LANES, (g + 1) * 2 * LANES)
        s = _scores_t(_stack_group(qb_ref[0, :, sl]).T, kb_ref[0, :, g * LANES:(g + 1) * LANES])
        vt = jnp.concatenate([vbt_ref[0, j, g * HEAD_DIM:(g + 1) * HEAD_DIM, :] for j in range(vbt_ref.shape[1])],
                             axis=1)
        o = _softmax_pv_t([s], [vt])
        ob_ref[0, :, sl] = _unstack_group_t(o, n).astype(BF16)


def _context_attention(cqa, cka, cvat, cqb, ckbd, cvbt):
    batch, n, _ = cqa.shape
    spec = lambda a: pl.BlockSpec((1,) + a.shape[1:], lambda b: (b,) + (0,) * (a.ndim - 1))
    args = (cqa, cka, cvat, cqb, ckbd, cvbt)
    return pl.pallas_call(
        _ctx_kernel,
        grid=(batch,),
        in_specs=[spec(a) for a in args],
        out_specs=[spec(cqa), spec(cqb)],
        out_shape=[jax.ShapeDtypeStruct(cqa.shape, BF16), jax.ShapeDtypeStruct(cqb.shape, BF16)],
        compiler_params=_params(("arbitrary",)),
        name="ctx",
    )(*args)


def _merge_kernel(a_ref, b_ref, x_ref, shift_ref, scale_ref, gate_ref, g_ref, win_ref,
                  woa_ref, wob_ref, wout_ref, o_ref):
    x = x_ref[0]
    h = _modulated_norm(x, g_ref, scale_ref, shift_ref)
    z_a, z_b, g_a, g_b = (_dot(h, win_ref[:, GATE_OFFSETS[i]:GATE_OFFSETS[i + 1]]) for i in range(4))
    o_a = _dot((a_ref[0].astype(F32) * (z_a * jax.nn.sigmoid(z_a))).astype(BF16), woa_ref[...])
    o_b = _dot((b_ref[0].astype(F32) * (z_b * jax.nn.sigmoid(z_b))).astype(BF16), wob_ref[...])
    merged = jax.nn.sigmoid(g_a) * o_a + jax.nn.sigmoid(g_b) * o_b
    o_ref[0] = x + gate_ref[0] * _dot(merged.astype(BF16), wout_ref[...])


def _merge(a_att, b_att, x, shift, scale, gate, norm_g, w_in, w_o_a, w_o_b, w_out, layer, rows):
    groups, total, _ = x.shape
    act = lambda w: pl.BlockSpec((1, rows, w), lambda g, i: (g, i, 0))
    mod = pl.BlockSpec((1, 1, D_MODEL), lambda g, i: (g, 0, 0))
    const = lambda a: pl.BlockSpec((None,) + a.shape[1:], lambda g, i: (layer, 0, 0),
                                   pipeline_mode=pl.Buffered(1))
    return pl.pallas_call(
        _merge_kernel,
        grid=(groups, total // rows),
        in_specs=[act(NA_WIDTH), act(GQA_WIDTH), act(D_MODEL), mod, mod, mod,
                  pl.BlockSpec((1, D_MODEL), lambda g, i: (0, 0)),
                  const(w_in), const(w_o_a), const(w_o_b), const(w_out)],
        out_specs=act(D_MODEL),
        out_shape=jax.ShapeDtypeStruct(x.shape, F32),
        compiler_params=_params(("arbitrary", "arbitrary")),
        name="merge",
    )(a_att, b_att, x, shift, scale, gate, norm_g, w_in, w_o_a, w_o_b, w_out)


def _rope_tables(seq):
    t = jnp.arange(seq, dtype=jnp.int32)
    inv = 1.0 / (ROPE_THETA ** (jnp.arange(ROT_HALF, dtype=F32) / ROT_HALF))
    ang_r = (t // GRID_W).astype(F32)[:, None] * inv[None, :]
    ang_c = (t % GRID_W).astype(F32)[:, None] * inv[None, :]
    cos = jnp.concatenate([jnp.cos(ang_r)] * 2 + [jnp.cos(ang_c)] * 2, axis=1)
    sin = jnp.concatenate([-jnp.sin(ang_r), jnp.sin(ang_r), -jnp.sin(ang_c), jnp.sin(ang_c)], axis=1)
    reps = LANES // HEAD_DIM
    return jnp.tile(cos, (1, reps)), jnp.tile(sin, (1, reps))


def _group_sum_matrix():
    idx = np.arange(MXU_COLS) // HEAD_DIM
    return jnp.asarray(idx[:, None] == idx[None, :], dtype=BF16)


def kernel(x, c, ctx, c_ctx, w_ada, b_ada, norm_g, w_in, q_norm_a, k_norm_a, q_norm_b, k_norm_b,
           rpb, w_o_a, w_o_b, w_out):
    batch, seq, _ = x.shape
    ctx_len = ctx.shape[1]
    depth = w_ada.shape[0]
    rows = seq // GRID_W
    assert seq % GRID_W == 0 and rows >= WIN_R and seq % PROJ_ROWS == 0 and seq % GQA_Q_ROWS == 0

    pad = -(batch + 1) % 8
    c_rows = jnp.concatenate([c, c_ctx[None, :], jnp.zeros((pad, D_MODEL), F32)], axis=0)
    mod = _modulation(c_rows, w_ada, b_ada)

    cos_x, sin_x = _rope_tables(seq)
    ctx_rows = batch * ctx_len
    ctx = ctx.reshape(1, ctx_rows, D_MODEL)
    cos_c = jnp.ones((ctx_rows, LANES), F32)
    sin_c = jnp.zeros((ctx_rows, LANES), F32)
    flat = lambda a: a.reshape((1, ctx_rows) + a.shape[2:])
    per_batch_ctx = lambda a: a.reshape((batch, a.shape[1] // batch) + a.shape[2:])
    gsum = _group_sum_matrix()
    tables = _bias_tables(rpb)
    w_in, w_o_a, w_o_b, w_out = (w.astype(BF16) for w in (w_in, w_o_a, w_o_b, w_out))

    for l in range(depth):
        update_ctx = l < depth - 1
        shift, scale, gate = (mod[l, :, i * D_MODEL:(i + 1) * D_MODEL] for i in range(3))
        per_batch = lambda m: m[:batch, None, :]
        for_ctx = lambda m: m[batch][None, None, :]
        tile = lambda g, n: jnp.tile(g, n)
        gains = jnp.stack([tile(k_norm_a[l], NA_HEADS), tile(k_norm_b[l], NA_HEADS),
                           tile(q_norm_a[l], NA_HEADS) * (QK_SCALE * LOG2E),
                           tile(q_norm_b[l], GQA_Q_HEADS) * (QK_SCALE * LOG2E)])
        g_row = norm_g[l][None, :]

        ka, vat, kbd, vbt, qa, qb = _project(
            x, per_batch(shift), per_batch(scale), g_row, w_in, l, gsum, gains, cos_x, sin_x,
            len(SEG_WIDTHS), PROJ_ROWS, PROJ_ROWS)
        nseg_c = len(SEG_WIDTHS) if update_ctx else KV_SEGS
        pc = _project(ctx, for_ctx(shift), for_ctx(scale), g_row, w_in, l, gsum, gains, cos_c, sin_c,
                      nseg_c, min(PROJ_ROWS, ctx_rows), ctx_len)
        pc = [per_batch_ctx(a) for a in pc]
        cka, cvat, ckbd, cvbt = pc[:KV_SEGS]

        a_att = _neighbourhood_attention(qa, ka, vat, cka, cvat, tables, l)
        b_att = _gqa_attention(qb, kbd, vbt, ckbd, cvbt)
        x_new = _merge(a_att, b_att, x, per_batch(shift), per_batch(scale), per_batch(gate), g_row,
                       w_in, w_o_a, w_o_b, w_out, l, MERGE_ROWS)

        if update_ctx:
            cqa, cqb = pc[KV_SEGS:]
            c_a, c_b = _context_attention(cqa, cka, cvat, cqb, ckbd, cvbt)
            ctx = _merge(flat(c_a), flat(c_b), ctx, for_ctx(shift), for_ctx(scale), for_ctx(gate), g_row,
                         w_in, w_o_a, w_o_b, w_out, l, min(MERGE_ROWS, ctx_rows))
        x = x_new
    return x
```

```python
import functools

import numpy as np
import jax
import jax.numpy as jnp
from jax import lax
from jax.experimental import pallas as pl
from jax.experimental.pallas import tpu as pltpu

D_MODEL = 1024
GRID_W = 64
HEAD_DIM = 64
NA_HEADS = 8
NA_WIDTH = NA_HEADS * HEAD_DIM
WIN_R = 8
WIN_C = 16
GQA_Q_HEADS = 8
GQA_KV_HEADS = 2
GQA_WIDTH = GQA_Q_HEADS * HEAD_DIM
GQA_KV_WIDTH = GQA_KV_HEADS * HEAD_DIM
ROPE_THETA = 10000.0
ROT_AXIS = HEAD_DIM // 2
ROT_HALF = ROT_AXIS // 2
EPS = 1e-6
QK_SCALE = HEAD_DIM ** -0.5
LOG2E = 1.4426950408889634

LANES = 128
MXU_COLS = 256
MASKED = -1e30
VMEM_LIMIT = 56 * 1024 * 1024

PROJ_ROWS = 1024
MERGE_ROWS = 1024
GQA_Q_ROWS = 256
GQA_Q_BLOCKS = 4
GQA_K_ROWS = 256
GQA_SUB_ROWS = 128
GQA_AHEAD = 1
QUAD = 4
ONES_ROWS = 16
NA_SUB_ROWS = 128
NA_AHEAD = 6
NA_UNROLL = 16

F32 = jnp.float32
BF16 = jnp.bfloat16

SEG_KA, SEG_VA, SEG_KB, SEG_VB, SEG_QA, SEG_QB = range(6)
SEG_WIDTHS = (NA_WIDTH, NA_WIDTH, GQA_KV_WIDTH, GQA_KV_WIDTH, NA_WIDTH, GQA_WIDTH)
SEG_OFFSETS = tuple(int(v) for v in np.cumsum((0,) + SEG_WIDTHS))
GATE_WIDTHS = (NA_WIDTH, GQA_WIDTH, D_MODEL, D_MODEL)
GATE_OFFSETS = tuple(int(v) for v in np.cumsum((SEG_OFFSETS[-1],) + GATE_WIDTHS))
SEG_OUT_WIDTHS = tuple(2 * w if s == SEG_KB else w for s, w in enumerate(SEG_WIDTHS))
KV_SEGS = 4
SEG_ISSUE_RANK = {SEG_QB: 0, SEG_KB: 1, SEG_QA: 2, SEG_KA: 3, SEG_VB: 4, SEG_VA: 5}
SEG_GAIN_ROW = {SEG_KA: 0, SEG_KB: 1, SEG_QA: 2, SEG_QB: 3}
SEG_ROPE = (SEG_KB, SEG_QB)
SEG_T_SLAB = {SEG_VA: LANES, SEG_VB: GQA_K_ROWS}


def _dot(a, b):
    return jnp.dot(a, b, preferred_element_type=F32)


def _dot_t(a, b):
    return lax.dot_general(a, b, (((1,), (1,)), ((), ())), preferred_element_type=F32)


def _params(semantics):
    return pltpu.CompilerParams(dimension_semantics=semantics, vmem_limit_bytes=VMEM_LIMIT)


def _mod_kernel(c_ref, w_ref, b_ref, o_ref):
    c = c_ref[...]
    a = c * jax.nn.sigmoid(c)
    o_ref[0] = jnp.dot(a, w_ref[0], preferred_element_type=F32,
                       precision=lax.Precision.HIGHEST) + b_ref[0]


def _modulation(c_rows, w_ada, b_ada):
    depth = w_ada.shape[0]
    rows = c_rows.shape[0]
    ncol = w_ada.shape[2] // D_MODEL
    return pl.pallas_call(
        _mod_kernel,
        grid=(depth, ncol),
        in_specs=[
            pl.BlockSpec((rows, D_MODEL), lambda l, j: (0, 0)),
            pl.BlockSpec((1, D_MODEL, D_MODEL), lambda l, j: (l, 0, j)),
            pl.BlockSpec((1, 1, D_MODEL), lambda l, j: (l, 0, j)),
        ],
        out_specs=pl.BlockSpec((1, rows, D_MODEL), lambda l, j: (l, 0, j)),
        out_shape=jax.ShapeDtypeStruct((depth, rows, ncol * D_MODEL), F32),
        compiler_params=_params(("arbitrary", "arbitrary")),
        name="mod",
    )(c_rows, w_ada, b_ada.reshape(depth, 1, -1))


def _head_norm(p, gsum, gain):
    sq = (p * p).astype(BF16)
    cols = min(MXU_COLS, p.shape[1])
    parts = []
    for j in range(p.shape[1] // cols):
        sl = slice(j * cols, (j + 1) * cols)
        ss = _dot(sq[:, sl], gsum[:cols, :cols])
        parts.append(p[:, sl] * lax.rsqrt(ss * (1.0 / HEAD_DIM) + EPS))
    y = parts[0] if len(parts) == 1 else jnp.concatenate(parts, axis=1)
    return y * gain


def _rope(y, cos, sin):
    width = y.shape[1]
    lane = lax.broadcasted_iota(jnp.int32, y.shape, 1)
    ahead = pltpu.roll(y, width - ROT_HALF, axis=1)
    behind = pltpu.roll(y, ROT_HALF, axis=1)
    partner = jnp.where((lane % ROT_AXIS) < ROT_HALF, ahead, behind)
    reps = width // LANES
    cos = cos if reps == 1 else jnp.concatenate([cos] * reps, axis=1)
    sin = sin if reps == 1 else jnp.concatenate([sin] * reps, axis=1)
    return y * cos + partner * sin


def _duplicate_heads(y):
    lo = lax.broadcasted_iota(jnp.int32, y.shape, 1) < HEAD_DIM
    swapped = pltpu.roll(y, HEAD_DIM, axis=1)
    return jnp.concatenate([jnp.where(lo, y, swapped), jnp.where(lo, swapped, y)], axis=1)


def _modulated_norm(x, g_ref, scale_ref, shift_ref):
    ms = jnp.mean(x * x, axis=-1, keepdims=True)
    gmod = g_ref[...] * (1.0 + scale_ref[0])
    return (x * lax.rsqrt(ms + EPS) * gmod + shift_ref[0]).astype(BF16)


def _proj_kernel(x_ref, shift_ref, scale_ref, g_ref, w_ref, gsum_ref, gain_ref, cos_ref, sin_ref,
                 *out_refs, nseg, max_slab):
    h = _modulated_norm(x_ref[0], g_ref, scale_ref, shift_ref)
    gsum = gsum_ref[...]
    for seg in sorted(range(nseg), key=SEG_ISSUE_RANK.__getitem__):
        width = SEG_WIDTHS[seg]
        p = _dot(h, w_ref[:, SEG_OFFSETS[seg]:SEG_OFFSETS[seg + 1]])
        if seg in SEG_GAIN_ROW:
            row = SEG_GAIN_ROW[seg]
            p = _head_norm(p, gsum, gain_ref[row:row + 1, :width])
        if seg in SEG_ROPE:
            p = _rope(p, cos_ref[...], sin_ref[...])
        if seg == SEG_KB:
            p = _duplicate_heads(p)
        if seg in SEG_T_SLAB:
            pt = p.T.astype(BF16)
            slab = min(SEG_T_SLAB[seg], max_slab)
            for j in range(pt.shape[1] // slab):
                out_refs[seg][0, j] = pt[:, j * slab:(j + 1) * slab]
        else:
            out_refs[seg][0] = p.astype(BF16)


def _project(x, shift, scale, norm_g, w_all, layer, gsum, gains, cos, sin, nseg, rows, max_slab):
    groups, total, _ = x.shape
    ncols = SEG_OFFSETS[nseg]
    const = lambda g, i: (0, 0)

    def out_layout(s):
        width = SEG_OUT_WIDTHS[s]
        if s not in SEG_T_SLAB:
            return (groups, total, width), pl.BlockSpec((1, rows, width), lambda g, i: (g, i, 0))
        slab = min(SEG_T_SLAB[s], max_slab)
        return ((groups, total // slab, width, slab),
                pl.BlockSpec((1, rows // slab, width, slab), lambda g, i: (g, i, 0, 0)))

    layouts = [out_layout(s) for s in range(nseg)]
    return pl.pallas_call(
        functools.partial(_proj_kernel, nseg=nseg, max_slab=max_slab),
        grid=(groups, total // rows),
        in_specs=[
            pl.BlockSpec((1, rows, D_MODEL), lambda g, i: (g, i, 0)),
            pl.BlockSpec((1, 1, D_MODEL), lambda g, i: (g, 0, 0)),
            pl.BlockSpec((1, 1, D_MODEL), lambda g, i: (g, 0, 0)),
            pl.BlockSpec((1, D_MODEL), const),
            pl.BlockSpec((None, D_MODEL, ncols), lambda g, i: (layer, 0, 0), pipeline_mode=pl.Buffered(1)),
            pl.BlockSpec((MXU_COLS, MXU_COLS), const),
            pl.BlockSpec(gains.shape, const),
            pl.BlockSpec((rows, LANES), lambda g, i: (i, 0)),
            pl.BlockSpec((rows, LANES), lambda g, i: (i, 0)),
        ],
        out_specs=[spec for _, spec in layouts],
        out_shape=[jax.ShapeDtypeStruct(shape, BF16) for shape, _ in layouts],
        compiler_params=_params(("arbitrary", "arbitrary")),
        name="proj",
    )(x, shift, scale, norm_g, w_all, gsum, gains, cos, sin)


def _stack_heads(pair):
    lo = lax.broadcasted_iota(jnp.int32, pair.shape, 1) < HEAD_DIM
    zero = jnp.zeros_like(pair)
    return jnp.concatenate([jnp.where(lo, pair, zero), jnp.where(lo, zero, pair)], axis=0)


def _stack_quad(q):
    head = lax.broadcasted_iota(jnp.int32, q.shape, 1) // HEAD_DIM
    zero = jnp.zeros_like(q)
    return jnp.concatenate([jnp.where(head == h, q, zero) for h in range(QUAD)], axis=0)


def _unstack_quad_t(o, rows):
    ot = o.T
    head = lax.broadcasted_iota(jnp.int32, (rows, QUAD * HEAD_DIM), 1) // HEAD_DIM
    out = ot[:rows]
    for h in range(1, QUAD):
        out = jnp.where(head == h, ot[h * rows:(h + 1) * rows], out)
    return out


def _pv_t(vt, s, m):
    lhs = jnp.concatenate([vt, jnp.ones((ONES_ROWS, vt.shape[1]), vt.dtype)], axis=0)
    return _dot(lhs, jnp.exp2((s - m).astype(BF16)))


def _normalise_t(acc, dims):
    return acc[:dims] * (1.0 / acc[dims:dims + 1])


def _softmax_pv_t(logits, values_t):
    m = functools.reduce(jnp.maximum, [jnp.max(s, axis=0, keepdims=True) for s in logits])
    acc = functools.reduce(jnp.add, [_pv_t(vt, s, m) for vt, s in zip(values_t, logits)])
    return _normalise_t(acc, values_t[0].shape[0])


def _na_window(r, rows, slabs):
    start = jnp.clip(r - WIN_R // 2, 0, rows - WIN_R)
    slab0 = jnp.minimum(start // 2, rows // 2 - slabs)
    first = 2 * slab0
    blocks = []
    for w in range(2 * slabs):
        key_row = first + w
        valid = (key_row >= start) & (key_row < start + WIN_R)
        blocks.append(jnp.where(valid, key_row - r + (WIN_R - 1), 2 * WIN_R - 1))
    return slab0, blocks


def _na_kernel(q_ref, k_ref, vt_ref, kc_ref, vct_ref, t_ref, o_ref, qt_ref, tab_ref, *, rows, unroll):
    n_off = t_ref.shape[1]
    for off in range(n_off - 1):
        tab_ref[off] = jnp.concatenate([t_ref[h, off] for h in range(QUAD)], axis=1)
    tab_ref[n_off - 1] = jnp.full(tab_ref.shape[1:], MASKED * LOG2E, F32)
    kc = kc_ref[0]
    vct = jnp.concatenate([vct_ref[0, j] for j in range(vct_ref.shape[1])], axis=1)
    assert unroll % 2 == 0 and rows % 2 == 0 and WIN_R % 2 == 0

    def logits(r, slot):
        slabs = WIN_R // 2 + slot % 2
        slab0, blocks = _na_window(r, rows, slabs)
        qt_ref[slot] = _stack_quad(q_ref[0, pl.ds(pl.multiple_of(r * GRID_W, GRID_W), GRID_W), :]).T
        qs_t = qt_ref[slot]
        kw = k_ref[0, pl.ds(pl.multiple_of(slab0 * LANES, LANES), slabs * LANES), :]
        bias = jnp.concatenate([tab_ref[b] for b in blocks], axis=0)
        return _dot(kw, qs_t) + bias, _dot(kc, qs_t), slab0

    def finish(r, s_win, s_ctx, slab0):
        slabs = s_win.shape[0] // LANES
        vt = vt_ref[0, pl.ds(slab0, slabs)]
        vtw = jnp.concatenate([vt[j] for j in range(slabs)], axis=1)
        sub = NA_SUB_ROWS
        parts = [s[i:i + sub] for s in (s_ctx, s_win) for i in range(0, s.shape[0], sub)]
        _, acc = _attend_t(parts, jnp.concatenate([vct, vtw], axis=1), None)
        o = _normalise_t(acc, vct.shape[0])
        o_ref[0, pl.ds(pl.multiple_of(r * GRID_W, GRID_W), GRID_W), :] = _unstack_quad_t(o, GRID_W).astype(BF16)

    def body(i, carry):
        pending = []
        for e in range(unroll):
            pending.append((i * unroll + e,) + logits(i * unroll + e, e))
            if len(pending) > NA_AHEAD:
                finish(*pending.pop(0))
        for args in pending:
            finish(*args)
        return carry

    lax.fori_loop(0, rows // unroll, body, 0)


def _neighbourhood_attention(qa, ka, vat, cka, cvat, tables, layer):
    batch, seq, _ = qa.shape
    ctx_len = cka.shape[1]
    rows = seq // GRID_W
    width = QUAD * HEAD_DIM
    blk = lambda n: pl.BlockSpec((1, n, width), lambda b, j: (b, 0, j))
    vblk = lambda a: pl.BlockSpec((1, a.shape[1], width, LANES), lambda b, j: (b, 0, j, 0))
    return pl.pallas_call(
        functools.partial(_na_kernel, rows=rows, unroll=NA_UNROLL),
        grid=(batch, NA_HEADS // QUAD),
        in_specs=[blk(seq), blk(seq), vblk(vat), blk(ctx_len), vblk(cvat),
                  pl.BlockSpec((None, None) + tables.shape[2:], lambda b, j: (layer, j, 0, 0, 0, 0))],
        out_specs=blk(seq),
        out_shape=jax.ShapeDtypeStruct((batch, seq, NA_WIDTH), BF16),
        scratch_shapes=[pltpu.VMEM((NA_UNROLL, width, QUAD * GRID_W), BF16),
                        pltpu.VMEM((tables.shape[3], GRID_W, QUAD * GRID_W), F32)],
        compiler_params=_params(("arbitrary", "arbitrary")),
        name="na",
    )(qa, ka, vat, cka, cvat, tables)


def _bias_kernel(rpb_ref, onehot_ref, mask_ref, o_ref):
    o_ref[...] = (jnp.dot(rpb_ref[...], onehot_ref[...], preferred_element_type=F32,
                          precision=lax.Precision.HIGHEST) + mask_ref[...]) * LOG2E


def _bias_tables(rpb):
    depth = rpb.shape[0]
    n_row_off, n_col_off = 2 * WIN_R - 1, 2 * WIN_C - 1
    cols = np.arange(GRID_W)
    cstart = np.clip(cols - WIN_C // 2, 0, GRID_W - WIN_C)
    kcol = np.arange(GRID_W)
    valid = (kcol[None, :] >= cstart[:, None]) & (kcol[None, :] < cstart[:, None] + WIN_C)
    col_off = kcol[None, :] - cols[:, None] + (WIN_C - 1)
    pad_r, pad_c = -n_row_off % 8, -n_col_off % 8
    assert pad_r >= 1
    onehot = (np.arange(n_col_off + pad_c)[:, None, None] == col_off.T[None]) & valid.T[None]
    onehot = jnp.asarray(onehot.reshape(n_col_off + pad_c, GRID_W * GRID_W), dtype=F32)
    mask = jnp.asarray(np.where(valid.T, 0.0, MASKED).reshape(1, GRID_W * GRID_W), dtype=F32)
    heads = depth * NA_HEADS
    rpb_p = jnp.pad(rpb.reshape(heads, n_row_off, n_col_off), ((0, 0), (0, pad_r), (0, pad_c)))
    rpb_p = rpb_p.reshape(heads * (n_row_off + pad_r), n_col_off + pad_c)
    col_blocks = 4
    cols_per = GRID_W * GRID_W // col_blocks
    toep = pl.pallas_call(
        _bias_kernel,
        grid=(col_blocks,),
        in_specs=[pl.BlockSpec(rpb_p.shape, lambda j: (0, 0)),
                  pl.BlockSpec((onehot.shape[0], cols_per), lambda j: (0, j)),
                  pl.BlockSpec((1, cols_per), lambda j: (0, j))],
        out_specs=pl.BlockSpec((rpb_p.shape[0], cols_per), lambda j: (0, j)),
        out_shape=jax.ShapeDtypeStruct((rpb_p.shape[0], GRID_W * GRID_W), F32),
        compiler_params=_params(("arbitrary",)),
        name="bias",
    )(rpb_p, onehot, mask)
    return toep.reshape(depth, NA_HEADS // QUAD, QUAD, n_row_off + pad_r, GRID_W, GRID_W)


def _stack_group(q):
    return jnp.concatenate([_stack_heads(q[:, :LANES]), _stack_heads(q[:, LANES:])], axis=0)


def _unstack_group_t(o, rows):
    pairs = [jnp.concatenate([o[:, (2 * j) * rows:(2 * j + 1) * rows],
                              o[:, (2 * j + 1) * rows:(2 * j + 2) * rows]], axis=0).T for j in range(2)]
    return jnp.concatenate(pairs, axis=1)


def _scores_t(qs_t, k):
    return _dot(k, qs_t)


def _attend_t(parts, vt, state):
    m_old = None if state is None else state[0]
    m, ps, ms = m_old, [], []
    for s in parts:
        m_part = jnp.max(s, axis=0, keepdims=True)
        m = m_part if m is None else jnp.maximum(m, m_part)
        ps.append(jnp.exp2((s - m).astype(BF16)))
        ms.append(m)
    ps = [p * jnp.exp2(m_i - m).astype(BF16) for p, m_i in zip(ps[:-1], ms[:-1])] + ps[-1:]
    lhs = jnp.concatenate([vt, jnp.ones((ONES_ROWS, vt.shape[1]), vt.dtype)], axis=0)
    pv = _dot(lhs, ps[0] if len(ps) == 1 else jnp.concatenate(ps, axis=0))
    if state is None:
        return m, pv
    return m, jnp.exp2(m_old - m) * state[1] + pv


def _gqa_kernel(q_ref, k_ref, vt_ref, kc_ref, vct_ref, o_ref, qt_ref, *, chunks, tq, tk, blocks):
    for j in range(blocks):
        qt_ref[j] = _stack_group(q_ref[0, j * tq:(j + 1) * tq, :]).T
    pending = []
    states = [None] * blocks

    def drain(limit):
        while len(pending) > limit:
            j, parts, vt = pending.pop(0)
            states[j] = _attend_t(parts, vt, states[j])

    def logits(qs, keys, c, rows):
        sub = min(GQA_SUB_ROWS, rows)
        return [_scores_t(qs, keys[0, c * rows + i * sub:c * rows + (i + 1) * sub, :]) for i in range(rows // sub)]

    for j in range(blocks):
        qs = qt_ref[j]
        for c in range(vct_ref.shape[1]):
            pending.append((j, logits(qs, kc_ref, c, vct_ref.shape[3]), vct_ref[0, c]))
            drain(GQA_AHEAD)
        for c in range(chunks):
            pending.append((j, logits(qs, k_ref, c, tk), vt_ref[0, c]))
            drain(GQA_AHEAD)
    drain(0)
    for j in range(blocks):
        _, acc = states[j]
        o_ref[0, j * tq:(j + 1) * tq, :] = _unstack_group_t(_normalise_t(acc, HEAD_DIM), tq).astype(BF16)


def _gqa_attention(qb, kbd, vbt, ckbd, cvbt):
    batch, seq, _ = qb.shape
    ctx_len = ckbd.shape[1]
    tq, tk, blocks = GQA_Q_ROWS, vbt.shape[3], GQA_Q_BLOCKS
    kspec = lambda n: pl.BlockSpec((1, n, LANES), lambda b, g, i: (b, 0, g))
    vspec = lambda a: pl.BlockSpec((1, a.shape[1], HEAD_DIM, a.shape[3]), lambda b, g, i: (b, 0, g, 0))
    qspec = pl.BlockSpec((1, blocks * tq, 2 * LANES), lambda b, g, i: (b, i, g))
    return pl.pallas_call(
        functools.partial(_gqa_kernel, chunks=seq // tk, tq=tq, tk=tk, blocks=blocks),
        grid=(batch, GQA_KV_HEADS, seq // (blocks * tq)),
        in_specs=[qspec, kspec(seq), vspec(vbt), kspec(ctx_len), vspec(cvbt)],
        out_specs=qspec,
        out_shape=jax.ShapeDtypeStruct((batch, seq, GQA_WIDTH), BF16),
        scratch_shapes=[pltpu.VMEM((blocks, LANES, (GQA_Q_HEADS // GQA_KV_HEADS) * tq), BF16)],
        compiler_params=_params(("arbitrary", "arbitrary", "arbitrary")),
        name="gqa",
    )(qb, kbd, vbt, ckbd, cvbt)


def _ctx_kernel(qa_ref, ka_ref, vat_ref, qb_ref, kb_ref, vbt_ref, oa_ref, ob_ref):
    n = qa_ref.shape[1]
    width = QUAD * HEAD_DIM
    for g in range(NA_HEADS // QUAD):
        sl = slice(g * width, (g + 1) * width)
        vt = jnp.concatenate([vat_ref[0, j, sl, :] for j in range(vat_ref.shape[1])], axis=1)
        o = _softmax_pv_t([_dot_t(ka_ref[0, :, sl], _stack_quad(qa_ref[0, :, sl]))], [vt])
        oa_ref[0, :, sl] = _unstack_quad_t(o, n).astype(BF16)
    for g in range(GQA_KV_HEADS):
        sl = slice(g * 2 * LANES, (g + 1) * 2 * LANES)
        s = _scores_t(_stack_group(qb_ref[0, :, sl]).T, kb_ref[0, :, g * LANES:(g + 1) * LANES])
        vt = jnp.concatenate([vbt_ref[0, j, g * HEAD_DIM:(g + 1) * HEAD_DIM, :] for j in range(vbt_ref.shape[1])],
                             axis=1)
        o = _softmax_pv_t([s], [vt])
        ob_ref[0, :, sl] = _unstack_group_t(o, n).astype(BF16)


def _context_attention(cqa, cka, cvat, cqb, ckbd, cvbt):
    batch, n, _ = cqa.shape
    spec = lambda a: pl.BlockSpec((1,) + a.shape[1:], lambda b: (b,) + (0,) * (a.ndim - 1))
    args = (cqa, cka, cvat, cqb, ckbd, cvbt)
    return pl.pallas_call(
        _ctx_kernel,
        grid=(batch,),
        in_specs=[spec(a) for a in args],
        out_specs=[spec(cqa), spec(cqb)],
        out_shape=[jax.ShapeDtypeStruct(cqa.shape, BF16), jax.ShapeDtypeStruct(cqb.shape, BF16)],
        compiler_params=_params(("arbitrary",)),
        name="ctx",
    )(*args)


def _merge_kernel(a_ref, b_ref, x_ref, shift_ref, scale_ref, gate_ref, g_ref, win_ref,
                  woa_ref, wob_ref, wout_ref, o_ref):
    x = x_ref[0]
    h = _modulated_norm(x, g_ref, scale_ref, shift_ref)
    z_a, z_b, g_a, g_b = (_dot(h, win_ref[:, GATE_OFFSETS[i]:GATE_OFFSETS[i + 1]]) for i in range(4))
    o_a = _dot((a_ref[0].astype(F32) * (z_a * jax.nn.sigmoid(z_a))).astype(BF16), woa_ref[...])
    o_b = _dot((b_ref[0].astype(F32) * (z_b * jax.nn.sigmoid(z_b))).astype(BF16), wob_ref[...])
    merged = jax.nn.sigmoid(g_a) * o_a + jax.nn.sigmoid(g_b) * o_b
    o_ref[0] = x + gate_ref[0] * _dot(merged.astype(BF16), wout_ref[...])


def _merge(a_att, b_att, x, shift, scale, gate, norm_g, w_in, w_o_a, w_o_b, w_out, layer, rows):
    groups, total, _ = x.shape
    act = lambda w: pl.BlockSpec((1, rows, w), lambda g, i: (g, i, 0))
    mod = pl.BlockSpec((1, 1, D_MODEL), lambda g, i: (g, 0, 0))
    const = lambda a: pl.BlockSpec((None,) + a.shape[1:], lambda g, i: (layer, 0, 0),
                                   pipeline_mode=pl.Buffered(1))
    return pl.pallas_call(
        _merge_kernel,
        grid=(groups, total // rows),
        in_specs=[act(NA_WIDTH), act(GQA_WIDTH), act(D_MODEL), mod, mod, mod,
                  pl.BlockSpec((1, D_MODEL), lambda g, i: (0, 0)),
                  const(w_in), const(w_o_a), const(w_o_b), const(w_out)],
        out_specs=act(D_MODEL),
        out_shape=jax.ShapeDtypeStruct(x.shape, F32),
        compiler_params=_params(("arbitrary", "arbitrary")),
        name="merge",
    )(a_att, b_att, x, shift, scale, gate, norm_g, w_in, w_o_a, w_o_b, w_out)


def _rope_tables(seq):
    t = jnp.arange(seq, dtype=jnp.int32)
    inv = 1.0 / (ROPE_THETA ** (jnp.arange(ROT_HALF, dtype=F32) / ROT_HALF))
    ang_r = (t // GRID_W).astype(F32)[:, None] * inv[None, :]
    ang_c = (t % GRID_W).astype(F32)[:, None] * inv[None, :]
    cos = jnp.concatenate([jnp.cos(ang_r)] * 2 + [jnp.cos(ang_c)] * 2, axis=1)
    sin = jnp.concatenate([-jnp.sin(ang_r), jnp.sin(ang_r), -jnp.sin(ang_c), jnp.sin(ang_c)], axis=1)
    reps = LANES // HEAD_DIM
    return jnp.tile(cos, (1, reps)), jnp.tile(sin, (1, reps))


def _group_sum_matrix():
    idx = np.arange(MXU_COLS) // HEAD_DIM
    return jnp.asarray(idx[:, None] == idx[None, :], dtype=BF16)


def kernel(x, c, ctx, c_ctx, w_ada, b_ada, norm_g, w_in, q_norm_a, k_norm_a, q_norm_b, k_norm_b,
           rpb, w_o_a, w_o_b, w_out):
    batch, seq, _ = x.shape
    ctx_len = ctx.shape[1]
    depth = w_ada.shape[0]
    rows = seq // GRID_W
    assert seq % GRID_W == 0 and rows >= WIN_R and seq % PROJ_ROWS == 0 and seq % GQA_Q_ROWS == 0

    pad = -(batch + 1) % 8
    c_rows = jnp.concatenate([c, c_ctx[None, :], jnp.zeros((pad, D_MODEL), F32)], axis=0)
    mod = _modulation(c_rows, w_ada, b_ada)

    cos_x, sin_x = _rope_tables(seq)
    ctx_rows = batch * ctx_len
    ctx = ctx.reshape(1, ctx_rows, D_MODEL)
    cos_c = jnp.ones((ctx_rows, LANES), F32)
    sin_c = jnp.zeros((ctx_rows, LANES), F32)
    flat = lambda a: a.reshape((1, ctx_rows) + a.shape[2:])
    per_batch_ctx = lambda a: a.reshape((batch, a.shape[1] // batch) + a.shape[2:])
    gsum = _group_sum_matrix()
    tables = _bias_tables(rpb)
    w_in, w_o_a, w_o_b, w_out = (w.astype(BF16) for w in (w_in, w_o_a, w_o_b, w_out))

    for l in range(depth):
        update_ctx = l < depth - 1
        shift, scale, gate = (mod[l, :, i * D_MODEL:(i + 1) * D_MODEL] for i in range(3))
        per_batch = lambda m: m[:batch, None, :]
        for_ctx = lambda m: m[batch][None, None, :]
        tile = lambda g, n: jnp.tile(g, n)
        gains = jnp.stack([tile(k_norm_a[l], NA_HEADS), tile(k_norm_b[l], NA_HEADS),
                           tile(q_norm_a[l], NA_HEADS) * (QK_SCALE * LOG2E),
                           tile(q_norm_b[l], GQA_Q_HEADS) * (QK_SCALE * LOG2E)])
        g_row = norm_g[l][None, :]

        ka, vat, kbd, vbt, qa, qb = _project(
            x, per_batch(shift), per_batch(scale), g_row, w_in, l, gsum, gains, cos_x, sin_x,
            len(SEG_WIDTHS), PROJ_ROWS, PROJ_ROWS)
        nseg_c = len(SEG_WIDTHS) if update_ctx else KV_SEGS
        pc = _project(ctx, for_ctx(shift), for_ctx(scale), g_row, w_in, l, gsum, gains, cos_c, sin_c,
                      nseg_c, min(PROJ_ROWS, ctx_rows), ctx_len)
        pc = [per_batch_ctx(a) for a in pc]
        cka, cvat, ckbd, cvbt = pc[:KV_SEGS]

        a_att = _neighbourhood_attention(qa, ka, vat, cka, cvat, tables, l)
        b_att = _gqa_attention(qb, kbd, vbt, ckbd, cvbt)
        x_new = _merge(a_att, b_att, x, per_batch(shift), per_batch(scale), per_batch(gate), g_row,
                       w_in, w_o_a, w_o_b, w_out, l, MERGE_ROWS)

        if update_ctx:
            cqa, cqb = pc[KV_SEGS:]
            c_a, c_b = _context_attention(cqa, cka, cvat, cqb, ckbd, cvbt)
            ctx = _merge(flat(c_a), flat(c_b), ctx, for_ctx(shift), for_ctx(scale), for_ctx(gate), g_row,
                         w_in, w_o_a, w_o_b, w_out, l, min(MERGE_ROWS, ctx_rows))
        x = x_new
    return x
```

```python
import functools

import numpy as np
import jax
import jax.numpy as jnp
from jax import lax
from jax.experimental import pallas as pl
from jax.experimental.pallas import tpu as pltpu

D_MODEL = 1024
GRID_W = 64
HEAD_DIM = 64
NA_HEADS = 8
NA_WIDTH = NA_HEADS * HEAD_DIM
WIN_R = 8
WIN_C = 16
GQA_Q_HEADS = 8
GQA_KV_HEADS = 2
GQA_WIDTH = GQA_Q_HEADS * HEAD_DIM
GQA_KV_WIDTH = GQA_KV_HEADS * HEAD_DIM
ROPE_THETA = 10000.0
ROT_AXIS = HEAD_DIM // 2
ROT_HALF = ROT_AXIS // 2
EPS = 1e-6
QK_SCALE = HEAD_DIM ** -0.5
LOG2E = 1.4426950408889634

LANES = 128
MXU_COLS = 256
MASKED = -1e30
VMEM_LIMIT = 56 * 1024 * 1024

PROJ_ROWS = 1024
MERGE_ROWS = 1024
GQA_Q_ROWS = 256
GQA_Q_BLOCKS = 4
GQA_K_ROWS = 256
GQA_SUB_ROWS = 128
GQA_AHEAD = 1
QUAD = 4
ONES_ROWS = 16
NA_SUB_ROWS = 128
NA_AHEAD = 6
NA_UNROLL = 16

F32 = jnp.float32
BF16 = jnp.bfloat16

SEG_KA, SEG_VA, SEG_KB, SEG_VB, SEG_QA, SEG_QB = range(6)
SEG_WIDTHS = (NA_WIDTH, NA_WIDTH, GQA_KV_WIDTH, GQA_KV_WIDTH, NA_WIDTH, GQA_WIDTH)
SEG_OFFSETS = tuple(int(v) for v in np.cumsum((0,) + SEG_WIDTHS))
GATE_WIDTHS = (NA_WIDTH, GQA_WIDTH, D_MODEL, D_MODEL)
GATE_OFFSETS = tuple(int(v) for v in np.cumsum((SEG_OFFSETS[-1],) + GATE_WIDTHS))
SEG_OUT_WIDTHS = tuple(2 * w if s == SEG_KB else w for s, w in enumerate(SEG_WIDTHS))
KV_SEGS = 4
SEG_ISSUE_RANK = {SEG_QB: 0, SEG_KB: 1, SEG_QA: 2, SEG_KA: 3, SEG_VB: 4, SEG_VA: 5}
SEG_GAIN_ROW = {SEG_KA: 0, SEG_KB: 1, SEG_QA: 2, SEG_QB: 3}
SEG_ROPE = (SEG_KB, SEG_QB)
SEG_T_SLAB = {SEG_VA: LANES, SEG_VB: GQA_K_ROWS}


def _dot(a, b):
    return jnp.dot(a, b, preferred_element_type=F32)


def _dot_t(a, b):
    return lax.dot_general(a, b, (((1,), (1,)), ((), ())), preferred_element_type=F32)


def _params(semantics):
    return pltpu.CompilerParams(dimension_semantics=semantics, vmem_limit_bytes=VMEM_LIMIT)


def _mod_kernel(c_ref, w_ref, b_ref, o_ref):
    c = c_ref[...]
    a = c * jax.nn.sigmoid(c)
    o_ref[0] = jnp.dot(a, w_ref[0], preferred_element_type=F32,
                       precision=lax.Precision.HIGHEST) + b_ref[0]


def _modulation(c_rows, w_ada, b_ada):
    depth = w_ada.shape[0]
    rows = c_rows.shape[0]
    ncol = w_ada.shape[2] // D_MODEL
    return pl.pallas_call(
        _mod_kernel,
        grid=(depth, ncol),
        in_specs=[
            pl.BlockSpec((rows, D_MODEL), lambda l, j: (0, 0)),
            pl.BlockSpec((1, D_MODEL, D_MODEL), lambda l, j: (l, 0, j)),
            pl.BlockSpec((1, 1, D_MODEL), lambda l, j: (l, 0, j)),
        ],
        out_specs=pl.BlockSpec((1, rows, D_MODEL), lambda l, j: (l, 0, j)),
        out_shape=jax.ShapeDtypeStruct((depth, rows, ncol * D_MODEL), F32),
        compiler_params=_params(("arbitrary", "arbitrary")),
        name="mod",
    )(c_rows, w_ada, b_ada.reshape(depth, 1, -1))


def _head_norm(p, gsum, gain):
    sq = (p * p).astype(BF16)
    cols = min(MXU_COLS, p.shape[1])
    parts = []
    for j in range(p.shape[1] // cols):
        sl = slice(j * cols, (j + 1) * cols)
        ss = _dot(sq[:, sl], gsum[:cols, :cols])
        parts.append(p[:, sl] * lax.rsqrt(ss * (1.0 / HEAD_DIM) + EPS))
    y = parts[0] if len(parts) == 1 else jnp.concatenate(parts, axis=1)
    return y * gain


def _rope(y, cos, sin):
    width = y.shape[1]
    lane = lax.broadcasted_iota(jnp.int32, y.shape, 1)
    ahead = pltpu.roll(y, width - ROT_HALF, axis=1)
    behind = pltpu.roll(y, ROT_HALF, axis=1)
    partner = jnp.where((lane % ROT_AXIS) < ROT_HALF, ahead, behind)
    reps = width // LANES
    cos = cos if reps == 1 else jnp.concatenate([cos] * reps, axis=1)
    sin = sin if reps == 1 else jnp.concatenate([sin] * reps, axis=1)
    return y * cos + partner * sin


def _duplicate_heads(y):
    lo = lax.broadcasted_iota(jnp.int32, y.shape, 1) < HEAD_DIM
    swapped = pltpu.roll(y, HEAD_DIM, axis=1)
    return jnp.concatenate([jnp.where(lo, y, swapped), jnp.where(lo, swapped, y)], axis=1)


def _modulated_norm(x, g_ref, scale_ref, shift_ref):
    ms = jnp.mean(x * x, axis=-1, keepdims=True)
    gmod = g_ref[...] * (1.0 + scale_ref[0])
    return (x * lax.rsqrt(ms + EPS) * gmod + shift_ref[0]).astype(BF16)


def _proj_kernel(x_ref, shift_ref, scale_ref, g_ref, w_ref, gsum_ref, gain_ref, cos_ref, sin_ref,
                 *out_refs, nseg, max_slab):
    h = _modulated_norm(x_ref[0], g_ref, scale_ref, shift_ref)
    gsum = gsum_ref[...]
    for seg in sorted(range(nseg), key=SEG_ISSUE_RANK.__getitem__):
        width = SEG_WIDTHS[seg]
        p = _dot(h, w_ref[:, SEG_OFFSETS[seg]:SEG_OFFSETS[seg + 1]])
        if seg in SEG_GAIN_ROW:
            row = SEG_GAIN_ROW[seg]
            p = _head_norm(p, gsum, gain_ref[row:row + 1, :width])
        if seg in SEG_ROPE:
            p = _rope(p, cos_ref[...], sin_ref[...])
        if seg == SEG_KB:
            p = _duplicate_heads(p)
        if seg in SEG_T_SLAB:
            pt = p.T.astype(BF16)
            slab = min(SEG_T_SLAB[seg], max_slab)
            for j in range(pt.shape[1] // slab):
                out_refs[seg][0, j] = pt[:, j * slab:(j + 1) * slab]
        else:
            out_refs[seg][0] = p.astype(BF16)


def _project(x, shift, scale, norm_g, w_all, layer, gsum, gains, cos, sin, nseg, rows, max_slab):
    groups, total, _ = x.shape
    ncols = SEG_OFFSETS[nseg]
    const = lambda g, i: (0, 0)

    def out_layout(s):
        width = SEG_OUT_WIDTHS[s]
        if s not in SEG_T_SLAB:
            return (groups, total, width), pl.BlockSpec((1, rows, width), lambda g, i: (g, i, 0))
        slab = min(SEG_T_SLAB[s], max_slab)
        return ((groups, total // slab, width, slab),
                pl.BlockSpec((1, rows // slab, width, slab), lambda g, i: (g, i, 0, 0)))

    layouts = [out_layout(s) for s in range(nseg)]
    return pl.pallas_call(
        functools.partial(_proj_kernel, nseg=nseg, max_slab=max_slab),
        grid=(groups, total // rows),
        in_specs=[
            pl.BlockSpec((1, rows, D_MODEL), lambda g, i: (g, i, 0)),
            pl.BlockSpec((1, 1, D_MODEL), lambda g, i: (g, 0, 0)),
            pl.BlockSpec((1, 1, D_MODEL), lambda g, i: (g, 0, 0)),
            pl.BlockSpec((1, D_MODEL), const),
            pl.BlockSpec((None, D_MODEL, ncols), lambda g, i: (layer, 0, 0), pipeline_mode=pl.Buffered(1)),
            pl.BlockSpec((MXU_COLS, MXU_COLS), const),
            pl.BlockSpec(gains.shape, const),
            pl.BlockSpec((rows, LANES), lambda g, i: (i, 0)),
            pl.BlockSpec((rows, LANES), lambda g, i: (i, 0)),
        ],
        out_specs=[spec for _, spec in layouts],
        out_shape=[jax.ShapeDtypeStruct(shape, BF16) for shape, _ in layouts],
        compiler_params=_params(("arbitrary", "arbitrary")),
        name="proj",
    )(x, shift, scale, norm_g, w_all, gsum, gains, cos, sin)


def _stack_heads(pair):
    lo = lax.broadcasted_iota(jnp.int32, pair.shape, 1) < HEAD_DIM
    zero = jnp.zeros_like(pair)
    return jnp.concatenate([jnp.where(lo, pair, zero), jnp.where(lo, zero, pair)], axis=0)


def _stack_quad(q):
    head = lax.broadcasted_iota(jnp.int32, q.shape, 1) // HEAD_DIM
    zero = jnp.zeros_like(q)
    return jnp.concatenate([jnp.where(head == h, q, zero) for h in range(QUAD)], axis=0)


def _unstack_quad_t(o, rows):
    ot = o.T
    head = lax.broadcasted_iota(jnp.int32, (rows, QUAD * HEAD_DIM), 1) // HEAD_DIM
    out = ot[:rows]
    for h in range(1, QUAD):
        out = jnp.where(head == h, ot[h * rows:(h + 1) * rows], out)
    return out


def _pv_t(vt, s, m):
    lhs = jnp.concatenate([vt, jnp.ones((ONES_ROWS, vt.shape[1]), vt.dtype)], axis=0)
    return _dot(lhs, jnp.exp2((s - m).astype(BF16)))


def _normalise_t(acc, dims):
    return acc[:dims] * (1.0 / acc[dims:dims + 1])


def _softmax_pv_t(logits, values_t):
    m = functools.reduce(jnp.maximum, [jnp.max(s, axis=0, keepdims=True) for s in logits])
    acc = functools.reduce(jnp.add, [_pv_t(vt, s, m) for vt, s in zip(values_t, logits)])
    return _normalise_t(acc, values_t[0].shape[0])


def _na_window(r, rows, slabs):
    start = jnp.clip(r - WIN_R // 2, 0, rows - WIN_R)
    slab0 = jnp.minimum(start // 2, rows // 2 - slabs)
    first = 2 * slab0
    blocks = []
    for w in range(2 * slabs):
        key_row = first + w
        valid = (key_row >= start) & (key_row < start + WIN_R)
        blocks.append(jnp.where(valid, key_row - r + (WIN_R - 1), 2 * WIN_R - 1))
    return slab0, blocks


def _na_kernel(q_ref, k_ref, vt_ref, kc_ref, vct_ref, t_ref, o_ref, qt_ref, tab_ref, *, rows, unroll):
    n_off = t_ref.shape[1]
    for off in range(n_off - 1):
        tab_ref[off] = jnp.concatenate([t_ref[h, off] for h in range(QUAD)], axis=1)
    tab_ref[n_off - 1] = jnp.full(tab_ref.shape[1:], MASKED * LOG2E, F32)
    kc = kc_ref[0]
    vct = jnp.concatenate([vct_ref[0, j] for j in range(vct_ref.shape[1])], axis=1)
    assert unroll % 2 == 0 and rows % 2 == 0 and WIN_R % 2 == 0

    def logits(r, slot):
        slabs = WIN_R // 2 + slot % 2
        slab0, blocks = _na_window(r, rows, slabs)
        qt_ref[slot] = _stack_quad(q_ref[0, pl.ds(pl.multiple_of(r * GRID_W, GRID_W), GRID_W), :]).T
        qs_t = qt_ref[slot]
        kw = k_ref[0, pl.ds(pl.multiple_of(slab0 * LANES, LANES), slabs * LANES), :]
        bias = jnp.concatenate([tab_ref[b] for b in blocks], axis=0)
        return _dot(kw, qs_t) + bias, _dot(kc, qs_t), slab0

    def finish(r, s_win, s_ctx, slab0):
        slabs = s_win.shape[0] // LANES
        vt = vt_ref[0, pl.ds(slab0, slabs)]
        vtw = jnp.concatenate([vt[j] for j in range(slabs)], axis=1)
        sub = NA_SUB_ROWS
        parts = [s[i:i + sub] for s in (s_ctx, s_win) for i in range(0, s.shape[0], sub)]
        _, acc = _attend_t(parts, jnp.concatenate([vct, vtw], axis=1), None)
        o = _normalise_t(acc, vct.shape[0])
        o_ref[0, pl.ds(pl.multiple_of(r * GRID_W, GRID_W), GRID_W), :] = _unstack_quad_t(o, GRID_W).astype(BF16)

    def body(i, carry):
        pending = []
        for e in range(unroll):
            pending.append((i * unroll + e,) + logits(i * unroll + e, e))
            if len(pending) > NA_AHEAD:
                finish(*pending.pop(0))
        for args in pending:
            finish(*args)
        return carry

    lax.fori_loop(0, rows // unroll, body, 0)


def _neighbourhood_attention(qa, ka, vat, cka, cvat, tables, layer):
    batch, seq, _ = qa.shape
    ctx_len = cka.shape[1]
    rows = seq // GRID_W
    width = QUAD * HEAD_DIM
    blk = lambda n: pl.BlockSpec((1, n, width), lambda b, j: (b, 0, j))
    vblk = lambda a: pl.BlockSpec((1, a.shape[1], width, LANES), lambda b, j: (b, 0, j, 0))
    return pl.pallas_call(
        functools.partial(_na_kernel, rows=rows, unroll=NA_UNROLL),
        grid=(batch, NA_HEADS // QUAD),
        in_specs=[blk(seq), blk(seq), vblk(vat), blk(ctx_len), vblk(cvat),
                  pl.BlockSpec((None, None) + tables.shape[2:], lambda b, j: (layer, j, 0, 0, 0, 0))],
        out_specs=blk(seq),
        out_shape=jax.ShapeDtypeStruct((batch, seq, NA_WIDTH), BF16),
        scratch_shapes=[pltpu.VMEM((NA_UNROLL, width, QUAD * GRID_W), BF16),
                        pltpu.VMEM((tables.shape[3], GRID_W, QUAD * GRID_W), F32)],
        compiler_params=_params(("arbitrary", "arbitrary")),
        name="na",
    )(qa, ka, vat, cka, cvat, tables)


def _bias_kernel(rpb_ref, onehot_ref, mask_ref, o_ref):
    o_ref[...] = (jnp.dot(rpb_ref[...], onehot_ref[...], preferred_element_type=F32,
                          precision=lax.Precision.HIGHEST) + mask_ref[...]) * LOG2E


def _bias_tables(rpb):
    depth = rpb.shape[0]
    n_row_off, n_col_off = 2 * WIN_R - 1, 2 * WIN_C - 1
    cols = np.arange(GRID_W)
    cstart = np.clip(cols - WIN_C // 2, 0, GRID_W - WIN_C)
    kcol = np.arange(GRID_W)
    valid = (kcol[None, :] >= cstart[:, None]) & (kcol[None, :] < cstart[:, None] + WIN_C)
    col_off = kcol[None, :] - cols[:, None] + (WIN_C - 1)
    pad_r, pad_c = -n_row_off % 8, -n_col_off % 8
    assert pad_r >= 1
    onehot = (np.arange(n_col_off + pad_c)[:, None, None] == col_off.T[None]) & valid.T[None]
    onehot = jnp.asarray(onehot.reshape(n_col_off + pad_c, GRID_W * GRID_W), dtype=F32)
    mask = jnp.asarray(np.where(valid.T, 0.0, MASKED).reshape(1, GRID_W * GRID_W), dtype=F32)
    heads = depth * NA_HEADS
    rpb_p = jnp.pad(rpb.reshape(heads, n_row_off, n_col_off), ((0, 0), (0, pad_r), (0, pad_c)))
    rpb_p = rpb_p.reshape(heads * (n_row_off + pad_r), n_col_off + pad_c)
    col_blocks = 4
    cols_per = GRID_W * GRID_W // col_blocks
    toep = pl.pallas_call(
        _bias_kernel,
        grid=(col_blocks,),
        in_specs=[pl.BlockSpec(rpb_p.shape, lambda j: (0, 0)),
                  pl.BlockSpec((onehot.shape[0], cols_per), lambda j: (0, j)),
                  pl.BlockSpec((1, cols_per), lambda j: (0, j))],
        out_specs=pl.BlockSpec((rpb_p.shape[0], cols_per), lambda j: (0, j)),
        out_shape=jax.ShapeDtypeStruct((rpb_p.shape[0], GRID_W * GRID_W), F32),
        compiler_params=_params(("arbitrary",)),
        name="bias",
    )(rpb_p, onehot, mask)
    return toep.reshape(depth, NA_HEADS // QUAD, QUAD, n_row_off + pad_r, GRID_W, GRID_W)


def _stack_group(q):
    return jnp.concatenate([_stack_heads(q[:, :LANES]), _stack_heads(q[:, LANES:])], axis=0)


def _unstack_group_t(o, rows):
    pairs = [jnp.concatenate([o[:, (2 * j) * rows:(2 * j + 1) * rows],
                              o[:, (2 * j + 1) * rows:(2 * j + 2) * rows]], axis=0).T for j in range(2)]
    return jnp.concatenate(pairs, axis=1)


def _scores_t(qs_t, k):
    return _dot(k, qs_t)


def _attend_t(parts, vt, state):
    m_old = None if state is None else state[0]
    m, ps, ms = m_old, [], []
    for s in parts:
        m_part = jnp.max(s, axis=0, keepdims=True)
        m = m_part if m is None else jnp.maximum(m, m_part)
        ps.append(jnp.exp2((s - m).astype(BF16)))
        ms.append(m)
    ps = [p * jnp.exp2(m_i - m).astype(BF16) for p, m_i in zip(ps[:-1], ms[:-1])] + ps[-1:]
    lhs = jnp.concatenate([vt, jnp.ones((ONES_ROWS, vt.shape[1]), vt.dtype)], axis=0)
    pv = _dot(lhs, ps[0] if len(ps) == 1 else jnp.concatenate(ps, axis=0))
    if state is None:
        return m, pv
    return m, jnp.exp2(m_old - m) * state[1] + pv


def _gqa_kernel(q_ref, k_ref, vt_ref, kc_ref, vct_ref, o_ref, qt_ref, *, chunks, tq, tk, blocks):
    for j in range(blocks):
        qt_ref[j] = _stack_group(q_ref[0, j * tq:(j + 1) * tq, :]).T
    pending = []
    states = [None] * blocks

    def drain(limit):
        while len(pending) > limit:
            j, parts, vt = pending.pop(0)
            states[j] = _attend_t(parts, vt, states[j])

    def logits(qs, keys, c, rows):
        sub = min(GQA_SUB_ROWS, rows)
        return [_scores_t(qs, keys[0, c * rows + i * sub:c * rows + (i + 1) * sub, :]) for i in range(rows // sub)]

    for j in range(blocks):
        qs = qt_ref[j]
        for c in range(vct_ref.shape[1]):
            pending.append((j, logits(qs, kc_ref, c, vct_ref.shape[3]), vct_ref[0, c]))
            drain(GQA_AHEAD)
        for c in range(chunks):
            pending.append((j, logits(qs, k_ref, c, tk), vt_ref[0, c]))
            drain(GQA_AHEAD)
    drain(0)
    for j in range(blocks):
        _, acc = states[j]
        o_ref[0, j * tq:(j + 1) * tq, :] = _unstack_group_t(_normalise_t(acc, HEAD_DIM), tq).astype(BF16)


def _gqa_attention(qb, kbd, vbt, ckbd, cvbt):
    batch, seq, _ = qb.shape
    ctx_len = ckbd.shape[1]
    tq, tk, blocks = GQA_Q_ROWS, vbt.shape[3], GQA_Q_BLOCKS
    kspec = lambda n: pl.BlockSpec((1, n, LANES), lambda b, g, i: (b, 0, g))
    vspec = lambda a: pl.BlockSpec((1, a.shape[1], HEAD_DIM, a.shape[3]), lambda b, g, i: (b, 0, g, 0))
    qspec = pl.BlockSpec((1, blocks * tq, 2 * LANES), lambda b, g, i: (b, i, g))
    return pl.pallas_call(
        functools.partial(_gqa_kernel, chunks=seq // tk, tq=tq, tk=tk, blocks=blocks),
        grid=(batch, GQA_KV_HEADS, seq // (blocks * tq)),
        in_specs=[qspec, kspec(seq), vspec(vbt), kspec(ctx_len), vspec(cvbt)],
        out_specs=qspec,
        out_shape=jax.ShapeDtypeStruct((batch, seq, GQA_WIDTH), BF16),
        scratch_shapes=[pltpu.VMEM((blocks, LANES, (GQA_Q_HEADS // GQA_KV_HEADS) * tq), BF16)],
        compiler_params=_params(("arbitrary", "arbitrary", "arbitrary")),
        name="gqa",
    )(qb, kbd, vbt, ckbd, cvbt)


def _ctx_kernel(x_ref, shift_ref, scale_ref, gate_ref, g_ref, win_ref, gsum_ref, gain_ref,
                woa_ref, wob_ref, wout_ref, ka_ref, vat_ref, kbd_ref, vbt_ref, o_ref):
    x = x_ref[0]
    n = x.shape[0]
    h = _modulated_norm(x, g_ref, scale_ref, shift_ref)
    gsum = gsum_ref[...]
    seg = lambda s: _dot(h, win_ref[:, SEG_OFFSETS[s]:SEG_OFFSETS[s + 1]])
    norm = lambda s: _head_norm(seg(s), gsum, gain_ref[SEG_GAIN_ROW[s]:SEG_GAIN_ROW[s] + 1, :SEG_WIDTHS[s]])
    ka, qa, qb = (norm(s).astype(BF16) for s in (SEG_KA, SEG_QA, SEG_QB))
    kbd = _duplicate_heads(norm(SEG_KB)).astype(BF16)
    vat, vbt = (seg(s).T.astype(BF16) for s in (SEG_VA, SEG_VB))
    ka_ref[0], kbd_ref[0] = ka, kbd
    for j in range(vat_ref.shape[1]):
        vat_ref[0, j] = vat[:, j * vat_ref.shape[3]:(j + 1) * vat_ref.shape[3]]
    for j in range(vbt_ref.shape[1]):
        vbt_ref[0, j] = vbt[:, j * vbt_ref.shape[3]:(j + 1) * vbt_ref.shape[3]]

    width = QUAD * HEAD_DIM
    parts = []
    for g in range(NA_HEADS // QUAD):
        sl = slice(g * width, (g + 1) * width)
        o = _softmax_pv_t([_dot_t(ka[:, sl], _stack_quad(qa[:, sl]))], [vat[sl]])
        parts.append(_unstack_quad_t(o, n))
    c_a = jnp.concatenate(parts, axis=1).astype(BF16)
    parts = []
    for g in range(GQA_KV_HEADS):
        sl = slice(g * 2 * LANES, (g + 1) * 2 * LANES)
        s = _scores_t(_stack_group(qb[:, sl]).T, kbd[:, g * LANES:(g + 1) * LANES])
        o = _softmax_pv_t([s], [vbt[g * HEAD_DIM:(g + 1) * HEAD_DIM]])
        parts.append(_unstack_group_t(o, n))
    c_b = jnp.concatenate(parts, axis=1).astype(BF16)

    z_a, z_b, g_a, g_b = (_dot(h, win_ref[:, GATE_OFFSETS[i]:GATE_OFFSETS[i + 1]]) for i in range(4))
    o_a = _dot((c_a.astype(F32) * (z_a * jax.nn.sigmoid(z_a))).astype(BF16), woa_ref[...])
    o_b = _dot((c_b.astype(F32) * (z_b * jax.nn.sigmoid(z_b))).astype(BF16), wob_ref[...])
    merged = jax.nn.sigmoid(g_a) * o_a + jax.nn.sigmoid(g_b) * o_b
    o_ref[0] = x + gate_ref[0] * _dot(merged.astype(BF16), wout_ref[...])


def _context_layer(ctx, shift, scale, gate, norm_g, w_in, gsum, gains, w_o_a, w_o_b, w_out, layer):
    batch, n, _ = ctx.shape
    slab_a, slab_b = min(SEG_T_SLAB[SEG_VA], n), min(SEG_T_SLAB[SEG_VB], n)
    row = lambda w: pl.BlockSpec((1, n, w), lambda b: (b, 0, 0))
    mod = pl.BlockSpec((1, 1, D_MODEL), lambda b: (0, 0, 0))
    weight = lambda a: pl.BlockSpec((None,) + a.shape[1:], lambda b: (layer, 0, 0), pipeline_mode=pl.Buffered(1))
    const = lambda a: pl.BlockSpec(a.shape, lambda b: (0, 0))
    tspec = lambda w, slab: pl.BlockSpec((1, n // slab, w, slab), lambda b: (b, 0, 0, 0))
    out_shapes = [(batch, n, SEG_OUT_WIDTHS[SEG_KA]), (batch, n // slab_a, SEG_OUT_WIDTHS[SEG_VA], slab_a),
                  (batch, n, SEG_OUT_WIDTHS[SEG_KB]), (batch, n // slab_b, SEG_OUT_WIDTHS[SEG_VB], slab_b)]
    *kv, new_ctx = pl.pallas_call(
        _ctx_kernel,
        grid=(batch,),
        in_specs=[row(D_MODEL), mod, mod, mod, const(norm_g), weight(w_in), const(gsum), const(gains),
                  weight(w_o_a), weight(w_o_b), weight(w_out)],
        out_specs=[row(SEG_OUT_WIDTHS[SEG_KA]), tspec(SEG_OUT_WIDTHS[SEG_VA], slab_a),
                   row(SEG_OUT_WIDTHS[SEG_KB]), tspec(SEG_OUT_WIDTHS[SEG_VB], slab_b), row(D_MODEL)],
        out_shape=[jax.ShapeDtypeStruct(s, BF16) for s in out_shapes] + [jax.ShapeDtypeStruct(ctx.shape, F32)],
        compiler_params=_params(("arbitrary",)),
        name="ctx",
    )(ctx, shift, scale, gate, norm_g, w_in, gsum, gains, w_o_a, w_o_b, w_out)
    return kv, new_ctx


def _merge_kernel(a_ref, b_ref, x_ref, shift_ref, scale_ref, gate_ref, g_ref, win_ref,
                  woa_ref, wob_ref, wout_ref, o_ref):
    x = x_ref[0]
    h = _modulated_norm(x, g_ref, scale_ref, shift_ref)
    z_a, z_b, g_a, g_b = (_dot(h, win_ref[:, GATE_OFFSETS[i]:GATE_OFFSETS[i + 1]]) for i in range(4))
    o_a = _dot((a_ref[0].astype(F32) * (z_a * jax.nn.sigmoid(z_a))).astype(BF16), woa_ref[...])
    o_b = _dot((b_ref[0].astype(F32) * (z_b * jax.nn.sigmoid(z_b))).astype(BF16), wob_ref[...])
    merged = jax.nn.sigmoid(g_a) * o_a + jax.nn.sigmoid(g_b) * o_b
    o_ref[0] = x + gate_ref[0] * _dot(merged.astype(BF16), wout_ref[...])


def _merge(a_att, b_att, x, shift, scale, gate, norm_g, w_in, w_o_a, w_o_b, w_out, layer, rows):
    groups, total, _ = x.shape
    act = lambda w: pl.BlockSpec((1, rows, w), lambda g, i: (g, i, 0))
    mod = pl.BlockSpec((1, 1, D_MODEL), lambda g, i: (g, 0, 0))
    const = lambda a: pl.BlockSpec((None,) + a.shape[1:], lambda g, i: (layer, 0, 0),
                                   pipeline_mode=pl.Buffered(1))
    return pl.pallas_call(
        _merge_kernel,
        grid=(groups, total // rows),
        in_specs=[act(NA_WIDTH), act(GQA_WIDTH), act(D_MODEL), mod, mod, mod,
                  pl.BlockSpec((1, D_MODEL), lambda g, i: (0, 0)),
                  const(w_in), const(w_o_a), const(w_o_b), const(w_out)],
        out_specs=act(D_MODEL),
        out_shape=jax.ShapeDtypeStruct(x.shape, F32),
        compiler_params=_params(("arbitrary", "arbitrary")),
        name="merge",
    )(a_att, b_att, x, shift, scale, gate, norm_g, w_in, w_o_a, w_o_b, w_out)


def _rope_tables(seq):
    t = jnp.arange(seq, dtype=jnp.int32)
    inv = 1.0 / (ROPE_THETA ** (jnp.arange(ROT_HALF, dtype=F32) / ROT_HALF))
    ang_r = (t // GRID_W).astype(F32)[:, None] * inv[None, :]
    ang_c = (t % GRID_W).astype(F32)[:, None] * inv[None, :]
    cos = jnp.concatenate([jnp.cos(ang_r)] * 2 + [jnp.cos(ang_c)] * 2, axis=1)
    sin = jnp.concatenate([-jnp.sin(ang_r), jnp.sin(ang_r), -jnp.sin(ang_c), jnp.sin(ang_c)], axis=1)
    reps = LANES // HEAD_DIM
    return jnp.tile(cos, (1, reps)), jnp.tile(sin, (1, reps))


def _group_sum_matrix():
    idx = np.arange(MXU_COLS) // HEAD_DIM
    return jnp.asarray(idx[:, None] == idx[None, :], dtype=BF16)


def kernel(x, c, ctx, c_ctx, w_ada, b_ada, norm_g, w_in, q_norm_a, k_norm_a, q_norm_b, k_norm_b,
           rpb, w_o_a, w_o_b, w_out):
    batch, seq, _ = x.shape
    ctx_len = ctx.shape[1]
    depth = w_ada.shape[0]
    rows = seq // GRID_W
    assert seq % GRID_W == 0 and rows >= WIN_R and seq % PROJ_ROWS == 0 and seq % GQA_Q_ROWS == 0

    pad = -(batch + 1) % 8
    c_rows = jnp.concatenate([c, c_ctx[None, :], jnp.zeros((pad, D_MODEL), F32)], axis=0)
    mod = _modulation(c_rows, w_ada, b_ada)

    cos_x, sin_x = _rope_tables(seq)
    ctx_rows = batch * ctx_len
    cos_c = jnp.ones((ctx_rows, LANES), F32)
    sin_c = jnp.zeros((ctx_rows, LANES), F32)
    per_batch_ctx = lambda a: a.reshape((batch, a.shape[1] // batch) + a.shape[2:])
    gsum = _group_sum_matrix()
    tables = _bias_tables(rpb)
    w_in, w_o_a, w_o_b, w_out = (w.astype(BF16) for w in (w_in, w_o_a, w_o_b, w_out))

    for l in range(depth):
        update_ctx = l < depth - 1
        shift, scale, gate = (mod[l, :, i * D_MODEL:(i + 1) * D_MODEL] for i in range(3))
        per_batch = lambda m: m[:batch, None, :]
        for_ctx = lambda m: m[batch][None, None, :]
        tile = lambda g, n: jnp.tile(g, n)
        gains = jnp.stack([tile(k_norm_a[l], NA_HEADS), tile(k_norm_b[l], NA_HEADS),
                           tile(q_norm_a[l], NA_HEADS) * (QK_SCALE * LOG2E),
                           tile(q_norm_b[l], GQA_Q_HEADS) * (QK_SCALE * LOG2E)])
        g_row = norm_g[l][None, :]

        ka, vat, kbd, vbt, qa, qb = _project(
            x, per_batch(shift), per_batch(scale), g_row, w_in, l, gsum, gains, cos_x, sin_x,
            len(SEG_WIDTHS), PROJ_ROWS, PROJ_ROWS)
        if update_ctx:
            (cka, cvat, ckbd, cvbt), ctx_new = _context_layer(
                ctx, for_ctx(shift), for_ctx(scale), for_ctx(gate), g_row, w_in, gsum, gains,
                w_o_a, w_o_b, w_out, l)
        else:
            pc = _project(ctx.reshape(1, ctx_rows, D_MODEL), for_ctx(shift), for_ctx(scale), g_row, w_in, l,
                          gsum, gains, cos_c, sin_c, KV_SEGS, min(PROJ_ROWS, ctx_rows), ctx_len)
            cka, cvat, ckbd, cvbt = (per_batch_ctx(a) for a in pc)
            ctx_new = ctx

        a_att = _neighbourhood_attention(qa, ka, vat, cka, cvat, tables, l)
        b_att = _gqa_attention(qb, kbd, vbt, ckbd, cvbt)
        x = _merge(a_att, b_att, x, per_batch(shift), per_batch(scale), per_batch(gate), g_row,
                   w_in, w_o_a, w_o_b, w_out, l, MERGE_ROWS)
        ctx = ctx_new
    return x
```

```python
import functools

import numpy as np
import jax
import jax.numpy as jnp
from jax import lax
from jax.experimental import pallas as pl
from jax.experimental.pallas import tpu as pltpu

D_MODEL = 1024
GRID_W = 64
HEAD_DIM = 64
NA_HEADS = 8
NA_WIDTH = NA_HEADS * HEAD_DIM
WIN_R = 8
WIN_C = 16
GQA_Q_HEADS = 8
GQA_KV_HEADS = 2
GQA_WIDTH = GQA_Q_HEADS * HEAD_DIM
GQA_KV_WIDTH = GQA_KV_HEADS * HEAD_DIM
ROPE_THETA = 10000.0
ROT_AXIS = HEAD_DIM // 2
ROT_HALF = ROT_AXIS // 2
EPS = 1e-6
QK_SCALE = HEAD_DIM ** -0.5
LOG2E = 1.4426950408889634

LANES = 128
MXU_COLS = 256
MASKED = -1e30
VMEM_LIMIT = 56 * 1024 * 1024

PROJ_ROWS = 1024
MERGE_ROWS = 1024
GQA_Q_ROWS = 256
GQA_Q_BLOCKS = 4
GQA_K_ROWS = 256
GQA_SUB_ROWS = 128
GQA_AHEAD = 1
QUAD = 4
ONES_ROWS = 16
NA_SUB_ROWS = 128
NA_AHEAD = 6
NA_UNROLL = 32

F32 = jnp.float32
BF16 = jnp.bfloat16

SEG_KA, SEG_VA, SEG_KB, SEG_VB, SEG_QA, SEG_QB = range(6)
SEG_WIDTHS = (NA_WIDTH, NA_WIDTH, GQA_KV_WIDTH, GQA_KV_WIDTH, NA_WIDTH, GQA_WIDTH)
SEG_OFFSETS = tuple(int(v) for v in np.cumsum((0,) + SEG_WIDTHS))
GATE_WIDTHS = (NA_WIDTH, GQA_WIDTH, D_MODEL, D_MODEL)
GATE_OFFSETS = tuple(int(v) for v in np.cumsum((SEG_OFFSETS[-1],) + GATE_WIDTHS))
SEG_OUT_WIDTHS = tuple(2 * w if s == SEG_KB else w for s, w in enumerate(SEG_WIDTHS))
KV_SEGS = 4
SEG_ISSUE_RANK = {SEG_QB: 0, SEG_KB: 1, SEG_QA: 2, SEG_KA: 3, SEG_VB: 4, SEG_VA: 5}
SEG_GAIN_ROW = {SEG_KA: 0, SEG_KB: 1, SEG_QA: 2, SEG_QB: 3}
SEG_ROPE = (SEG_KB, SEG_QB)
SEG_T_SLAB = {SEG_VA: LANES, SEG_VB: GQA_K_ROWS}


def _dot(a, b):
    return jnp.dot(a, b, preferred_element_type=F32)


def _dot_t(a, b):
    return lax.dot_general(a, b, (((1,), (1,)), ((), ())), preferred_element_type=F32)


def _params(semantics):
    return pltpu.CompilerParams(dimension_semantics=semantics, vmem_limit_bytes=VMEM_LIMIT)


def _mod_kernel(c_ref, w_ref, b_ref, o_ref):
    c = c_ref[...]
    a = c * jax.nn.sigmoid(c)
    o_ref[0] = jnp.dot(a, w_ref[0], preferred_element_type=F32,
                       precision=lax.Precision.HIGHEST) + b_ref[0]


def _modulation(c_rows, w_ada, b_ada):
    depth = w_ada.shape[0]
    rows = c_rows.shape[0]
    ncol = w_ada.shape[2] // D_MODEL
    return pl.pallas_call(
        _mod_kernel,
        grid=(depth, ncol),
        in_specs=[
            pl.BlockSpec((rows, D_MODEL), lambda l, j: (0, 0)),
            pl.BlockSpec((1, D_MODEL, D_MODEL), lambda l, j: (l, 0, j)),
            pl.BlockSpec((1, 1, D_MODEL), lambda l, j: (l, 0, j)),
        ],
        out_specs=pl.BlockSpec((1, rows, D_MODEL), lambda l, j: (l, 0, j)),
        out_shape=jax.ShapeDtypeStruct((depth, rows, ncol * D_MODEL), F32),
        compiler_params=_params(("arbitrary", "arbitrary")),
        name="mod",
    )(c_rows, w_ada, b_ada.reshape(depth, 1, -1))


def _head_norm(p, gsum, gain):
    sq = (p * p).astype(BF16)
    cols = min(MXU_COLS, p.shape[1])
    parts = []
    for j in range(p.shape[1] // cols):
        sl = slice(j * cols, (j + 1) * cols)
        ss = _dot(sq[:, sl], gsum[:cols, :cols])
        parts.append(p[:, sl] * lax.rsqrt(ss * (1.0 / HEAD_DIM) + EPS))
    y = parts[0] if len(parts) == 1 else jnp.concatenate(parts, axis=1)
    return y * gain


def _rope(y, cos, sin):
    width = y.shape[1]
    lane = lax.broadcasted_iota(jnp.int32, y.shape, 1)
    ahead = pltpu.roll(y, width - ROT_HALF, axis=1)
    behind = pltpu.roll(y, ROT_HALF, axis=1)
    partner = jnp.where((lane % ROT_AXIS) < ROT_HALF, ahead, behind)
    reps = width // LANES
    cos = cos if reps == 1 else jnp.concatenate([cos] * reps, axis=1)
    sin = sin if reps == 1 else jnp.concatenate([sin] * reps, axis=1)
    return y * cos + partner * sin


def _duplicate_heads(y):
    lo = lax.broadcasted_iota(jnp.int32, y.shape, 1) < HEAD_DIM
    swapped = pltpu.roll(y, HEAD_DIM, axis=1)
    return jnp.concatenate([jnp.where(lo, y, swapped), jnp.where(lo, swapped, y)], axis=1)


def _modulated_norm(x, g_ref, scale_ref, shift_ref):
    ms = jnp.mean(x * x, axis=-1, keepdims=True)
    gmod = g_ref[...] * (1.0 + scale_ref[0])
    return (x * lax.rsqrt(ms + EPS) * gmod + shift_ref[0]).astype(BF16)


def _proj_kernel(x_ref, shift_ref, scale_ref, g_ref, w_ref, gsum_ref, gain_ref, cos_ref, sin_ref,
                 *out_refs, nseg, max_slab):
    h = _modulated_norm(x_ref[0], g_ref, scale_ref, shift_ref)
    gsum = gsum_ref[...]
    for seg in sorted(range(nseg), key=SEG_ISSUE_RANK.__getitem__):
        width = SEG_WIDTHS[seg]
        p = _dot(h, w_ref[:, SEG_OFFSETS[seg]:SEG_OFFSETS[seg + 1]])
        if seg in SEG_GAIN_ROW:
            row = SEG_GAIN_ROW[seg]
            p = _head_norm(p, gsum, gain_ref[row:row + 1, :width])
        if seg in SEG_ROPE:
            p = _rope(p, cos_ref[...], sin_ref[...])
        if seg == SEG_KB:
            p = _duplicate_heads(p)
        if seg in SEG_T_SLAB:
            pt = p.T.astype(BF16)
            slab = min(SEG_T_SLAB[seg], max_slab)
            for j in range(pt.shape[1] // slab):
                out_refs[seg][0, j] = pt[:, j * slab:(j + 1) * slab]
        else:
            out_refs[seg][0] = p.astype(BF16)


def _project(x, shift, scale, norm_g, w_all, layer, gsum, gains, cos, sin, nseg, rows, max_slab):
    groups, total, _ = x.shape
    ncols = SEG_OFFSETS[nseg]
    const = lambda g, i: (0, 0)

    def out_layout(s):
        width = SEG_OUT_WIDTHS[s]
        if s not in SEG_T_SLAB:
            return (groups, total, width), pl.BlockSpec((1, rows, width), lambda g, i: (g, i, 0))
        slab = min(SEG_T_SLAB[s], max_slab)
        return ((groups, total // slab, width, slab),
                pl.BlockSpec((1, rows // slab, width, slab), lambda g, i: (g, i, 0, 0)))

    layouts = [out_layout(s) for s in range(nseg)]
    return pl.pallas_call(
        functools.partial(_proj_kernel, nseg=nseg, max_slab=max_slab),
        grid=(groups, total // rows),
        in_specs=[
            pl.BlockSpec((1, rows, D_MODEL), lambda g, i: (g, i, 0)),
            pl.BlockSpec((1, 1, D_MODEL), lambda g, i: (g, 0, 0)),
            pl.BlockSpec((1, 1, D_MODEL), lambda g, i: (g, 0, 0)),
            pl.BlockSpec((1, D_MODEL), const),
            pl.BlockSpec((None, D_MODEL, ncols), lambda g, i: (layer, 0, 0), pipeline_mode=pl.Buffered(1)),
            pl.BlockSpec((MXU_COLS, MXU_COLS), const),
            pl.BlockSpec(gains.shape, const),
            pl.BlockSpec((rows, LANES), lambda g, i: (i, 0)),
            pl.BlockSpec((rows, LANES), lambda g, i: (i, 0)),
        ],
        out_specs=[spec for _, spec in layouts],
        out_shape=[jax.ShapeDtypeStruct(shape, BF16) for shape, _ in layouts],
        compiler_params=_params(("arbitrary", "arbitrary")),
        name="proj",
    )(x, shift, scale, norm_g, w_all, gsum, gains, cos, sin)


def _stack_heads(pair):
    lo = lax.broadcasted_iota(jnp.int32, pair.shape, 1) < HEAD_DIM
    zero = jnp.zeros_like(pair)
    return jnp.concatenate([jnp.where(lo, pair, zero), jnp.where(lo, zero, pair)], axis=0)


def _stack_quad(q):
    head = lax.broadcasted_iota(jnp.int32, q.shape, 1) // HEAD_DIM
    zero = jnp.zeros_like(q)
    return jnp.concatenate([jnp.where(head == h, q, zero) for h in range(QUAD)], axis=0)


def _unstack_quad_t(o, rows):
    ot = o.T
    head = lax.broadcasted_iota(jnp.int32, (rows, QUAD * HEAD_DIM), 1) // HEAD_DIM
    out = ot[:rows]
    for h in range(1, QUAD):
        out = jnp.where(head == h, ot[h * rows:(h + 1) * rows], out)
    return out


def _pv_t(vt, s, m):
    lhs = jnp.concatenate([vt, jnp.ones((ONES_ROWS, vt.shape[1]), vt.dtype)], axis=0)
    return _dot(lhs, jnp.exp2((s - m).astype(BF16)))


def _normalise_t(acc, dims):
    return acc[:dims] * (1.0 / acc[dims:dims + 1])


def _softmax_pv_t(logits, values_t):
    m = functools.reduce(jnp.maximum, [jnp.max(s, axis=0, keepdims=True) for s in logits])
    acc = functools.reduce(jnp.add, [_pv_t(vt, s, m) for vt, s in zip(values_t, logits)])
    return _normalise_t(acc, values_t[0].shape[0])


def _na_window(r, rows, slabs):
    start = jnp.clip(r - WIN_R // 2, 0, rows - WIN_R)
    slab0 = jnp.minimum(start // 2, rows // 2 - slabs)
    first = 2 * slab0
    blocks = []
    for w in range(2 * slabs):
        key_row = first + w
        valid = (key_row >= start) & (key_row < start + WIN_R)
        blocks.append(jnp.where(valid, key_row - r + (WIN_R - 1), 2 * WIN_R - 1))
    return slab0, blocks


def _na_kernel(q_ref, k_ref, vt_ref, kc_ref, vct_ref, t_ref, o_ref, qt_ref, tab_ref, *, rows, unroll):
    n_off = t_ref.shape[1]
    for off in range(n_off - 1):
        tab_ref[off] = jnp.concatenate([t_ref[h, off] for h in range(QUAD)], axis=1)
    tab_ref[n_off - 1] = jnp.full(tab_ref.shape[1:], MASKED * LOG2E, F32)
    kc = kc_ref[0]
    vct = jnp.concatenate([vct_ref[0, j] for j in range(vct_ref.shape[1])], axis=1)
    assert unroll % 2 == 0 and rows % 2 == 0 and WIN_R % 2 == 0

    def logits(r, slot):
        slabs = WIN_R // 2 + slot % 2
        slab0, blocks = _na_window(r, rows, slabs)
        qt_ref[slot] = _stack_quad(q_ref[0, pl.ds(pl.multiple_of(r * GRID_W, GRID_W), GRID_W), :]).T
        qs_t = qt_ref[slot]
        kw = k_ref[0, pl.ds(pl.multiple_of(slab0 * LANES, LANES), slabs * LANES), :]
        bias = jnp.concatenate([tab_ref[b] for b in blocks], axis=0)
        return _dot(kw, qs_t) + bias, _dot(kc, qs_t), slab0

    def finish(r, s_win, s_ctx, slab0):
        slabs = s_win.shape[0] // LANES
        vt = vt_ref[0, pl.ds(slab0, slabs)]
        vtw = jnp.concatenate([vt[j] for j in range(slabs)], axis=1)
        sub = NA_SUB_ROWS
        parts = [s[i:i + sub] for s in (s_ctx, s_win) for i in range(0, s.shape[0], sub)]
        _, acc = _attend_t(parts, jnp.concatenate([vct, vtw], axis=1), None)
        o = _normalise_t(acc, vct.shape[0])
        o_ref[0, pl.ds(pl.multiple_of(r * GRID_W, GRID_W), GRID_W), :] = _unstack_quad_t(o, GRID_W).astype(BF16)

    def body(i, carry):
        pending = []
        for e in range(unroll):
            pending.append((i * unroll + e,) + logits(i * unroll + e, e))
            if len(pending) > NA_AHEAD:
                finish(*pending.pop(0))
        for args in pending:
            finish(*args)
        return carry

    lax.fori_loop(0, rows // unroll, body, 0)


def _neighbourhood_attention(qa, ka, vat, cka, cvat, tables, layer):
    batch, seq, _ = qa.shape
    ctx_len = cka.shape[1]
    rows = seq // GRID_W
    width = QUAD * HEAD_DIM
    blk = lambda n: pl.BlockSpec((1, n, width), lambda b, j: (b, 0, j))
    vblk = lambda a: pl.BlockSpec((1, a.shape[1], width, LANES), lambda b, j: (b, 0, j, 0))
    return pl.pallas_call(
        functools.partial(_na_kernel, rows=rows, unroll=NA_UNROLL),
        grid=(batch, NA_HEADS // QUAD),
        in_specs=[blk(seq), blk(seq), vblk(vat), blk(ctx_len), vblk(cvat),
                  pl.BlockSpec((None, None) + tables.shape[2:], lambda b, j: (layer, j, 0, 0, 0, 0))],
        out_specs=blk(seq),
        out_shape=jax.ShapeDtypeStruct((batch, seq, NA_WIDTH), BF16),
        scratch_shapes=[pltpu.VMEM((NA_UNROLL, width, QUAD * GRID_W), BF16),
                        pltpu.VMEM((tables.shape[3], GRID_W, QUAD * GRID_W), F32)],
        compiler_params=_params(("arbitrary", "arbitrary")),
        name="na",
    )(qa, ka, vat, cka, cvat, tables)


def _bias_kernel(rpb_ref, onehot_ref, mask_ref, o_ref):
    o_ref[...] = (jnp.dot(rpb_ref[...], onehot_ref[...], preferred_element_type=F32,
                          precision=lax.Precision.HIGHEST) + mask_ref[...]) * LOG2E


def _bias_tables(rpb):
    depth = rpb.shape[0]
    n_row_off, n_col_off = 2 * WIN_R - 1, 2 * WIN_C - 1
    cols = np.arange(GRID_W)
    cstart = np.clip(cols - WIN_C // 2, 0, GRID_W - WIN_C)
    kcol = np.arange(GRID_W)
    valid = (kcol[None, :] >= cstart[:, None]) & (kcol[None, :] < cstart[:, None] + WIN_C)
    col_off = kcol[None, :] - cols[:, None] + (WIN_C - 1)
    pad_r, pad_c = -n_row_off % 8, -n_col_off % 8
    assert pad_r >= 1
    onehot = (np.arange(n_col_off + pad_c)[:, None, None] == col_off.T[None]) & valid.T[None]
    onehot = jnp.asarray(onehot.reshape(n_col_off + pad_c, GRID_W * GRID_W), dtype=F32)
    mask = jnp.asarray(np.where(valid.T, 0.0, MASKED).reshape(1, GRID_W * GRID_W), dtype=F32)
    heads = depth * NA_HEADS
    rpb_p = jnp.pad(rpb.reshape(heads, n_row_off, n_col_off), ((0, 0), (0, pad_r), (0, pad_c)))
    rpb_p = rpb_p.reshape(heads * (n_row_off + pad_r), n_col_off + pad_c)
    col_blocks = 4
    cols_per = GRID_W * GRID_W // col_blocks
    toep = pl.pallas_call(
        _bias_kernel,
        grid=(col_blocks,),
        in_specs=[pl.BlockSpec(rpb_p.shape, lambda j: (0, 0)),
                  pl.BlockSpec((onehot.shape[0], cols_per), lambda j: (0, j)),
                  pl.BlockSpec((1, cols_per), lambda j: (0, j))],
        out_specs=pl.BlockSpec((rpb_p.shape[0], cols_per), lambda j: (0, j)),
        out_shape=jax.ShapeDtypeStruct((rpb_p.shape[0], GRID_W * GRID_W), F32),
        compiler_params=_params(("arbitrary",)),
        name="bias",
    )(rpb_p, onehot, mask)
    return toep.reshape(depth, NA_HEADS // QUAD, QUAD, n_row_off + pad_r, GRID_W, GRID_W)


def _stack_group(q):
    return jnp.concatenate([_stack_heads(q[:, :LANES]), _stack_heads(q[:, LANES:])], axis=0)


def _unstack_group_t(o, rows):
    pairs = [jnp.concatenate([o[:, (2 * j) * rows:(2 * j + 1) * rows],
                              o[:, (2 * j + 1) * rows:(2 * j + 2) * rows]], axis=0).T for j in range(2)]
    return jnp.concatenate(pairs, axis=1)


def _scores_t(qs_t, k):
    return _dot(k, qs_t)


def _attend_t(parts, vt, state):
    m_old = None if state is None else state[0]
    m, ps, ms = m_old, [], []
    for s in parts:
        m_part = jnp.max(s, axis=0, keepdims=True)
        m = m_part if m is None else jnp.maximum(m, m_part)
        ps.append(jnp.exp2((s - m).astype(BF16)))
        ms.append(m)
    ps = [p * jnp.exp2(m_i - m).astype(BF16) for p, m_i in zip(ps[:-1], ms[:-1])] + ps[-1:]
    lhs = jnp.concatenate([vt, jnp.ones((ONES_ROWS, vt.shape[1]), vt.dtype)], axis=0)
    pv = _dot(lhs, ps[0] if len(ps) == 1 else jnp.concatenate(ps, axis=0))
    if state is None:
        return m, pv
    return m, jnp.exp2(m_old - m) * state[1] + pv


def _gqa_kernel(q_ref, k_ref, vt_ref, kc_ref, vct_ref, o_ref, qt_ref, *, chunks, tq, tk, blocks):
    for j in range(blocks):
        qt_ref[j] = _stack_group(q_ref[0, j * tq:(j + 1) * tq, :]).T
    pending = []
    states = [None] * blocks

    def drain(limit):
        while len(pending) > limit:
            j, parts, vt = pending.pop(0)
            states[j] = _attend_t(parts, vt, states[j])

    def logits(qs, keys, c, rows):
        sub = min(GQA_SUB_ROWS, rows)
        return [_scores_t(qs, keys[0, c * rows + i * sub:c * rows + (i + 1) * sub, :]) for i in range(rows // sub)]

    for j in range(blocks):
        qs = qt_ref[j]
        for c in range(vct_ref.shape[1]):
            pending.append((j, logits(qs, kc_ref, c, vct_ref.shape[3]), vct_ref[0, c]))
            drain(GQA_AHEAD)
        for c in range(chunks):
            pending.append((j, logits(qs, k_ref, c, tk), vt_ref[0, c]))
            drain(GQA_AHEAD)
    drain(0)
    for j in range(blocks):
        _, acc = states[j]
        o_ref[0, j * tq:(j + 1) * tq, :] = _unstack_group_t(_normalise_t(acc, HEAD_DIM), tq).astype(BF16)


def _gqa_attention(qb, kbd, vbt, ckbd, cvbt):
    batch, seq, _ = qb.shape
    ctx_len = ckbd.shape[1]
    tq, tk, blocks = GQA_Q_ROWS, vbt.shape[3], GQA_Q_BLOCKS
    kspec = lambda n: pl.BlockSpec((1, n, LANES), lambda b, g, i: (b, 0, g))
    vspec = lambda a: pl.BlockSpec((1, a.shape[1], HEAD_DIM, a.shape[3]), lambda b, g, i: (b, 0, g, 0))
    qspec = pl.BlockSpec((1, blocks * tq, 2 * LANES), lambda b, g, i: (b, i, g))
    return pl.pallas_call(
        functools.partial(_gqa_kernel, chunks=seq // tk, tq=tq, tk=tk, blocks=blocks),
        grid=(batch, GQA_KV_HEADS, seq // (blocks * tq)),
        in_specs=[qspec, kspec(seq), vspec(vbt), kspec(ctx_len), vspec(cvbt)],
        out_specs=qspec,
        out_shape=jax.ShapeDtypeStruct((batch, seq, GQA_WIDTH), BF16),
        scratch_shapes=[pltpu.VMEM((blocks, LANES, (GQA_Q_HEADS // GQA_KV_HEADS) * tq), BF16)],
        compiler_params=_params(("arbitrary", "arbitrary", "arbitrary")),
        name="gqa",
    )(qb, kbd, vbt, ckbd, cvbt)


def _ctx_kernel(x_ref, shift_ref, scale_ref, gate_ref, g_ref, win_ref, gsum_ref, gain_ref,
                woa_ref, wob_ref, wout_ref, ka_ref, vat_ref, kbd_ref, vbt_ref, o_ref):
    x = x_ref[0]
    n = x.shape[0]
    h = _modulated_norm(x, g_ref, scale_ref, shift_ref)
    gsum = gsum_ref[...]
    seg = lambda s: _dot(h, win_ref[:, SEG_OFFSETS[s]:SEG_OFFSETS[s + 1]])
    norm = lambda s: _head_norm(seg(s), gsum, gain_ref[SEG_GAIN_ROW[s]:SEG_GAIN_ROW[s] + 1, :SEG_WIDTHS[s]])
    ka, qa, qb = (norm(s).astype(BF16) for s in (SEG_KA, SEG_QA, SEG_QB))
    kbd = _duplicate_heads(norm(SEG_KB)).astype(BF16)
    vat, vbt = (seg(s).T.astype(BF16) for s in (SEG_VA, SEG_VB))
    ka_ref[0], kbd_ref[0] = ka, kbd
    for j in range(vat_ref.shape[1]):
        vat_ref[0, j] = vat[:, j * vat_ref.shape[3]:(j + 1) * vat_ref.shape[3]]
    for j in range(vbt_ref.shape[1]):
        vbt_ref[0, j] = vbt[:, j * vbt_ref.shape[3]:(j + 1) * vbt_ref.shape[3]]

    width = QUAD * HEAD_DIM
    parts = []
    for g in range(NA_HEADS // QUAD):
        sl = slice(g * width, (g + 1) * width)
        o = _softmax_pv_t([_dot_t(ka[:, sl], _stack_quad(qa[:, sl]))], [vat[sl]])
        parts.append(_unstack_quad_t(o, n))
    c_a = jnp.concatenate(parts, axis=1).astype(BF16)
    parts = []
    for g in range(GQA_KV_HEADS):
        sl = slice(g * 2 * LANES, (g + 1) * 2 * LANES)
        s = _scores_t(_stack_group(qb[:, sl]).T, kbd[:, g * LANES:(g + 1) * LANES])
        o = _softmax_pv_t([s], [vbt[g * HEAD_DIM:(g + 1) * HEAD_DIM]])
        parts.append(_unstack_group_t(o, n))
    c_b = jnp.concatenate(parts, axis=1).astype(BF16)

    z_a, z_b, g_a, g_b = (_dot(h, win_ref[:, GATE_OFFSETS[i]:GATE_OFFSETS[i + 1]]) for i in range(4))
    o_a = _dot((c_a.astype(F32) * (z_a * jax.nn.sigmoid(z_a))).astype(BF16), woa_ref[...])
    o_b = _dot((c_b.astype(F32) * (z_b * jax.nn.sigmoid(z_b))).astype(BF16), wob_ref[...])
    merged = jax.nn.sigmoid(g_a) * o_a + jax.nn.sigmoid(g_b) * o_b
    o_ref[0] = x + gate_ref[0] * _dot(merged.astype(BF16), wout_ref[...])


def _context_layer(ctx, shift, scale, gate, norm_g, w_in, gsum, gains, w_o_a, w_o_b, w_out, layer):
    batch, n, _ = ctx.shape
    slab_a, slab_b = min(SEG_T_SLAB[SEG_VA], n), min(SEG_T_SLAB[SEG_VB], n)
    row = lambda w: pl.BlockSpec((1, n, w), lambda b: (b, 0, 0))
    mod = pl.BlockSpec((1, 1, D_MODEL), lambda b: (0, 0, 0))
    weight = lambda a: pl.BlockSpec((None,) + a.shape[1:], lambda b: (layer, 0, 0), pipeline_mode=pl.Buffered(1))
    const = lambda a: pl.BlockSpec(a.shape, lambda b: (0, 0))
    tspec = lambda w, slab: pl.BlockSpec((1, n // slab, w, slab), lambda b: (b, 0, 0, 0))
    out_shapes = [(batch, n, SEG_OUT_WIDTHS[SEG_KA]), (batch, n // slab_a, SEG_OUT_WIDTHS[SEG_VA], slab_a),
                  (batch, n, SEG_OUT_WIDTHS[SEG_KB]), (batch, n // slab_b, SEG_OUT_WIDTHS[SEG_VB], slab_b)]
    *kv, new_ctx = pl.pallas_call(
        _ctx_kernel,
        grid=(batch,),
        in_specs=[row(D_MODEL), mod, mod, mod, const(norm_g), weight(w_in), const(gsum), const(gains),
                  weight(w_o_a), weight(w_o_b), weight(w_out)],
        out_specs=[row(SEG_OUT_WIDTHS[SEG_KA]), tspec(SEG_OUT_WIDTHS[SEG_VA], slab_a),
                   row(SEG_OUT_WIDTHS[SEG_KB]), tspec(SEG_OUT_WIDTHS[SEG_VB], slab_b), row(D_MODEL)],
        out_shape=[jax.ShapeDtypeStruct(s, BF16) for s in out_shapes] + [jax.ShapeDtypeStruct(ctx.shape, F32)],
        compiler_params=_params(("arbitrary",)),
        name="ctx",
    )(ctx, shift, scale, gate, norm_g, w_in, gsum, gains, w_o_a, w_o_b, w_out)
    return kv, new_ctx


def _merge_kernel(a_ref, b_ref, x_ref, shift_ref, scale_ref, gate_ref, g_ref, win_ref,
                  woa_ref, wob_ref, wout_ref, o_ref):
    x = x_ref[0]
    h = _modulated_norm(x, g_ref, scale_ref, shift_ref)
    z_a, z_b, g_a, g_b = (_dot(h, win_ref[:, GATE_OFFSETS[i]:GATE_OFFSETS[i + 1]]) for i in range(4))
    o_a = _dot((a_ref[0].astype(F32) * (z_a * jax.nn.sigmoid(z_a))).astype(BF16), woa_ref[...])
    o_b = _dot((b_ref[0].astype(F32) * (z_b * jax.nn.sigmoid(z_b))).astype(BF16), wob_ref[...])
    merged = jax.nn.sigmoid(g_a) * o_a + jax.nn.sigmoid(g_b) * o_b
    o_ref[0] = x + gate_ref[0] * _dot(merged.astype(BF16), wout_ref[...])


def _merge(a_att, b_att, x, shift, scale, gate, norm_g, w_in, w_o_a, w_o_b, w_out, layer, rows):
    groups, total, _ = x.shape
    act = lambda w: pl.BlockSpec((1, rows, w), lambda g, i: (g, i, 0))
    mod = pl.BlockSpec((1, 1, D_MODEL), lambda g, i: (g, 0, 0))
    const = lambda a: pl.BlockSpec((None,) + a.shape[1:], lambda g, i: (layer, 0, 0),
                                   pipeline_mode=pl.Buffered(1))
    return pl.pallas_call(
        _merge_kernel,
        grid=(groups, total // rows),
        in_specs=[act(NA_WIDTH), act(GQA_WIDTH), act(D_MODEL), mod, mod, mod,
                  pl.BlockSpec((1, D_MODEL), lambda g, i: (0, 0)),
                  const(w_in), const(w_o_a), const(w_o_b), const(w_out)],
        out_specs=act(D_MODEL),
        out_shape=jax.ShapeDtypeStruct(x.shape, F32),
        compiler_params=_params(("arbitrary", "arbitrary")),
        name="merge",
    )(a_att, b_att, x, shift, scale, gate, norm_g, w_in, w_o_a, w_o_b, w_out)


def _rope_tables(seq):
    t = jnp.arange(seq, dtype=jnp.int32)
    inv = 1.0 / (ROPE_THETA ** (jnp.arange(ROT_HALF, dtype=F32) / ROT_HALF))
    ang_r = (t // GRID_W).astype(F32)[:, None] * inv[None, :]
    ang_c = (t % GRID_W).astype(F32)[:, None] * inv[None, :]
    cos = jnp.concatenate([jnp.cos(ang_r)] * 2 + [jnp.cos(ang_c)] * 2, axis=1)
    sin = jnp.concatenate([-jnp.sin(ang_r), jnp.sin(ang_r), -jnp.sin(ang_c), jnp.sin(ang_c)], axis=1)
    reps = LANES // HEAD_DIM
    return jnp.tile(cos, (1, reps)), jnp.tile(sin, (1, reps))


def _group_sum_matrix():
    idx = np.arange(MXU_COLS) // HEAD_DIM
    return jnp.asarray(idx[:, None] == idx[None, :], dtype=BF16)


def kernel(x, c, ctx, c_ctx, w_ada, b_ada, norm_g, w_in, q_norm_a, k_norm_a, q_norm_b, k_norm_b,
           rpb, w_o_a, w_o_b, w_out):
    batch, seq, _ = x.shape
    ctx_len = ctx.shape[1]
    depth = w_ada.shape[0]
    rows = seq // GRID_W
    assert seq % GRID_W == 0 and rows >= WIN_R and seq % PROJ_ROWS == 0 and seq % GQA_Q_ROWS == 0

    pad = -(batch + 1) % 8
    c_rows = jnp.concatenate([c, c_ctx[None, :], jnp.zeros((pad, D_MODEL), F32)], axis=0)
    mod = _modulation(c_rows, w_ada, b_ada)

    cos_x, sin_x = _rope_tables(seq)
    ctx_rows = batch * ctx_len
    cos_c = jnp.ones((ctx_rows, LANES), F32)
    sin_c = jnp.zeros((ctx_rows, LANES), F32)
    per_batch_ctx = lambda a: a.reshape((batch, a.shape[1] // batch) + a.shape[2:])
    gsum = _group_sum_matrix()
    tables = _bias_tables(rpb)
    w_in, w_o_a, w_o_b, w_out = (w.astype(BF16) for w in (w_in, w_o_a, w_o_b, w_out))

    for l in range(depth):
        update_ctx = l < depth - 1
        shift, scale, gate = (mod[l, :, i * D_MODEL:(i + 1) * D_MODEL] for i in range(3))
        per_batch = lambda m: m[:batch, None, :]
        for_ctx = lambda m: m[batch][None, None, :]
        tile = lambda g, n: jnp.tile(g, n)
        gains = jnp.stack([tile(k_norm_a[l], NA_HEADS), tile(k_norm_b[l], NA_HEADS),
                           tile(q_norm_a[l], NA_HEADS) * (QK_SCALE * LOG2E),
                           tile(q_norm_b[l], GQA_Q_HEADS) * (QK_SCALE * LOG2E)])
        g_row = norm_g[l][None, :]

        ka, vat, kbd, vbt, qa, qb = _project(
            x, per_batch(shift), per_batch(scale), g_row, w_in, l, gsum, gains, cos_x, sin_x,
            len(SEG_WIDTHS), PROJ_ROWS, PROJ_ROWS)
        if update_ctx:
            (cka, cvat, ckbd, cvbt), ctx_new = _context_layer(
                ctx, for_ctx(shift), for_ctx(scale), for_ctx(gate), g_row, w_in, gsum, gains,
                w_o_a, w_o_b, w_out, l)
        else:
            pc = _project(ctx.reshape(1, ctx_rows, D_MODEL), for_ctx(shift), for_ctx(scale), g_row, w_in, l,
                          gsum, gains, cos_c, sin_c, KV_SEGS, min(PROJ_ROWS, ctx_rows), ctx_len)
            cka, cvat, ckbd, cvbt = (per_batch_ctx(a) for a in pc)
            ctx_new = ctx

        a_att = _neighbourhood_attention(qa, ka, vat, cka, cvat, tables, l)
        b_att = _gqa_attention(qb, kbd, vbt, ckbd, cvbt)
        x = _merge(a_att, b_att, x, per_batch(shift), per_batch(scale), per_batch(gate), g_row,
                   w_in, w_o_a, w_o_b, w_out, l, MERGE_ROWS)
        ctx = ctx_new
    return x
```

```python
import functools

import numpy as np
import jax
import jax.numpy as jnp
from jax import lax
from jax.experimental import pallas as pl
from jax.experimental.pallas import tpu as pltpu

D_MODEL = 1024
GRID_W = 64
HEAD_DIM = 64
NA_HEADS = 8
NA_WIDTH = NA_HEADS * HEAD_DIM
WIN_R = 8
WIN_C = 16
GQA_Q_HEADS = 8
GQA_KV_HEADS = 2
GQA_WIDTH = GQA_Q_HEADS * HEAD_DIM
GQA_KV_WIDTH = GQA_KV_HEADS * HEAD_DIM
ROPE_THETA = 10000.0
ROT_AXIS = HEAD_DIM // 2
ROT_HALF = ROT_AXIS // 2
EPS = 1e-6
QK_SCALE = HEAD_DIM ** -0.5
LOG2E = 1.4426950408889634

LANES = 128
MXU_COLS = 256
MASKED = -1e30
VMEM_LIMIT = 56 * 1024 * 1024

PROJ_ROWS = 1024
PROJ_HALVES = 2
MERGE_ROWS = 1024
GQA_Q_ROWS = 256
GQA_Q_BLOCKS = 4
GQA_K_ROWS = 256
GQA_SUB_ROWS = 128
GQA_AHEAD = 1
QUAD = 4
ONES_ROWS = 16
NA_SUB_ROWS = 128
NA_AHEAD = 6
NA_UNROLL = 32

F32 = jnp.float32
BF16 = jnp.bfloat16

SEG_KA, SEG_VA, SEG_KB, SEG_VB, SEG_QA, SEG_QB = range(6)
SEG_WIDTHS = (NA_WIDTH, NA_WIDTH, GQA_KV_WIDTH, GQA_KV_WIDTH, NA_WIDTH, GQA_WIDTH)
SEG_OFFSETS = tuple(int(v) for v in np.cumsum((0,) + SEG_WIDTHS))
GATE_WIDTHS = (NA_WIDTH, GQA_WIDTH, D_MODEL, D_MODEL)
GATE_OFFSETS = tuple(int(v) for v in np.cumsum((SEG_OFFSETS[-1],) + GATE_WIDTHS))
SEG_OUT_WIDTHS = tuple(2 * w if s == SEG_KB else w for s, w in enumerate(SEG_WIDTHS))
KV_SEGS = 4
SEG_ISSUE_RANK = {SEG_QB: 0, SEG_KB: 1, SEG_QA: 2, SEG_KA: 3, SEG_VB: 4, SEG_VA: 5}
SEG_GAIN_ROW = {SEG_KA: 0, SEG_KB: 1, SEG_QA: 2, SEG_QB: 3}
SEG_ROPE = (SEG_KB, SEG_QB)
SEG_T_SLAB = {SEG_VA: LANES, SEG_VB: GQA_K_ROWS}


def _dot(a, b):
    return jnp.dot(a, b, preferred_element_type=F32)


def _dot_t(a, b):
    return lax.dot_general(a, b, (((1,), (1,)), ((), ())), preferred_element_type=F32)


def _params(semantics):
    return pltpu.CompilerParams(dimension_semantics=semantics, vmem_limit_bytes=VMEM_LIMIT)


def _mod_kernel(c_ref, w_ref, b_ref, o_ref):
    c = c_ref[...]
    a = c * jax.nn.sigmoid(c)
    o_ref[0] = jnp.dot(a, w_ref[0], preferred_element_type=F32,
                       precision=lax.Precision.HIGHEST) + b_ref[0]


def _modulation(c_rows, w_ada, b_ada):
    depth = w_ada.shape[0]
    rows = c_rows.shape[0]
    ncol = w_ada.shape[2] // D_MODEL
    return pl.pallas_call(
        _mod_kernel,
        grid=(depth, ncol),
        in_specs=[
            pl.BlockSpec((rows, D_MODEL), lambda l, j: (0, 0)),
            pl.BlockSpec((1, D_MODEL, D_MODEL), lambda l, j: (l, 0, j)),
            pl.BlockSpec((1, 1, D_MODEL), lambda l, j: (l, 0, j)),
        ],
        out_specs=pl.BlockSpec((1, rows, D_MODEL), lambda l, j: (l, 0, j)),
        out_shape=jax.ShapeDtypeStruct((depth, rows, ncol * D_MODEL), F32),
        compiler_params=_params(("arbitrary", "arbitrary")),
        name="mod",
    )(c_rows, w_ada, b_ada.reshape(depth, 1, -1))


def _head_norm(p, gsum, gain):
    sq = (p * p).astype(BF16)
    cols = min(MXU_COLS, p.shape[1])
    parts = []
    for j in range(p.shape[1] // cols):
        sl = slice(j * cols, (j + 1) * cols)
        ss = _dot(sq[:, sl], gsum[:cols, :cols])
        parts.append(p[:, sl] * lax.rsqrt(ss * (1.0 / HEAD_DIM) + EPS))
    y = parts[0] if len(parts) == 1 else jnp.concatenate(parts, axis=1)
    return y * gain


def _rope(y, cos, sin):
    width = y.shape[1]
    lane = lax.broadcasted_iota(jnp.int32, y.shape, 1)
    ahead = pltpu.roll(y, width - ROT_HALF, axis=1)
    behind = pltpu.roll(y, ROT_HALF, axis=1)
    partner = jnp.where((lane % ROT_AXIS) < ROT_HALF, ahead, behind)
    reps = width // LANES
    cos = cos if reps == 1 else jnp.concatenate([cos] * reps, axis=1)
    sin = sin if reps == 1 else jnp.concatenate([sin] * reps, axis=1)
    return y * cos + partner * sin


def _duplicate_heads(y):
    lo = lax.broadcasted_iota(jnp.int32, y.shape, 1) < HEAD_DIM
    swapped = pltpu.roll(y, HEAD_DIM, axis=1)
    return jnp.concatenate([jnp.where(lo, y, swapped), jnp.where(lo, swapped, y)], axis=1)


def _modulated_norm(x, g_ref, scale_ref, shift_ref):
    ms = jnp.mean(x * x, axis=-1, keepdims=True)
    gmod = g_ref[...] * (1.0 + scale_ref[0])
    return (x * lax.rsqrt(ms + EPS) * gmod + shift_ref[0]).astype(BF16)


def _proj_kernel(x_ref, shift_ref, scale_ref, g_ref, w_ref, gsum_ref, gain_ref, cos_ref, sin_ref,
                 *out_refs, nseg, max_slab):
    gsum = gsum_ref[...]
    rows = x_ref.shape[1]
    halves = PROJ_HALVES if rows % (PROJ_HALVES * max(SEG_T_SLAB.values())) == 0 else 1
    part = rows // halves
    for hf in range(halves):
        r0 = hf * part
        h = _modulated_norm(x_ref[0, r0:r0 + part], g_ref, scale_ref, shift_ref)
        for seg in sorted(range(nseg), key=SEG_ISSUE_RANK.__getitem__):
            width = SEG_WIDTHS[seg]
            p = _dot(h, w_ref[:, SEG_OFFSETS[seg]:SEG_OFFSETS[seg + 1]])
            if seg in SEG_GAIN_ROW:
                row = SEG_GAIN_ROW[seg]
                p = _head_norm(p, gsum, gain_ref[row:row + 1, :width])
            if seg in SEG_ROPE:
                p = _rope(p, cos_ref[r0:r0 + part], sin_ref[r0:r0 + part])
            if seg == SEG_KB:
                p = _duplicate_heads(p)
            if seg in SEG_T_SLAB:
                pt = p.T.astype(BF16)
                slab = min(SEG_T_SLAB[seg], max_slab)
                per_part = part // slab
                for j in range(per_part):
                    out_refs[seg][0, hf * per_part + j] = pt[:, j * slab:(j + 1) * slab]
            else:
                out_refs[seg][0, r0:r0 + part] = p.astype(BF16)


def _project(x, shift, scale, norm_g, w_all, layer, gsum, gains, cos, sin, nseg, rows, max_slab):
    groups, total, _ = x.shape
    ncols = SEG_OFFSETS[nseg]
    const = lambda g, i: (0, 0)

    def out_layout(s):
        width = SEG_OUT_WIDTHS[s]
        if s not in SEG_T_SLAB:
            return (groups, total, width), pl.BlockSpec((1, rows, width), lambda g, i: (g, i, 0))
        slab = min(SEG_T_SLAB[s], max_slab)
        return ((groups, total // slab, width, slab),
                pl.BlockSpec((1, rows // slab, width, slab), lambda g, i: (g, i, 0, 0)))

    layouts = [out_layout(s) for s in range(nseg)]
    return pl.pallas_call(
        functools.partial(_proj_kernel, nseg=nseg, max_slab=max_slab),
        grid=(groups, total // rows),
        in_specs=[
            pl.BlockSpec((1, rows, D_MODEL), lambda g, i: (g, i, 0)),
            pl.BlockSpec((1, 1, D_MODEL), lambda g, i: (g, 0, 0)),
            pl.BlockSpec((1, 1, D_MODEL), lambda g, i: (g, 0, 0)),
            pl.BlockSpec((1, D_MODEL), const),
            pl.BlockSpec((None, D_MODEL, ncols), lambda g, i: (layer, 0, 0), pipeline_mode=pl.Buffered(1)),
            pl.BlockSpec((MXU_COLS, MXU_COLS), const),
            pl.BlockSpec(gains.shape, const),
            pl.BlockSpec((rows, LANES), lambda g, i: (i, 0)),
            pl.BlockSpec((rows, LANES), lambda g, i: (i, 0)),
        ],
        out_specs=[spec for _, spec in layouts],
        out_shape=[jax.ShapeDtypeStruct(shape, BF16) for shape, _ in layouts],
        compiler_params=_params(("arbitrary", "arbitrary")),
        name="proj",
    )(x, shift, scale, norm_g, w_all, gsum, gains, cos, sin)


def _stack_heads(pair):
    lo = lax.broadcasted_iota(jnp.int32, pair.shape, 1) < HEAD_DIM
    zero = jnp.zeros_like(pair)
    return jnp.concatenate([jnp.where(lo, pair, zero), jnp.where(lo, zero, pair)], axis=0)


def _stack_quad(q):
    head = lax.broadcasted_iota(jnp.int32, q.shape, 1) // HEAD_DIM
    zero = jnp.zeros_like(q)
    return jnp.concatenate([jnp.where(head == h, q, zero) for h in range(QUAD)], axis=0)


def _unstack_quad_t(o, rows):
    ot = o.T
    head = lax.broadcasted_iota(jnp.int32, (rows, QUAD * HEAD_DIM), 1) // HEAD_DIM
    out = ot[:rows]
    for h in range(1, QUAD):
        out = jnp.where(head == h, ot[h * rows:(h + 1) * rows], out)
    return out


def _pv_t(vt, s, m):
    lhs = jnp.concatenate([vt, jnp.ones((ONES_ROWS, vt.shape[1]), vt.dtype)], axis=0)
    return _dot(lhs, jnp.exp2((s - m).astype(BF16)))


def _normalise_t(acc, dims):
    return acc[:dims] * (1.0 / acc[dims:dims + 1])


def _softmax_pv_t(logits, values_t):
    m = functools.reduce(jnp.maximum, [jnp.max(s, axis=0, keepdims=True) for s in logits])
    acc = functools.reduce(jnp.add, [_pv_t(vt, s, m) for vt, s in zip(values_t, logits)])
    return _normalise_t(acc, values_t[0].shape[0])


def _na_window(r, rows, slabs):
    start = jnp.clip(r - WIN_R // 2, 0, rows - WIN_R)
    slab0 = jnp.minimum(start // 2, rows // 2 - slabs)
    first = 2 * slab0
    blocks = []
    for w in range(2 * slabs):
        key_row = first + w
        valid = (key_row >= start) & (key_row < start + WIN_R)
        blocks.append(jnp.where(valid, key_row - r + (WIN_R - 1), 2 * WIN_R - 1))
    return slab0, blocks


def _na_kernel(q_ref, k_ref, vt_ref, kc_ref, vct_ref, t_ref, o_ref, qt_ref, tab_ref, *, rows, unroll):
    n_off = t_ref.shape[1]
    for off in range(n_off - 1):
        tab_ref[off] = jnp.concatenate([t_ref[h, off] for h in range(QUAD)], axis=1)
    tab_ref[n_off - 1] = jnp.full(tab_ref.shape[1:], MASKED * LOG2E, F32)
    kc = kc_ref[0]
    vct = jnp.concatenate([vct_ref[0, j] for j in range(vct_ref.shape[1])], axis=1)
    assert unroll % 2 == 0 and rows % 2 == 0 and WIN_R % 2 == 0

    def logits(r, slot):
        slabs = WIN_R // 2 + slot % 2
        slab0, blocks = _na_window(r, rows, slabs)
        qt_ref[slot] = _stack_quad(q_ref[0, pl.ds(pl.multiple_of(r * GRID_W, GRID_W), GRID_W), :]).T
        qs_t = qt_ref[slot]
        kw = k_ref[0, pl.ds(pl.multiple_of(slab0 * LANES, LANES), slabs * LANES), :]
        bias = jnp.concatenate([tab_ref[b] for b in blocks], axis=0)
        return _dot(kw, qs_t) + bias, _dot(kc, qs_t), slab0

    def finish(r, s_win, s_ctx, slab0):
        slabs = s_win.shape[0] // LANES
        vt = vt_ref[0, pl.ds(slab0, slabs)]
        vtw = jnp.concatenate([vt[j] for j in range(slabs)], axis=1)
        sub = NA_SUB_ROWS
        parts = [s[i:i + sub] for s in (s_ctx, s_win) for i in range(0, s.shape[0], sub)]
        _, acc = _attend_t(parts, jnp.concatenate([vct, vtw], axis=1), None)
        o = _normalise_t(acc, vct.shape[0])
        o_ref[0, pl.ds(pl.multiple_of(r * GRID_W, GRID_W), GRID_W), :] = _unstack_quad_t(o, GRID_W).astype(BF16)

    def body(i, carry):
        pending = []
        for e in range(unroll):
            pending.append((i * unroll + e,) + logits(i * unroll + e, e))
            if len(pending) > NA_AHEAD:
                finish(*pending.pop(0))
        for args in pending:
            finish(*args)
        return carry

    lax.fori_loop(0, rows // unroll, body, 0)


def _neighbourhood_attention(qa, ka, vat, cka, cvat, tables, layer):
    batch, seq, _ = qa.shape
    ctx_len = cka.shape[1]
    rows = seq // GRID_W
    width = QUAD * HEAD_DIM
    blk = lambda n: pl.BlockSpec((1, n, width), lambda b, j: (b, 0, j))
    vblk = lambda a: pl.BlockSpec((1, a.shape[1], width, LANES), lambda b, j: (b, 0, j, 0))
    return pl.pallas_call(
        functools.partial(_na_kernel, rows=rows, unroll=NA_UNROLL),
        grid=(batch, NA_HEADS // QUAD),
        in_specs=[blk(seq), blk(seq), vblk(vat), blk(ctx_len), vblk(cvat),
                  pl.BlockSpec((None, None) + tables.shape[2:], lambda b, j: (layer, j, 0, 0, 0, 0))],
        out_specs=blk(seq),
        out_shape=jax.ShapeDtypeStruct((batch, seq, NA_WIDTH), BF16),
        scratch_shapes=[pltpu.VMEM((NA_UNROLL, width, QUAD * GRID_W), BF16),
                        pltpu.VMEM((tables.shape[3], GRID_W, QUAD * GRID_W), F32)],
        compiler_params=_params(("arbitrary", "arbitrary")),
        name="na",
    )(qa, ka, vat, cka, cvat, tables)


def _bias_kernel(rpb_ref, onehot_ref, mask_ref, o_ref):
    o_ref[...] = (jnp.dot(rpb_ref[...], onehot_ref[...], preferred_element_type=F32,
                          precision=lax.Precision.HIGHEST) + mask_ref[...]) * LOG2E


def _bias_tables(rpb):
    depth = rpb.shape[0]
    n_row_off, n_col_off = 2 * WIN_R - 1, 2 * WIN_C - 1
    cols = np.arange(GRID_W)
    cstart = np.clip(cols - WIN_C // 2, 0, GRID_W - WIN_C)
    kcol = np.arange(GRID_W)
    valid = (kcol[None, :] >= cstart[:, None]) & (kcol[None, :] < cstart[:, None] + WIN_C)
    col_off = kcol[None, :] - cols[:, None] + (WIN_C - 1)
    pad_r, pad_c = -n_row_off % 8, -n_col_off % 8
    assert pad_r >= 1
    onehot = (np.arange(n_col_off + pad_c)[:, None, None] == col_off.T[None]) & valid.T[None]
    onehot = jnp.asarray(onehot.reshape(n_col_off + pad_c, GRID_W * GRID_W), dtype=F32)
    mask = jnp.asarray(np.where(valid.T, 0.0, MASKED).reshape(1, GRID_W * GRID_W), dtype=F32)
    heads = depth * NA_HEADS
    rpb_p = jnp.pad(rpb.reshape(heads, n_row_off, n_col_off), ((0, 0), (0, pad_r), (0, pad_c)))
    rpb_p = rpb_p.reshape(heads * (n_row_off + pad_r), n_col_off + pad_c)
    col_blocks = 4
    cols_per = GRID_W * GRID_W // col_blocks
    toep = pl.pallas_call(
        _bias_kernel,
        grid=(col_blocks,),
        in_specs=[pl.BlockSpec(rpb_p.shape, lambda j: (0, 0)),
                  pl.BlockSpec((onehot.shape[0], cols_per), lambda j: (0, j)),
                  pl.BlockSpec((1, cols_per), lambda j: (0, j))],
        out_specs=pl.BlockSpec((rpb_p.shape[0], cols_per), lambda j: (0, j)),
        out_shape=jax.ShapeDtypeStruct((rpb_p.shape[0], GRID_W * GRID_W), F32),
        compiler_params=_params(("arbitrary",)),
        name="bias",
    )(rpb_p, onehot, mask)
    return toep.reshape(depth, NA_HEADS // QUAD, QUAD, n_row_off + pad_r, GRID_W, GRID_W)


def _stack_group(q):
    return jnp.concatenate([_stack_heads(q[:, :LANES]), _stack_heads(q[:, LANES:])], axis=0)


def _unstack_group_t(o, rows):
    pairs = [jnp.concatenate([o[:, (2 * j) * rows:(2 * j + 1) * rows],
                              o[:, (2 * j + 1) * rows:(2 * j + 2) * rows]], axis=0).T for j in range(2)]
    return jnp.concatenate(pairs, axis=1)


def _scores_t(qs_t, k):
    return _dot(k, qs_t)


def _attend_t(parts, vt, state):
    m_old = None if state is None else state[0]
    m, ps, ms = m_old, [], []
    for s in parts:
        m_part = jnp.max(s, axis=0, keepdims=True)
        m = m_part if m is None else jnp.maximum(m, m_part)
        ps.append(jnp.exp2((s - m).astype(BF16)))
        ms.append(m)
    ps = [p * jnp.exp2(m_i - m).astype(BF16) for p, m_i in zip(ps[:-1], ms[:-1])] + ps[-1:]
    lhs = jnp.concatenate([vt, jnp.ones((ONES_ROWS, vt.shape[1]), vt.dtype)], axis=0)
    pv = _dot(lhs, ps[0] if len(ps) == 1 else jnp.concatenate(ps, axis=0))
    if state is None:
        return m, pv
    return m, jnp.exp2(m_old - m) * state[1] + pv


def _gqa_kernel(q_ref, k_ref, vt_ref, kc_ref, vct_ref, o_ref, qt_ref, *, chunks, tq, tk, blocks):
    for j in range(blocks):
        qt_ref[j] = _stack_group(q_ref[0, j * tq:(j + 1) * tq, :]).T
    pending = []
    states = [None] * blocks

    def drain(limit):
        while len(pending) > limit:
            j, parts, vt = pending.pop(0)
            states[j] = _attend_t(parts, vt, states[j])

    def logits(qs, keys, c, rows):
        sub = min(GQA_SUB_ROWS, rows)
        return [_scores_t(qs, keys[0, c * rows + i * sub:c * rows + (i + 1) * sub, :]) for i in range(rows // sub)]

    for j in range(blocks):
        qs = qt_ref[j]
        for c in range(vct_ref.shape[1]):
            pending.append((j, logits(qs, kc_ref, c, vct_ref.shape[3]), vct_ref[0, c]))
            drain(GQA_AHEAD)
        for c in range(chunks):
            pending.append((j, logits(qs, k_ref, c, tk), vt_ref[0, c]))
            drain(GQA_AHEAD)
    drain(0)
    for j in range(blocks):
        _, acc = states[j]
        o_ref[0, j * tq:(j + 1) * tq, :] = _unstack_group_t(_normalise_t(acc, HEAD_DIM), tq).astype(BF16)


def _gqa_attention(qb, kbd, vbt, ckbd, cvbt):
    batch, seq, _ = qb.shape
    ctx_len = ckbd.shape[1]
    tq, tk, blocks = GQA_Q_ROWS, vbt.shape[3], GQA_Q_BLOCKS
    kspec = lambda n: pl.BlockSpec((1, n, LANES), lambda b, g, i: (b, 0, g))
    vspec = lambda a: pl.BlockSpec((1, a.shape[1], HEAD_DIM, a.shape[3]), lambda b, g, i: (b, 0, g, 0))
    qspec = pl.BlockSpec((1, blocks * tq, 2 * LANES), lambda b, g, i: (b, i, g))
    return pl.pallas_call(
        functools.partial(_gqa_kernel, chunks=seq // tk, tq=tq, tk=tk, blocks=blocks),
        grid=(batch, GQA_KV_HEADS, seq // (blocks * tq)),
        in_specs=[qspec, kspec(seq), vspec(vbt), kspec(ctx_len), vspec(cvbt)],
        out_specs=qspec,
        out_shape=jax.ShapeDtypeStruct((batch, seq, GQA_WIDTH), BF16),
        scratch_shapes=[pltpu.VMEM((blocks, LANES, (GQA_Q_HEADS // GQA_KV_HEADS) * tq), BF16)],
        compiler_params=_params(("arbitrary", "arbitrary", "arbitrary")),
        name="gqa",
    )(qb, kbd, vbt, ckbd, cvbt)


def _ctx_kernel(x_ref, shift_ref, scale_ref, gate_ref, g_ref, win_ref, gsum_ref, gain_ref,
                woa_ref, wob_ref, wout_ref, ka_ref, vat_ref, kbd_ref, vbt_ref, o_ref):
    x = x_ref[0]
    n = x.shape[0]
    h = _modulated_norm(x, g_ref, scale_ref, shift_ref)
    gsum = gsum_ref[...]
    seg = lambda s: _dot(h, win_ref[:, SEG_OFFSETS[s]:SEG_OFFSETS[s + 1]])
    norm = lambda s: _head_norm(seg(s), gsum, gain_ref[SEG_GAIN_ROW[s]:SEG_GAIN_ROW[s] + 1, :SEG_WIDTHS[s]])
    ka, qa, qb = (norm(s).astype(BF16) for s in (SEG_KA, SEG_QA, SEG_QB))
    kbd = _duplicate_heads(norm(SEG_KB)).astype(BF16)
    vat, vbt = (seg(s).T.astype(BF16) for s in (SEG_VA, SEG_VB))
    ka_ref[0], kbd_ref[0] = ka, kbd
    for j in range(vat_ref.shape[1]):
        vat_ref[0, j] = vat[:, j * vat_ref.shape[3]:(j + 1) * vat_ref.shape[3]]
    for j in range(vbt_ref.shape[1]):
        vbt_ref[0, j] = vbt[:, j * vbt_ref.shape[3]:(j + 1) * vbt_ref.shape[3]]

    width = QUAD * HEAD_DIM
    parts = []
    for g in range(NA_HEADS // QUAD):
        sl = slice(g * width, (g + 1) * width)
        o = _softmax_pv_t([_dot_t(ka[:, sl], _stack_quad(qa[:, sl]))], [vat[sl]])
        parts.append(_unstack_quad_t(o, n))
    c_a = jnp.concatenate(parts, axis=1).astype(BF16)
    parts = []
    for g in range(GQA_KV_HEADS):
        sl = slice(g * 2 * LANES, (g + 1) * 2 * LANES)
        s = _scores_t(_stack_group(qb[:, sl]).T, kbd[:, g * LANES:(g + 1) * LANES])
        o = _softmax_pv_t([s], [vbt[g * HEAD_DIM:(g + 1) * HEAD_DIM]])
        parts.append(_unstack_group_t(o, n))
    c_b = jnp.concatenate(parts, axis=1).astype(BF16)

    z_a, z_b, g_a, g_b = (_dot(h, win_ref[:, GATE_OFFSETS[i]:GATE_OFFSETS[i + 1]]) for i in range(4))
    o_a = _dot((c_a.astype(F32) * (z_a * jax.nn.sigmoid(z_a))).astype(BF16), woa_ref[...])
    o_b = _dot((c_b.astype(F32) * (z_b * jax.nn.sigmoid(z_b))).astype(BF16), wob_ref[...])
    merged = jax.nn.sigmoid(g_a) * o_a + jax.nn.sigmoid(g_b) * o_b
    o_ref[0] = x + gate_ref[0] * _dot(merged.astype(BF16), wout_ref[...])


def _context_layer(ctx, shift, scale, gate, norm_g, w_in, gsum, gains, w_o_a, w_o_b, w_out, layer):
    batch, n, _ = ctx.shape
    slab_a, slab_b = min(SEG_T_SLAB[SEG_VA], n), min(SEG_T_SLAB[SEG_VB], n)
    row = lambda w: pl.BlockSpec((1, n, w), lambda b: (b, 0, 0))
    mod = pl.BlockSpec((1, 1, D_MODEL), lambda b: (0, 0, 0))
    weight = lambda a: pl.BlockSpec((None,) + a.shape[1:], lambda b: (layer, 0, 0), pipeline_mode=pl.Buffered(1))
    const = lambda a: pl.BlockSpec(a.shape, lambda b: (0, 0))
    tspec = lambda w, slab: pl.BlockSpec((1, n // slab, w, slab), lambda b: (b, 0, 0, 0))
    out_shapes = [(batch, n, SEG_OUT_WIDTHS[SEG_KA]), (batch, n // slab_a, SEG_OUT_WIDTHS[SEG_VA], slab_a),
                  (batch, n, SEG_OUT_WIDTHS[SEG_KB]), (batch, n // slab_b, SEG_OUT_WIDTHS[SEG_VB], slab_b)]
    *kv, new_ctx = pl.pallas_call(
        _ctx_kernel,
        grid=(batch,),
        in_specs=[row(D_MODEL), mod, mod, mod, const(norm_g), weight(w_in), const(gsum), const(gains),
                  weight(w_o_a), weight(w_o_b), weight(w_out)],
        out_specs=[row(SEG_OUT_WIDTHS[SEG_KA]), tspec(SEG_OUT_WIDTHS[SEG_VA], slab_a),
                   row(SEG_OUT_WIDTHS[SEG_KB]), tspec(SEG_OUT_WIDTHS[SEG_VB], slab_b), row(D_MODEL)],
        out_shape=[jax.ShapeDtypeStruct(s, BF16) for s in out_shapes] + [jax.ShapeDtypeStruct(ctx.shape, F32)],
        compiler_params=_params(("arbitrary",)),
        name="ctx",
    )(ctx, shift, scale, gate, norm_g, w_in, gsum, gains, w_o_a, w_o_b, w_out)
    return kv, new_ctx


def _merge_kernel(a_ref, b_ref, x_ref, shift_ref, scale_ref, gate_ref, g_ref, win_ref,
                  woa_ref, wob_ref, wout_ref, o_ref):
    x = x_ref[0]
    h = _modulated_norm(x, g_ref, scale_ref, shift_ref)
    z_a, z_b, g_a, g_b = (_dot(h, win_ref[:, GATE_OFFSETS[i]:GATE_OFFSETS[i + 1]]) for i in range(4))
    o_a = _dot((a_ref[0].astype(F32) * (z_a * jax.nn.sigmoid(z_a))).astype(BF16), woa_ref[...])
    o_b = _dot((b_ref[0].astype(F32) * (z_b * jax.nn.sigmoid(z_b))).astype(BF16), wob_ref[...])
    merged = jax.nn.sigmoid(g_a) * o_a + jax.nn.sigmoid(g_b) * o_b
    o_ref[0] = x + gate_ref[0] * _dot(merged.astype(BF16), wout_ref[...])


def _merge(a_att, b_att, x, shift, scale, gate, norm_g, w_in, w_o_a, w_o_b, w_out, layer, rows):
    groups, total, _ = x.shape
    act = lambda w: pl.BlockSpec((1, rows, w), lambda g, i: (g, i, 0))
    mod = pl.BlockSpec((1, 1, D_MODEL), lambda g, i: (g, 0, 0))
    const = lambda a: pl.BlockSpec((None,) + a.shape[1:], lambda g, i: (layer, 0, 0),
                                   pipeline_mode=pl.Buffered(1))
    return pl.pallas_call(
        _merge_kernel,
        grid=(groups, total // rows),
        in_specs=[act(NA_WIDTH), act(GQA_WIDTH), act(D_MODEL), mod, mod, mod,
                  pl.BlockSpec((1, D_MODEL), lambda g, i: (0, 0)),
                  const(w_in), const(w_o_a), const(w_o_b), const(w_out)],
        out_specs=act(D_MODEL),
        out_shape=jax.ShapeDtypeStruct(x.shape, F32),
        compiler_params=_params(("arbitrary", "arbitrary")),
        name="merge",
    )(a_att, b_att, x, shift, scale, gate, norm_g, w_in, w_o_a, w_o_b, w_out)


def _rope_tables(seq):
    t = jnp.arange(seq, dtype=jnp.int32)
    inv = 1.0 / (ROPE_THETA ** (jnp.arange(ROT_HALF, dtype=F32) / ROT_HALF))
    ang_r = (t // GRID_W).astype(F32)[:, None] * inv[None, :]
    ang_c = (t % GRID_W).astype(F32)[:, None] * inv[None, :]
    cos = jnp.concatenate([jnp.cos(ang_r)] * 2 + [jnp.cos(ang_c)] * 2, axis=1)
    sin = jnp.concatenate([-jnp.sin(ang_r), jnp.sin(ang_r), -jnp.sin(ang_c), jnp.sin(ang_c)], axis=1)
    reps = LANES // HEAD_DIM
    return jnp.tile(cos, (1, reps)), jnp.tile(sin, (1, reps))


def _group_sum_matrix():
    idx = np.arange(MXU_COLS) // HEAD_DIM
    return jnp.asarray(idx[:, None] == idx[None, :], dtype=BF16)


def kernel(x, c, ctx, c_ctx, w_ada, b_ada, norm_g, w_in, q_norm_a, k_norm_a, q_norm_b, k_norm_b,
           rpb, w_o_a, w_o_b, w_out):
    batch, seq, _ = x.shape
    ctx_len = ctx.shape[1]
    depth = w_ada.shape[0]
    rows = seq // GRID_W
    assert seq % GRID_W == 0 and rows >= WIN_R and seq % PROJ_ROWS == 0 and seq % GQA_Q_ROWS == 0

    pad = -(batch + 1) % 8
    c_rows = jnp.concatenate([c, c_ctx[None, :], jnp.zeros((pad, D_MODEL), F32)], axis=0)
    mod = _modulation(c_rows, w_ada, b_ada)

    cos_x, sin_x = _rope_tables(seq)
    ctx_rows = batch * ctx_len
    cos_c = jnp.ones((ctx_rows, LANES), F32)
    sin_c = jnp.zeros((ctx_rows, LANES), F32)
    per_batch_ctx = lambda a: a.reshape((batch, a.shape[1] // batch) + a.shape[2:])
    gsum = _group_sum_matrix()
    tables = _bias_tables(rpb)
    w_in, w_o_a, w_o_b, w_out = (w.astype(BF16) for w in (w_in, w_o_a, w_o_b, w_out))

    for l in range(depth):
        update_ctx = l < depth - 1
        shift, scale, gate = (mod[l, :, i * D_MODEL:(i + 1) * D_MODEL] for i in range(3))
        per_batch = lambda m: m[:batch, None, :]
        for_ctx = lambda m: m[batch][None, None, :]
        tile = lambda g, n: jnp.tile(g, n)
        gains = jnp.stack([tile(k_norm_a[l], NA_HEADS), tile(k_norm_b[l], NA_HEADS),
                           tile(q_norm_a[l], NA_HEADS) * (QK_SCALE * LOG2E),
                           tile(q_norm_b[l], GQA_Q_HEADS) * (QK_SCALE * LOG2E)])
        g_row = norm_g[l][None, :]

        ka, vat, kbd, vbt, qa, qb = _project(
            x, per_batch(shift), per_batch(scale), g_row, w_in, l, gsum, gains, cos_x, sin_x,
            len(SEG_WIDTHS), PROJ_ROWS, PROJ_ROWS)
        if update_ctx:
            (cka, cvat, ckbd, cvbt), ctx_new = _context_layer(
                ctx, for_ctx(shift), for_ctx(scale), for_ctx(gate), g_row, w_in, gsum, gains,
                w_o_a, w_o_b, w_out, l)
        else:
            pc = _project(ctx.reshape(1, ctx_rows, D_MODEL), for_ctx(shift), for_ctx(scale), g_row, w_in, l,
                          gsum, gains, cos_c, sin_c, KV_SEGS, min(PROJ_ROWS, ctx_rows), ctx_len)
            cka, cvat, ckbd, cvbt = (per_batch_ctx(a) for a in pc)
            ctx_new = ctx

        a_att = _neighbourhood_attention(qa, ka, vat, cka, cvat, tables, l)
        b_att = _gqa_attention(qb, kbd, vbt, ckbd, cvbt)
        x = _merge(a_att, b_att, x, per_batch(shift), per_batch(scale), per_batch(gate), g_row,
                   w_in, w_o_a, w_o_b, w_out, l, MERGE_ROWS)
        ctx = ctx_new
    return x
```
